```python
import math
import jax, jax.numpy as jnp
from jax import lax
import numpy as np

D_MODEL = 1024
BATCH = 1
SEQ = 16384
DEPTH = 1
DEC_BATCH = 32
DEC_SEQ = 4
PAST_LEN = 16384
PAGE_SIZE = 128

HEAD_DIM = 64
N_GROUPS_A = 3
HEADS_PER_GROUP = 8
WINDOWS = (128, 512, 2048)
DILATIONS = (1, 4, 16)
N_KEYS = WINDOWS[0] // DILATIONS[0] + 1
W_MAX = 2048
Q_BLOCK = 128
ATTN_WIDTH = N_GROUPS_A * HEADS_PER_GROUP * HEAD_DIM
A_OUT = HEADS_PER_GROUP * HEAD_DIM
N_BUCKETS = 32
MAX_EXACT = 16
MAX_DISTANCE = 2048
CONV_WIDTH = 3
CONV_CHANNELS = 512
N_EXPERT_GROUPS = 4
EXPERTS_PER_GROUP = 8
N_EXPERTS = N_EXPERT_GROUPS * EXPERTS_PER_GROUP
TOP_K = 2
D_EXPERT = 512
MOE_BLOCK = 128
LN_EPS = 1e-5
PROJ_WIDTH = 3 * ATTN_WIDTH + 3 * CONV_CHANNELS + 2 * D_MODEL

kernel_name = "dilated_swa_shortconv_hmoe_deepnorm_step"


def layer_norm(x, g, b):
    xf = x.astype(jnp.float32)
    mu = jnp.mean(xf, axis=-1, keepdims=True)
    var = jnp.mean(jnp.square(xf - mu), axis=-1, keepdims=True)
    return ((xf - mu) * lax.rsqrt(var + LN_EPS) * g + b).astype(x.dtype)


def t5_bucket(n):
    nf = jnp.maximum(n, 1).astype(jnp.float32)
    large = MAX_EXACT + (jnp.log(nf / MAX_EXACT) / math.log(MAX_DISTANCE / MAX_EXACT)
                         * (N_BUCKETS - MAX_EXACT)).astype(jnp.int32)
    large = jnp.minimum(large, N_BUCKETS - 1)
    return jnp.where(n < MAX_EXACT, n, large)


def strided_offsets():
    return jnp.arange(N_KEYS, dtype=jnp.int32)[None, :] * jnp.array(DILATIONS, jnp.int32)[:, None]


def rel_bias_per_pattern(rel_bias):
    bucket = t5_bucket(strided_offsets())
    table = rel_bias.reshape(N_BUCKETS, N_GROUPS_A, HEADS_PER_GROUP)
    b = table[bucket, jnp.arange(N_GROUPS_A)[:, None]]
    return jnp.transpose(b, (0, 2, 1)).astype(jnp.float32)


def attend_strided(q, k_sel, v_sel, valid, bias):
    s = jnp.einsum('bqghd,bgqjhd->bghqj', q, k_sel).astype(jnp.float32) * (HEAD_DIM ** -0.5)
    s = s + bias[None, :, :, None, :]
    s = jnp.where(valid[None, :, None], s, -jnp.inf)
    lse = jax.nn.logsumexp(s, axis=-1)
    p = jnp.exp(s - lse[..., None]).astype(v_sel.dtype)
    o = jnp.einsum('bghqj,bgqjhd->bghqd', p, v_sel)
    w = jax.nn.softmax(lse, axis=1).astype(v_sel.dtype)
    o = jnp.einsum('bghq,bghqd->bqhd', w, o)
    return o.reshape(o.shape[0], o.shape[1], A_OUT)


def dilated_attn_prompt(q, k, v, bias):
    B, S = q.shape[0], q.shape[1]
    pad = ((0, 0), (W_MAX, 0), (0, 0), (0, 0), (0, 0))
    kpad = jnp.pad(k, pad)
    vpad = jnp.pad(v, pad)
    offs = strided_offsets()
    g_ar = jnp.arange(N_GROUPS_A)[:, None, None]

    def block(i):
        t = i * Q_BLOCK + jnp.arange(Q_BLOCK, dtype=jnp.int32)
        pos = t[None, :, None] - offs[:, None, :]
        idx = pos + W_MAX
        kb = kpad[:, idx, g_ar]
        vb = vpad[:, idx, g_ar]
        qb = lax.dynamic_slice_in_dim(q, i * Q_BLOCK, Q_BLOCK, axis=1)
        return attend_strided(qb, kb, vb, pos >= 0, bias)

    o = lax.map(block, jnp.arange(S // Q_BLOCK))
    o = jnp.moveaxis(o, 0, 1).reshape(B, S, A_OUT)
    rows = []
    for g, w in enumerate(WINDOWS):
        L = min(w, S)
        rows.append(jnp.stack([k[:, S - L:, g], v[:, S - L:, g]], axis=2))
    return o, rows


def dilated_attn_sample(q, k, v, caches, bias):
    T = q.shape[1]
    offs = strided_offsets()
    kbs, vbs, valids, new_rows = [], [], [], []
    for g in range(N_GROUPS_A):
        cache = caches[g]
        L = cache.shape[1]
        kv_new = jnp.stack([k[:, :, g], v[:, :, g]], axis=2)
        full = jnp.concatenate([cache.astype(kv_new.dtype), kv_new], axis=1)
        rows = L + jnp.arange(T, dtype=jnp.int32)[:, None] - offs[g][None, :]
        sel = full[:, jnp.maximum(rows, 0)]
        kbs.append(sel[:, :, :, 0])
        vbs.append(sel[:, :, :, 1])
        valids.append(rows >= 0)
        new_rows.append(full[:, T:])
    o = attend_strided(q, jnp.stack(kbs, axis=1), jnp.stack(vbs, axis=1),
                       jnp.stack(valids, axis=0), bias)
    return o, new_rows


def short_conv(u, prev, w_conv):
    T = u.shape[1]
    full = jnp.concatenate([prev.astype(u.dtype), u], axis=1)
    y = full[:, 0:T] * w_conv[0]
    for i in range(1, CONV_WIDTH):
        y = y + full[:, i:i + T] * w_conv[i]
    return y, full[:, T:]


def split_projection(x, w_in):
    p = x @ w_in
    B, T = x.shape[0], x.shape[1]
    sizes = [ATTN_WIDTH] * 3 + [CONV_CHANNELS] * 3 + [D_MODEL] * 2
    points = [int(s) for s in np.cumsum(sizes)[:-1]]
    q, k, v, bg, cg, hg, ga, gb = jnp.split(p, points, axis=-1)
    shp = (B, T, N_GROUPS_A, HEADS_PER_GROUP, HEAD_DIM)
    return q.reshape(shp), k.reshape(shp), v.reshape(shp), bg, cg, hg, ga, gb


def dropless_expert_ffn(xt, expert, gate, w_g, w_u, w_d):
    n_tok, d = xt.shape
    M = expert.shape[0]
    tok = jnp.arange(M, dtype=jnp.int32) // TOP_K
    order = jnp.argsort(expert)
    e_s, tok_s, gate_s = expert[order], tok[order], gate[order]
    counts = jnp.zeros((N_EXPERTS,), jnp.int32).at[expert].add(1)
    start = jnp.cumsum(counts) - counts
    padded = (counts + MOE_BLOCK - 1) // MOE_BLOCK * MOE_BLOCK
    pend = jnp.cumsum(padded)
    pstart = pend - padded
    dest = pstart[e_s] + jnp.arange(M, dtype=jnp.int32) - start[e_s]
    n_blocks = (M + N_EXPERTS * (MOE_BLOCK - 1) + MOE_BLOCK - 1) // MOE_BLOCK
    buf = jnp.zeros((n_blocks * MOE_BLOCK, d), xt.dtype).at[dest].set(xt[tok_s])
    block_start = jnp.arange(n_blocks, dtype=jnp.int32) * MOE_BLOCK
    block_expert = jnp.minimum(jnp.sum(pend[None, :] <= block_start[:, None], axis=1), N_EXPERTS - 1)

    def run_block(args):
        xb, e = args
        h = jax.nn.silu(xb @ w_g[e]) * (xb @ w_u[e])
        return h @ w_d[e]

    out = lax.map(run_block, (buf.reshape(n_blocks, MOE_BLOCK, d), block_expert))
    out = out.reshape(n_blocks * MOE_BLOCK, d)
    contrib = out[dest] * gate_s[:, None].astype(out.dtype)
    return jax.ops.segment_sum(contrib, tok_s, num_segments=n_tok)


def hier_moe(x, w_rg, w_re, w_g, w_u, w_d):
    B, T, D = x.shape
    xt = x.reshape(B * T, D)
    g_logits = (xt @ w_rg).astype(jnp.float32)
    g_idx = jnp.argmax(g_logits, axis=-1).astype(jnp.int32)
    g_prob = jnp.take_along_axis(jax.nn.softmax(g_logits, axis=-1), g_idx[:, None], axis=-1)
    e_logits = (xt @ w_re).astype(jnp.float32).reshape(B * T, N_EXPERT_GROUPS, EXPERTS_PER_GROUP)
    e_in_group = jnp.take_along_axis(e_logits, g_idx[:, None, None], axis=1)[:, 0]
    top_val, top_idx = lax.top_k(e_in_group, TOP_K)
    gate = g_prob * jax.nn.softmax(top_val, axis=-1)
    expert = g_idx[:, None] * EXPERTS_PER_GROUP + top_idx.astype(jnp.int32)
    y = dropless_expert_ffn(xt, expert.reshape(-1), gate.reshape(-1), w_g, w_u, w_d)
    return y.reshape(B, T, D)


def trunk_layer(x, attend, conv_prev, w_in, w_conv, w_pa, w_pb, w_o, ln1_g, ln1_b,
                w_rg, w_re, w_eg, w_eu, w_ed, ln2_g, ln2_b):
    alpha = (2.0 * DEPTH) ** 0.25
    q, k, v, bg, cg, hg, ga, gb = split_projection(x, w_in)
    ya, kv_rows = attend(q, k, v)
    conv_out, conv_state = short_conv(cg * hg, conv_prev, w_conv)
    yb = bg * conv_out
    mix = (jax.nn.sigmoid(ga) * (ya @ w_pa) + jax.nn.sigmoid(gb) * (yb @ w_pb)) @ w_o
    x1 = layer_norm(alpha * x + mix, ln1_g, ln1_b)
    y = layer_norm(alpha * x1 + hier_moe(x1, w_rg, w_re, w_eg, w_eu, w_ed), ln2_g, ln2_b)
    return y, kv_rows, conv_state


def setup_inputs(seed: int = 0) -> dict:
    key = jax.random.key(seed)
    ks = jax.random.split(key, 24)
    beta = (8.0 * DEPTH) ** -0.25

    def normal(k, shape, scale):
        return jax.random.normal(k, shape, jnp.float32) * scale

    col_scale = jnp.concatenate([
        jnp.ones((2 * ATTN_WIDTH,), jnp.float32),
        jnp.full((ATTN_WIDTH,), beta, jnp.float32),
        jnp.ones((2 * CONV_CHANNELS,), jnp.float32),
        jnp.full((CONV_CHANNELS,), beta, jnp.float32),
        jnp.ones((2 * D_MODEL,), jnp.float32)])
    inputs = {
        "x_prompt": normal(ks[0], (BATCH, SEQ, D_MODEL), 1.0),
        "x_sample": normal(ks[1], (DEC_BATCH, DEC_SEQ, D_MODEL), 1.0),
        "cache_attn_w128": normal(ks[2], (DEPTH, DEC_BATCH, min(WINDOWS[0], PAST_LEN), 2, HEADS_PER_GROUP, HEAD_DIM), 1.0),
        "cache_attn_w512": normal(ks[3], (DEPTH, DEC_BATCH, min(WINDOWS[1], PAST_LEN), 2, HEADS_PER_GROUP, HEAD_DIM), 1.0),
        "cache_attn_w2048": normal(ks[4], (DEPTH, DEC_BATCH, min(WINDOWS[2], PAST_LEN), 2, HEADS_PER_GROUP, HEAD_DIM), 1.0),
        "state_conv": normal(ks[5], (DEPTH, DEC_BATCH, CONV_WIDTH - 1, CONV_CHANNELS), 0.5),
        "rel_bias": normal(ks[6], (N_BUCKETS, N_GROUPS_A * HEADS_PER_GROUP), 0.5),
        "w_in": normal(ks[7], (DEPTH, D_MODEL, PROJ_WIDTH), D_MODEL ** -0.5) * col_scale,
        "w_conv": normal(ks[8], (DEPTH, CONV_WIDTH, CONV_CHANNELS), CONV_WIDTH ** -0.5),
        "w_pa": normal(ks[9], (DEPTH, A_OUT, D_MODEL), beta * A_OUT ** -0.5),
        "w_pb": normal(ks[10], (DEPTH, CONV_CHANNELS, D_MODEL), beta * CONV_CHANNELS ** -0.5),
        "w_o": normal(ks[11], (DEPTH, D_MODEL, D_MODEL), beta * D_MODEL ** -0.5),
        "ln1_g": 1.0 + normal(ks[12], (DEPTH, D_MODEL), 0.02),
        "ln1_b": normal(ks[13], (DEPTH, D_MODEL), 0.02),
        "w_router_group": normal(ks[14], (DEPTH, D_MODEL, N_EXPERT_GROUPS), D_MODEL ** -0.5),
        "w_router_expert": normal(ks[15], (DEPTH, D_MODEL, N_EXPERTS), D_MODEL ** -0.5),
        "w_expert_gate": normal(ks[16], (DEPTH, N_EXPERTS, D_MODEL, D_EXPERT), D_MODEL ** -0.5),
        "w_expert_up": normal(ks[17], (DEPTH, N_EXPERTS, D_MODEL, D_EXPERT), beta * D_MODEL ** -0.5),
        "w_expert_down": normal(ks[18], (DEPTH, N_EXPERTS, D_EXPERT, D_MODEL), beta * D_EXPERT ** -0.5),
        "ln2_g": 1.0 + normal(ks[19], (DEPTH, D_MODEL), 0.02),
        "ln2_b": normal(ks[20], (DEPTH, D_MODEL), 0.02),
    }
    return inputs


def reference(x_prompt, x_sample, cache_attn_w128, cache_attn_w512, cache_attn_w2048, state_conv,
              rel_bias, w_in, w_conv, w_pa, w_pb, w_o, ln1_g, ln1_b, w_router_group, w_router_expert,
              w_expert_gate, w_expert_up, w_expert_down, ln2_g, ln2_b):
    bias = rel_bias_per_pattern(rel_bias)
    caches = (cache_attn_w128, cache_attn_w512, cache_attn_w2048)
    hp, hs = x_prompt, x_sample
    kvp, kvs, convp, convs = [], [], [], []
    for l in range(DEPTH):
        wl = (w_in[l], w_conv[l], w_pa[l], w_pb[l], w_o[l], ln1_g[l], ln1_b[l],
              w_router_group[l], w_router_expert[l], w_expert_gate[l], w_expert_up[l],
              w_expert_down[l], ln2_g[l], ln2_b[l])
        zero_conv = jnp.zeros((hp.shape[0], CONV_WIDTH - 1, CONV_CHANNELS), hp.dtype)
        hp, kv_p, c_p = trunk_layer(hp, lambda q, k, v: dilated_attn_prompt(q, k, v, bias), zero_conv, *wl)
        hs, kv_s, c_s = trunk_layer(
            hs, lambda q, k, v, l=l: dilated_attn_sample(q, k, v, [c[l] for c in caches], bias),
            state_conv[l], *wl)
        kvp.append(kv_p)
        kvs.append(kv_s)
        convp.append(c_p)
        convs.append(c_s)
    new_kv128_prompt = jnp.stack([s[0] for s in kvp])
    new_kv512_prompt = jnp.stack([s[1] for s in kvp])
    new_kv2048_prompt = jnp.stack([s[2] for s in kvp])
    new_conv_prompt = jnp.stack(convp)
    new_kv128_sample = jnp.stack([s[0] for s in kvs])
    new_kv512_sample = jnp.stack([s[1] for s in kvs])
    new_kv2048_sample = jnp.stack([s[2] for s in kvs])
    new_conv_sample = jnp.stack(convs)
    return (hp, hs, new_kv128_prompt, new_kv512_prompt, new_kv2048_prompt, new_conv_prompt,
            new_kv128_sample, new_kv512_sample, new_kv2048_sample, new_conv_sample)
```

```python
import functools
import math

import numpy as np
import jax
import jax.numpy as jnp
from jax import lax
from jax.experimental import pallas as pl
from jax.experimental.pallas import tpu as pltpu

F32 = jnp.float32
BF16 = jnp.bfloat16
I32 = jnp.int32

D_MODEL = 1024
N_GROUPS = 3
HEADS = 8
HEAD_DIM = 64
GROUP_WIDTH = HEADS * HEAD_DIM
ATTN_WIDTH = N_GROUPS * GROUP_WIDTH
DILATIONS = (1, 4, 16)
WINDOW_KEYS = 128
N_KEYS = WINDOW_KEYS + 1
N_BUCKETS = 32
MAX_EXACT = 16
MAX_DISTANCE = 2048
CONV_CHANNELS = 512
N_EXPERT_GROUPS = 4
EXPERTS_PER_GROUP = 8
N_EXPERTS = 32
TOP_K = 2
D_EXPERT = 512
LN_EPS = 1e-5
PROJ_WIDTH = 3 * ATTN_WIDTH + 3 * CONV_CHANNELS + 2 * D_MODEL
ROUTER_ROWS = 8 + N_EXPERTS
Q_BLOCK = 128
NEG_INF = float("-inf")
VMEM_LIMIT = 56 * 1024 * 1024


def _sigmoid(x):
    return 1.0 / (1.0 + jnp.exp(-x))


def _params(limit=VMEM_LIMIT):
    return pltpu.CompilerParams(vmem_limit_bytes=limit)


def _proj_kernel(*refs, tm, tail_rows, sample_mode):
    if sample_mode:
        (x_ref, w_ref, wc_ref, s0_ref, s1_ref,
         q_ref, kb_ref, vb_ref, k32_ref, v32_ref, yb_ref, sga_ref, sgb_ref, ut_ref) = refs
    else:
        (x_ref, w_ref, wc_ref,
         q_ref, kb_ref, vb_ref, k32_ref, v32_ref, yb_ref, sga_ref, sgb_ref, ut_ref, carry_ref) = refs
    xb = x_ref[...].astype(BF16)

    def col(c0, width):
        return jnp.dot(xb, w_ref[:, c0:c0 + width], preferred_element_type=F32)

    q_ref[...] = col(0, ATTN_WIDTH).astype(q_ref.dtype)
    k = col(ATTN_WIDTH, ATTN_WIDTH)
    kb_ref[...] = k.astype(BF16)
    k32_ref[...] = k
    v = col(2 * ATTN_WIDTH, ATTN_WIDTH)
    vb_ref[...] = v.astype(BF16)
    v32_ref[...] = v

    c0 = 3 * ATTN_WIDTH
    bg = col(c0, CONV_CHANNELS)
    u = col(c0 + CONV_CHANNELS, CONV_CHANNELS) * col(c0 + 2 * CONV_CHANNELS, CONV_CHANNELS)
    row = lax.broadcasted_iota(I32, (tm, CONV_CHANNELS), 0)
    r1 = pltpu.roll(u, 1, axis=0)
    r2 = pltpu.roll(u, 2, axis=0)
    if sample_mode:
        t = row & 3
        s0 = s0_ref[...]
        s1 = s1_ref[...]
        prev1 = jnp.where(t == 0, s1, r1)
        prev2 = jnp.where(t == 0, s0, jnp.where(t == 1, s1, r2))
    else:
        @pl.when(pl.program_id(0) == 0)
        def _():
            carry_ref[...] = jnp.zeros_like(carry_ref)
        c6 = carry_ref[6:7, :]
        c7 = carry_ref[7:8, :]
        prev1 = jnp.where(row == 0, c7, r1)
        prev2 = jnp.where(row == 0, c6, jnp.where(row == 1, c7, r2))
        carry_ref[...] = u[tm - 8:tm, :]
    conv = prev2 * wc_ref[0:1, :] + prev1 * wc_ref[1:2, :] + u * wc_ref[2:3, :]
    yb_ref[...] = (bg * conv).astype(BF16)
    ut_ref[...] = u[tm - tail_rows:tm, :]

    c1 = c0 + 3 * CONV_CHANNELS
    sga_ref[...] = _sigmoid(col(c1, D_MODEL)).astype(BF16)
    sgb_ref[...] = _sigmoid(col(c1 + D_MODEL, D_MODEL)).astype(BF16)


def _proj(x, w_in_bf, w_conv, conv_prev, *, tm, kv_tail, u_tail, q_dtype):
    n = x.shape[0]
    sample_mode = conv_prev is not None
    nt = n // tm
    tail_first = (n - kv_tail) // tm

    def row_spec(width):
        return pl.BlockSpec((tm, width), lambda i: (i, 0))

    def tail_spec(width):
        return pl.BlockSpec((tm, width), lambda i: (jnp.maximum(i - tail_first, 0), 0))

    in_specs = [
        row_spec(D_MODEL),
        pl.BlockSpec((D_MODEL, PROJ_WIDTH), lambda i: (0, 0), pipeline_mode=pl.Buffered(1)),
        pl.BlockSpec((3, CONV_CHANNELS), lambda i: (0, 0)),
    ]
    args = [x, w_in_bf, w_conv]
    scratch = []
    if sample_mode:
        in_specs += [row_spec(CONV_CHANNELS), row_spec(CONV_CHANNELS)]
        args += [conv_prev[0], conv_prev[1]]
    else:
        scratch = [pltpu.VMEM((8, CONV_CHANNELS), F32)]
    out_shape = [
        jax.ShapeDtypeStruct((n, ATTN_WIDTH), q_dtype),
        jax.ShapeDtypeStruct((n, ATTN_WIDTH), BF16),
        jax.ShapeDtypeStruct((n, ATTN_WIDTH), BF16),
        jax.ShapeDtypeStruct((kv_tail, ATTN_WIDTH), F32),
        jax.ShapeDtypeStruct((kv_tail, ATTN_WIDTH), F32),
        jax.ShapeDtypeStruct((n, CONV_CHANNELS), BF16),
        jax.ShapeDtypeStruct((n, D_MODEL), BF16),
        jax.ShapeDtypeStruct((n, D_MODEL), BF16),
        jax.ShapeDtypeStruct((u_tail, CONV_CHANNELS), F32),
    ]
    out_specs = [
        row_spec(ATTN_WIDTH), row_spec(ATTN_WIDTH), row_spec(ATTN_WIDTH),
        tail_spec(ATTN_WIDTH), tail_spec(ATTN_WIDTH),
        row_spec(CONV_CHANNELS), row_spec(D_MODEL), row_spec(D_MODEL),
        pl.BlockSpec((u_tail, CONV_CHANNELS), lambda i: (0, 0)),
    ]
    return pl.pallas_call(
        functools.partial(_proj_kernel, tm=tm, tail_rows=u_tail, sample_mode=sample_mode),
        grid=(nt,),
        in_specs=in_specs,
        out_specs=out_specs,
        out_shape=out_shape,
        scratch_shapes=scratch,
        compiler_params=_params(),
        name="proj",
    )(*args)


def _attn_kernel(q_ref, kp_ref, kc_ref, vp_ref, vc_ref, tb_ref, o_ref, lse_ref):
    lane = lax.broadcasted_iota(I32, (Q_BLOCK, 128), 1)
    first = lane < HEAD_DIM
    for pr in range(HEADS // 2):
        sl = slice(pr * 128, (pr + 1) * 128)
        q = q_ref[:, sl]
        k = jnp.concatenate([kp_ref[:, sl], kc_ref[:, sl]], axis=0)
        v = jnp.concatenate([vp_ref[:, sl], vc_ref[:, sl]], axis=0)
        qf = q.astype(F32)
        qq = jnp.concatenate([jnp.where(first, qf, 0.0), jnp.where(first, 0.0, qf)], axis=0).astype(BF16)
        s = lax.dot_general(qq, k, (((1,), (1,)), ((), ())), preferred_element_type=F32)
        s = s * (HEAD_DIM ** -0.5) + tb_ref[0, pr]
        m = jnp.max(s, axis=-1, keepdims=True)
        p = jnp.exp(s - m)
        l = jnp.sum(p, axis=-1, keepdims=True)
        o = jnp.dot(p.astype(BF16), v, preferred_element_type=F32) / l
        lse = m + jnp.log(l)
        o_ref[:, sl] = jnp.where(first, o[:Q_BLOCK], o[Q_BLOCK:]).astype(o_ref.dtype)
        lse_ref[:, sl] = jnp.where(first, jnp.broadcast_to(lse[:Q_BLOCK], (Q_BLOCK, 128)),
                                   jnp.broadcast_to(lse[Q_BLOCK:], (Q_BLOCK, 128)))


def _attn_prompt_group(q, kb, vb, tb, g):
    s = q.shape[0]
    d = DILATIONS[g]
    rows = s // d
    nb = rows // Q_BLOCK
    qv = q.reshape(rows, d * ATTN_WIDTH)
    kv = kb.reshape(rows, d * ATTN_WIDTH)
    vv = vb.reshape(rows, d * ATTN_WIDTH)

    cur = pl.BlockSpec((Q_BLOCK, GROUP_WIDTH), lambda c, i: (i, c * N_GROUPS + g))
    prev = pl.BlockSpec((Q_BLOCK, GROUP_WIDTH), lambda c, i: (jnp.maximum(i - 1, 0), c * N_GROUPS + g))
    out = pl.BlockSpec((Q_BLOCK, GROUP_WIDTH), lambda c, i: (i, c))
    o, lse = pl.pallas_call(
        _attn_kernel,
        grid=(d, nb),
        in_specs=[cur, prev, cur, prev, cur,
                  pl.BlockSpec((1, HEADS // 2, 2 * Q_BLOCK, 2 * Q_BLOCK), lambda c, i: (jnp.minimum(i, 1), 0, 0, 0))],
        out_specs=[out, out],
        out_shape=[jax.ShapeDtypeStruct((rows, d * GROUP_WIDTH), BF16),
                   jax.ShapeDtypeStruct((rows, d * GROUP_WIDTH), F32)],
        compiler_params=_params(),
        name=f"attn_prompt_g{g}",
    )(qv, kv, kv, vv, vv, tb)
    return o.reshape(s, GROUP_WIDTH), lse.reshape(s, GROUP_WIDTH)


def _attn_sample_kernel(q_ref, kn_ref, vn_ref, c0_ref, c1_ref, c2_ref, bc_ref, bn_ref, o_ref, lse_ref):
    scale = HEAD_DIM ** -0.5
    for g in range(N_GROUPS):
        kn = kn_ref[:, g]
        vn = vn_ref[:, g]
        for t in range(4):
            if g == 0:
                kc = c0_ref[:, 0]
                vc = c0_ref[:, 1]
            elif g == 1:
                kc = c1_ref[:, t, 0]
                vc = c1_ref[:, t, 1]
            else:
                kc = c2_ref[:, t, 0]
                vc = c2_ref[:, t, 1]
            qt = q_ref[t, g][None]
            s_c = jnp.sum(kc * qt, axis=-1, keepdims=True) * scale + bc_ref[g, t]
            s_n = jnp.sum(kn * qt, axis=-1, keepdims=True) * scale + bn_ref[g, t]
            m = jnp.maximum(jnp.max(s_c, axis=0), jnp.max(s_n, axis=0))
            p_c = jnp.exp(s_c - m[None])
            p_n = jnp.exp(s_n - m[None])
            l = jnp.sum(p_c, axis=0) + jnp.sum(p_n, axis=0)
            acc = jnp.sum(p_c * vc, axis=0) + jnp.sum(p_n * vn, axis=0)
            o_ref[g, t] = acc / l
            lse_ref[g, t] = jnp.broadcast_to(m + jnp.log(l), (HEADS, HEAD_DIM))


def _attn_sample(q5, kn5, vn5, caches, bc, bn):
    b = q5.shape[0]
    c0 = caches[0]
    c1 = caches[1].reshape(b, WINDOW_KEYS, DILATIONS[1], 2, HEADS, HEAD_DIM)
    c2 = caches[2].reshape(b, WINDOW_KEYS, DILATIONS[2], 2, HEADS, HEAD_DIM)
    tok = pl.BlockSpec((None, 4, N_GROUPS, HEADS, HEAD_DIM), lambda i: (i, 0, 0, 0, 0))
    out = pl.BlockSpec((None, N_GROUPS, 4, HEADS, HEAD_DIM), lambda i: (i, 0, 0, 0, 0))
    cls = pl.BlockSpec((None, WINDOW_KEYS, 4, 2, HEADS, HEAD_DIM), lambda i: (i, 0, 0, 0, 0, 0))
    return pl.pallas_call(
        _attn_sample_kernel,
        grid=(b,),
        in_specs=[tok, tok, tok,
                  pl.BlockSpec((None, WINDOW_KEYS, 2, HEADS, HEAD_DIM), lambda i: (i, 0, 0, 0, 0)),
                  cls, cls,
                  pl.BlockSpec(bc.shape, lambda i: (0, 0, 0, 0, 0)),
                  pl.BlockSpec(bn.shape, lambda i: (0, 0, 0, 0, 0))],
        out_specs=[out, out],
        out_shape=[jax.ShapeDtypeStruct((b, N_GROUPS, 4, HEADS, HEAD_DIM), F32)] * 2,
        compiler_params=_params(),
        name="attn_sample",
    )(q5, kn5, vn5, c0, c1, c2, bc, bn)


def _mix_kernel(x_ref, o0_ref, o1_ref, o2_ref, l0_ref, l1_ref, l2_ref, yb_ref, sga_ref, sgb_ref,
                wpa_ref, wpb_ref, wo_ref, g_ref, b_ref, wrh_ref, wrl_ref,
                x1_ref, ei_ref, gt_ref, *, tm, alpha):
    l0 = l0_ref[...]
    l1 = l1_ref[...]
    l2 = l2_ref[...]
    mx = jnp.maximum(jnp.maximum(l0, l1), l2)
    e0 = jnp.exp(l0 - mx)
    e1 = jnp.exp(l1 - mx)
    e2 = jnp.exp(l2 - mx)
    ya = (e0 * o0_ref[...].astype(F32) + e1 * o1_ref[...].astype(F32) + e2 * o2_ref[...].astype(F32)) / (e0 + e1 + e2)
    pa = jnp.dot(ya.astype(BF16), wpa_ref[...], preferred_element_type=F32)
    pb = jnp.dot(yb_ref[...], wpb_ref[...], preferred_element_type=F32)
    gated = sga_ref[...].astype(F32) * pa + sgb_ref[...].astype(F32) * pb
    mix = jnp.dot(gated.astype(BF16), wo_ref[...], preferred_element_type=F32)
    z = alpha * x_ref[...] + mix
    mu = jnp.mean(z, axis=-1, keepdims=True)
    zc = z - mu
    var = jnp.mean(zc * zc, axis=-1, keepdims=True)
    x1 = zc * lax.rsqrt(var + LN_EPS) * g_ref[...] + b_ref[...]
    x1_ref[...] = x1

    xh = x1.astype(BF16)
    xl = (x1 - xh.astype(F32)).astype(BF16)
    nt = (((1,), (1,)), ((), ()))
    wrh = wrh_ref[...]
    lt = (lax.dot_general(wrh, xh, nt, preferred_element_type=F32)
          + lax.dot_general(wrh, xl, nt, preferred_element_type=F32)
          + lax.dot_general(wrl_ref[...], xh, nt, preferred_element_type=F32))

    gl = lt[0:N_EXPERT_GROUPS]
    gmax = jnp.max(gl, axis=0, keepdims=True)
    idx4 = lax.broadcasted_iota(I32, (N_EXPERT_GROUPS, tm), 0)
    g_idx = jnp.min(jnp.where(gl == gmax, idx4, N_EXPERT_GROUPS), axis=0, keepdims=True)
    g_prob = 1.0 / jnp.sum(jnp.exp(gl - gmax), axis=0, keepdims=True)
    e_sel = lt[8:16]
    for grp in range(1, N_EXPERT_GROUPS):
        e_sel = jnp.where(g_idx == grp, lt[8 + 8 * grp:16 + 8 * grp], e_sel)
    idx8 = lax.broadcasted_iota(I32, (EXPERTS_PER_GROUP, tm), 0)
    v1 = jnp.max(e_sel, axis=0, keepdims=True)
    i1 = jnp.min(jnp.where(e_sel == v1, idx8, EXPERTS_PER_GROUP), axis=0, keepdims=True)
    rest = jnp.where(idx8 == i1, NEG_INF, e_sel)
    v2 = jnp.max(rest, axis=0, keepdims=True)
    i2 = jnp.min(jnp.where(rest == v2, idx8, EXPERTS_PER_GROUP), axis=0, keepdims=True)
    r = jnp.exp(v2 - v1)
    gate1 = g_prob / (1.0 + r)
    gate2 = g_prob * r / (1.0 + r)
    ex1 = g_idx * EXPERTS_PER_GROUP + i1
    ex2 = g_idx * EXPERTS_PER_GROUP + i2
    ei_ref[...] = jnp.where(idx8 == 0, ex1, jnp.where(idx8 == 1, ex2, 0))
    gt_ref[...] = jnp.where(idx8 == 0, gate1, jnp.where(idx8 == 1, gate2, 0.0))


def _mix(x, o, lse, yb, sga, sgb, w_pa, w_pb, w_o, ln_g, ln_b, wr_hi, wr_lo, *, tm, alpha):
    n = x.shape[0]

    def row_spec(width):
        return pl.BlockSpec((tm, width), lambda i: (i, 0))

    def full(a):
        return pl.BlockSpec(a.shape, lambda i: (0,) * a.ndim)

    lane_spec = pl.BlockSpec((8, tm), lambda i: (0, i))
    return pl.pallas_call(
        functools.partial(_mix_kernel, tm=tm, alpha=alpha),
        grid=(n // tm,),
        in_specs=[row_spec(D_MODEL)] + [row_spec(GROUP_WIDTH)] * 7 + [row_spec(D_MODEL)] * 2
                 + [full(w_pa), full(w_pb), full(w_o), full(ln_g), full(ln_b), full(wr_hi), full(wr_lo)],
        out_specs=[row_spec(D_MODEL), lane_spec, lane_spec],
        out_shape=[jax.ShapeDtypeStruct((n, D_MODEL), F32),
                   jax.ShapeDtypeStruct((8, n), I32),
                   jax.ShapeDtypeStruct((8, n), F32)],
        compiler_params=_params(),
        name="mix",
    )(x, o[0], o[1], o[2], lse[0], lse[1], lse[2], yb, sga, sgb, w_pa, w_pb, w_o, ln_g, ln_b, wr_hi, wr_lo)


def _rank_kernel(ei_ref, rank_ref, cnt_ref, carry_ref, *, tl):
    @pl.when(pl.program_id(0) == 0)
    def _():
        carry_ref[...] = jnp.zeros_like(carry_ref)
    ex = lax.broadcasted_iota(I32, (N_EXPERTS, tl), 0)
    oh0 = (ex == ei_ref[0:1, :]).astype(F32)
    oh1 = (ex == ei_ref[1:2, :]).astype(F32)
    a = lax.broadcasted_iota(I32, (tl, tl), 0)
    b = lax.broadcasted_iota(I32, (tl, tl), 1)
    upper = (a < b).astype(BF16)
    pre0 = jnp.dot(oh0.astype(BF16), upper, preferred_element_type=F32)
    pre1 = jnp.dot(oh1.astype(BF16), upper, preferred_element_type=F32)
    carry = carry_ref[:, 0:1]
    cnt0 = jnp.sum(oh0, axis=1, keepdims=True)
    cnt1 = jnp.sum(oh1, axis=1, keepdims=True)
    rank0 = jnp.sum(oh0 * (pre0 + carry), axis=0, keepdims=True)
    rank1 = jnp.sum(oh1 * (pre1 + cnt0 + carry), axis=0, keepdims=True)
    row = lax.broadcasted_iota(I32, (8, tl), 0)
    rank_ref[...] = jnp.where(row == 0, rank0.astype(I32), jnp.where(row == 1, rank1.astype(I32), 0))
    total = carry_ref[...] + cnt0 + cnt1
    carry_ref[...] = total
    cnt_ref[...] = total.astype(I32)


def _rank(ei, *, tl):
    n = ei.shape[1]
    return pl.pallas_call(
        functools.partial(_rank_kernel, tl=tl),
        grid=(n // tl,),
        in_specs=[pl.BlockSpec((8, tl), lambda i: (0, i))],
        out_specs=[pl.BlockSpec((8, tl), lambda i: (0, i)),
                   pl.BlockSpec((N_EXPERTS, 128), lambda i: (0, 0))],
        out_shape=[jax.ShapeDtypeStruct((8, n), I32), jax.ShapeDtypeStruct((N_EXPERTS, 128), I32)],
        scratch_shapes=[pltpu.VMEM((N_EXPERTS, 128), F32)],
        name="moe_rank",
    )(ei)


def _row_copy(src, src_row, dst, dst_row, sem):
    return pltpu.make_async_copy(src.at[pl.ds(src_row, 1)], dst.at[pl.ds(dst_row, 1)], sem)


def _dispatch_kernel(e_ref, r_ref, ps_ref, x1_ref, buf_in_ref, buf_ref, sem, *, n, tm):
    del buf_in_ref
    base = pl.program_id(0) * tm

    def body(r, carry):
        tok = base + r
        for k in range(TOP_K):
            slot = ps_ref[e_ref[k * n + tok]] + r_ref[k * n + tok]
            _row_copy(x1_ref, tok, buf_ref, slot, sem).start()
        return carry

    lax.fori_loop(0, tm, body, 0)
    for _ in range(TOP_K):
        pltpu.make_async_copy(x1_ref.at[pl.ds(0, tm)], buf_ref.at[pl.ds(0, tm)], sem).wait()


def _dispatch(e_flat, r_flat, pstart, x1, buf, *, tm):
    n = x1.shape[0]
    return pl.pallas_call(
        functools.partial(_dispatch_kernel, n=n, tm=tm),
        grid_spec=pltpu.PrefetchScalarGridSpec(
            num_scalar_prefetch=3,
            grid=(n // tm,),
            in_specs=[pl.BlockSpec(memory_space=pl.ANY), pl.BlockSpec(memory_space=pl.ANY)],
            out_specs=pl.BlockSpec(memory_space=pl.ANY),
            scratch_shapes=[pltpu.SemaphoreType.DMA(())],
        ),
        out_shape=jax.ShapeDtypeStruct(buf.shape, buf.dtype),
        input_output_aliases={4: 0},
        name="moe_dispatch",
    )(e_flat, r_flat, pstart, x1, buf)


def _expert_kernel(be_ref, nu_ref, xb_ref, wg_ref, wu_ref, wd_ref, out_ref, wg_bf, wu_bf, wd_bf):
    j = pl.program_id(0)
    used = j < nu_ref[0]
    changed = jnp.logical_or(j == 0, be_ref[j] != be_ref[jnp.maximum(j - 1, 0)])

    @pl.when(jnp.logical_and(used, changed))
    def _():
        wg_bf[...] = wg_ref[...].astype(BF16)
        wu_bf[...] = wu_ref[...].astype(BF16)
        wd_bf[...] = wd_ref[...].astype(BF16)

    @pl.when(used)
    def _():
        xb = xb_ref[...].astype(BF16)
        a = jnp.dot(xb, wg_bf[...], preferred_element_type=F32)
        b = jnp.dot(xb, wu_bf[...], preferred_element_type=F32)
        h = (a * _sigmoid(a)) * b
        out_ref[...] = jnp.dot(h.astype(BF16), wd_bf[...], preferred_element_type=F32)

    @pl.when(jnp.logical_not(used))
    def _():
        out_ref[...] = jnp.zeros_like(out_ref)


def _experts(block_expert, n_used, buf, w_g, w_u, w_d, *, bm):
    nblk = buf.shape[0] // bm

    def row_map(j, be, nu):
        return (jnp.minimum(j, nu[0] - 1), 0)

    def w_map(j, be, nu):
        return (be[j], 0, 0)

    return pl.pallas_call(
        _expert_kernel,
        grid_spec=pltpu.PrefetchScalarGridSpec(
            num_scalar_prefetch=2,
            grid=(nblk,),
            in_specs=[pl.BlockSpec((bm, D_MODEL), row_map),
                      pl.BlockSpec((None, D_MODEL, D_EXPERT), w_map),
                      pl.BlockSpec((None, D_MODEL, D_EXPERT), w_map),
                      pl.BlockSpec((None, D_EXPERT, D_MODEL), w_map)],
            out_specs=pl.BlockSpec((bm, D_MODEL), lambda j, be, nu: (j, 0)),
            scratch_shapes=[pltpu.VMEM((D_MODEL, D_EXPERT), BF16),
                            pltpu.VMEM((D_MODEL, D_EXPERT), BF16),
                            pltpu.VMEM((D_EXPERT, D_MODEL), BF16)],
        ),
        out_shape=jax.ShapeDtypeStruct(buf.shape, F32),
        compiler_params=_params(),
        name="moe_experts",
    )(block_expert, n_used, buf, w_g, w_u, w_d)


def _combine_kernel(e_ref, r_ref, ps_ref, x1_ref, gc_ref, g_ref, b_ref, eo_ref, y_ref, rows, sem, *, n, tm, alpha):
    base = pl.program_id(0) * tm

    def body(r, carry):
        tok = base + r
        for k in range(TOP_K):
            slot = ps_ref[e_ref[k * n + tok]] + r_ref[k * n + tok]
            _row_copy(eo_ref, slot, rows.at[k], r, sem).start()
        return carry

    lax.fori_loop(0, tm, body, 0)
    for k in range(TOP_K):
        pltpu.make_async_copy(eo_ref.at[pl.ds(0, tm)], rows.at[k], sem).wait()
    gc = gc_ref[...]
    z = alpha * x1_ref[...] + gc[:, 0:1] * rows[0] + gc[:, 1:2] * rows[1]
    mu = jnp.mean(z, axis=-1, keepdims=True)
    zc = z - mu
    var = jnp.mean(zc * zc, axis=-1, keepdims=True)
    y_ref[...] = zc * lax.rsqrt(var + LN_EPS) * g_ref[...] + b_ref[...]


def _combine(e_flat, r_flat, pstart, x1, gate_cols, ln_g, ln_b, expert_out, *, tm, alpha):
    n = x1.shape[0]
    return pl.pallas_call(
        functools.partial(_combine_kernel, n=n, tm=tm, alpha=alpha),
        grid_spec=pltpu.PrefetchScalarGridSpec(
            num_scalar_prefetch=3,
            grid=(n // tm,),
            in_specs=[pl.BlockSpec((tm, D_MODEL), lambda i, e, r, p: (i, 0)),
                      pl.BlockSpec((tm, TOP_K), lambda i, e, r, p: (i, 0)),
                      pl.BlockSpec((1, D_MODEL), lambda i, e, r, p: (0, 0)),
                      pl.BlockSpec((1, D_MODEL), lambda i, e, r, p: (0, 0)),
                      pl.BlockSpec(memory_space=pl.ANY)],
            out_specs=pl.BlockSpec((tm, D_MODEL), lambda i, e, r, p: (i, 0)),
            scratch_shapes=[pltpu.VMEM((TOP_K, tm, D_MODEL), F32), pltpu.SemaphoreType.DMA(())],
        ),
        out_shape=jax.ShapeDtypeStruct((n, D_MODEL), F32),
        compiler_params=_params(),
        name="moe_combine",
    )(e_flat, r_flat, pstart, x1, gate_cols, ln_g, ln_b, expert_out)


def _hier_moe_ln(x1, ei, gt, w_g, w_u, w_d, ln_g, ln_b, *, tl, tm, bm, alpha):
    n = x1.shape[0]
    m = n * TOP_K
    rank, cnt = _rank(ei, tl=tl)
    counts = cnt[:, 0]
    padded = (counts + bm - 1) // bm * bm
    pend = jnp.cumsum(padded)
    pstart = (pend - padded).astype(I32)
    nblk = (m + N_EXPERTS * (bm - 1) + bm - 1) // bm
    blk_start = jnp.arange(nblk, dtype=I32) * bm
    n_used = (pend[-1] // bm).astype(I32)
    be = jnp.minimum(jnp.sum(pend[None, :] <= blk_start[:, None], axis=1), N_EXPERTS - 1).astype(I32)
    be = jnp.where(jnp.arange(nblk) < n_used, be, jnp.take(be, n_used - 1))
    e_flat = ei[0:TOP_K].reshape(m)
    r_flat = rank[0:TOP_K].reshape(m)
    buf = _dispatch(e_flat, r_flat, pstart, x1, jnp.zeros((nblk * bm, D_MODEL), F32), tm=tm)
    eo = _experts(be, n_used.reshape(1), buf, w_g, w_u, w_d, bm=bm)
    gate_cols = gt[0:TOP_K].T
    return _combine(e_flat, r_flat, pstart, x1, gate_cols, ln_g, ln_b, eo, tm=tm, alpha=alpha)


def _cache_shift_kernel(*refs, n_chunks):
    caches, news, outs, sem = refs[0:3], refs[3:6], refs[6:9], refs[9]
    copies = []
    for c_ref, n_ref, o_ref in zip(caches, news, outs):
        nb, length = c_ref.shape[0], c_ref.shape[1]
        t = n_ref.shape[1]
        step = nb // n_chunks
        for ch in range(n_chunks):
            bs = pl.ds(ch * step, step)
            copies.append((c_ref.at[bs, pl.ds(t, length - t)], o_ref.at[bs, pl.ds(0, length - t)]))
        copies.append((n_ref, o_ref.at[:, pl.ds(length - t, t)]))
    descs = [pltpu.make_async_copy(s, d, sem.at[i]) for i, (s, d) in enumerate(copies)]
    for dsc in descs:
        dsc.start()
    for dsc in descs:
        dsc.wait()


def _cache_shift(caches, news, *, n_chunks=4):
    any_spec = pl.BlockSpec(memory_space=pl.ANY)
    return pl.pallas_call(
        functools.partial(_cache_shift_kernel, n_chunks=n_chunks),
        in_specs=[any_spec] * 6,
        out_specs=[any_spec] * 3,
        out_shape=[jax.ShapeDtypeStruct(c.shape, c.dtype) for c in caches],
        scratch_shapes=[pltpu.SemaphoreType.DMA((3 * (n_chunks + 1),))],
        name="cache_shift",
    )(*caches, *news)


def _t5_bucket(n):
    nf = jnp.maximum(n, 1).astype(F32)
    large = MAX_EXACT + (jnp.log(nf / MAX_EXACT) / math.log(MAX_DISTANCE / MAX_EXACT)
                         * (N_BUCKETS - MAX_EXACT)).astype(I32)
    large = jnp.minimum(large, N_BUCKETS - 1)
    return jnp.where(n < MAX_EXACT, n, large)


def _bias_per_group(rel_bias):
    offs = jnp.arange(N_KEYS, dtype=I32)[None, :] * jnp.array(DILATIONS, I32)[:, None]
    bucket = _t5_bucket(offs)
    table = rel_bias.reshape(N_BUCKETS, N_GROUPS, HEADS)
    b = table[bucket, jnp.arange(N_GROUPS)[:, None]]
    return jnp.transpose(b, (0, 2, 1)).astype(F32)


def _prompt_bias_tables(bias):
    qi = np.arange(Q_BLOCK)[:, None]
    kk = np.arange(2 * Q_BLOCK)[None, :]
    dist = qi + Q_BLOCK - kk
    valid = (dist >= 0) & (dist <= WINDOW_KEYS)
    first_valid = valid & (kk >= Q_BLOCK)
    dist_c = np.clip(dist, 0, WINDOW_KEYS)
    vals = bias[:, :, dist_c]
    later = jnp.where(valid[None, None], vals, NEG_INF)
    first = jnp.where(first_valid[None, None], vals, NEG_INF)
    tb = jnp.stack([first, later], axis=1)
    return tb.reshape(N_GROUPS, 2, HEADS // 2, 2 * Q_BLOCK, 2 * Q_BLOCK)


def _sample_bias_tables(bias, t_len):
    t = np.arange(t_len)[:, None]
    p = np.arange(WINDOW_KEYS)[None, :]
    bc, bn = [], []
    for g, d in enumerate(DILATIONS):
        if d == 1:
            j = WINDOW_KEYS + t - p
            ok = j <= WINDOW_KEYS
        else:
            j = np.broadcast_to(WINDOW_KEYS - p, (t_len, WINDOW_KEYS))
            ok = np.ones_like(j, dtype=bool)
        vals = bias[g][:, np.clip(j, 0, WINDOW_KEYS)]
        bc.append(jnp.where(ok[None], vals, NEG_INF))
        tn = np.arange(t_len)[None, :]
        jn = t - tn
        okn = (jn >= 0) if d == 1 else (jn == 0)
        valsn = bias[g][:, np.clip(jn, 0, WINDOW_KEYS)]
        bn.append(jnp.where(okn[None], valsn, NEG_INF))
    bc = jnp.transpose(jnp.stack(bc), (0, 2, 3, 1))[..., None]
    bn = jnp.transpose(jnp.stack(bn), (0, 2, 3, 1))[..., None]
    return bc, bn


def _split_bf16(w):
    hi = w.astype(BF16)
    lo = (w - hi.astype(F32)).astype(BF16)
    return hi, lo


def kernel(x_prompt, x_sample, cache_attn_w128, cache_attn_w512, cache_attn_w2048, state_conv, rel_bias, w_in, w_conv, w_pa, w_pb, w_o, ln1_g, ln1_b, w_router_group, w_router_expert, w_expert_gate, w_expert_up, w_expert_down, ln2_g, ln2_b):
    depth = w_in.shape[0]
    assert depth == 1 and x_prompt.shape[0] == 1
    alpha = (2.0 * depth) ** 0.25
    s = x_prompt.shape[1]
    bd, t_len = x_sample.shape[0], x_sample.shape[1]
    assert t_len == 4 and s % (DILATIONS[-1] * Q_BLOCK) == 0

    bias = _bias_per_group(rel_bias)
    tb = _prompt_bias_tables(bias)
    bc, bn = _sample_bias_tables(bias, t_len)

    w_in_bf = w_in[0].astype(BF16)
    w_pa_bf = w_pa[0].astype(BF16)
    w_pb_bf = w_pb[0].astype(BF16)
    w_o_bf = w_o[0].astype(BF16)
    wr = jnp.zeros((ROUTER_ROWS, D_MODEL), F32)
    wr = wr.at[0:N_EXPERT_GROUPS].set(w_router_group[0].T).at[8:8 + N_EXPERTS].set(w_router_expert[0].T)
    wr_hi, wr_lo = _split_bf16(wr)
    g1, b1 = ln1_g[0][None], ln1_b[0][None]
    g2, b2 = ln2_g[0][None], ln2_b[0][None]
    wg, wu, wd = w_expert_gate[0], w_expert_up[0], w_expert_down[0]

    xp = x_prompt[0]
    kv_tail = min(MAX_DISTANCE, s)
    q, kb, vb, k32, v32, yb, sga, sgb, ut = _proj(
        xp, w_in_bf, w_conv[0], None, tm=256, kv_tail=kv_tail, u_tail=8, q_dtype=BF16)
    o_l = [_attn_prompt_group(q, kb, vb, tb[g], g) for g in range(N_GROUPS)]
    x1, ei, gt = _mix(xp, [a[0] for a in o_l], [a[1] for a in o_l], yb, sga, sgb,
                      w_pa_bf, w_pb_bf, w_o_bf, g1, b1, wr_hi, wr_lo, tm=512, alpha=alpha)
    y_prompt = _hier_moe_ln(x1, ei, gt, wg, wu, wd, g2, b2, tl=512, tm=256, bm=256, alpha=alpha)[None]

    kv_prompt = []
    for g, d in enumerate(DILATIONS):
        length = min(WINDOW_KEYS * d, s)
        cols = slice(g * GROUP_WIDTH, (g + 1) * GROUP_WIDTH)
        kg = k32[kv_tail - length:, cols].reshape(length, HEADS, HEAD_DIM)
        vg = v32[kv_tail - length:, cols].reshape(length, HEADS, HEAD_DIM)
        kv_prompt.append(jnp.stack([kg, vg], axis=1)[None, None])
    conv_prompt = ut[6:8][None, None]

    ns = bd * t_len
    xs = x_sample.reshape(ns, D_MODEL)
    st = state_conv[0]
    s0 = jnp.repeat(st[:, 0], t_len, axis=0)
    s1 = jnp.repeat(st[:, 1], t_len, axis=0)
    qs, _, _, k32s, v32s, ybs, sgas, sgbs, us = _proj(
        xs, w_in_bf, w_conv[0], (s0, s1), tm=ns, kv_tail=ns, u_tail=ns, q_dtype=F32)
    shape5 = (bd, t_len, N_GROUPS, HEADS, HEAD_DIM)
    q5, kn5, vn5 = qs.reshape(shape5), k32s.reshape(shape5), v32s.reshape(shape5)
    caches = (cache_attn_w128[0], cache_attn_w512[0], cache_attn_w2048[0])
    o_s, lse_s = _attn_sample(q5, kn5, vn5, caches, bc, bn)
    o_s = jnp.transpose(o_s, (1, 0, 2, 3, 4)).reshape(N_GROUPS, ns, GROUP_WIDTH)
    lse_s = jnp.transpose(lse_s, (1, 0, 2, 3, 4)).reshape(N_GROUPS, ns, GROUP_WIDTH)
    x1s, eis, gts = _mix(xs, o_s, lse_s, ybs, sgas, sgbs, w_pa_bf, w_pb_bf, w_o_bf, g1, b1, wr_hi, wr_lo,
                         tm=ns, alpha=alpha)
    y_sample = _hier_moe_ln(x1s, eis, gts, wg, wu, wd, g2, b2, tl=ns, tm=ns, bm=128, alpha=alpha)
    y_sample = y_sample.reshape(bd, t_len, D_MODEL)

    news = [jnp.stack([kn5[:, :, g], vn5[:, :, g]], axis=2) for g in range(N_GROUPS)]
    kv_sample = [c[None] for c in _cache_shift(caches, news)]
    conv_sample = us.reshape(bd, t_len, CONV_CHANNELS)[:, t_len - 2:][None]

    return (y_prompt, y_sample, kv_prompt[0], kv_prompt[1], kv_prompt[2], conv_prompt,
            kv_sample[0], kv_sample[1], kv_sample[2], conv_sample)
```

```python
import functools
import math

import numpy as np
import jax
import jax.numpy as jnp
from jax import lax
from jax.experimental import pallas as pl
from jax.experimental.pallas import tpu as pltpu

F32 = jnp.float32
BF16 = jnp.bfloat16
I32 = jnp.int32

D_MODEL = 1024
N_GROUPS = 3
HEADS = 8
HEAD_DIM = 64
GROUP_WIDTH = HEADS * HEAD_DIM
ATTN_WIDTH = N_GROUPS * GROUP_WIDTH
DILATIONS = (1, 4, 16)
NO_DILATION = (1, 1, 1)
WINDOW_KEYS = 128
N_KEYS = WINDOW_KEYS + 1
N_BUCKETS = 32
MAX_EXACT = 16
MAX_DISTANCE = 2048
CONV_CHANNELS = 512
N_EXPERT_GROUPS = 4
EXPERTS_PER_GROUP = 8
N_EXPERTS = 32
TOP_K = 2
D_EXPERT = 512
LN_EPS = 1e-5
PROJ_WIDTH = 3 * ATTN_WIDTH + 3 * CONV_CHANNELS + 2 * D_MODEL
ROUTER_ROWS = 8 + N_EXPERTS
Q_BLOCK = 128
T_NEW = 4
NEG_INF = float("-inf")
VMEM_LIMIT = 56 * 1024 * 1024


def _sigmoid(x):
    return 1.0 / (1.0 + jnp.exp(-x))


def _params(limit=VMEM_LIMIT):
    return pltpu.CompilerParams(vmem_limit_bytes=limit)


def _proj_kernel(*refs, tm, tail_rows, sample_mode, dils):
    n_in = 5 if sample_mode else 3
    x_ref, w_ref, wc_ref = refs[0:3]
    outs = refs[n_in:]
    q_refs, k_refs, v_refs = outs[0:3], outs[3:6], outs[6:9]
    k32_ref, v32_ref, yb_ref, sga_ref, sgb_ref, ut_ref, cls_ref = outs[9:16]
    xb = x_ref[...].astype(BF16)

    def col(c0, width):
        return jnp.dot(xb, w_ref[:, c0:c0 + width], preferred_element_type=F32)

    def write_classes(val, group_refs):
        for g, d in enumerate(dils):
            part = val[:, g * GROUP_WIDTH:(g + 1) * GROUP_WIDTH]
            ref = group_refs[g]
            if d == 1:
                ref[0] = part.astype(ref.dtype)
            else:
                for kk in range(GROUP_WIDTH // 128):
                    lanes = slice(kk * 128, (kk + 1) * 128)
                    cls_ref[kk] = part[:, lanes]
                    for c in range(d):
                        ref[c, :, lanes] = cls_ref[kk, pl.ds(c, tm // d, stride=d), :].astype(ref.dtype)

    write_classes(col(0, ATTN_WIDTH), q_refs)
    k = col(ATTN_WIDTH, ATTN_WIDTH)
    k32_ref[...] = k
    write_classes(k, k_refs)
    v = col(2 * ATTN_WIDTH, ATTN_WIDTH)
    v32_ref[...] = v
    write_classes(v, v_refs)

    c0 = 3 * ATTN_WIDTH
    bg = col(c0, CONV_CHANNELS)
    u = col(c0 + CONV_CHANNELS, CONV_CHANNELS) * col(c0 + 2 * CONV_CHANNELS, CONV_CHANNELS)
    row = lax.broadcasted_iota(I32, (tm, CONV_CHANNELS), 0)
    r1 = pltpu.roll(u, 1, axis=0)
    r2 = pltpu.roll(u, 2, axis=0)
    if sample_mode:
        s0 = refs[3][...]
        s1 = refs[4][...]
        t = row & (T_NEW - 1)
        prev1 = jnp.where(t == 0, s1, r1)
        prev2 = jnp.where(t == 0, s0, jnp.where(t == 1, s1, r2))
    else:
        carry_ref = outs[16]

        @pl.when(pl.program_id(0) == 0)
        def _():
            carry_ref[...] = jnp.zeros_like(carry_ref)
        c6 = carry_ref[6:7, :]
        c7 = carry_ref[7:8, :]
        prev1 = jnp.where(row == 0, c7, r1)
        prev2 = jnp.where(row == 0, c6, jnp.where(row == 1, c7, r2))
        carry_ref[...] = u[tm - 8:tm, :]
    conv = prev2 * wc_ref[0:1, :] + prev1 * wc_ref[1:2, :] + u * wc_ref[2:3, :]
    yb_ref[...] = (bg * conv).astype(BF16)
    ut_ref[...] = u[tm - tail_rows:tm, :]

    c1 = c0 + 3 * CONV_CHANNELS
    sga_ref[...] = _sigmoid(col(c1, D_MODEL)).astype(BF16)
    sgb_ref[...] = _sigmoid(col(c1 + D_MODEL, D_MODEL)).astype(BF16)


def _proj(x, w_in_bf, w_conv, conv_prev, *, tm, kv_tail, u_tail, q_dtype, dils):
    n = x.shape[0]
    sample_mode = conv_prev is not None
    nt = n // tm
    tail_first = (n - kv_tail) // tm

    def row_spec(width):
        return pl.BlockSpec((tm, width), lambda i: (i, 0))

    def tail_spec(width):
        return pl.BlockSpec((tm, width), lambda i: (jnp.maximum(i - tail_first, 0), 0))

    def class_spec(d):
        return pl.BlockSpec((d, tm // d, GROUP_WIDTH), lambda i: (0, i, 0))

    def class_shape(d, dtype):
        return jax.ShapeDtypeStruct((d, n // d, GROUP_WIDTH), dtype)

    in_specs = [
        row_spec(D_MODEL),
        pl.BlockSpec((D_MODEL, PROJ_WIDTH), lambda i: (0, 0), pipeline_mode=pl.Buffered(1)),
        pl.BlockSpec((3, CONV_CHANNELS), lambda i: (0, 0)),
    ]
    args = [x, w_in_bf, w_conv]
    scratch = [pltpu.VMEM((GROUP_WIDTH // 128, tm, 128), F32)]
    if sample_mode:
        in_specs += [row_spec(CONV_CHANNELS), row_spec(CONV_CHANNELS)]
        args += [conv_prev[0], conv_prev[1]]
    else:
        scratch.append(pltpu.VMEM((8, CONV_CHANNELS), F32))
    out_shape = (
        [class_shape(d, q_dtype) for d in dils] + [class_shape(d, BF16) for d in dils] * 2
        + [jax.ShapeDtypeStruct((kv_tail, ATTN_WIDTH), F32),
           jax.ShapeDtypeStruct((kv_tail, ATTN_WIDTH), F32),
           jax.ShapeDtypeStruct((n, CONV_CHANNELS), BF16),
           jax.ShapeDtypeStruct((n, D_MODEL), BF16),
           jax.ShapeDtypeStruct((n, D_MODEL), BF16),
           jax.ShapeDtypeStruct((u_tail, CONV_CHANNELS), F32)])
    out_specs = (
        [class_spec(d) for d in dils] * 3
        + [tail_spec(ATTN_WIDTH), tail_spec(ATTN_WIDTH),
           row_spec(CONV_CHANNELS), row_spec(D_MODEL), row_spec(D_MODEL),
           pl.BlockSpec((u_tail, CONV_CHANNELS), lambda i: (0, 0))])
    res = pl.pallas_call(
        functools.partial(_proj_kernel, tm=tm, tail_rows=u_tail, sample_mode=sample_mode, dils=dils),
        grid=(nt,),
        in_specs=in_specs,
        out_specs=out_specs,
        out_shape=out_shape,
        scratch_shapes=scratch,
        compiler_params=_params(),
        name="proj",
    )(*args)
    return res[0:3], res[3:6], res[6:9], res[9:]


def _attn_kernel(q_ref, kp_ref, kc_ref, vp_ref, vc_ref, tb_ref, o_ref, lse_ref):
    lane = lax.broadcasted_iota(I32, (Q_BLOCK, 128), 1)
    first = lane < HEAD_DIM
    for pr in range(HEADS // 2):
        sl = slice(pr * 128, (pr + 1) * 128)
        q = q_ref[:, sl]
        k = jnp.concatenate([kp_ref[:, sl], kc_ref[:, sl]], axis=0)
        v = jnp.concatenate([vp_ref[:, sl], vc_ref[:, sl]], axis=0)
        qf = q.astype(F32)
        qq = jnp.concatenate([jnp.where(first, qf, 0.0), jnp.where(first, 0.0, qf)], axis=0).astype(BF16)
        s = lax.dot_general(qq, k, (((1,), (1,)), ((), ())), preferred_element_type=F32)
        s = s * (HEAD_DIM ** -0.5) + tb_ref[0, pr]
        m = jnp.max(s, axis=-1, keepdims=True)
        p = jnp.exp(s - m)
        l = jnp.sum(p, axis=-1, keepdims=True)
        o = jnp.dot(p.astype(BF16), v, preferred_element_type=F32) / l
        lse = m + jnp.log(l)
        o_ref[:, sl] = jnp.where(first, o[:Q_BLOCK], o[Q_BLOCK:]).astype(o_ref.dtype)
        lse_ref[:, sl] = jnp.where(first, jnp.broadcast_to(lse[:Q_BLOCK], (Q_BLOCK, 128)),
                                   jnp.broadcast_to(lse[Q_BLOCK:], (Q_BLOCK, 128)))


def _attn_prompt_group(q, kb, vb, tb, g):
    d, rows = q.shape[0], q.shape[1]
    cur = pl.BlockSpec((None, Q_BLOCK, GROUP_WIDTH), lambda c, i: (c, i, 0))
    prev = pl.BlockSpec((None, Q_BLOCK, GROUP_WIDTH), lambda c, i: (c, jnp.maximum(i - 1, 0), 0))
    return pl.pallas_call(
        _attn_kernel,
        grid=(d, rows // Q_BLOCK),
        in_specs=[cur, prev, cur, prev, cur,
                  pl.BlockSpec((1, HEADS // 2, 2 * Q_BLOCK, 2 * Q_BLOCK), lambda c, i: (jnp.minimum(i, 1), 0, 0, 0))],
        out_specs=[cur, cur],
        out_shape=[jax.ShapeDtypeStruct((d, rows, GROUP_WIDTH), BF16),
                   jax.ShapeDtypeStruct((d, rows, GROUP_WIDTH), F32)],
        compiler_params=_params(),
        name=f"attn_prompt_g{g}",
    )(q, kb, kb, vb, vb, tb)


PACK_Q, PACK_K, PACK_V = 0, N_GROUPS * T_NEW, 2 * N_GROUPS * T_NEW


def _sample_cache_kernel(qkv_ref, c0_ref, c1_ref, c2_ref, b0_ref, b1_ref, b2_ref, bn_ref,
                         n0_ref, n1_ref, n2_ref, o_ref, lse_ref):
    scale = HEAD_DIM ** -0.5
    lane = lax.broadcasted_iota(I32, (128, 128), 1)
    o_cols = jnp.zeros((128, 128), F32)
    lse_cols = jnp.zeros((128, 128), F32)
    groups = ((c0_ref, b0_ref, n0_ref), (c1_ref, b1_ref, n1_ref), (c2_ref, b2_ref, n2_ref))
    for g, (c_ref, b_ref, n_ref) in enumerate(groups):
        length = c_ref.shape[-1]
        k_t = c_ref[0]
        v_t = c_ref[1]
        k_new = qkv_ref[:, PACK_K + g * T_NEW:PACK_K + (g + 1) * T_NEW]
        v_new = qkv_ref[:, PACK_V + g * T_NEW:PACK_V + (g + 1) * T_NEW]
        for t in range(T_NEW):
            o_parts, lse_parts = [], []
            for j in range(2):
                rows = slice(j * HEAD_DIM, (j + 1) * HEAD_DIM)
                qc = qkv_ref[rows, PACK_Q + g * T_NEW + t:PACK_Q + g * T_NEW + t + 1]
                s_c = jnp.sum(k_t[rows] * qc, axis=0, keepdims=True) * scale + b_ref[j, t:t + 1, :]
                s_n = jnp.sum(k_new[rows] * qc, axis=0, keepdims=True) * scale + bn_ref[g, j, t:t + 1, :]
                m = jnp.maximum(jnp.max(s_c, axis=1, keepdims=True), jnp.max(s_n, axis=1, keepdims=True))
                p_c = jnp.exp(s_c - m)
                p_n = jnp.exp(s_n - m)
                l = jnp.sum(p_c, axis=1, keepdims=True) + jnp.sum(p_n, axis=1, keepdims=True)
                acc = (jnp.sum(v_t[rows] * p_c, axis=1, keepdims=True)
                       + jnp.sum(v_new[rows] * p_n, axis=1, keepdims=True))
                o_parts.append(acc / l)
                lse_parts.append(jnp.broadcast_to(m + jnp.log(l), (HEAD_DIM, 1)))
            sel = lane == g * T_NEW + t
            o_cols = jnp.where(sel, jnp.concatenate(o_parts, axis=0), o_cols)
            lse_cols = jnp.where(sel, jnp.concatenate(lse_parts, axis=0), lse_cols)

        for kv, new in ((0, k_new), (1, v_new)):
            rolled = pltpu.roll(c_ref[kv], length - T_NEW, axis=1)
            tail = rolled[:, length - 128:]
            for t in range(T_NEW):
                tail = jnp.where(lane == 128 - T_NEW + t, new[:, t:t + 1], tail)
            if length > 128:
                n_ref[kv, :, 0:length - 128] = rolled[:, 0:length - 128]
            n_ref[kv, :, length - 128:] = tail
    o_ref[...] = o_cols.T[0:16, :]
    lse_ref[...] = lse_cols.T[0:16, :]


def _sample_cache(qkv_t, caches_t, bcs, bn):
    b = qkv_t.shape[0]

    def cache_spec(c):
        return pl.BlockSpec((None, 2, 128, c.shape[-1]), lambda i, h: (i, 0, h, 0))

    def bias_spec(t):
        return pl.BlockSpec((2, T_NEW, t.shape[-1]), lambda i, h: (h, 0, 0))

    out = pl.BlockSpec((None, 16, 128), lambda i, h: (i, 0, h))
    return pl.pallas_call(
        _sample_cache_kernel,
        grid=(b, HEADS // 2),
        in_specs=[pl.BlockSpec((None, 128, 128), lambda i, h: (i, h, 0))]
                 + [cache_spec(c) for c in caches_t] + [bias_spec(t) for t in bcs]
                 + [pl.BlockSpec((N_GROUPS, 2, T_NEW, T_NEW), lambda i, h: (0, h, 0, 0))],
        out_specs=[cache_spec(c) for c in caches_t] + [out, out],
        out_shape=[jax.ShapeDtypeStruct(c.shape, c.dtype) for c in caches_t]
                  + [jax.ShapeDtypeStruct((b, 16, GROUP_WIDTH), F32)] * 2,
        compiler_params=_params(),
        name="sample_cache",
    )(qkv_t, *caches_t, *bcs, bn)


def _mix_kernel(*refs, tm, alpha, dils):
    (x_ref, o0_ref, o1_ref, o2_ref, l0_ref, l1_ref, l2_ref, yb_ref, sga_ref, sgb_ref,
     wpa_ref, wpb_ref, wo_ref, g_ref, b_ref, wrh_ref, wrl_ref, x1_ref, ei_ref, gt_ref) = refs[0:20]
    scratch = list(refs[20:])

    def natural(ref, d):
        if d == 1:
            return ref[0].astype(F32)
        scr = scratch.pop()
        for kk in range(GROUP_WIDTH // 128):
            for c in range(d):
                scr[kk, pl.ds(c, tm // d, stride=d), :] = ref[c, :, kk * 128:(kk + 1) * 128].astype(F32)
        return jnp.concatenate([scr[kk] for kk in range(GROUP_WIDTH // 128)], axis=1)

    l0, l1, l2 = natural(l0_ref, dils[0]), natural(l1_ref, dils[1]), natural(l2_ref, dils[2])
    mx = jnp.maximum(jnp.maximum(l0, l1), l2)
    e0 = jnp.exp(l0 - mx)
    e1 = jnp.exp(l1 - mx)
    e2 = jnp.exp(l2 - mx)
    ya = (e0 * natural(o0_ref, dils[0]) + e1 * natural(o1_ref, dils[1]) + e2 * natural(o2_ref, dils[2])) / (e0 + e1 + e2)
    pa = jnp.dot(ya.astype(BF16), wpa_ref[...], preferred_element_type=F32)
    pb = jnp.dot(yb_ref[...], wpb_ref[...], preferred_element_type=F32)
    gated = sga_ref[...].astype(F32) * pa + sgb_ref[...].astype(F32) * pb
    mix = jnp.dot(gated.astype(BF16), wo_ref[...], preferred_element_type=F32)
    z = alpha * x_ref[...] + mix
    mu = jnp.mean(z, axis=-1, keepdims=True)
    zc = z - mu
    var = jnp.mean(zc * zc, axis=-1, keepdims=True)
    x1 = zc * lax.rsqrt(var + LN_EPS) * g_ref[...] + b_ref[...]
    x1_ref[...] = x1

    xh = x1.astype(BF16)
    xl = (x1 - xh.astype(F32)).astype(BF16)
    nt = (((1,), (1,)), ((), ()))
    wrh = wrh_ref[...]
    lt = (lax.dot_general(wrh, xh, nt, preferred_element_type=F32)
          + lax.dot_general(wrh, xl, nt, preferred_element_type=F32)
          + lax.dot_general(wrl_ref[...], xh, nt, preferred_element_type=F32))

    gl = lt[0:N_EXPERT_GROUPS]
    gmax = jnp.max(gl, axis=0, keepdims=True)
    idx4 = lax.broadcasted_iota(I32, (N_EXPERT_GROUPS, tm), 0)
    g_idx = jnp.min(jnp.where(gl == gmax, idx4, N_EXPERT_GROUPS), axis=0, keepdims=True)
    g_prob = 1.0 / jnp.sum(jnp.exp(gl - gmax), axis=0, keepdims=True)
    e_sel = lt[8:16]
    for grp in range(1, N_EXPERT_GROUPS):
        e_sel = jnp.where(g_idx == grp, lt[8 + 8 * grp:16 + 8 * grp], e_sel)
    idx8 = lax.broadcasted_iota(I32, (EXPERTS_PER_GROUP, tm), 0)
    v1 = jnp.max(e_sel, axis=0, keepdims=True)
    i1 = jnp.min(jnp.where(e_sel == v1, idx8, EXPERTS_PER_GROUP), axis=0, keepdims=True)
    rest = jnp.where(idx8 == i1, NEG_INF, e_sel)
    v2 = jnp.max(rest, axis=0, keepdims=True)
    i2 = jnp.min(jnp.where(rest == v2, idx8, EXPERTS_PER_GROUP), axis=0, keepdims=True)
    r = jnp.exp(v2 - v1)
    gate1 = g_prob / (1.0 + r)
    gate2 = g_prob * r / (1.0 + r)
    ex1 = g_idx * EXPERTS_PER_GROUP + i1
    ex2 = g_idx * EXPERTS_PER_GROUP + i2
    ei_ref[...] = jnp.where(idx8 == 0, ex1, jnp.where(idx8 == 1, ex2, 0))
    gt_ref[...] = jnp.where(idx8 == 0, gate1, jnp.where(idx8 == 1, gate2, 0.0))


def _mix(x, o, lse, yb, sga, sgb, w_pa, w_pb, w_o, ln_g, ln_b, wr_hi, wr_lo, *, tm, alpha, dils):
    n = x.shape[0]

    def row_spec(width):
        return pl.BlockSpec((tm, width), lambda i: (i, 0))

    def class_spec(d):
        return pl.BlockSpec((d, tm // d, GROUP_WIDTH), lambda i: (0, i, 0))

    def full(a):
        return pl.BlockSpec(a.shape, lambda i: (0,) * a.ndim)

    lane_spec = pl.BlockSpec((8, tm), lambda i: (0, i))
    n_scratch = 2 * sum(1 for d in dils if d > 1)
    return pl.pallas_call(
        functools.partial(_mix_kernel, tm=tm, alpha=alpha, dils=dils),
        grid=(n // tm,),
        in_specs=[row_spec(D_MODEL)] + [class_spec(d) for d in dils] * 2 + [row_spec(CONV_CHANNELS)]
                 + [row_spec(D_MODEL)] * 2
                 + [full(w_pa), full(w_pb), full(w_o), full(ln_g), full(ln_b), full(wr_hi), full(wr_lo)],
        out_specs=[row_spec(D_MODEL), lane_spec, lane_spec],
        out_shape=[jax.ShapeDtypeStruct((n, D_MODEL), F32),
                   jax.ShapeDtypeStruct((8, n), I32),
                   jax.ShapeDtypeStruct((8, n), F32)],
        scratch_shapes=[pltpu.VMEM((GROUP_WIDTH // 128, tm, 128), F32)] * n_scratch,
        compiler_params=_params(),
        name="mix",
    )(x, o[0], o[1], o[2], lse[0], lse[1], lse[2], yb, sga, sgb, w_pa, w_pb, w_o, ln_g, ln_b, wr_hi, wr_lo)


def _rank_kernel(ei_ref, rank_ref, cnt_ref, carry_ref, *, tl):
    @pl.when(pl.program_id(0) == 0)
    def _():
        carry_ref[...] = jnp.zeros_like(carry_ref)
    ex = lax.broadcasted_iota(I32, (N_EXPERTS, tl), 0)
    oh0 = (ex == ei_ref[0:1, :]).astype(F32)
    oh1 = (ex == ei_ref[1:2, :]).astype(F32)
    a = lax.broadcasted_iota(I32, (tl, tl), 0)
    b = lax.broadcasted_iota(I32, (tl, tl), 1)
    upper = (a < b).astype(BF16)
    pre0 = jnp.dot(oh0.astype(BF16), upper, preferred_element_type=F32)
    pre1 = jnp.dot(oh1.astype(BF16), upper, preferred_element_type=F32)
    carry = carry_ref[:, 0:1]
    cnt0 = jnp.sum(oh0, axis=1, keepdims=True)
    cnt1 = jnp.sum(oh1, axis=1, keepdims=True)
    rank0 = jnp.sum(oh0 * (pre0 + carry), axis=0, keepdims=True)
    rank1 = jnp.sum(oh1 * (pre1 + cnt0 + carry), axis=0, keepdims=True)
    row = lax.broadcasted_iota(I32, (8, tl), 0)
    rank_ref[...] = jnp.where(row == 0, rank0.astype(I32), jnp.where(row == 1, rank1.astype(I32), 0))
    total = carry_ref[...] + cnt0 + cnt1
    carry_ref[...] = total
    cnt_ref[...] = total.astype(I32)


def _rank(ei, *, tl):
    n = ei.shape[1]
    return pl.pallas_call(
        functools.partial(_rank_kernel, tl=tl),
        grid=(n // tl,),
        in_specs=[pl.BlockSpec((8, tl), lambda i: (0, i))],
        out_specs=[pl.BlockSpec((8, tl), lambda i: (0, i)),
                   pl.BlockSpec((N_EXPERTS, 128), lambda i: (0, 0))],
        out_shape=[jax.ShapeDtypeStruct((8, n), I32), jax.ShapeDtypeStruct((N_EXPERTS, 128), I32)],
        scratch_shapes=[pltpu.VMEM((N_EXPERTS, 128), F32)],
        name="moe_rank",
    )(ei)


def _row_copy(src, src_row, dst, dst_row, sem):
    return pltpu.make_async_copy(src.at[pl.ds(src_row, 1)], dst.at[pl.ds(dst_row, 1)], sem)


def _dispatch_kernel(e_ref, r_ref, ps_ref, x1_ref, buf_in_ref, buf_ref, sem, *, n, tm):
    del buf_in_ref
    base = pl.program_id(0) * tm

    def body(r, carry):
        tok = base + r
        for k in range(TOP_K):
            slot = ps_ref[e_ref[k * n + tok]] + r_ref[k * n + tok]
            _row_copy(x1_ref, r, buf_ref, slot, sem).start()
        return carry

    lax.fori_loop(0, tm, body, 0)
    for _ in range(TOP_K):
        pltpu.make_async_copy(x1_ref, buf_ref.at[pl.ds(0, tm)], sem).wait()


def _dispatch(e_flat, r_flat, pstart, x1, buf, *, tm):
    n = x1.shape[0]
    return pl.pallas_call(
        functools.partial(_dispatch_kernel, n=n, tm=tm),
        grid_spec=pltpu.PrefetchScalarGridSpec(
            num_scalar_prefetch=3,
            grid=(n // tm,),
            in_specs=[pl.BlockSpec((tm, D_MODEL), lambda i, e, r, p: (i, 0)), pl.BlockSpec(memory_space=pl.ANY)],
            out_specs=pl.BlockSpec(memory_space=pl.ANY),
            scratch_shapes=[pltpu.SemaphoreType.DMA(())],
        ),
        out_shape=jax.ShapeDtypeStruct(buf.shape, buf.dtype),
        input_output_aliases={4: 0},
        name="moe_dispatch",
    )(e_flat, r_flat, pstart, x1, buf)


def _expert_kernel(be_ref, nu_ref, xb_ref, wg_ref, wu_ref, wd_ref, out_ref, wg_bf, wu_bf, wd_bf):
    j = pl.program_id(0)
    used = j < nu_ref[0]
    changed = jnp.logical_or(j == 0, be_ref[j] != be_ref[jnp.maximum(j - 1, 0)])

    @pl.when(jnp.logical_and(used, changed))
    def _():
        wg_bf[...] = wg_ref[...].astype(BF16)
        wu_bf[...] = wu_ref[...].astype(BF16)
        wd_bf[...] = wd_ref[...].astype(BF16)

    @pl.when(used)
    def _():
        xb = xb_ref[...].astype(BF16)
        a = jnp.dot(xb, wg_bf[...], preferred_element_type=F32)
        b = jnp.dot(xb, wu_bf[...], preferred_element_type=F32)
        h = (a * _sigmoid(a)) * b
        out_ref[...] = jnp.dot(h.astype(BF16), wd_bf[...], preferred_element_type=F32)

    @pl.when(jnp.logical_not(used))
    def _():
        out_ref[...] = jnp.zeros_like(out_ref)


def _experts(block_expert, n_used, buf, w_g, w_u, w_d, *, bm):
    nblk = buf.shape[0] // bm

    def row_map(j, be, nu):
        return (jnp.minimum(j, nu[0] - 1), 0)

    def w_map(j, be, nu):
        return (be[j], 0, 0)

    return pl.pallas_call(
        _expert_kernel,
        grid_spec=pltpu.PrefetchScalarGridSpec(
            num_scalar_prefetch=2,
            grid=(nblk,),
            in_specs=[pl.BlockSpec((bm, D_MODEL), row_map),
                      pl.BlockSpec((None, D_MODEL, D_EXPERT), w_map),
                      pl.BlockSpec((None, D_MODEL, D_EXPERT), w_map),
                      pl.BlockSpec((None, D_EXPERT, D_MODEL), w_map)],
            out_specs=pl.BlockSpec((bm, D_MODEL), lambda j, be, nu: (j, 0)),
            scratch_shapes=[pltpu.VMEM((D_MODEL, D_EXPERT), BF16),
                            pltpu.VMEM((D_MODEL, D_EXPERT), BF16),
                            pltpu.VMEM((D_EXPERT, D_MODEL), BF16)],
        ),
        out_shape=jax.ShapeDtypeStruct(buf.shape, F32),
        compiler_params=_params(),
        name="moe_experts",
    )(block_expert, n_used, buf, w_g, w_u, w_d)


def _combine_kernel(e_ref, r_ref, ps_ref, x1_ref, gc_ref, g_ref, b_ref, eo_ref, y_ref, rows, sem, *, n, tm, alpha):
    base = pl.program_id(0) * tm

    def body(r, carry):
        tok = base + r
        for k in range(TOP_K):
            slot = ps_ref[e_ref[k * n + tok]] + r_ref[k * n + tok]
            _row_copy(eo_ref, slot, rows.at[k], r, sem).start()
        return carry

    lax.fori_loop(0, tm, body, 0)
    for k in range(TOP_K):
        pltpu.make_async_copy(eo_ref.at[pl.ds(0, tm)], rows.at[k], sem).wait()
    gc = gc_ref[...]
    z = alpha * x1_ref[...] + gc[:, 0:1] * rows[0] + gc[:, 1:2] * rows[1]
    mu = jnp.mean(z, axis=-1, keepdims=True)
    zc = z - mu
    var = jnp.mean(zc * zc, axis=-1, keepdims=True)
    y_ref[...] = zc * lax.rsqrt(var + LN_EPS) * g_ref[...] + b_ref[...]


def _combine(e_flat, r_flat, pstart, x1, gate_cols, ln_g, ln_b, expert_out, *, tm, alpha):
    n = x1.shape[0]
    return pl.pallas_call(
        functools.partial(_combine_kernel, n=n, tm=tm, alpha=alpha),
        grid_spec=pltpu.PrefetchScalarGridSpec(
            num_scalar_prefetch=3,
            grid=(n // tm,),
            in_specs=[pl.BlockSpec((tm, D_MODEL), lambda i, e, r, p: (i, 0)),
                      pl.BlockSpec((tm, TOP_K), lambda i, e, r, p: (i, 0)),
                      pl.BlockSpec((1, D_MODEL), lambda i, e, r, p: (0, 0)),
                      pl.BlockSpec((1, D_MODEL), lambda i, e, r, p: (0, 0)),
                      pl.BlockSpec(memory_space=pl.ANY)],
            out_specs=pl.BlockSpec((tm, D_MODEL), lambda i, e, r, p: (i, 0)),
            scratch_shapes=[pltpu.VMEM((TOP_K, tm, D_MODEL), F32), pltpu.SemaphoreType.DMA(())],
        ),
        out_shape=jax.ShapeDtypeStruct((n, D_MODEL), F32),
        compiler_params=_params(),
        name="moe_combine",
    )(e_flat, r_flat, pstart, x1, gate_cols, ln_g, ln_b, expert_out)


def _hier_moe_ln(x1, ei, gt, w_g, w_u, w_d, ln_g, ln_b, *, tl, tm, bm, alpha):
    n = x1.shape[0]
    m = n * TOP_K
    rank, cnt = _rank(ei, tl=tl)
    counts = cnt[:, 0]
    padded = (counts + bm - 1) // bm * bm
    pend = jnp.cumsum(padded)
    pstart = (pend - padded).astype(I32)
    nblk = (m + N_EXPERTS * (bm - 1) + bm - 1) // bm
    blk_start = jnp.arange(nblk, dtype=I32) * bm
    n_used = (pend[-1] // bm).astype(I32)
    be = jnp.minimum(jnp.sum(pend[None, :] <= blk_start[:, None], axis=1), N_EXPERTS - 1).astype(I32)
    be = jnp.where(jnp.arange(nblk) < n_used, be, jnp.take(be, n_used - 1))
    e_flat = ei[0:TOP_K].reshape(m)
    r_flat = rank[0:TOP_K].reshape(m)
    buf = _dispatch(e_flat, r_flat, pstart, x1, jnp.zeros((nblk * bm, D_MODEL), F32), tm=tm)
    eo = _experts(be, n_used.reshape(1), buf, w_g, w_u, w_d, bm=bm)
    gate_cols = gt[0:TOP_K].T
    return _combine(e_flat, r_flat, pstart, x1, gate_cols, ln_g, ln_b, eo, tm=tm, alpha=alpha)


def _t5_bucket(n):
    nf = jnp.maximum(n, 1).astype(F32)
    large = MAX_EXACT + (jnp.log(nf / MAX_EXACT) / math.log(MAX_DISTANCE / MAX_EXACT)
                         * (N_BUCKETS - MAX_EXACT)).astype(I32)
    large = jnp.minimum(large, N_BUCKETS - 1)
    return jnp.where(n < MAX_EXACT, n, large)


def _bias_per_group(rel_bias):
    offs = jnp.arange(N_KEYS, dtype=I32)[None, :] * jnp.array(DILATIONS, I32)[:, None]
    bucket = _t5_bucket(offs)
    table = rel_bias.reshape(N_BUCKETS, N_GROUPS, HEADS)
    b = table[bucket, jnp.arange(N_GROUPS)[:, None]]
    return jnp.transpose(b, (0, 2, 1)).astype(F32)


def _prompt_bias_tables(bias):
    width = 3 * Q_BLOCK
    neg = jnp.full((N_GROUPS, HEADS, Q_BLOCK - 1), NEG_INF, F32)
    r = jnp.concatenate([neg, bias[:, :, ::-1], neg, jnp.full((N_GROUPS, HEADS, 1), NEG_INF, F32)], axis=-1)
    flat = jnp.tile(r, (1, 1, Q_BLOCK))[:, :, :Q_BLOCK * (width - 1)]
    skew = flat.reshape(N_GROUPS, HEADS, Q_BLOCK, width - 1)
    later = skew[:, :, :, Q_BLOCK - 1:3 * Q_BLOCK - 1]
    has_prev = (np.arange(2 * Q_BLOCK) >= Q_BLOCK)[None, None, None, :]
    first = jnp.where(has_prev, later, NEG_INF)
    tb = jnp.stack([first, later], axis=1)
    return tb.reshape(N_GROUPS, 2, HEADS // 2, 2 * Q_BLOCK, 2 * Q_BLOCK)


def _sample_bias_tables(bias):
    bcs = []
    for g, d in enumerate(DILATIONS):
        rev = bias[g][:, ::-1][:, :WINDOW_KEYS]
        per_t = []
        for t in range(T_NEW):
            if d == 1:
                row = jnp.concatenate([jnp.full((HEADS, t), NEG_INF, F32), rev[:, :WINDOW_KEYS - t]], axis=1)
            else:
                cls = np.arange(d)[None, None, :] == t
                row = jnp.where(cls, rev[:, :, None], NEG_INF).reshape(HEADS, WINDOW_KEYS * d)
            per_t.append(row)
        bcs.append(jnp.stack(per_t, axis=1))
    bn = []
    for g, d in enumerate(DILATIONS):
        rows = []
        for t in range(T_NEW):
            cols = []
            for tn in range(T_NEW):
                ok = (tn <= t) if d == 1 else (tn == t)
                cols.append(bias[g][:, t - tn] if ok else jnp.full((HEADS,), NEG_INF, F32))
            rows.append(jnp.stack(cols, axis=-1))
        bn.append(jnp.stack(rows, axis=1))
    return bcs, jnp.stack(bn)


def _split_bf16(w):
    hi = w.astype(BF16)
    lo = (w - hi.astype(F32)).astype(BF16)
    return hi, lo


def kernel(x_prompt, x_sample, cache_attn_w128, cache_attn_w512, cache_attn_w2048, state_conv, rel_bias, w_in, w_conv, w_pa, w_pb, w_o, ln1_g, ln1_b, w_router_group, w_router_expert, w_expert_gate, w_expert_up, w_expert_down, ln2_g, ln2_b):
    depth = w_in.shape[0]
    assert depth == 1 and x_prompt.shape[0] == 1
    alpha = (2.0 * depth) ** 0.25
    s = x_prompt.shape[1]
    bd, t_len = x_sample.shape[0], x_sample.shape[1]
    assert t_len == T_NEW and s % (DILATIONS[-1] * Q_BLOCK) == 0

    bias = _bias_per_group(rel_bias)
    tb = _prompt_bias_tables(bias)
    bcs, bn = _sample_bias_tables(bias)

    w_in_bf = w_in[0].astype(BF16)
    w_pa_bf = w_pa[0].astype(BF16)
    w_pb_bf = w_pb[0].astype(BF16)
    w_o_bf = w_o[0].astype(BF16)
    wr = jnp.zeros((ROUTER_ROWS, D_MODEL), F32)
    wr = wr.at[0:N_EXPERT_GROUPS].set(w_router_group[0].T).at[8:8 + N_EXPERTS].set(w_router_expert[0].T)
    wr_hi, wr_lo = _split_bf16(wr)
    g1, b1 = ln1_g[0][None], ln1_b[0][None]
    g2, b2 = ln2_g[0][None], ln2_b[0][None]
    wg, wu, wd = w_expert_gate[0], w_expert_up[0], w_expert_down[0]

    xp = x_prompt[0]
    kv_tail = min(MAX_DISTANCE, s)
    q, kb, vb, (k32, v32, yb, sga, sgb, ut) = _proj(
        xp, w_in_bf, w_conv[0], None, tm=256, kv_tail=kv_tail, u_tail=8, q_dtype=BF16, dils=DILATIONS)
    o_l = [_attn_prompt_group(q[g], kb[g], vb[g], tb[g], g) for g in range(N_GROUPS)]
    x1, ei, gt = _mix(xp, [a[0] for a in o_l], [a[1] for a in o_l], yb, sga, sgb,
                      w_pa_bf, w_pb_bf, w_o_bf, g1, b1, wr_hi, wr_lo, tm=512, alpha=alpha, dils=DILATIONS)
    y_prompt = _hier_moe_ln(x1, ei, gt, wg, wu, wd, g2, b2, tl=512, tm=256, bm=256, alpha=alpha)[None]

    kv_prompt = []
    for g, d in enumerate(DILATIONS):
        length = min(WINDOW_KEYS * d, s)
        cols = slice(g * GROUP_WIDTH, (g + 1) * GROUP_WIDTH)
        kg = k32[kv_tail - length:, cols].reshape(length, HEADS, HEAD_DIM)
        vg = v32[kv_tail - length:, cols].reshape(length, HEADS, HEAD_DIM)
        kv_prompt.append(jnp.stack([kg, vg], axis=1)[None, None])
    conv_prompt = ut[6:8][None, None]

    ns = bd * t_len
    xs = x_sample.reshape(ns, D_MODEL)
    st = state_conv[0]
    s0 = jnp.repeat(st[:, 0], t_len, axis=0)
    s1 = jnp.repeat(st[:, 1], t_len, axis=0)
    qs, _, _, (k32s, v32s, ybs, sgas, sgbs, us) = _proj(
        xs, w_in_bf, w_conv[0], (s0, s1), tm=ns, kv_tail=ns, u_tail=ns, q_dtype=F32, dils=NO_DILATION)
    qs = jnp.concatenate([a[0] for a in qs], axis=1)
    packed = jnp.stack([qs, k32s, v32s]).reshape(3, bd, t_len, N_GROUPS, GROUP_WIDTH)
    qkv_t = jnp.transpose(packed, (1, 4, 0, 3, 2)).reshape(bd, GROUP_WIDTH, 3 * N_GROUPS * t_len)
    qkv_t = jnp.pad(qkv_t, ((0, 0), (0, 0), (0, 128 - 3 * N_GROUPS * t_len)))
    caches = (cache_attn_w128[0], cache_attn_w512[0], cache_attn_w2048[0])
    caches_t = [jnp.transpose(c, (0, 2, 3, 4, 1)).reshape(bd, 2, GROUP_WIDTH, c.shape[1]) for c in caches]
    n0, n1, n2, o_s, lse_s = _sample_cache(qkv_t, caches_t, bcs, bn)

    def unpack(a):
        a = a[:, :N_GROUPS * t_len].reshape(bd, N_GROUPS, t_len, GROUP_WIDTH)
        return jnp.transpose(a, (1, 0, 2, 3)).reshape(N_GROUPS, 1, ns, GROUP_WIDTH)

    o_s, lse_s = unpack(o_s), unpack(lse_s)
    x1s, eis, gts = _mix(xs, o_s, lse_s, ybs, sgas, sgbs, w_pa_bf, w_pb_bf, w_o_bf, g1, b1, wr_hi, wr_lo,
                         tm=ns, alpha=alpha, dils=NO_DILATION)
    y_sample = _hier_moe_ln(x1s, eis, gts, wg, wu, wd, g2, b2, tl=ns, tm=ns, bm=128, alpha=alpha)
    y_sample = y_sample.reshape(bd, t_len, D_MODEL)

    kv_sample = [jnp.transpose(c.reshape(bd, 2, HEADS, HEAD_DIM, c.shape[-1]), (0, 4, 1, 2, 3))[None]
                 for c in (n0, n1, n2)]
    conv_sample = us.reshape(bd, t_len, CONV_CHANNELS)[:, t_len - 2:][None]

    return (y_prompt, y_sample, kv_prompt[0], kv_prompt[1], kv_prompt[2], conv_prompt,
            kv_sample[0], kv_sample[1], kv_sample[2], conv_sample)
```

```python
import functools
import math

import numpy as np
import jax
import jax.numpy as jnp
from jax import lax
from jax.experimental import pallas as pl
from jax.experimental.pallas import tpu as pltpu

F32 = jnp.float32
BF16 = jnp.bfloat16
I32 = jnp.int32

D_MODEL = 1024
N_GROUPS = 3
HEADS = 8
HEAD_DIM = 64
GROUP_WIDTH = HEADS * HEAD_DIM
ATTN_WIDTH = N_GROUPS * GROUP_WIDTH
DILATIONS = (1, 4, 16)
NO_DILATION = (1, 1, 1)
WINDOW_KEYS = 128
N_KEYS = WINDOW_KEYS + 1
N_BUCKETS = 32
MAX_EXACT = 16
MAX_DISTANCE = 2048
CONV_CHANNELS = 512
N_EXPERT_GROUPS = 4
EXPERTS_PER_GROUP = 8
N_EXPERTS = 32
TOP_K = 2
D_EXPERT = 512
LN_EPS = 1e-5
PROJ_WIDTH = 3 * ATTN_WIDTH + 3 * CONV_CHANNELS + 2 * D_MODEL
ROUTER_ROWS = 8 + N_EXPERTS
Q_BLOCK = 128
T_NEW = 4
NEG_INF = float("-inf")
VMEM_LIMIT = 56 * 1024 * 1024


def _sigmoid(x):
    return 1.0 / (1.0 + jnp.exp(-x))


def _params(limit=VMEM_LIMIT):
    return pltpu.CompilerParams(vmem_limit_bytes=limit)


def _proj_kernel(*refs, tm, tail_rows, sample_mode, dils):
    n_in = 5 if sample_mode else 3
    x_ref, w_ref, wc_ref = refs[0:3]
    outs = refs[n_in:]
    q_refs, k_refs, v_refs = outs[0:3], outs[3:6], outs[6:9]
    k32_ref, v32_ref, yb_ref, sga_ref, sgb_ref, ut_ref, cls_ref = outs[9:16]
    xb = x_ref[...].astype(BF16)

    def col(c0, width):
        return jnp.dot(xb, w_ref[:, c0:c0 + width], preferred_element_type=F32)

    def write_classes(val, group_refs):
        for g, d in enumerate(dils):
            part = val[:, g * GROUP_WIDTH:(g + 1) * GROUP_WIDTH]
            ref = group_refs[g]
            if d == 1:
                ref[0] = part.astype(ref.dtype)
            else:
                for kk in range(GROUP_WIDTH // 128):
                    lanes = slice(kk * 128, (kk + 1) * 128)
                    cls_ref[kk] = part[:, lanes]
                    for c in range(d):
                        ref[c, :, lanes] = cls_ref[kk, pl.ds(c, tm // d, stride=d), :].astype(ref.dtype)

    write_classes(col(0, ATTN_WIDTH), q_refs)
    k = col(ATTN_WIDTH, ATTN_WIDTH)
    k32_ref[...] = k
    write_classes(k, k_refs)
    v = col(2 * ATTN_WIDTH, ATTN_WIDTH)
    v32_ref[...] = v
    write_classes(v, v_refs)

    c0 = 3 * ATTN_WIDTH
    bg = col(c0, CONV_CHANNELS)
    u = col(c0 + CONV_CHANNELS, CONV_CHANNELS) * col(c0 + 2 * CONV_CHANNELS, CONV_CHANNELS)
    row = lax.broadcasted_iota(I32, (tm, CONV_CHANNELS), 0)
    r1 = pltpu.roll(u, 1, axis=0)
    r2 = pltpu.roll(u, 2, axis=0)
    if sample_mode:
        s0 = refs[3][...]
        s1 = refs[4][...]
        t = row & (T_NEW - 1)
        prev1 = jnp.where(t == 0, s1, r1)
        prev2 = jnp.where(t == 0, s0, jnp.where(t == 1, s1, r2))
    else:
        carry_ref = outs[16]

        @pl.when(pl.program_id(0) == 0)
        def _():
            carry_ref[...] = jnp.zeros_like(carry_ref)
        c6 = carry_ref[6:7, :]
        c7 = carry_ref[7:8, :]
        prev1 = jnp.where(row == 0, c7, r1)
        prev2 = jnp.where(row == 0, c6, jnp.where(row == 1, c7, r2))
        carry_ref[...] = u[tm - 8:tm, :]
    conv = prev2 * wc_ref[0:1, :] + prev1 * wc_ref[1:2, :] + u * wc_ref[2:3, :]
    yb_ref[...] = (bg * conv).astype(BF16)
    ut_ref[...] = u[tm - tail_rows:tm, :]

    c1 = c0 + 3 * CONV_CHANNELS
    sga_ref[...] = _sigmoid(col(c1, D_MODEL)).astype(BF16)
    sgb_ref[...] = _sigmoid(col(c1 + D_MODEL, D_MODEL)).astype(BF16)


def _proj(x, w_in_bf, w_conv, conv_prev, *, tm, kv_tail, u_tail, q_dtype, dils):
    n = x.shape[0]
    sample_mode = conv_prev is not None
    nt = n // tm
    tail_first = (n - kv_tail) // tm

    def row_spec(width):
        return pl.BlockSpec((tm, width), lambda i: (i, 0))

    def tail_spec(width):
        return pl.BlockSpec((tm, width), lambda i: (jnp.maximum(i - tail_first, 0), 0))

    def class_spec(d):
        return pl.BlockSpec((d, tm // d, GROUP_WIDTH), lambda i: (0, i, 0))

    def class_shape(d, dtype):
        return jax.ShapeDtypeStruct((d, n // d, GROUP_WIDTH), dtype)

    in_specs = [
        row_spec(D_MODEL),
        pl.BlockSpec((D_MODEL, PROJ_WIDTH), lambda i: (0, 0), pipeline_mode=pl.Buffered(1)),
        pl.BlockSpec((3, CONV_CHANNELS), lambda i: (0, 0)),
    ]
    args = [x, w_in_bf, w_conv]
    scratch = [pltpu.VMEM((GROUP_WIDTH // 128, tm, 128), F32)]
    if sample_mode:
        in_specs += [row_spec(CONV_CHANNELS), row_spec(CONV_CHANNELS)]
        args += [conv_prev[0], conv_prev[1]]
    else:
        scratch.append(pltpu.VMEM((8, CONV_CHANNELS), F32))
    out_shape = (
        [class_shape(d, q_dtype) for d in dils] + [class_shape(d, BF16) for d in dils] * 2
        + [jax.ShapeDtypeStruct((kv_tail, ATTN_WIDTH), F32),
           jax.ShapeDtypeStruct((kv_tail, ATTN_WIDTH), F32),
           jax.ShapeDtypeStruct((n, CONV_CHANNELS), BF16),
           jax.ShapeDtypeStruct((n, D_MODEL), BF16),
           jax.ShapeDtypeStruct((n, D_MODEL), BF16),
           jax.ShapeDtypeStruct((u_tail, CONV_CHANNELS), F32)])
    out_specs = (
        [class_spec(d) for d in dils] * 3
        + [tail_spec(ATTN_WIDTH), tail_spec(ATTN_WIDTH),
           row_spec(CONV_CHANNELS), row_spec(D_MODEL), row_spec(D_MODEL),
           pl.BlockSpec((u_tail, CONV_CHANNELS), lambda i: (0, 0))])
    res = pl.pallas_call(
        functools.partial(_proj_kernel, tm=tm, tail_rows=u_tail, sample_mode=sample_mode, dils=dils),
        grid=(nt,),
        in_specs=in_specs,
        out_specs=out_specs,
        out_shape=out_shape,
        scratch_shapes=scratch,
        compiler_params=_params(),
        name="proj",
    )(*args)
    return res[0:3], res[3:6], res[6:9], res[9:]


def _attn_kernel(q_ref, kp_ref, kc_ref, vp_ref, vc_ref, tb_ref, o_ref, lse_ref):
    lane = lax.broadcasted_iota(I32, (Q_BLOCK, 128), 1)
    first = lane < HEAD_DIM
    for pr in range(HEADS // 2):
        sl = slice(pr * 128, (pr + 1) * 128)
        q = q_ref[:, sl]
        k = jnp.concatenate([kp_ref[:, sl], kc_ref[:, sl]], axis=0)
        v = jnp.concatenate([vp_ref[:, sl], vc_ref[:, sl]], axis=0)
        qf = q.astype(F32)
        qq = jnp.concatenate([jnp.where(first, qf, 0.0), jnp.where(first, 0.0, qf)], axis=0).astype(BF16)
        s = lax.dot_general(qq, k, (((1,), (1,)), ((), ())), preferred_element_type=F32)
        s = s * (HEAD_DIM ** -0.5) + tb_ref[0, pr]
        m = jnp.max(s, axis=-1, keepdims=True)
        p = jnp.exp(s - m)
        l = jnp.sum(p, axis=-1, keepdims=True)
        o = jnp.dot(p.astype(BF16), v, preferred_element_type=F32) / l
        lse = m + jnp.log(l)
        o_ref[:, sl] = jnp.where(first, o[:Q_BLOCK], o[Q_BLOCK:]).astype(o_ref.dtype)
        lse_ref[:, sl] = jnp.where(first, jnp.broadcast_to(lse[:Q_BLOCK], (Q_BLOCK, 128)),
                                   jnp.broadcast_to(lse[Q_BLOCK:], (Q_BLOCK, 128)))


def _attn_prompt_group(q, kb, vb, tb, g):
    d, rows = q.shape[0], q.shape[1]
    cur = pl.BlockSpec((None, Q_BLOCK, GROUP_WIDTH), lambda c, i: (c, i, 0))
    prev = pl.BlockSpec((None, Q_BLOCK, GROUP_WIDTH), lambda c, i: (c, jnp.maximum(i - 1, 0), 0))
    return pl.pallas_call(
        _attn_kernel,
        grid=(d, rows // Q_BLOCK),
        in_specs=[cur, prev, cur, prev, cur,
                  pl.BlockSpec((1, HEADS // 2, 2 * Q_BLOCK, 2 * Q_BLOCK), lambda c, i: (jnp.minimum(i, 1), 0, 0, 0))],
        out_specs=[cur, cur],
        out_shape=[jax.ShapeDtypeStruct((d, rows, GROUP_WIDTH), BF16),
                   jax.ShapeDtypeStruct((d, rows, GROUP_WIDTH), F32)],
        compiler_params=_params(),
        name=f"attn_prompt_g{g}",
    )(q, kb, kb, vb, vb, tb)


PACK_Q, PACK_K, PACK_V = 0, N_GROUPS * T_NEW, 2 * N_GROUPS * T_NEW


def _sample_cache_kernel(qbd_ref, nr_ref, qkv_ref, c0_ref, c1_ref, c2_ref, b0_ref, b1_ref, b2_ref, bn_ref,
                         n0_ref, n1_ref, n2_ref, o_ref, lse_ref):
    scale = HEAD_DIM ** -0.5
    nt = (((1,), (1,)), ((), ()))
    lane = lax.broadcasted_iota(I32, (128, 128), 1)
    head0 = lane[0:T_NEW] < HEAD_DIM
    groups = ((c0_ref, b0_ref, n0_ref), (c1_ref, b1_ref, n1_ref), (c2_ref, b2_ref, n2_ref))
    for g, (c_ref, b_ref, n_ref) in enumerate(groups):
        length = c_ref.shape[-1]
        k_new = qkv_ref[:, PACK_K + g * T_NEW:PACK_K + (g + 1) * T_NEW]
        v_new = qkv_ref[:, PACK_V + g * T_NEW:PACK_V + (g + 1) * T_NEW]
        qbd = qbd_ref[g]
        s_c = jnp.dot(qbd.astype(BF16), c_ref[0].astype(BF16), preferred_element_type=F32) * scale + b_ref[...]
        bn = bn_ref[g]
        s_n = [jnp.sum(qbd * nr_ref[g, 0, tn:tn + 1, :], axis=1, keepdims=True) * scale + bn[:, tn:tn + 1]
               for tn in range(T_NEW)]
        m = jnp.max(s_c, axis=1, keepdims=True)
        for x in s_n:
            m = jnp.maximum(m, x)
        p_c = jnp.exp(s_c - m)
        p_n = [jnp.exp(x - m) for x in s_n]
        l = jnp.sum(p_c, axis=1, keepdims=True)
        acc = lax.dot_general(p_c.astype(BF16), c_ref[1].astype(BF16), nt, preferred_element_type=F32)
        for tn in range(T_NEW):
            l = l + p_n[tn]
            acc = acc + p_n[tn] * nr_ref[g, 1, tn:tn + 1, :]
        o = acc / l
        lse = jnp.broadcast_to(m + jnp.log(l), (2 * T_NEW, 128))
        o_ref[g * T_NEW:(g + 1) * T_NEW, :] = jnp.where(head0, o[0:T_NEW], o[T_NEW:])
        lse_ref[g * T_NEW:(g + 1) * T_NEW, :] = jnp.where(head0, lse[0:T_NEW], lse[T_NEW:])

        for kv, new in ((0, k_new), (1, v_new)):
            rolled = pltpu.roll(c_ref[kv], length - T_NEW, axis=1)
            tail = rolled[:, length - 128:]
            for t in range(T_NEW):
                tail = jnp.where(lane == 128 - T_NEW + t, new[:, t:t + 1], tail)
            if length > 128:
                n_ref[kv, :, 0:length - 128] = rolled[:, 0:length - 128]
            n_ref[kv, :, length - 128:] = tail
    pad_rows = slice(N_GROUPS * T_NEW, 16)
    o_ref[pad_rows, :] = jnp.zeros((16 - N_GROUPS * T_NEW, 128), F32)
    lse_ref[pad_rows, :] = jnp.zeros((16 - N_GROUPS * T_NEW, 128), F32)


def _sample_cache(qbd, new_rows, qkv_t, caches_t, bcs, bn):
    b = qkv_t.shape[0]

    def cache_spec(c):
        return pl.BlockSpec((None, 2, 128, c.shape[-1]), lambda i, h: (i, 0, h, 0))

    def bias_spec(t):
        return pl.BlockSpec((None, 2 * T_NEW, t.shape[-1]), lambda i, h: (h, 0, 0))

    out = pl.BlockSpec((None, 16, 128), lambda i, h: (i, 0, h))
    return pl.pallas_call(
        _sample_cache_kernel,
        grid=(b, HEADS // 2),
        in_specs=[pl.BlockSpec((None, None, N_GROUPS, 2 * T_NEW, 128), lambda i, h: (i, h, 0, 0, 0)),
                  pl.BlockSpec((None, None, N_GROUPS, 2, T_NEW, 128), lambda i, h: (i, h, 0, 0, 0, 0)),
                  pl.BlockSpec((None, 128, 128), lambda i, h: (i, h, 0))]
                 + [cache_spec(c) for c in caches_t] + [bias_spec(t) for t in bcs]
                 + [pl.BlockSpec((N_GROUPS, None, 2 * T_NEW, T_NEW), lambda i, h: (0, h, 0, 0))],
        out_specs=[cache_spec(c) for c in caches_t] + [out, out],
        out_shape=[jax.ShapeDtypeStruct(c.shape, c.dtype) for c in caches_t]
                  + [jax.ShapeDtypeStruct((b, 16, GROUP_WIDTH), F32)] * 2,
        compiler_params=_params(),
        name="sample_cache",
    )(qbd, new_rows, qkv_t, *caches_t, *bcs, bn)


def _mix_kernel(*refs, tm, alpha, dils):
    (x_ref, o0_ref, o1_ref, o2_ref, l0_ref, l1_ref, l2_ref, yb_ref, sga_ref, sgb_ref,
     wpa_ref, wpb_ref, wo_ref, g_ref, b_ref, wrh_ref, wrl_ref, x1_ref, ei_ref, gt_ref) = refs[0:20]
    scratch = list(refs[20:])

    def natural(ref, d):
        if d == 1:
            return ref[0].astype(F32)
        scr = scratch.pop()
        for kk in range(GROUP_WIDTH // 128):
            for c in range(d):
                scr[kk, pl.ds(c, tm // d, stride=d), :] = ref[c, :, kk * 128:(kk + 1) * 128].astype(F32)
        return jnp.concatenate([scr[kk] for kk in range(GROUP_WIDTH // 128)], axis=1)

    l0, l1, l2 = natural(l0_ref, dils[0]), natural(l1_ref, dils[1]), natural(l2_ref, dils[2])
    mx = jnp.maximum(jnp.maximum(l0, l1), l2)
    e0 = jnp.exp(l0 - mx)
    e1 = jnp.exp(l1 - mx)
    e2 = jnp.exp(l2 - mx)
    ya = (e0 * natural(o0_ref, dils[0]) + e1 * natural(o1_ref, dils[1]) + e2 * natural(o2_ref, dils[2])) / (e0 + e1 + e2)
    pa = jnp.dot(ya.astype(BF16), wpa_ref[...], preferred_element_type=F32)
    pb = jnp.dot(yb_ref[...], wpb_ref[...], preferred_element_type=F32)
    gated = sga_ref[...].astype(F32) * pa + sgb_ref[...].astype(F32) * pb
    mix = jnp.dot(gated.astype(BF16), wo_ref[...], preferred_element_type=F32)
    z = alpha * x_ref[...] + mix
    mu = jnp.mean(z, axis=-1, keepdims=True)
    zc = z - mu
    var = jnp.mean(zc * zc, axis=-1, keepdims=True)
    x1 = zc * lax.rsqrt(var + LN_EPS) * g_ref[...] + b_ref[...]
    x1_ref[...] = x1

    xh = x1.astype(BF16)
    xl = (x1 - xh.astype(F32)).astype(BF16)
    nt = (((1,), (1,)), ((), ()))
    wrh = wrh_ref[...]
    lt = (lax.dot_general(wrh, xh, nt, preferred_element_type=F32)
          + lax.dot_general(wrh, xl, nt, preferred_element_type=F32)
          + lax.dot_general(wrl_ref[...], xh, nt, preferred_element_type=F32))

    gl = lt[0:N_EXPERT_GROUPS]
    gmax = jnp.max(gl, axis=0, keepdims=True)
    idx4 = lax.broadcasted_iota(I32, (N_EXPERT_GROUPS, tm), 0)
    g_idx = jnp.min(jnp.where(gl == gmax, idx4, N_EXPERT_GROUPS), axis=0, keepdims=True)
    g_prob = 1.0 / jnp.sum(jnp.exp(gl - gmax), axis=0, keepdims=True)
    e_sel = lt[8:16]
    for grp in range(1, N_EXPERT_GROUPS):
        e_sel = jnp.where(g_idx == grp, lt[8 + 8 * grp:16 + 8 * grp], e_sel)
    idx8 = lax.broadcasted_iota(I32, (EXPERTS_PER_GROUP, tm), 0)
    v1 = jnp.max(e_sel, axis=0, keepdims=True)
    i1 = jnp.min(jnp.where(e_sel == v1, idx8, EXPERTS_PER_GROUP), axis=0, keepdims=True)
    rest = jnp.where(idx8 == i1, NEG_INF, e_sel)
    v2 = jnp.max(rest, axis=0, keepdims=True)
    i2 = jnp.min(jnp.where(rest == v2, idx8, EXPERTS_PER_GROUP), axis=0, keepdims=True)
    r = jnp.exp(v2 - v1)
    gate1 = g_prob / (1.0 + r)
    gate2 = g_prob * r / (1.0 + r)
    ex1 = g_idx * EXPERTS_PER_GROUP + i1
    ex2 = g_idx * EXPERTS_PER_GROUP + i2
    ei_ref[...] = jnp.where(idx8 == 0, ex1, jnp.where(idx8 == 1, ex2, 0))
    gt_ref[...] = jnp.where(idx8 == 0, gate1, jnp.where(idx8 == 1, gate2, 0.0))


def _mix(x, o, lse, yb, sga, sgb, w_pa, w_pb, w_o, ln_g, ln_b, wr_hi, wr_lo, *, tm, alpha, dils):
    n = x.shape[0]

    def row_spec(width):
        return pl.BlockSpec((tm, width), lambda i: (i, 0))

    def class_spec(d):
        return pl.BlockSpec((d, tm // d, GROUP_WIDTH), lambda i: (0, i, 0))

    def full(a):
        return pl.BlockSpec(a.shape, lambda i: (0,) * a.ndim)

    lane_spec = pl.BlockSpec((8, tm), lambda i: (0, i))
    n_scratch = 2 * sum(1 for d in dils if d > 1)
    return pl.pallas_call(
        functools.partial(_mix_kernel, tm=tm, alpha=alpha, dils=dils),
        grid=(n // tm,),
        in_specs=[row_spec(D_MODEL)] + [class_spec(d) for d in dils] * 2 + [row_spec(CONV_CHANNELS)]
                 + [row_spec(D_MODEL)] * 2
                 + [full(w_pa), full(w_pb), full(w_o), full(ln_g), full(ln_b), full(wr_hi), full(wr_lo)],
        out_specs=[row_spec(D_MODEL), lane_spec, lane_spec],
        out_shape=[jax.ShapeDtypeStruct((n, D_MODEL), F32),
                   jax.ShapeDtypeStruct((8, n), I32),
                   jax.ShapeDtypeStruct((8, n), F32)],
        scratch_shapes=[pltpu.VMEM((GROUP_WIDTH // 128, tm, 128), F32)] * n_scratch,
        compiler_params=_params(),
        name="mix",
    )(x, o[0], o[1], o[2], lse[0], lse[1], lse[2], yb, sga, sgb, w_pa, w_pb, w_o, ln_g, ln_b, wr_hi, wr_lo)


def _slot_kernel(ei_ref, slot_ref, cnt_ref, carry_ref, start_ref, *, tl, bm):
    phase = pl.program_id(0)
    i = pl.program_id(1)

    @pl.when(jnp.logical_and(phase == 0, i == 0))
    def _():
        carry_ref[...] = jnp.zeros_like(carry_ref)

    ex = lax.broadcasted_iota(I32, (N_EXPERTS, tl), 0)
    oh0 = (ex == ei_ref[0:1, :]).astype(F32)
    oh1 = (ex == ei_ref[1:2, :]).astype(F32)
    cnt0 = jnp.sum(oh0, axis=1, keepdims=True)
    cnt1 = jnp.sum(oh1, axis=1, keepdims=True)

    @pl.when(phase == 0)
    def _():
        total = carry_ref[...] + cnt0 + cnt1
        carry_ref[...] = total
        cnt_ref[...] = total.astype(I32)
        slot_ref[...] = jnp.zeros_like(slot_ref)

    @pl.when(jnp.logical_and(phase == 1, i == 0))
    def _():
        blocks = jnp.floor((carry_ref[...] + (bm - 1)) * (1.0 / bm))
        a = lax.broadcasted_iota(I32, (N_EXPERTS, N_EXPERTS), 0)
        b = lax.broadcasted_iota(I32, (N_EXPERTS, N_EXPERTS), 1)
        before = (b < a).astype(BF16)
        start_ref[...] = jnp.dot(before, blocks.astype(BF16), preferred_element_type=F32) * bm
        carry_ref[...] = jnp.zeros_like(carry_ref)

    @pl.when(phase == 1)
    def _():
        a = lax.broadcasted_iota(I32, (tl, tl), 0)
        b = lax.broadcasted_iota(I32, (tl, tl), 1)
        upper = (a < b).astype(BF16)
        pre0 = jnp.dot(oh0.astype(BF16), upper, preferred_element_type=F32)
        pre1 = jnp.dot(oh1.astype(BF16), upper, preferred_element_type=F32)
        base = carry_ref[:, 0:1] + start_ref[:, 0:1]
        slot0 = jnp.sum(oh0 * (pre0 + base), axis=0, keepdims=True)
        slot1 = jnp.sum(oh1 * (pre1 + cnt0 + base), axis=0, keepdims=True)
        row = lax.broadcasted_iota(I32, (8, tl), 0)
        slot_ref[...] = jnp.where(row == 0, slot0.astype(I32), jnp.where(row == 1, slot1.astype(I32), 0))
        carry_ref[...] = carry_ref[...] + cnt0 + cnt1


def _slots(ei, *, tl, bm):
    n = ei.shape[1]
    return pl.pallas_call(
        functools.partial(_slot_kernel, tl=tl, bm=bm),
        grid=(2, n // tl),
        in_specs=[pl.BlockSpec((8, tl), lambda p, i: (0, i))],
        out_specs=[pl.BlockSpec((8, tl), lambda p, i: (0, i * p)),
                   pl.BlockSpec((N_EXPERTS, 128), lambda p, i: (0, 0))],
        out_shape=[jax.ShapeDtypeStruct((8, n), I32), jax.ShapeDtypeStruct((N_EXPERTS, 128), I32)],
        scratch_shapes=[pltpu.VMEM((N_EXPERTS, 128), F32), pltpu.VMEM((N_EXPERTS, 128), F32)],
        name="moe_slots",
    )(ei)


def _row_copy(src, src_row, dst, dst_row, sem):
    return pltpu.make_async_copy(src.at[pl.ds(src_row, 1)], dst.at[pl.ds(dst_row, 1)], sem)


ISSUE_UNROLL = 8


def _dispatch_kernel(slot_ref, x1_ref, buf_in_ref, buf_ref, sem, *, n, tm):
    del buf_in_ref
    base = pl.program_id(0) * tm

    def body(r, carry):
        for k in range(TOP_K):
            _row_copy(x1_ref, r, buf_ref, slot_ref[k * n + base + r], sem).start()
        return carry

    lax.fori_loop(0, tm, body, 0, unroll=ISSUE_UNROLL)
    for _ in range(TOP_K):
        pltpu.make_async_copy(x1_ref, buf_ref.at[pl.ds(0, tm)], sem).wait()


def _dispatch(slot_flat, x1, buf, *, tm):
    n = x1.shape[0]
    return pl.pallas_call(
        functools.partial(_dispatch_kernel, n=n, tm=tm),
        grid_spec=pltpu.PrefetchScalarGridSpec(
            num_scalar_prefetch=1,
            grid=(n // tm,),
            in_specs=[pl.BlockSpec((tm, D_MODEL), lambda i, s: (i, 0)), pl.BlockSpec(memory_space=pl.ANY)],
            out_specs=pl.BlockSpec(memory_space=pl.ANY),
            scratch_shapes=[pltpu.SemaphoreType.DMA(())],
        ),
        out_shape=jax.ShapeDtypeStruct(buf.shape, buf.dtype),
        input_output_aliases={2: 0},
        name="moe_dispatch",
    )(slot_flat, x1, buf)


def _expert_kernel(be_ref, nu_ref, xb_ref, wg_ref, wu_ref, wd_ref, out_ref, wg_bf, wu_bf, wd_bf):
    j = pl.program_id(0)
    used = j < nu_ref[0]
    changed = jnp.logical_or(j == 0, be_ref[j] != be_ref[jnp.maximum(j - 1, 0)])

    @pl.when(jnp.logical_and(used, changed))
    def _():
        wg_bf[...] = wg_ref[...].astype(BF16)
        wu_bf[...] = wu_ref[...].astype(BF16)
        wd_bf[...] = wd_ref[...].astype(BF16)

    @pl.when(used)
    def _():
        xb = xb_ref[...].astype(BF16)
        a = jnp.dot(xb, wg_bf[...], preferred_element_type=F32)
        b = jnp.dot(xb, wu_bf[...], preferred_element_type=F32)
        h = (a * _sigmoid(a)) * b
        out_ref[...] = jnp.dot(h.astype(BF16), wd_bf[...], preferred_element_type=F32)

    @pl.when(jnp.logical_not(used))
    def _():
        out_ref[...] = jnp.zeros_like(out_ref)


def _experts(block_expert, n_used, buf, w_g, w_u, w_d, *, bm):
    nblk = buf.shape[0] // bm

    def row_map(j, be, nu):
        return (jnp.minimum(j, nu[0] - 1), 0)

    def w_map(j, be, nu):
        return (be[j], 0, 0)

    return pl.pallas_call(
        _expert_kernel,
        grid_spec=pltpu.PrefetchScalarGridSpec(
            num_scalar_prefetch=2,
            grid=(nblk,),
            in_specs=[pl.BlockSpec((bm, D_MODEL), row_map),
                      pl.BlockSpec((None, D_MODEL, D_EXPERT), w_map),
                      pl.BlockSpec((None, D_MODEL, D_EXPERT), w_map),
                      pl.BlockSpec((None, D_EXPERT, D_MODEL), w_map)],
            out_specs=pl.BlockSpec((bm, D_MODEL), lambda j, be, nu: (j, 0)),
            scratch_shapes=[pltpu.VMEM((D_MODEL, D_EXPERT), BF16),
                            pltpu.VMEM((D_MODEL, D_EXPERT), BF16),
                            pltpu.VMEM((D_EXPERT, D_MODEL), BF16)],
        ),
        out_shape=jax.ShapeDtypeStruct(buf.shape, F32),
        compiler_params=_params(),
        name="moe_experts",
    )(block_expert, n_used, buf, w_g, w_u, w_d)


def _combine_kernel(slot_ref, x1_ref, gc_ref, g_ref, b_ref, eo_ref, y_ref, rows, sem, *, n, tm, alpha):
    base = pl.program_id(0) * tm

    def body(r, carry):
        for k in range(TOP_K):
            _row_copy(eo_ref, slot_ref[k * n + base + r], rows.at[k], r, sem).start()
        return carry

    lax.fori_loop(0, tm, body, 0, unroll=ISSUE_UNROLL)
    for k in range(TOP_K):
        pltpu.make_async_copy(eo_ref.at[pl.ds(0, tm)], rows.at[k], sem).wait()
    gc = gc_ref[...]
    z = alpha * x1_ref[...] + gc[:, 0:1] * rows[0] + gc[:, 1:2] * rows[1]
    mu = jnp.mean(z, axis=-1, keepdims=True)
    zc = z - mu
    var = jnp.mean(zc * zc, axis=-1, keepdims=True)
    y_ref[...] = zc * lax.rsqrt(var + LN_EPS) * g_ref[...] + b_ref[...]


def _combine(slot_flat, x1, gate_cols, ln_g, ln_b, expert_out, *, tm, alpha):
    n = x1.shape[0]
    return pl.pallas_call(
        functools.partial(_combine_kernel, n=n, tm=tm, alpha=alpha),
        grid_spec=pltpu.PrefetchScalarGridSpec(
            num_scalar_prefetch=1,
            grid=(n // tm,),
            in_specs=[pl.BlockSpec((tm, D_MODEL), lambda i, s: (i, 0)),
                      pl.BlockSpec((tm, TOP_K), lambda i, s: (i, 0)),
                      pl.BlockSpec((1, D_MODEL), lambda i, s: (0, 0)),
                      pl.BlockSpec((1, D_MODEL), lambda i, s: (0, 0)),
                      pl.BlockSpec(memory_space=pl.ANY)],
            out_specs=pl.BlockSpec((tm, D_MODEL), lambda i, s: (i, 0)),
            scratch_shapes=[pltpu.VMEM((TOP_K, tm, D_MODEL), F32), pltpu.SemaphoreType.DMA(())],
        ),
        out_shape=jax.ShapeDtypeStruct((n, D_MODEL), F32),
        compiler_params=_params(),
        name="moe_combine",
    )(slot_flat, x1, gate_cols, ln_g, ln_b, expert_out)


def _hier_moe_ln(x1, ei, gt, w_g, w_u, w_d, ln_g, ln_b, *, tl, tm, bm, alpha):
    n = x1.shape[0]
    m = n * TOP_K
    slot, cnt = _slots(ei, tl=tl, bm=bm)
    counts = cnt[:, 0]
    pend = jnp.cumsum((counts + bm - 1) // bm * bm)
    nblk = (m + N_EXPERTS * (bm - 1) + bm - 1) // bm
    blk_start = jnp.arange(nblk, dtype=I32) * bm
    n_used = (pend[-1] // bm).astype(I32)
    be = jnp.minimum(jnp.sum(pend[None, :] <= blk_start[:, None], axis=1), N_EXPERTS - 1).astype(I32)
    be = jnp.where(jnp.arange(nblk) < n_used, be, jnp.take(be, n_used - 1))
    slot_flat = slot[0:TOP_K].reshape(m)
    buf = _dispatch(slot_flat, x1, jnp.zeros((nblk * bm, D_MODEL), F32), tm=tm)
    eo = _experts(be, n_used.reshape(1), buf, w_g, w_u, w_d, bm=bm)
    gate_cols = gt[0:TOP_K].T
    return _combine(slot_flat, x1, gate_cols, ln_g, ln_b, eo, tm=tm, alpha=alpha)


def _t5_bucket(n):
    nf = jnp.maximum(n, 1).astype(F32)
    large = MAX_EXACT + (jnp.log(nf / MAX_EXACT) / math.log(MAX_DISTANCE / MAX_EXACT)
                         * (N_BUCKETS - MAX_EXACT)).astype(I32)
    large = jnp.minimum(large, N_BUCKETS - 1)
    return jnp.where(n < MAX_EXACT, n, large)


def _bias_per_group(rel_bias):
    offs = jnp.arange(N_KEYS, dtype=I32)[None, :] * jnp.array(DILATIONS, I32)[:, None]
    bucket = _t5_bucket(offs)
    table = rel_bias.reshape(N_BUCKETS, N_GROUPS, HEADS)
    b = table[bucket, jnp.arange(N_GROUPS)[:, None]]
    return jnp.transpose(b, (0, 2, 1)).astype(F32)


def _prompt_bias_tables(bias):
    width = 3 * Q_BLOCK
    neg = jnp.full((N_GROUPS, HEADS, Q_BLOCK - 1), NEG_INF, F32)
    r = jnp.concatenate([neg, bias[:, :, ::-1], neg, jnp.full((N_GROUPS, HEADS, 1), NEG_INF, F32)], axis=-1)
    flat = jnp.tile(r, (1, 1, Q_BLOCK))[:, :, :Q_BLOCK * (width - 1)]
    skew = flat.reshape(N_GROUPS, HEADS, Q_BLOCK, width - 1)
    later = skew[:, :, :, Q_BLOCK - 1:3 * Q_BLOCK - 1]
    has_prev = (np.arange(2 * Q_BLOCK) >= Q_BLOCK)[None, None, None, :]
    first = jnp.where(has_prev, later, NEG_INF)
    tb = jnp.stack([first, later], axis=1)
    return tb.reshape(N_GROUPS, 2, HEADS // 2, 2 * Q_BLOCK, 2 * Q_BLOCK)


def _sample_bias_tables(bias):
    bcs = []
    for g, d in enumerate(DILATIONS):
        rev = bias[g][:, ::-1][:, :WINDOW_KEYS]
        per_t = []
        for t in range(T_NEW):
            if d == 1:
                row = jnp.concatenate([jnp.full((HEADS, t), NEG_INF, F32), rev[:, :WINDOW_KEYS - t]], axis=1)
            else:
                cls = np.arange(d)[None, None, :] == t
                row = jnp.where(cls, rev[:, :, None], NEG_INF).reshape(HEADS, WINDOW_KEYS * d)
            per_t.append(row)
        bcs.append(jnp.stack(per_t, axis=1))
    bn = []
    for g, d in enumerate(DILATIONS):
        rows = []
        for t in range(T_NEW):
            cols = []
            for tn in range(T_NEW):
                ok = (tn <= t) if d == 1 else (tn == t)
                cols.append(bias[g][:, t - tn] if ok else jnp.full((HEADS,), NEG_INF, F32))
            rows.append(jnp.stack(cols, axis=-1))
        bn.append(jnp.stack(rows, axis=1))
    return bcs, jnp.stack(bn)


def _split_bf16(w):
    hi = w.astype(BF16)
    lo = (w - hi.astype(F32)).astype(BF16)
    return hi, lo


def kernel(x_prompt, x_sample, cache_attn_w128, cache_attn_w512, cache_attn_w2048, state_conv, rel_bias, w_in, w_conv, w_pa, w_pb, w_o, ln1_g, ln1_b, w_router_group, w_router_expert, w_expert_gate, w_expert_up, w_expert_down, ln2_g, ln2_b):
    depth = w_in.shape[0]
    assert depth == 1 and x_prompt.shape[0] == 1
    alpha = (2.0 * depth) ** 0.25
    s = x_prompt.shape[1]
    bd, t_len = x_sample.shape[0], x_sample.shape[1]
    assert t_len == T_NEW and s % (DILATIONS[-1] * Q_BLOCK) == 0

    bias = _bias_per_group(rel_bias)
    tb = _prompt_bias_tables(bias)
    bcs, bn = _sample_bias_tables(bias)

    w_in_bf = w_in[0].astype(BF16)
    w_pa_bf = w_pa[0].astype(BF16)
    w_pb_bf = w_pb[0].astype(BF16)
    w_o_bf = w_o[0].astype(BF16)
    wr = jnp.zeros((ROUTER_ROWS, D_MODEL), F32)
    wr = wr.at[0:N_EXPERT_GROUPS].set(w_router_group[0].T).at[8:8 + N_EXPERTS].set(w_router_expert[0].T)
    wr_hi, wr_lo = _split_bf16(wr)
    g1, b1 = ln1_g[0][None], ln1_b[0][None]
    g2, b2 = ln2_g[0][None], ln2_b[0][None]
    wg, wu, wd = w_expert_gate[0], w_expert_up[0], w_expert_down[0]

    xp = x_prompt[0]
    kv_tail = min(MAX_DISTANCE, s)
    q, kb, vb, (k32, v32, yb, sga, sgb, ut) = _proj(
        xp, w_in_bf, w_conv[0], None, tm=256, kv_tail=kv_tail, u_tail=8, q_dtype=BF16, dils=DILATIONS)
    o_l = [_attn_prompt_group(q[g], kb[g], vb[g], tb[g], g) for g in range(N_GROUPS)]
    x1, ei, gt = _mix(xp, [a[0] for a in o_l], [a[1] for a in o_l], yb, sga, sgb,
                      w_pa_bf, w_pb_bf, w_o_bf, g1, b1, wr_hi, wr_lo, tm=512, alpha=alpha, dils=DILATIONS)
    y_prompt = _hier_moe_ln(x1, ei, gt, wg, wu, wd, g2, b2, tl=512, tm=256, bm=256, alpha=alpha)[None]

    kv_prompt = []
    for g, d in enumerate(DILATIONS):
        length = min(WINDOW_KEYS * d, s)
        cols = slice(g * GROUP_WIDTH, (g + 1) * GROUP_WIDTH)
        kg = k32[kv_tail - length:, cols].reshape(length, HEADS, HEAD_DIM)
        vg = v32[kv_tail - length:, cols].reshape(length, HEADS, HEAD_DIM)
        kv_prompt.append(jnp.stack([kg, vg], axis=1)[None, None])
    conv_prompt = ut[6:8][None, None]

    ns = bd * t_len
    xs = x_sample.reshape(ns, D_MODEL)
    st = state_conv[0]
    s0 = jnp.repeat(st[:, 0], t_len, axis=0)
    s1 = jnp.repeat(st[:, 1], t_len, axis=0)
    qs, _, _, (k32s, v32s, ybs, sgas, sgbs, us) = _proj(
        xs, w_in_bf, w_conv[0], (s0, s1), tm=ns, kv_tail=ns, u_tail=ns, q_dtype=F32, dils=NO_DILATION)
    qs = jnp.concatenate([a[0] for a in qs], axis=1)
    packed = jnp.stack([qs, k32s, v32s]).reshape(3, bd, t_len, N_GROUPS, GROUP_WIDTH)
    qkv_t = jnp.transpose(packed, (1, 4, 0, 3, 2)).reshape(bd, GROUP_WIDTH, 3 * N_GROUPS * t_len)
    qkv_t = jnp.pad(qkv_t, ((0, 0), (0, 0), (0, 128 - 3 * N_GROUPS * t_len)))
    caches = (cache_attn_w128[0], cache_attn_w512[0], cache_attn_w2048[0])
    caches_t = [jnp.transpose(c, (0, 2, 3, 4, 1)).reshape(bd, 2, GROUP_WIDTH, c.shape[1]) for c in caches]
    pair = (bd, t_len, N_GROUPS, HEADS // 2, 2, HEAD_DIM)
    q6 = jnp.transpose(qs.reshape(pair), (0, 3, 2, 4, 1, 5))
    zeros = jnp.zeros_like(q6[:, :, :, 0])
    qbd = jnp.stack([jnp.concatenate([q6[:, :, :, 0], zeros], axis=-1),
                     jnp.concatenate([zeros, q6[:, :, :, 1]], axis=-1)], axis=3)
    qbd = qbd.reshape(bd, HEADS // 2, N_GROUPS, 2 * t_len, 128)
    new_rows = jnp.stack([k32s, v32s]).reshape(2, bd, t_len, N_GROUPS, HEADS // 2, 128)
    new_rows = jnp.transpose(new_rows, (1, 4, 3, 0, 2, 5))
    bcs = [t.reshape(HEADS // 2, 2 * t_len, t.shape[-1]) for t in bcs]
    bn = bn.reshape(N_GROUPS, HEADS // 2, 2 * t_len, t_len)
    n0, n1, n2, o_s, lse_s = _sample_cache(qbd, new_rows, qkv_t, caches_t, bcs, bn)

    def unpack(a):
        a = a[:, :N_GROUPS * t_len].reshape(bd, N_GROUPS, t_len, GROUP_WIDTH)
        return jnp.transpose(a, (1, 0, 2, 3)).reshape(N_GROUPS, 1, ns, GROUP_WIDTH)

    o_s, lse_s = unpack(o_s), unpack(lse_s)
    x1s, eis, gts = _mix(xs, o_s, lse_s, ybs, sgas, sgbs, w_pa_bf, w_pb_bf, w_o_bf, g1, b1, wr_hi, wr_lo,
                         tm=ns, alpha=alpha, dils=NO_DILATION)
    y_sample = _hier_moe_ln(x1s, eis, gts, wg, wu, wd, g2, b2, tl=ns, tm=ns, bm=128, alpha=alpha)
    y_sample = y_sample.reshape(bd, t_len, D_MODEL)

    kv_sample = [jnp.transpose(c.reshape(bd, 2, HEADS, HEAD_DIM, c.shape[-1]), (0, 4, 1, 2, 3))[None]
                 for c in (n0, n1, n2)]
    conv_sample = us.reshape(bd, t_len, CONV_CHANNELS)[:, t_len - 2:][None]

    return (y_prompt, y_sample, kv_prompt[0], kv_prompt[1], kv_prompt[2], conv_prompt,
            kv_sample[0], kv_sample[1], kv_sample[2], conv_sample)
```

```python
import functools
import math

import numpy as np
import jax
import jax.numpy as jnp
from jax import lax
from jax.experimental import pallas as pl
from jax.experimental.pallas import tpu as pltpu

F32 = jnp.float32
BF16 = jnp.bfloat16
I32 = jnp.int32

D_MODEL = 1024
N_GROUPS = 3
HEADS = 8
HEAD_DIM = 64
GROUP_WIDTH = HEADS * HEAD_DIM
ATTN_WIDTH = N_GROUPS * GROUP_WIDTH
DILATIONS = (1, 4, 16)
NO_DILATION = (1, 1, 1)
WINDOW_KEYS = 128
N_KEYS = WINDOW_KEYS + 1
N_BUCKETS = 32
MAX_EXACT = 16
MAX_DISTANCE = 2048
CONV_CHANNELS = 512
N_EXPERT_GROUPS = 4
EXPERTS_PER_GROUP = 8
N_EXPERTS = 32
TOP_K = 2
D_EXPERT = 512
LN_EPS = 1e-5
PROJ_WIDTH = 3 * ATTN_WIDTH + 3 * CONV_CHANNELS + 2 * D_MODEL
ROUTER_ROWS = 8 + N_EXPERTS
Q_BLOCK = 128
T_NEW = 4
NEG_INF = float("-inf")
VMEM_LIMIT = 56 * 1024 * 1024


def _sigmoid(x):
    return 1.0 / (1.0 + jnp.exp(-x))


def _params(limit=VMEM_LIMIT):
    return pltpu.CompilerParams(vmem_limit_bytes=limit)


def _proj_kernel(*refs, tm, tail_rows, sample_mode, dils):
    n_in = 5 if sample_mode else 3
    x_ref, w_ref, wc_ref = refs[0:3]
    outs = refs[n_in:]
    q_refs, k_refs, v_refs = outs[0:3], outs[3:6], outs[6:9]
    k32_ref, v32_ref, yb_ref, sga_ref, sgb_ref, ut_ref, cls_ref = outs[9:16]
    xb = x_ref[...].astype(BF16)

    def col(c0, width):
        return jnp.dot(xb, w_ref[:, c0:c0 + width], preferred_element_type=F32)

    def write_classes(val, group_refs):
        for g, d in enumerate(dils):
            part = val[:, g * GROUP_WIDTH:(g + 1) * GROUP_WIDTH]
            ref = group_refs[g]
            if d == 1:
                ref[0] = part.astype(ref.dtype)
            else:
                for kk in range(GROUP_WIDTH // 128):
                    lanes = slice(kk * 128, (kk + 1) * 128)
                    cls_ref[kk] = part[:, lanes]
                    for c in range(d):
                        ref[c, :, lanes] = cls_ref[kk, pl.ds(c, tm // d, stride=d), :].astype(ref.dtype)

    write_classes(col(0, ATTN_WIDTH), q_refs)
    k = col(ATTN_WIDTH, ATTN_WIDTH)
    k32_ref[...] = k
    write_classes(k, k_refs)
    v = col(2 * ATTN_WIDTH, ATTN_WIDTH)
    v32_ref[...] = v
    write_classes(v, v_refs)

    c0 = 3 * ATTN_WIDTH
    bg = col(c0, CONV_CHANNELS)
    u = col(c0 + CONV_CHANNELS, CONV_CHANNELS) * col(c0 + 2 * CONV_CHANNELS, CONV_CHANNELS)
    row = lax.broadcasted_iota(I32, (tm, CONV_CHANNELS), 0)
    r1 = pltpu.roll(u, 1, axis=0)
    r2 = pltpu.roll(u, 2, axis=0)
    if sample_mode:
        s0 = refs[3][...]
        s1 = refs[4][...]
        t = row & (T_NEW - 1)
        prev1 = jnp.where(t == 0, s1, r1)
        prev2 = jnp.where(t == 0, s0, jnp.where(t == 1, s1, r2))
    else:
        carry_ref = outs[16]

        @pl.when(pl.program_id(0) == 0)
        def _():
            carry_ref[...] = jnp.zeros_like(carry_ref)
        c6 = carry_ref[6:7, :]
        c7 = carry_ref[7:8, :]
        prev1 = jnp.where(row == 0, c7, r1)
        prev2 = jnp.where(row == 0, c6, jnp.where(row == 1, c7, r2))
        carry_ref[...] = u[tm - 8:tm, :]
    conv = prev2 * wc_ref[0:1, :] + prev1 * wc_ref[1:2, :] + u * wc_ref[2:3, :]
    yb_ref[...] = (bg * conv).astype(BF16)
    ut_ref[...] = u[tm - tail_rows:tm, :]

    c1 = c0 + 3 * CONV_CHANNELS
    sga_ref[...] = _sigmoid(col(c1, D_MODEL)).astype(BF16)
    sgb_ref[...] = _sigmoid(col(c1 + D_MODEL, D_MODEL)).astype(BF16)


def _proj(x, w_in_bf, w_conv, conv_prev, *, tm, kv_tail, u_tail, q_dtype, dils):
    n = x.shape[0]
    sample_mode = conv_prev is not None
    nt = n // tm
    tail_first = (n - kv_tail) // tm

    def row_spec(width):
        return pl.BlockSpec((tm, width), lambda i: (i, 0))

    def tail_spec(width):
        return pl.BlockSpec((tm, width), lambda i: (jnp.maximum(i - tail_first, 0), 0))

    def class_spec(d):
        return pl.BlockSpec((d, tm // d, GROUP_WIDTH), lambda i: (0, i, 0))

    def class_shape(d, dtype):
        return jax.ShapeDtypeStruct((d, n // d, GROUP_WIDTH), dtype)

    in_specs = [
        row_spec(D_MODEL),
        pl.BlockSpec((D_MODEL, PROJ_WIDTH), lambda i: (0, 0), pipeline_mode=pl.Buffered(1)),
        pl.BlockSpec((3, CONV_CHANNELS), lambda i: (0, 0)),
    ]
    args = [x, w_in_bf, w_conv]
    scratch = [pltpu.VMEM((GROUP_WIDTH // 128, tm, 128), F32)]
    if sample_mode:
        in_specs += [row_spec(CONV_CHANNELS), row_spec(CONV_CHANNELS)]
        args += [conv_prev[0], conv_prev[1]]
    else:
        scratch.append(pltpu.VMEM((8, CONV_CHANNELS), F32))
    out_shape = (
        [class_shape(d, q_dtype) for d in dils] + [class_shape(d, BF16) for d in dils] * 2
        + [jax.ShapeDtypeStruct((kv_tail, ATTN_WIDTH), F32),
           jax.ShapeDtypeStruct((kv_tail, ATTN_WIDTH), F32),
           jax.ShapeDtypeStruct((n, CONV_CHANNELS), BF16),
           jax.ShapeDtypeStruct((n, D_MODEL), BF16),
           jax.ShapeDtypeStruct((n, D_MODEL), BF16),
           jax.ShapeDtypeStruct((u_tail, CONV_CHANNELS), F32)])
    out_specs = (
        [class_spec(d) for d in dils] * 3
        + [tail_spec(ATTN_WIDTH), tail_spec(ATTN_WIDTH),
           row_spec(CONV_CHANNELS), row_spec(D_MODEL), row_spec(D_MODEL),
           pl.BlockSpec((u_tail, CONV_CHANNELS), lambda i: (0, 0))])
    res = pl.pallas_call(
        functools.partial(_proj_kernel, tm=tm, tail_rows=u_tail, sample_mode=sample_mode, dils=dils),
        grid=(nt,),
        in_specs=in_specs,
        out_specs=out_specs,
        out_shape=out_shape,
        scratch_shapes=scratch,
        compiler_params=_params(),
        name="proj",
    )(*args)
    return res[0:3], res[3:6], res[6:9], res[9:]


def _attn_kernel(q_ref, kp_ref, kc_ref, vp_ref, vc_ref, tb_ref, o_ref, lse_ref):
    lane = lax.broadcasted_iota(I32, (Q_BLOCK, 128), 1)
    first = lane < HEAD_DIM
    for pr in range(HEADS // 2):
        sl = slice(pr * 128, (pr + 1) * 128)
        q = q_ref[:, sl]
        k = jnp.concatenate([kp_ref[:, sl], kc_ref[:, sl]], axis=0)
        v = jnp.concatenate([vp_ref[:, sl], vc_ref[:, sl]], axis=0)
        qf = q.astype(F32)
        qq = jnp.concatenate([jnp.where(first, qf, 0.0), jnp.where(first, 0.0, qf)], axis=0).astype(BF16)
        s = lax.dot_general(qq, k, (((1,), (1,)), ((), ())), preferred_element_type=F32)
        s = s * (HEAD_DIM ** -0.5) + tb_ref[0, pr]
        m = jnp.max(s, axis=-1, keepdims=True)
        p = jnp.exp(s - m)
        l = jnp.sum(p, axis=-1, keepdims=True)
        o = jnp.dot(p.astype(BF16), v, preferred_element_type=F32) / l
        lse = m + jnp.log(l)
        o_ref[:, sl] = jnp.where(first, o[:Q_BLOCK], o[Q_BLOCK:]).astype(o_ref.dtype)
        lse_ref[:, sl] = jnp.where(first, jnp.broadcast_to(lse[:Q_BLOCK], (Q_BLOCK, 128)),
                                   jnp.broadcast_to(lse[Q_BLOCK:], (Q_BLOCK, 128)))


def _attn_prompt_group(q, kb, vb, tb, g):
    d, rows = q.shape[0], q.shape[1]
    cur = pl.BlockSpec((None, Q_BLOCK, GROUP_WIDTH), lambda c, i: (c, i, 0))
    prev = pl.BlockSpec((None, Q_BLOCK, GROUP_WIDTH), lambda c, i: (c, jnp.maximum(i - 1, 0), 0))
    return pl.pallas_call(
        _attn_kernel,
        grid=(d, rows // Q_BLOCK),
        in_specs=[cur, prev, cur, prev, cur,
                  pl.BlockSpec((1, HEADS // 2, 2 * Q_BLOCK, 2 * Q_BLOCK), lambda c, i: (jnp.minimum(i, 1), 0, 0, 0))],
        out_specs=[cur, cur],
        out_shape=[jax.ShapeDtypeStruct((d, rows, GROUP_WIDTH), BF16),
                   jax.ShapeDtypeStruct((d, rows, GROUP_WIDTH), F32)],
        compiler_params=_params(),
        name=f"attn_prompt_g{g}",
    )(q, kb, kb, vb, vb, tb)


PACK_Q, PACK_K, PACK_V = 0, N_GROUPS * T_NEW, 2 * N_GROUPS * T_NEW


def _sample_cache_kernel(qbd_ref, nr_ref, qkv_ref, c0_ref, c1_ref, c2_ref, b0_ref, b1_ref, b2_ref, bn_ref,
                         n0_ref, n1_ref, n2_ref, o_ref, lse_ref):
    scale = HEAD_DIM ** -0.5
    nt = (((1,), (1,)), ((), ()))
    lane = lax.broadcasted_iota(I32, (128, 128), 1)
    head0 = lane[0:T_NEW] < HEAD_DIM
    groups = ((c0_ref, b0_ref, n0_ref), (c1_ref, b1_ref, n1_ref), (c2_ref, b2_ref, n2_ref))
    for g, (c_ref, b_ref, n_ref) in enumerate(groups):
        length = c_ref.shape[-1]
        k_new = qkv_ref[:, PACK_K + g * T_NEW:PACK_K + (g + 1) * T_NEW]
        v_new = qkv_ref[:, PACK_V + g * T_NEW:PACK_V + (g + 1) * T_NEW]
        qbd = qbd_ref[g]
        s_c = jnp.dot(qbd.astype(BF16), c_ref[0].astype(BF16), preferred_element_type=F32) * scale + b_ref[...]
        bn = bn_ref[g]
        s_n = [jnp.sum(qbd * nr_ref[g, 0, tn:tn + 1, :], axis=1, keepdims=True) * scale + bn[:, tn:tn + 1]
               for tn in range(T_NEW)]
        m = jnp.max(s_c, axis=1, keepdims=True)
        for x in s_n:
            m = jnp.maximum(m, x)
        p_c = jnp.exp(s_c - m)
        p_n = [jnp.exp(x - m) for x in s_n]
        l = jnp.sum(p_c, axis=1, keepdims=True)
        acc = lax.dot_general(p_c.astype(BF16), c_ref[1].astype(BF16), nt, preferred_element_type=F32)
        for tn in range(T_NEW):
            l = l + p_n[tn]
            acc = acc + p_n[tn] * nr_ref[g, 1, tn:tn + 1, :]
        o = acc / l
        lse = jnp.broadcast_to(m + jnp.log(l), (2 * T_NEW, 128))
        o_ref[g * T_NEW:(g + 1) * T_NEW, :] = jnp.where(head0, o[0:T_NEW], o[T_NEW:])
        lse_ref[g * T_NEW:(g + 1) * T_NEW, :] = jnp.where(head0, lse[0:T_NEW], lse[T_NEW:])

        for kv, new in ((0, k_new), (1, v_new)):
            rolled = pltpu.roll(c_ref[kv], length - T_NEW, axis=1)
            tail = rolled[:, length - 128:]
            for t in range(T_NEW):
                tail = jnp.where(lane == 128 - T_NEW + t, new[:, t:t + 1], tail)
            if length > 128:
                n_ref[kv, :, 0:length - 128] = rolled[:, 0:length - 128]
            n_ref[kv, :, length - 128:] = tail
    pad_rows = slice(N_GROUPS * T_NEW, 16)
    o_ref[pad_rows, :] = jnp.zeros((16 - N_GROUPS * T_NEW, 128), F32)
    lse_ref[pad_rows, :] = jnp.zeros((16 - N_GROUPS * T_NEW, 128), F32)


def _sample_cache(qbd, new_rows, qkv_t, caches_t, bcs, bn):
    b = qkv_t.shape[0]

    def cache_spec(c):
        return pl.BlockSpec((None, 2, 128, c.shape[-1]), lambda i, h: (i, 0, h, 0))

    def bias_spec(t):
        return pl.BlockSpec((None, 2 * T_NEW, t.shape[-1]), lambda i, h: (h, 0, 0))

    out = pl.BlockSpec((None, 16, 128), lambda i, h: (i, 0, h))
    return pl.pallas_call(
        _sample_cache_kernel,
        grid=(b, HEADS // 2),
        in_specs=[pl.BlockSpec((None, None, N_GROUPS, 2 * T_NEW, 128), lambda i, h: (i, h, 0, 0, 0)),
                  pl.BlockSpec((None, None, N_GROUPS, 2, T_NEW, 128), lambda i, h: (i, h, 0, 0, 0, 0)),
                  pl.BlockSpec((None, 128, 128), lambda i, h: (i, h, 0))]
                 + [cache_spec(c) for c in caches_t] + [bias_spec(t) for t in bcs]
                 + [pl.BlockSpec((N_GROUPS, None, 2 * T_NEW, T_NEW), lambda i, h: (0, h, 0, 0))],
        out_specs=[cache_spec(c) for c in caches_t] + [out, out],
        out_shape=[jax.ShapeDtypeStruct(c.shape, c.dtype) for c in caches_t]
                  + [jax.ShapeDtypeStruct((b, 16, GROUP_WIDTH), F32)] * 2,
        compiler_params=_params(),
        name="sample_cache",
    )(qbd, new_rows, qkv_t, *caches_t, *bcs, bn)


def _mix_kernel(*refs, tm, alpha, dils):
    (x_ref, o0_ref, o1_ref, o2_ref, l0_ref, l1_ref, l2_ref, yb_ref, sga_ref, sgb_ref,
     wpa_ref, wpb_ref, wo_ref, g_ref, b_ref, wrh_ref, wrl_ref, x1_ref, ei_ref, gt_ref) = refs[0:20]
    scratch = list(refs[20:])

    def natural(ref, d):
        if d == 1:
            return ref[0].astype(F32)
        scr = scratch.pop()
        for kk in range(GROUP_WIDTH // 128):
            for c in range(d):
                scr[kk, pl.ds(c, tm // d, stride=d), :] = ref[c, :, kk * 128:(kk + 1) * 128].astype(F32)
        return jnp.concatenate([scr[kk] for kk in range(GROUP_WIDTH // 128)], axis=1)

    l0, l1, l2 = natural(l0_ref, dils[0]), natural(l1_ref, dils[1]), natural(l2_ref, dils[2])
    mx = jnp.maximum(jnp.maximum(l0, l1), l2)
    e0 = jnp.exp(l0 - mx)
    e1 = jnp.exp(l1 - mx)
    e2 = jnp.exp(l2 - mx)
    ya = (e0 * natural(o0_ref, dils[0]) + e1 * natural(o1_ref, dils[1]) + e2 * natural(o2_ref, dils[2])) / (e0 + e1 + e2)
    pa = jnp.dot(ya.astype(BF16), wpa_ref[...], preferred_element_type=F32)
    pb = jnp.dot(yb_ref[...], wpb_ref[...], preferred_element_type=F32)
    gated = sga_ref[...].astype(F32) * pa + sgb_ref[...].astype(F32) * pb
    mix = jnp.dot(gated.astype(BF16), wo_ref[...], preferred_element_type=F32)
    z = alpha * x_ref[...] + mix
    mu = jnp.mean(z, axis=-1, keepdims=True)
    zc = z - mu
    var = jnp.mean(zc * zc, axis=-1, keepdims=True)
    x1 = zc * lax.rsqrt(var + LN_EPS) * g_ref[...] + b_ref[...]
    x1_ref[...] = x1

    xh = x1.astype(BF16)
    xl = (x1 - xh.astype(F32)).astype(BF16)
    nt = (((1,), (1,)), ((), ()))
    wrh = wrh_ref[...]
    lt = (lax.dot_general(wrh, xh, nt, preferred_element_type=F32)
          + lax.dot_general(wrh, xl, nt, preferred_element_type=F32)
          + lax.dot_general(wrl_ref[...], xh, nt, preferred_element_type=F32))

    gl = lt[0:N_EXPERT_GROUPS]
    gmax = jnp.max(gl, axis=0, keepdims=True)
    idx4 = lax.broadcasted_iota(I32, (N_EXPERT_GROUPS, tm), 0)
    g_idx = jnp.min(jnp.where(gl == gmax, idx4, N_EXPERT_GROUPS), axis=0, keepdims=True)
    g_prob = 1.0 / jnp.sum(jnp.exp(gl - gmax), axis=0, keepdims=True)
    e_sel = lt[8:16]
    for grp in range(1, N_EXPERT_GROUPS):
        e_sel = jnp.where(g_idx == grp, lt[8 + 8 * grp:16 + 8 * grp], e_sel)
    idx8 = lax.broadcasted_iota(I32, (EXPERTS_PER_GROUP, tm), 0)
    v1 = jnp.max(e_sel, axis=0, keepdims=True)
    i1 = jnp.min(jnp.where(e_sel == v1, idx8, EXPERTS_PER_GROUP), axis=0, keepdims=True)
    rest = jnp.where(idx8 == i1, NEG_INF, e_sel)
    v2 = jnp.max(rest, axis=0, keepdims=True)
    i2 = jnp.min(jnp.where(rest == v2, idx8, EXPERTS_PER_GROUP), axis=0, keepdims=True)
    r = jnp.exp(v2 - v1)
    gate1 = g_prob / (1.0 + r)
    gate2 = g_prob * r / (1.0 + r)
    ex1 = g_idx * EXPERTS_PER_GROUP + i1
    ex2 = g_idx * EXPERTS_PER_GROUP + i2
    ei_ref[...] = jnp.where(idx8 == 0, ex1, jnp.where(idx8 == 1, ex2, 0))
    gt_ref[...] = jnp.where(idx8 == 0, gate1, jnp.where(idx8 == 1, gate2, 0.0))


def _mix(x, o, lse, yb, sga, sgb, w_pa, w_pb, w_o, ln_g, ln_b, wr_hi, wr_lo, *, tm, alpha, dils):
    n = x.shape[0]

    def row_spec(width):
        return pl.BlockSpec((tm, width), lambda i: (i, 0))

    def class_spec(d):
        return pl.BlockSpec((d, tm // d, GROUP_WIDTH), lambda i: (0, i, 0))

    def full(a):
        return pl.BlockSpec(a.shape, lambda i: (0,) * a.ndim)

    lane_spec = pl.BlockSpec((8, tm), lambda i: (0, i))
    n_scratch = 2 * sum(1 for d in dils if d > 1)
    return pl.pallas_call(
        functools.partial(_mix_kernel, tm=tm, alpha=alpha, dils=dils),
        grid=(n // tm,),
        in_specs=[row_spec(D_MODEL)] + [class_spec(d) for d in dils] * 2 + [row_spec(CONV_CHANNELS)]
                 + [row_spec(D_MODEL)] * 2
                 + [full(w_pa), full(w_pb), full(w_o), full(ln_g), full(ln_b), full(wr_hi), full(wr_lo)],
        out_specs=[row_spec(D_MODEL), lane_spec, lane_spec],
        out_shape=[jax.ShapeDtypeStruct((n, D_MODEL), F32),
                   jax.ShapeDtypeStruct((8, n), I32),
                   jax.ShapeDtypeStruct((8, n), F32)],
        scratch_shapes=[pltpu.VMEM((GROUP_WIDTH // 128, tm, 128), F32)] * n_scratch,
        compiler_params=_params(),
        name="mix",
    )(x, o[0], o[1], o[2], lse[0], lse[1], lse[2], yb, sga, sgb, w_pa, w_pb, w_o, ln_g, ln_b, wr_hi, wr_lo)


def _slot_kernel(ei_ref, slot_ref, cnt_ref, carry_ref, start_ref, *, tl, bm):
    phase = pl.program_id(0)
    i = pl.program_id(1)

    @pl.when(jnp.logical_and(phase == 0, i == 0))
    def _():
        carry_ref[...] = jnp.zeros_like(carry_ref)

    ex = lax.broadcasted_iota(I32, (N_EXPERTS, tl), 0)
    oh0 = (ex == ei_ref[0:1, :]).astype(F32)
    oh1 = (ex == ei_ref[1:2, :]).astype(F32)
    cnt0 = jnp.sum(oh0, axis=1, keepdims=True)
    cnt1 = jnp.sum(oh1, axis=1, keepdims=True)

    @pl.when(phase == 0)
    def _():
        total = carry_ref[...] + cnt0 + cnt1
        carry_ref[...] = total
        cnt_ref[...] = total.astype(I32)
        slot_ref[...] = jnp.zeros_like(slot_ref)

    @pl.when(jnp.logical_and(phase == 1, i == 0))
    def _():
        blocks = jnp.floor((carry_ref[...] + (bm - 1)) * (1.0 / bm))
        a = lax.broadcasted_iota(I32, (N_EXPERTS, N_EXPERTS), 0)
        b = lax.broadcasted_iota(I32, (N_EXPERTS, N_EXPERTS), 1)
        before = (b < a).astype(BF16)
        start_ref[...] = jnp.dot(before, blocks.astype(BF16), preferred_element_type=F32) * bm
        carry_ref[...] = jnp.zeros_like(carry_ref)

    @pl.when(phase == 1)
    def _():
        a = lax.broadcasted_iota(I32, (tl, tl), 0)
        b = lax.broadcasted_iota(I32, (tl, tl), 1)
        upper = (a < b).astype(BF16)
        pre0 = jnp.dot(oh0.astype(BF16), upper, preferred_element_type=F32)
        pre1 = jnp.dot(oh1.astype(BF16), upper, preferred_element_type=F32)
        base = carry_ref[:, 0:1] + start_ref[:, 0:1]
        slot0 = jnp.sum(oh0 * (pre0 + base), axis=0, keepdims=True)
        slot1 = jnp.sum(oh1 * (pre1 + cnt0 + base), axis=0, keepdims=True)
        row = lax.broadcasted_iota(I32, (8, tl), 0)
        slot_ref[...] = jnp.where(row == 0, slot0.astype(I32), jnp.where(row == 1, slot1.astype(I32), 0))
        carry_ref[...] = carry_ref[...] + cnt0 + cnt1


def _slots(ei, *, tl, bm):
    n = ei.shape[1]
    return pl.pallas_call(
        functools.partial(_slot_kernel, tl=tl, bm=bm),
        grid=(2, n // tl),
        in_specs=[pl.BlockSpec((8, tl), lambda p, i: (0, i))],
        out_specs=[pl.BlockSpec((8, tl), lambda p, i: (0, i * p)),
                   pl.BlockSpec((N_EXPERTS, 128), lambda p, i: (0, 0))],
        out_shape=[jax.ShapeDtypeStruct((8, n), I32), jax.ShapeDtypeStruct((N_EXPERTS, 128), I32)],
        scratch_shapes=[pltpu.VMEM((N_EXPERTS, 128), F32), pltpu.VMEM((N_EXPERTS, 128), F32)],
        name="moe_slots",
    )(ei)


def _row_copy(src, src_row, dst, dst_row, sem):
    return pltpu.make_async_copy(src.at[pl.ds(src_row, 1)], dst.at[pl.ds(dst_row, 1)], sem)


ISSUE_UNROLL = 8


def _expert_kernel(be_ref, nu_ref, slot_ref, x1_ref, wg_ref, wu_ref, wd_ref, out_ref,
                   wg_bf, wu_bf, wd_bf, xbuf, tok_of_slot, sem, *, n, bm):
    j = pl.program_id(0)
    n_used = nu_ref[0]
    used = j < n_used
    cur = j % 2

    def gather(block, buf, unrolled):
        def start(r):
            _row_copy(x1_ref, tok_of_slot[block * bm + r], xbuf.at[buf], r, sem.at[buf]).start()
        if unrolled:
            for r in range(bm):
                start(r)
        else:
            def body(r, c):
                start(r)
                return c
            lax.fori_loop(0, bm, body, 0, unroll=ISSUE_UNROLL)

    def wait(buf):
        pltpu.make_async_copy(x1_ref.at[pl.ds(0, bm)], xbuf.at[buf], sem.at[buf]).wait()

    @pl.when(j == 0)
    def _():
        def clear(s, c):
            tok_of_slot[s] = 0
            return c
        lax.fori_loop(0, tok_of_slot.shape[0], clear, 0, unroll=ISSUE_UNROLL)
        for k in range(TOP_K):
            def put(t, c, k=k):
                tok_of_slot[slot_ref[k * n + t]] = t
                return c
            lax.fori_loop(0, n, put, 0, unroll=ISSUE_UNROLL)
        gather(0, 0, unrolled=False)

    changed = jnp.logical_or(j == 0, be_ref[j] != be_ref[jnp.maximum(j - 1, 0)])

    @pl.when(jnp.logical_and(used, changed))
    def _():
        wg_bf[...] = wg_ref[...].astype(BF16)
        wu_bf[...] = wu_ref[...].astype(BF16)
        wd_bf[...] = wd_ref[...].astype(BF16)

    @pl.when(used)
    def _():
        wait(cur)
        gather(jnp.minimum(j + 1, n_used - 1), 1 - cur, unrolled=True)
        xb = xbuf[cur].astype(BF16)
        a = jnp.dot(xb, wg_bf[...], preferred_element_type=F32)
        b = jnp.dot(xb, wu_bf[...], preferred_element_type=F32)
        h = (a * _sigmoid(a)) * b
        out_ref[...] = jnp.dot(h.astype(BF16), wd_bf[...], preferred_element_type=F32)

    @pl.when(j == n_used - 1)
    def _():
        wait(1 - cur)

    @pl.when(jnp.logical_not(used))
    def _():
        out_ref[...] = jnp.zeros_like(out_ref)


def _experts(block_expert, n_used, slot_flat, x1, w_g, w_u, w_d, *, bm, nblk):
    n = x1.shape[0]

    def w_map(j, be, nu, sl):
        return (be[j], 0, 0)

    return pl.pallas_call(
        functools.partial(_expert_kernel, n=n, bm=bm),
        grid_spec=pltpu.PrefetchScalarGridSpec(
            num_scalar_prefetch=3,
            grid=(nblk,),
            in_specs=[pl.BlockSpec(memory_space=pl.ANY),
                      pl.BlockSpec((None, D_MODEL, D_EXPERT), w_map),
                      pl.BlockSpec((None, D_MODEL, D_EXPERT), w_map),
                      pl.BlockSpec((None, D_EXPERT, D_MODEL), w_map)],
            out_specs=pl.BlockSpec((bm, D_MODEL), lambda j, be, nu, sl: (j, 0)),
            scratch_shapes=[pltpu.VMEM((D_MODEL, D_EXPERT), BF16),
                            pltpu.VMEM((D_MODEL, D_EXPERT), BF16),
                            pltpu.VMEM((D_EXPERT, D_MODEL), BF16),
                            pltpu.VMEM((2, bm, D_MODEL), F32),
                            pltpu.SMEM((nblk * bm,), I32),
                            pltpu.SemaphoreType.DMA((2,))],
        ),
        out_shape=jax.ShapeDtypeStruct((nblk * bm, D_MODEL), F32),
        compiler_params=_params(),
        name="moe_experts",
    )(block_expert, n_used, slot_flat, x1, w_g, w_u, w_d)


def _combine_kernel(slot_ref, x1_ref, gc_ref, g_ref, b_ref, eo_ref, y_ref, rows, sem, *, n, tm, alpha):
    i = pl.program_id(0)
    last = pl.num_programs(0) - 1
    cur = i % 2

    def start(tile, buf, r):
        for k in range(TOP_K):
            _row_copy(eo_ref, slot_ref[k * n + tile * tm + r], rows.at[buf, k], r, sem.at[buf]).start()

    def wait(buf):
        for k in range(TOP_K):
            pltpu.make_async_copy(eo_ref.at[pl.ds(0, tm)], rows.at[buf, k], sem.at[buf]).wait()

    @pl.when(i == 0)
    def _():
        def body(r, c):
            start(0, 0, r)
            return c
        lax.fori_loop(0, tm, body, 0, unroll=ISSUE_UNROLL)

    wait(cur)
    nxt = jnp.minimum(i + 1, last)
    for r in range(tm):
        start(nxt, 1 - cur, r)
    gc = gc_ref[...]
    z = alpha * x1_ref[...] + gc[:, 0:1] * rows[cur, 0] + gc[:, 1:2] * rows[cur, 1]
    mu = jnp.mean(z, axis=-1, keepdims=True)
    zc = z - mu
    var = jnp.mean(zc * zc, axis=-1, keepdims=True)
    y_ref[...] = zc * lax.rsqrt(var + LN_EPS) * g_ref[...] + b_ref[...]

    @pl.when(i == last)
    def _():
        wait(1 - cur)


def _combine(slot_flat, x1, gate_cols, ln_g, ln_b, expert_out, *, tm, alpha):
    n = x1.shape[0]
    return pl.pallas_call(
        functools.partial(_combine_kernel, n=n, tm=tm, alpha=alpha),
        grid_spec=pltpu.PrefetchScalarGridSpec(
            num_scalar_prefetch=1,
            grid=(n // tm,),
            in_specs=[pl.BlockSpec((tm, D_MODEL), lambda i, s: (i, 0)),
                      pl.BlockSpec((tm, TOP_K), lambda i, s: (i, 0)),
                      pl.BlockSpec((1, D_MODEL), lambda i, s: (0, 0)),
                      pl.BlockSpec((1, D_MODEL), lambda i, s: (0, 0)),
                      pl.BlockSpec(memory_space=pl.ANY)],
            out_specs=pl.BlockSpec((tm, D_MODEL), lambda i, s: (i, 0)),
            scratch_shapes=[pltpu.VMEM((2, TOP_K, tm, D_MODEL), F32), pltpu.SemaphoreType.DMA((2,))],
        ),
        out_shape=jax.ShapeDtypeStruct((n, D_MODEL), F32),
        compiler_params=_params(),
        name="moe_combine",
    )(slot_flat, x1, gate_cols, ln_g, ln_b, expert_out)


def _hier_moe_ln(x1, ei, gt, w_g, w_u, w_d, ln_g, ln_b, *, tl, tm, bm, alpha):
    n = x1.shape[0]
    m = n * TOP_K
    slot, cnt = _slots(ei, tl=tl, bm=bm)
    counts = cnt[:, 0]
    pend = jnp.cumsum((counts + bm - 1) // bm * bm)
    nblk = (m + N_EXPERTS * (bm - 1) + bm - 1) // bm
    blk_start = jnp.arange(nblk, dtype=I32) * bm
    n_used = (pend[-1] // bm).astype(I32)
    be = jnp.minimum(jnp.sum(pend[None, :] <= blk_start[:, None], axis=1), N_EXPERTS - 1).astype(I32)
    be = jnp.where(jnp.arange(nblk) < n_used, be, jnp.take(be, n_used - 1))
    slot_flat = slot[0:TOP_K].reshape(m)
    eo = _experts(be, n_used.reshape(1), slot_flat, x1, w_g, w_u, w_d, bm=bm, nblk=nblk)
    gate_cols = gt[0:TOP_K].T
    return _combine(slot_flat, x1, gate_cols, ln_g, ln_b, eo, tm=tm, alpha=alpha)


def _t5_bucket(n):
    nf = jnp.maximum(n, 1).astype(F32)
    large = MAX_EXACT + (jnp.log(nf / MAX_EXACT) / math.log(MAX_DISTANCE / MAX_EXACT)
                         * (N_BUCKETS - MAX_EXACT)).astype(I32)
    large = jnp.minimum(large, N_BUCKETS - 1)
    return jnp.where(n < MAX_EXACT, n, large)


def _bias_per_group(rel_bias):
    offs = jnp.arange(N_KEYS, dtype=I32)[None, :] * jnp.array(DILATIONS, I32)[:, None]
    bucket = _t5_bucket(offs)
    table = rel_bias.reshape(N_BUCKETS, N_GROUPS, HEADS)
    b = table[bucket, jnp.arange(N_GROUPS)[:, None]]
    return jnp.transpose(b, (0, 2, 1)).astype(F32)


def _prompt_bias_tables(bias):
    width = 3 * Q_BLOCK
    neg = jnp.full((N_GROUPS, HEADS, Q_BLOCK - 1), NEG_INF, F32)
    r = jnp.concatenate([neg, bias[:, :, ::-1], neg, jnp.full((N_GROUPS, HEADS, 1), NEG_INF, F32)], axis=-1)
    flat = jnp.tile(r, (1, 1, Q_BLOCK))[:, :, :Q_BLOCK * (width - 1)]
    skew = flat.reshape(N_GROUPS, HEADS, Q_BLOCK, width - 1)
    later = skew[:, :, :, Q_BLOCK - 1:3 * Q_BLOCK - 1]
    has_prev = (np.arange(2 * Q_BLOCK) >= Q_BLOCK)[None, None, None, :]
    first = jnp.where(has_prev, later, NEG_INF)
    tb = jnp.stack([first, later], axis=1)
    return tb.reshape(N_GROUPS, 2, HEADS // 2, 2 * Q_BLOCK, 2 * Q_BLOCK)


def _sample_bias_tables(bias):
    bcs = []
    for g, d in enumerate(DILATIONS):
        rev = bias[g][:, ::-1][:, :WINDOW_KEYS]
        per_t = []
        for t in range(T_NEW):
            if d == 1:
                row = jnp.concatenate([jnp.full((HEADS, t), NEG_INF, F32), rev[:, :WINDOW_KEYS - t]], axis=1)
            else:
                cls = np.arange(d)[None, None, :] == t
                row = jnp.where(cls, rev[:, :, None], NEG_INF).reshape(HEADS, WINDOW_KEYS * d)
            per_t.append(row)
        bcs.append(jnp.stack(per_t, axis=1))
    bn = []
    for g, d in enumerate(DILATIONS):
        rows = []
        for t in range(T_NEW):
            cols = []
            for tn in range(T_NEW):
                ok = (tn <= t) if d == 1 else (tn == t)
                cols.append(bias[g][:, t - tn] if ok else jnp.full((HEADS,), NEG_INF, F32))
            rows.append(jnp.stack(cols, axis=-1))
        bn.append(jnp.stack(rows, axis=1))
    return bcs, jnp.stack(bn)


def _split_bf16(w):
    hi = w.astype(BF16)
    lo = (w - hi.astype(F32)).astype(BF16)
    return hi, lo


def kernel(x_prompt, x_sample, cache_attn_w128, cache_attn_w512, cache_attn_w2048, state_conv, rel_bias, w_in, w_conv, w_pa, w_pb, w_o, ln1_g, ln1_b, w_router_group, w_router_expert, w_expert_gate, w_expert_up, w_expert_down, ln2_g, ln2_b):
    depth = w_in.shape[0]
    assert depth == 1 and x_prompt.shape[0] == 1
    alpha = (2.0 * depth) ** 0.25
    s = x_prompt.shape[1]
    bd, t_len = x_sample.shape[0], x_sample.shape[1]
    assert t_len == T_NEW and s % (DILATIONS[-1] * Q_BLOCK) == 0

    bias = _bias_per_group(rel_bias)
    tb = _prompt_bias_tables(bias)
    bcs, bn = _sample_bias_tables(bias)

    w_in_bf = w_in[0].astype(BF16)
    w_pa_bf = w_pa[0].astype(BF16)
    w_pb_bf = w_pb[0].astype(BF16)
    w_o_bf = w_o[0].astype(BF16)
    wr = jnp.zeros((ROUTER_ROWS, D_MODEL), F32)
    wr = wr.at[0:N_EXPERT_GROUPS].set(w_router_group[0].T).at[8:8 + N_EXPERTS].set(w_router_expert[0].T)
    wr_hi, wr_lo = _split_bf16(wr)
    g1, b1 = ln1_g[0][None], ln1_b[0][None]
    g2, b2 = ln2_g[0][None], ln2_b[0][None]
    wg, wu, wd = w_expert_gate[0], w_expert_up[0], w_expert_down[0]

    xp = x_prompt[0]
    kv_tail = min(MAX_DISTANCE, s)
    q, kb, vb, (k32, v32, yb, sga, sgb, ut) = _proj(
        xp, w_in_bf, w_conv[0], None, tm=256, kv_tail=kv_tail, u_tail=8, q_dtype=BF16, dils=DILATIONS)
    o_l = [_attn_prompt_group(q[g], kb[g], vb[g], tb[g], g) for g in range(N_GROUPS)]
    x1, ei, gt = _mix(xp, [a[0] for a in o_l], [a[1] for a in o_l], yb, sga, sgb,
                      w_pa_bf, w_pb_bf, w_o_bf, g1, b1, wr_hi, wr_lo, tm=512, alpha=alpha, dils=DILATIONS)
    y_prompt = _hier_moe_ln(x1, ei, gt, wg, wu, wd, g2, b2, tl=512, tm=256, bm=256, alpha=alpha)[None]

    kv_prompt = []
    for g, d in enumerate(DILATIONS):
        length = min(WINDOW_KEYS * d, s)
        cols = slice(g * GROUP_WIDTH, (g + 1) * GROUP_WIDTH)
        kg = k32[kv_tail - length:, cols].reshape(length, HEADS, HEAD_DIM)
        vg = v32[kv_tail - length:, cols].reshape(length, HEADS, HEAD_DIM)
        kv_prompt.append(jnp.stack([kg, vg], axis=1)[None, None])
    conv_prompt = ut[6:8][None, None]

    ns = bd * t_len
    xs = x_sample.reshape(ns, D_MODEL)
    st = state_conv[0]
    s0 = jnp.repeat(st[:, 0], t_len, axis=0)
    s1 = jnp.repeat(st[:, 1], t_len, axis=0)
    qs, _, _, (k32s, v32s, ybs, sgas, sgbs, us) = _proj(
        xs, w_in_bf, w_conv[0], (s0, s1), tm=ns, kv_tail=ns, u_tail=ns, q_dtype=F32, dils=NO_DILATION)
    qs = jnp.concatenate([a[0] for a in qs], axis=1)
    packed = jnp.stack([qs, k32s, v32s]).reshape(3, bd, t_len, N_GROUPS, GROUP_WIDTH)
    qkv_t = jnp.transpose(packed, (1, 4, 0, 3, 2)).reshape(bd, GROUP_WIDTH, 3 * N_GROUPS * t_len)
    qkv_t = jnp.pad(qkv_t, ((0, 0), (0, 0), (0, 128 - 3 * N_GROUPS * t_len)))
    caches = (cache_attn_w128[0], cache_attn_w512[0], cache_attn_w2048[0])
    caches_t = [jnp.transpose(c, (0, 2, 3, 4, 1)).reshape(bd, 2, GROUP_WIDTH, c.shape[1]) for c in caches]
    pair = (bd, t_len, N_GROUPS, HEADS // 2, 2, HEAD_DIM)
    q6 = jnp.transpose(qs.reshape(pair), (0, 3, 2, 4, 1, 5))
    zeros = jnp.zeros_like(q6[:, :, :, 0])
    qbd = jnp.stack([jnp.concatenate([q6[:, :, :, 0], zeros], axis=-1),
                     jnp.concatenate([zeros, q6[:, :, :, 1]], axis=-1)], axis=3)
    qbd = qbd.reshape(bd, HEADS // 2, N_GROUPS, 2 * t_len, 128)
    new_rows = jnp.stack([k32s, v32s]).reshape(2, bd, t_len, N_GROUPS, HEADS // 2, 128)
    new_rows = jnp.transpose(new_rows, (1, 4, 3, 0, 2, 5))
    bcs = [t.reshape(HEADS // 2, 2 * t_len, t.shape[-1]) for t in bcs]
    bn = bn.reshape(N_GROUPS, HEADS // 2, 2 * t_len, t_len)
    n0, n1, n2, o_s, lse_s = _sample_cache(qbd, new_rows, qkv_t, caches_t, bcs, bn)

    def unpack(a):
        a = a[:, :N_GROUPS * t_len].reshape(bd, N_GROUPS, t_len, GROUP_WIDTH)
        return jnp.transpose(a, (1, 0, 2, 3)).reshape(N_GROUPS, 1, ns, GROUP_WIDTH)

    o_s, lse_s = unpack(o_s), unpack(lse_s)
    x1s, eis, gts = _mix(xs, o_s, lse_s, ybs, sgas, sgbs, w_pa_bf, w_pb_bf, w_o_bf, g1, b1, wr_hi, wr_lo,
                         tm=ns, alpha=alpha, dils=NO_DILATION)
    y_sample = _hier_moe_ln(x1s, eis, gts, wg, wu, wd, g2, b2, tl=ns, tm=ns, bm=128, alpha=alpha)
    y_sample = y_sample.reshape(bd, t_len, D_MODEL)

    kv_sample = [jnp.transpose(c.reshape(bd, 2, HEADS, HEAD_DIM, c.shape[-1]), (0, 4, 1, 2, 3))[None]
                 for c in (n0, n1, n2)]
    conv_sample = us.reshape(bd, t_len, CONV_CHANNELS)[:, t_len - 2:][None]

    return (y_prompt, y_sample, kv_prompt[0], kv_prompt[1], kv_prompt[2], conv_prompt,
            kv_sample[0], kv_sample[1], kv_sample[2], conv_sample)
```

```python
import functools
import math

import numpy as np
import jax
import jax.numpy as jnp
from jax import lax
from jax.experimental import pallas as pl
from jax.experimental.pallas import tpu as pltpu

F32 = jnp.float32
BF16 = jnp.bfloat16
I32 = jnp.int32

D_MODEL = 1024
N_GROUPS = 3
HEADS = 8
HEAD_DIM = 64
GROUP_WIDTH = HEADS * HEAD_DIM
ATTN_WIDTH = N_GROUPS * GROUP_WIDTH
DILATIONS = (1, 4, 16)
NO_DILATION = (1, 1, 1)
WINDOW_KEYS = 128
N_KEYS = WINDOW_KEYS + 1
N_BUCKETS = 32
MAX_EXACT = 16
MAX_DISTANCE = 2048
CONV_CHANNELS = 512
N_EXPERT_GROUPS = 4
EXPERTS_PER_GROUP = 8
N_EXPERTS = 32
TOP_K = 2
D_EXPERT = 512
LN_EPS = 1e-5
PROJ_WIDTH = 3 * ATTN_WIDTH + 3 * CONV_CHANNELS + 2 * D_MODEL
ROUTER_ROWS = 8 + N_EXPERTS
Q_BLOCK = 128
T_NEW = 4
NEG_INF = float("-inf")
VMEM_LIMIT = 56 * 1024 * 1024


def _sigmoid(x):
    return 1.0 / (1.0 + jnp.exp(-x))


def _params(limit=VMEM_LIMIT):
    return pltpu.CompilerParams(vmem_limit_bytes=limit)


def _proj_kernel(*refs, tm, tail_rows, sample_mode, dils):
    n_in = 5 if sample_mode else 3
    x_ref, w_ref, wc_ref = refs[0:3]
    outs = refs[n_in:]
    q_refs, k_refs, v_refs = outs[0:3], outs[3:6], outs[6:9]
    k32_ref, v32_ref, yb_ref, sga_ref, sgb_ref, ut_ref, cls_ref = outs[9:16]
    xb = x_ref[...].astype(BF16)

    def col(c0, width):
        return jnp.dot(xb, w_ref[:, c0:c0 + width], preferred_element_type=F32)

    def write_classes(val, group_refs):
        for g, d in enumerate(dils):
            part = val[:, g * GROUP_WIDTH:(g + 1) * GROUP_WIDTH]
            ref = group_refs[g]
            if d == 1:
                ref[0] = part.astype(ref.dtype)
            else:
                for kk in range(GROUP_WIDTH // 128):
                    lanes = slice(kk * 128, (kk + 1) * 128)
                    cls_ref[kk] = part[:, lanes]
                    for c in range(d):
                        ref[c, :, lanes] = cls_ref[kk, pl.ds(c, tm // d, stride=d), :].astype(ref.dtype)

    write_classes(col(0, ATTN_WIDTH), q_refs)
    k = col(ATTN_WIDTH, ATTN_WIDTH)
    k32_ref[...] = k
    write_classes(k, k_refs)
    v = col(2 * ATTN_WIDTH, ATTN_WIDTH)
    v32_ref[...] = v
    write_classes(v, v_refs)

    c0 = 3 * ATTN_WIDTH
    bg = col(c0, CONV_CHANNELS)
    u = col(c0 + CONV_CHANNELS, CONV_CHANNELS) * col(c0 + 2 * CONV_CHANNELS, CONV_CHANNELS)
    row = lax.broadcasted_iota(I32, (tm, CONV_CHANNELS), 0)
    r1 = pltpu.roll(u, 1, axis=0)
    r2 = pltpu.roll(u, 2, axis=0)
    if sample_mode:
        s0 = refs[3][...]
        s1 = refs[4][...]
        t = row & (T_NEW - 1)
        prev1 = jnp.where(t == 0, s1, r1)
        prev2 = jnp.where(t == 0, s0, jnp.where(t == 1, s1, r2))
    else:
        carry_ref = outs[16]

        @pl.when(pl.program_id(0) == 0)
        def _():
            carry_ref[...] = jnp.zeros_like(carry_ref)
        c6 = carry_ref[6:7, :]
        c7 = carry_ref[7:8, :]
        prev1 = jnp.where(row == 0, c7, r1)
        prev2 = jnp.where(row == 0, c6, jnp.where(row == 1, c7, r2))
        carry_ref[...] = u[tm - 8:tm, :]
    conv = prev2 * wc_ref[0:1, :] + prev1 * wc_ref[1:2, :] + u * wc_ref[2:3, :]
    yb_ref[...] = (bg * conv).astype(BF16)
    ut_ref[...] = u[tm - tail_rows:tm, :]

    c1 = c0 + 3 * CONV_CHANNELS
    sga_ref[...] = _sigmoid(col(c1, D_MODEL)).astype(BF16)
    sgb_ref[...] = _sigmoid(col(c1 + D_MODEL, D_MODEL)).astype(BF16)


def _proj(x, w_in_bf, w_conv, conv_prev, *, tm, kv_tail, u_tail, q_dtype, dils):
    n = x.shape[0]
    sample_mode = conv_prev is not None
    nt = n // tm
    tail_first = (n - kv_tail) // tm

    def row_spec(width):
        return pl.BlockSpec((tm, width), lambda i: (i, 0))

    def tail_spec(width):
        return pl.BlockSpec((tm, width), lambda i: (jnp.maximum(i - tail_first, 0), 0))

    def class_spec(d):
        return pl.BlockSpec((d, tm // d, GROUP_WIDTH), lambda i: (0, i, 0))

    def class_shape(d, dtype):
        return jax.ShapeDtypeStruct((d, n // d, GROUP_WIDTH), dtype)

    in_specs = [
        row_spec(D_MODEL),
        pl.BlockSpec((D_MODEL, PROJ_WIDTH), lambda i: (0, 0), pipeline_mode=pl.Buffered(1)),
        pl.BlockSpec((3, CONV_CHANNELS), lambda i: (0, 0)),
    ]
    args = [x, w_in_bf, w_conv]
    scratch = [pltpu.VMEM((GROUP_WIDTH // 128, tm, 128), F32)]
    if sample_mode:
        in_specs += [row_spec(CONV_CHANNELS), row_spec(CONV_CHANNELS)]
        args += [conv_prev[0], conv_prev[1]]
    else:
        scratch.append(pltpu.VMEM((8, CONV_CHANNELS), F32))
    out_shape = (
        [class_shape(d, q_dtype) for d in dils] + [class_shape(d, BF16) for d in dils] * 2
        + [jax.ShapeDtypeStruct((kv_tail, ATTN_WIDTH), F32),
           jax.ShapeDtypeStruct((kv_tail, ATTN_WIDTH), F32),
           jax.ShapeDtypeStruct((n, CONV_CHANNELS), BF16),
           jax.ShapeDtypeStruct((n, D_MODEL), BF16),
           jax.ShapeDtypeStruct((n, D_MODEL), BF16),
           jax.ShapeDtypeStruct((u_tail, CONV_CHANNELS), F32)])
    out_specs = (
        [class_spec(d) for d in dils] * 3
        + [tail_spec(ATTN_WIDTH), tail_spec(ATTN_WIDTH),
           row_spec(CONV_CHANNELS), row_spec(D_MODEL), row_spec(D_MODEL),
           pl.BlockSpec((u_tail, CONV_CHANNELS), lambda i: (0, 0))])
    res = pl.pallas_call(
        functools.partial(_proj_kernel, tm=tm, tail_rows=u_tail, sample_mode=sample_mode, dils=dils),
        grid=(nt,),
        in_specs=in_specs,
        out_specs=out_specs,
        out_shape=out_shape,
        scratch_shapes=scratch,
        compiler_params=_params(),
        name="proj",
    )(*args)
    return res[0:3], res[3:6], res[6:9], res[9:]


def _attn_kernel(q_ref, kp_ref, kc_ref, vp_ref, vc_ref, tb_ref, o_ref, lse_ref):
    lane = lax.broadcasted_iota(I32, (Q_BLOCK, 128), 1)
    first = lane < HEAD_DIM
    for pr in range(HEADS // 2):
        sl = slice(pr * 128, (pr + 1) * 128)
        q = q_ref[:, sl]
        k = jnp.concatenate([kp_ref[:, sl], kc_ref[:, sl]], axis=0)
        v = jnp.concatenate([vp_ref[:, sl], vc_ref[:, sl]], axis=0)
        qf = q.astype(F32)
        qq = jnp.concatenate([jnp.where(first, qf, 0.0), jnp.where(first, 0.0, qf)], axis=0).astype(BF16)
        s = lax.dot_general(qq, k, (((1,), (1,)), ((), ())), preferred_element_type=F32)
        s = s * (HEAD_DIM ** -0.5) + tb_ref[0, pr]
        m = jnp.max(s, axis=-1, keepdims=True)
        p = jnp.exp(s - m)
        l = jnp.sum(p, axis=-1, keepdims=True)
        o = jnp.dot(p.astype(BF16), v, preferred_element_type=F32) / l
        lse = m + jnp.log(l)
        o_ref[:, sl] = jnp.where(first, o[:Q_BLOCK], o[Q_BLOCK:]).astype(o_ref.dtype)
        lse_ref[:, sl] = jnp.where(first, jnp.broadcast_to(lse[:Q_BLOCK], (Q_BLOCK, 128)),
                                   jnp.broadcast_to(lse[Q_BLOCK:], (Q_BLOCK, 128)))


def _attn_prompt_group(q, kb, vb, tb, g):
    d, rows = q.shape[0], q.shape[1]
    cur = pl.BlockSpec((None, Q_BLOCK, GROUP_WIDTH), lambda c, i: (c, i, 0))
    prev = pl.BlockSpec((None, Q_BLOCK, GROUP_WIDTH), lambda c, i: (c, jnp.maximum(i - 1, 0), 0))
    return pl.pallas_call(
        _attn_kernel,
        grid=(d, rows // Q_BLOCK),
        in_specs=[cur, prev, cur, prev, cur,
                  pl.BlockSpec((1, HEADS // 2, 2 * Q_BLOCK, 2 * Q_BLOCK), lambda c, i: (jnp.minimum(i, 1), 0, 0, 0))],
        out_specs=[cur, cur],
        out_shape=[jax.ShapeDtypeStruct((d, rows, GROUP_WIDTH), BF16),
                   jax.ShapeDtypeStruct((d, rows, GROUP_WIDTH), F32)],
        compiler_params=_params(),
        name=f"attn_prompt_g{g}",
    )(q, kb, kb, vb, vb, tb)


PACK_Q, PACK_K, PACK_V = 0, N_GROUPS * T_NEW, 2 * N_GROUPS * T_NEW


PAIRS_PER_STEP = 2


def _sample_cache_kernel(qbd_ref, nr_ref, qkv_ref, c0_ref, c1_ref, c2_ref, b0_ref, b1_ref, b2_ref, bn_ref,
                         n0_ref, n1_ref, n2_ref, o_ref, lse_ref):
    for pp in range(PAIRS_PER_STEP):
        rows = pl.ds(pp * 128, 128)
        kv = pl.ds(0, 2)
        _sample_pair(qbd_ref.at[pp], nr_ref.at[pp], qkv_ref.at[rows],
                     [c.at[kv, rows] for c in (c0_ref, c1_ref, c2_ref)],
                     [b.at[pp] for b in (b0_ref, b1_ref, b2_ref)], bn_ref.at[pl.ds(0, N_GROUPS), pp],
                     [c.at[kv, rows] for c in (n0_ref, n1_ref, n2_ref)],
                     o_ref.at[pl.ds(0, 16), rows], lse_ref.at[pl.ds(0, 16), rows])


def _sample_pair(qbd_ref, nr_ref, qkv_ref, c_refs, b_refs, bn_ref, n_refs, o_ref, lse_ref):
    scale = HEAD_DIM ** -0.5
    nt = (((1,), (1,)), ((), ()))
    lane = lax.broadcasted_iota(I32, (128, 128), 1)
    head0 = lane[0:T_NEW] < HEAD_DIM
    for g, (c_ref, b_ref, n_ref) in enumerate(zip(c_refs, b_refs, n_refs)):
        length = c_ref.shape[-1]
        k_new = qkv_ref[:, PACK_K + g * T_NEW:PACK_K + (g + 1) * T_NEW]
        v_new = qkv_ref[:, PACK_V + g * T_NEW:PACK_V + (g + 1) * T_NEW]
        qbd = qbd_ref[g]
        s_c = jnp.dot(qbd.astype(BF16), c_ref[0].astype(BF16), preferred_element_type=F32) * scale + b_ref[...]
        bn = bn_ref[g]
        s_n = [jnp.sum(qbd * nr_ref[g, 0, tn:tn + 1, :], axis=1, keepdims=True) * scale + bn[:, tn:tn + 1]
               for tn in range(T_NEW)]
        m = jnp.max(s_c, axis=1, keepdims=True)
        for x in s_n:
            m = jnp.maximum(m, x)
        p_c = jnp.exp(s_c - m)
        p_n = [jnp.exp(x - m) for x in s_n]
        l = jnp.sum(p_c, axis=1, keepdims=True)
        acc = lax.dot_general(p_c.astype(BF16), c_ref[1].astype(BF16), nt, preferred_element_type=F32)
        for tn in range(T_NEW):
            l = l + p_n[tn]
            acc = acc + p_n[tn] * nr_ref[g, 1, tn:tn + 1, :]
        o = acc / l
        lse = jnp.broadcast_to(m + jnp.log(l), (2 * T_NEW, 128))
        o_ref[g * T_NEW:(g + 1) * T_NEW, :] = jnp.where(head0, o[0:T_NEW], o[T_NEW:])
        lse_ref[g * T_NEW:(g + 1) * T_NEW, :] = jnp.where(head0, lse[0:T_NEW], lse[T_NEW:])

        for kv, new in ((0, k_new), (1, v_new)):
            rolled = pltpu.roll(c_ref[kv], length - T_NEW, axis=1)
            tail = rolled[:, length - 128:]
            for t in range(T_NEW):
                tail = jnp.where(lane == 128 - T_NEW + t, new[:, t:t + 1], tail)
            if length > 128:
                n_ref[kv, :, 0:length - 128] = rolled[:, 0:length - 128]
            n_ref[kv, :, length - 128:] = tail
    pad_rows = slice(N_GROUPS * T_NEW, 16)
    o_ref[pad_rows, :] = jnp.zeros((16 - N_GROUPS * T_NEW, 128), F32)
    lse_ref[pad_rows, :] = jnp.zeros((16 - N_GROUPS * T_NEW, 128), F32)


def _sample_cache(qbd, new_rows, qkv_t, caches_t, bcs, bn):
    b = qkv_t.shape[0]

    pp = PAIRS_PER_STEP

    def cache_spec(c):
        return pl.BlockSpec((None, 2, pp * 128, c.shape[-1]), lambda i, h: (i, 0, h, 0))

    def bias_spec(t):
        return pl.BlockSpec((pp, 2 * T_NEW, t.shape[-1]), lambda i, h: (h, 0, 0))

    out = pl.BlockSpec((None, 16, pp * 128), lambda i, h: (i, 0, h))
    return pl.pallas_call(
        _sample_cache_kernel,
        grid=(b, HEADS // 2 // pp),
        in_specs=[pl.BlockSpec((None, pp, N_GROUPS, 2 * T_NEW, 128), lambda i, h: (i, h, 0, 0, 0)),
                  pl.BlockSpec((None, pp, N_GROUPS, 2, T_NEW, 128), lambda i, h: (i, h, 0, 0, 0, 0)),
                  pl.BlockSpec((None, pp * 128, 128), lambda i, h: (i, h, 0))]
                 + [cache_spec(c) for c in caches_t] + [bias_spec(t) for t in bcs]
                 + [pl.BlockSpec((N_GROUPS, pp, 2 * T_NEW, T_NEW), lambda i, h: (0, h, 0, 0))],
        out_specs=[cache_spec(c) for c in caches_t] + [out, out],
        out_shape=[jax.ShapeDtypeStruct(c.shape, c.dtype) for c in caches_t]
                  + [jax.ShapeDtypeStruct((b, 16, GROUP_WIDTH), F32)] * 2,
        compiler_params=_params(),
        name="sample_cache",
    )(qbd, new_rows, qkv_t, *caches_t, *bcs, bn)


def _mix_kernel(*refs, tm, alpha, dils):
    (x_ref, o0_ref, o1_ref, o2_ref, l0_ref, l1_ref, l2_ref, yb_ref, sga_ref, sgb_ref,
     wpa_ref, wpb_ref, wo_ref, g_ref, b_ref, wrh_ref, wrl_ref, x1_ref, ei_ref, gt_ref) = refs[0:20]
    scratch = list(refs[20:])

    def natural(ref, d):
        if d == 1:
            return ref[0].astype(F32)
        scr = scratch.pop()
        for kk in range(GROUP_WIDTH // 128):
            for c in range(d):
                scr[kk, pl.ds(c, tm // d, stride=d), :] = ref[c, :, kk * 128:(kk + 1) * 128].astype(F32)
        return jnp.concatenate([scr[kk] for kk in range(GROUP_WIDTH // 128)], axis=1)

    l0, l1, l2 = natural(l0_ref, dils[0]), natural(l1_ref, dils[1]), natural(l2_ref, dils[2])
    mx = jnp.maximum(jnp.maximum(l0, l1), l2)
    e0 = jnp.exp(l0 - mx)
    e1 = jnp.exp(l1 - mx)
    e2 = jnp.exp(l2 - mx)
    ya = (e0 * natural(o0_ref, dils[0]) + e1 * natural(o1_ref, dils[1]) + e2 * natural(o2_ref, dils[2])) / (e0 + e1 + e2)
    pa = jnp.dot(ya.astype(BF16), wpa_ref[...], preferred_element_type=F32)
    pb = jnp.dot(yb_ref[...], wpb_ref[...], preferred_element_type=F32)
    gated = sga_ref[...].astype(F32) * pa + sgb_ref[...].astype(F32) * pb
    mix = jnp.dot(gated.astype(BF16), wo_ref[...], preferred_element_type=F32)
    z = alpha * x_ref[...] + mix
    mu = jnp.mean(z, axis=-1, keepdims=True)
    zc = z - mu
    var = jnp.mean(zc * zc, axis=-1, keepdims=True)
    x1 = zc * lax.rsqrt(var + LN_EPS) * g_ref[...] + b_ref[...]
    x1_ref[...] = x1

    xh = x1.astype(BF16)
    xl = (x1 - xh.astype(F32)).astype(BF16)
    nt = (((1,), (1,)), ((), ()))
    wrh = wrh_ref[...]
    lt = (lax.dot_general(wrh, xh, nt, preferred_element_type=F32)
          + lax.dot_general(wrh, xl, nt, preferred_element_type=F32)
          + lax.dot_general(wrl_ref[...], xh, nt, preferred_element_type=F32))

    gl = lt[0:N_EXPERT_GROUPS]
    gmax = jnp.max(gl, axis=0, keepdims=True)
    idx4 = lax.broadcasted_iota(I32, (N_EXPERT_GROUPS, tm), 0)
    g_idx = jnp.min(jnp.where(gl == gmax, idx4, N_EXPERT_GROUPS), axis=0, keepdims=True)
    g_prob = 1.0 / jnp.sum(jnp.exp(gl - gmax), axis=0, keepdims=True)
    e_sel = lt[8:16]
    for grp in range(1, N_EXPERT_GROUPS):
        e_sel = jnp.where(g_idx == grp, lt[8 + 8 * grp:16 + 8 * grp], e_sel)
    idx8 = lax.broadcasted_iota(I32, (EXPERTS_PER_GROUP, tm), 0)
    v1 = jnp.max(e_sel, axis=0, keepdims=True)
    i1 = jnp.min(jnp.where(e_sel == v1, idx8, EXPERTS_PER_GROUP), axis=0, keepdims=True)
    rest = jnp.where(idx8 == i1, NEG_INF, e_sel)
    v2 = jnp.max(rest, axis=0, keepdims=True)
    i2 = jnp.min(jnp.where(rest == v2, idx8, EXPERTS_PER_GROUP), axis=0, keepdims=True)
    r = jnp.exp(v2 - v1)
    gate1 = g_prob / (1.0 + r)
    gate2 = g_prob * r / (1.0 + r)
    ex1 = g_idx * EXPERTS_PER_GROUP + i1
    ex2 = g_idx * EXPERTS_PER_GROUP + i2
    ei_ref[...] = jnp.where(idx8 == 0, ex1, jnp.where(idx8 == 1, ex2, 0))
    gt_ref[...] = jnp.where(idx8 == 0, gate1, jnp.where(idx8 == 1, gate2, 0.0))


def _mix(x, o, lse, yb, sga, sgb, w_pa, w_pb, w_o, ln_g, ln_b, wr_hi, wr_lo, *, tm, alpha, dils):
    n = x.shape[0]

    def row_spec(width):
        return pl.BlockSpec((tm, width), lambda i: (i, 0))

    def class_spec(d):
        return pl.BlockSpec((d, tm // d, GROUP_WIDTH), lambda i: (0, i, 0))

    def full(a):
        return pl.BlockSpec(a.shape, lambda i: (0,) * a.ndim)

    lane_spec = pl.BlockSpec((8, tm), lambda i: (0, i))
    n_scratch = 2 * sum(1 for d in dils if d > 1)
    return pl.pallas_call(
        functools.partial(_mix_kernel, tm=tm, alpha=alpha, dils=dils),
        grid=(n // tm,),
        in_specs=[row_spec(D_MODEL)] + [class_spec(d) for d in dils] * 2 + [row_spec(CONV_CHANNELS)]
                 + [row_spec(D_MODEL)] * 2
                 + [full(w_pa), full(w_pb), full(w_o), full(ln_g), full(ln_b), full(wr_hi), full(wr_lo)],
        out_specs=[row_spec(D_MODEL), lane_spec, lane_spec],
        out_shape=[jax.ShapeDtypeStruct((n, D_MODEL), F32),
                   jax.ShapeDtypeStruct((8, n), I32),
                   jax.ShapeDtypeStruct((8, n), F32)],
        scratch_shapes=[pltpu.VMEM((GROUP_WIDTH // 128, tm, 128), F32)] * n_scratch,
        compiler_params=_params(),
        name="mix",
    )(x, o[0], o[1], o[2], lse[0], lse[1], lse[2], yb, sga, sgb, w_pa, w_pb, w_o, ln_g, ln_b, wr_hi, wr_lo)


def _slot_kernel(ei_ref, slot_ref, cnt_ref, carry_ref, start_ref, *, tl, bm):
    phase = pl.program_id(0)
    i = pl.program_id(1)

    @pl.when(jnp.logical_and(phase == 0, i == 0))
    def _():
        carry_ref[...] = jnp.zeros_like(carry_ref)

    ex = lax.broadcasted_iota(I32, (N_EXPERTS, tl), 0)
    oh0 = (ex == ei_ref[0:1, :]).astype(F32)
    oh1 = (ex == ei_ref[1:2, :]).astype(F32)
    cnt0 = jnp.sum(oh0, axis=1, keepdims=True)
    cnt1 = jnp.sum(oh1, axis=1, keepdims=True)

    @pl.when(phase == 0)
    def _():
        total = carry_ref[...] + cnt0 + cnt1
        carry_ref[...] = total
        cnt_ref[...] = total.astype(I32)
        slot_ref[...] = jnp.zeros_like(slot_ref)

    @pl.when(jnp.logical_and(phase == 1, i == 0))
    def _():
        blocks = jnp.floor((carry_ref[...] + (bm - 1)) * (1.0 / bm))
        a = lax.broadcasted_iota(I32, (N_EXPERTS, N_EXPERTS), 0)
        b = lax.broadcasted_iota(I32, (N_EXPERTS, N_EXPERTS), 1)
        before = (b < a).astype(BF16)
        start_ref[...] = jnp.dot(before, blocks.astype(BF16), preferred_element_type=F32) * bm
        carry_ref[...] = jnp.zeros_like(carry_ref)

    @pl.when(phase == 1)
    def _():
        a = lax.broadcasted_iota(I32, (tl, tl), 0)
        b = lax.broadcasted_iota(I32, (tl, tl), 1)
        upper = (a < b).astype(BF16)
        pre0 = jnp.dot(oh0.astype(BF16), upper, preferred_element_type=F32)
        pre1 = jnp.dot(oh1.astype(BF16), upper, preferred_element_type=F32)
        base = carry_ref[:, 0:1] + start_ref[:, 0:1]
        slot0 = jnp.sum(oh0 * (pre0 + base), axis=0, keepdims=True)
        slot1 = jnp.sum(oh1 * (pre1 + cnt0 + base), axis=0, keepdims=True)
        row = lax.broadcasted_iota(I32, (8, tl), 0)
        slot_ref[...] = jnp.where(row == 0, slot0.astype(I32), jnp.where(row == 1, slot1.astype(I32), 0))
        carry_ref[...] = carry_ref[...] + cnt0 + cnt1


def _slots(ei, *, tl, bm):
    n = ei.shape[1]
    return pl.pallas_call(
        functools.partial(_slot_kernel, tl=tl, bm=bm),
        grid=(2, n // tl),
        in_specs=[pl.BlockSpec((8, tl), lambda p, i: (0, i))],
        out_specs=[pl.BlockSpec((8, tl), lambda p, i: (0, i * p)),
                   pl.BlockSpec((N_EXPERTS, 128), lambda p, i: (0, 0))],
        out_shape=[jax.ShapeDtypeStruct((8, n), I32), jax.ShapeDtypeStruct((N_EXPERTS, 128), I32)],
        scratch_shapes=[pltpu.VMEM((N_EXPERTS, 128), F32), pltpu.VMEM((N_EXPERTS, 128), F32)],
        name="moe_slots",
    )(ei)


def _row_copy(src, src_row, dst, dst_row, sem):
    return pltpu.make_async_copy(src.at[pl.ds(src_row, 1)], dst.at[pl.ds(dst_row, 1)], sem)


ISSUE_UNROLL = 8


def _dispatch_kernel(slot_ref, pend_ref, cnt_ref, x1_ref, buf_ref, zero_ref, sem, zsem, *, n, tm, bm):
    base = pl.program_id(0) * tm

    @pl.when(pl.program_id(0) == 0)
    def _():
        zero_ref[...] = jnp.zeros_like(zero_ref)

        def zero_copy(e):
            start = pl.multiple_of(pend_ref[e] - bm, bm)
            return pltpu.make_async_copy(zero_ref, buf_ref.at[pl.ds(start, bm)], zsem)

        def tail_copy(blk):
            return pltpu.make_async_copy(zero_ref, buf_ref.at[pl.ds(pl.multiple_of(blk * bm, bm), bm)], zsem)

        def tail_start(blk, c):
            tail_copy(blk).start()
            return c

        def tail_wait(blk, c):
            tail_copy(blk).wait()
            return c

        for e in range(N_EXPERTS):
            @pl.when(cnt_ref[e] > 0)
            def _(e=e):
                zero_copy(e).start()
        first_unused = pend_ref[N_EXPERTS - 1] // bm
        lax.fori_loop(first_unused, buf_ref.shape[0] // bm, tail_start, 0)
        for e in range(N_EXPERTS):
            @pl.when(cnt_ref[e] > 0)
            def _(e=e):
                zero_copy(e).wait()
        lax.fori_loop(first_unused, buf_ref.shape[0] // bm, tail_wait, 0)

    def body(r, carry):
        for k in range(TOP_K):
            _row_copy(x1_ref, r, buf_ref, slot_ref[k * n + base + r], sem).start()
        return carry

    lax.fori_loop(0, tm, body, 0, unroll=ISSUE_UNROLL)
    for _ in range(TOP_K):
        pltpu.make_async_copy(x1_ref, buf_ref.at[pl.ds(0, tm)], sem).wait()


def _dispatch(slot_flat, pend, counts, x1, *, tm, bm, nblk):
    n = x1.shape[0]
    return pl.pallas_call(
        functools.partial(_dispatch_kernel, n=n, tm=tm, bm=bm),
        grid_spec=pltpu.PrefetchScalarGridSpec(
            num_scalar_prefetch=3,
            grid=(n // tm,),
            in_specs=[pl.BlockSpec((tm, D_MODEL), lambda i, s, p, c: (i, 0))],
            out_specs=pl.BlockSpec(memory_space=pl.ANY),
            scratch_shapes=[pltpu.VMEM((bm, D_MODEL), F32), pltpu.SemaphoreType.DMA(()),
                            pltpu.SemaphoreType.DMA(())],
        ),
        out_shape=jax.ShapeDtypeStruct((nblk * bm, D_MODEL), F32),
        name="moe_dispatch",
    )(slot_flat, pend, counts, x1)


def _expert_kernel(be_ref, nu_ref, xb_ref, wg_ref, wu_ref, wd_ref, out_ref, wg_bf, wu_bf, wd_bf):
    j = pl.program_id(0)
    used = j < nu_ref[0]
    changed = jnp.logical_or(j == 0, be_ref[j] != be_ref[jnp.maximum(j - 1, 0)])

    @pl.when(jnp.logical_and(used, changed))
    def _():
        wg_bf[...] = wg_ref[...].astype(BF16)
        wu_bf[...] = wu_ref[...].astype(BF16)
        wd_bf[...] = wd_ref[...].astype(BF16)

    @pl.when(used)
    def _():
        xb = xb_ref[...].astype(BF16)
        a = jnp.dot(xb, wg_bf[...], preferred_element_type=F32)
        b = jnp.dot(xb, wu_bf[...], preferred_element_type=F32)
        h = (a * _sigmoid(a)) * b
        out_ref[...] = jnp.dot(h.astype(BF16), wd_bf[...], preferred_element_type=F32)

    @pl.when(jnp.logical_not(used))
    def _():
        out_ref[...] = jnp.zeros_like(out_ref)


def _experts(block_expert, n_used, buf, w_g, w_u, w_d, *, bm):
    nblk = buf.shape[0] // bm

    def row_map(j, be, nu):
        return (jnp.minimum(j, nu[0] - 1), 0)

    def w_map(j, be, nu):
        return (be[j], 0, 0)

    return pl.pallas_call(
        _expert_kernel,
        grid_spec=pltpu.PrefetchScalarGridSpec(
            num_scalar_prefetch=2,
            grid=(nblk,),
            in_specs=[pl.BlockSpec((bm, D_MODEL), row_map),
                      pl.BlockSpec((None, D_MODEL, D_EXPERT), w_map),
                      pl.BlockSpec((None, D_MODEL, D_EXPERT), w_map),
                      pl.BlockSpec((None, D_EXPERT, D_MODEL), w_map)],
            out_specs=pl.BlockSpec((bm, D_MODEL), lambda j, be, nu: (j, 0)),
            scratch_shapes=[pltpu.VMEM((D_MODEL, D_EXPERT), BF16),
                            pltpu.VMEM((D_MODEL, D_EXPERT), BF16),
                            pltpu.VMEM((D_EXPERT, D_MODEL), BF16)],
        ),
        out_shape=jax.ShapeDtypeStruct(buf.shape, F32),
        compiler_params=_params(),
        name="moe_experts",
    )(block_expert, n_used, buf, w_g, w_u, w_d)


def _combine_kernel(slot_ref, x1_ref, gc_ref, g_ref, b_ref, eo_ref, y_ref, rows, sem, *, n, tm, alpha):
    i = pl.program_id(0)
    last = pl.num_programs(0) - 1
    cur = i % 2

    def start(tile, buf, r):
        for k in range(TOP_K):
            _row_copy(eo_ref, slot_ref[k * n + tile * tm + r], rows.at[buf, k], r, sem.at[buf]).start()

    def wait(buf):
        for k in range(TOP_K):
            pltpu.make_async_copy(eo_ref.at[pl.ds(0, tm)], rows.at[buf, k], sem.at[buf]).wait()

    @pl.when(i == 0)
    def _():
        def body(r, c):
            start(0, 0, r)
            return c
        lax.fori_loop(0, tm, body, 0, unroll=ISSUE_UNROLL)

    wait(cur)
    nxt = jnp.minimum(i + 1, last)
    for r in range(tm):
        start(nxt, 1 - cur, r)
    gc = gc_ref[...]
    z = alpha * x1_ref[...] + gc[:, 0:1] * rows[cur, 0] + gc[:, 1:2] * rows[cur, 1]
    mu = jnp.mean(z, axis=-1, keepdims=True)
    zc = z - mu
    var = jnp.mean(zc * zc, axis=-1, keepdims=True)
    y_ref[...] = zc * lax.rsqrt(var + LN_EPS) * g_ref[...] + b_ref[...]

    @pl.when(i == last)
    def _():
        wait(1 - cur)


def _combine(slot_flat, x1, gate_cols, ln_g, ln_b, expert_out, *, tm, alpha):
    n = x1.shape[0]
    return pl.pallas_call(
        functools.partial(_combine_kernel, n=n, tm=tm, alpha=alpha),
        grid_spec=pltpu.PrefetchScalarGridSpec(
            num_scalar_prefetch=1,
            grid=(n // tm,),
            in_specs=[pl.BlockSpec((tm, D_MODEL), lambda i, s: (i, 0)),
                      pl.BlockSpec((tm, TOP_K), lambda i, s: (i, 0)),
                      pl.BlockSpec((1, D_MODEL), lambda i, s: (0, 0)),
                      pl.BlockSpec((1, D_MODEL), lambda i, s: (0, 0)),
                      pl.BlockSpec(memory_space=pl.ANY)],
            out_specs=pl.BlockSpec((tm, D_MODEL), lambda i, s: (i, 0)),
            scratch_shapes=[pltpu.VMEM((2, TOP_K, tm, D_MODEL), F32), pltpu.SemaphoreType.DMA((2,))],
        ),
        out_shape=jax.ShapeDtypeStruct((n, D_MODEL), F32),
        compiler_params=_params(),
        name="moe_combine",
    )(slot_flat, x1, gate_cols, ln_g, ln_b, expert_out)


def _hier_moe_ln(x1, ei, gt, w_g, w_u, w_d, ln_g, ln_b, *, tl, tm, bm, alpha):
    n = x1.shape[0]
    m = n * TOP_K
    slot, cnt = _slots(ei, tl=tl, bm=bm)
    counts = cnt[:, 0]
    pend = jnp.cumsum((counts + bm - 1) // bm * bm)
    nblk = (m + N_EXPERTS * (bm - 1) + bm - 1) // bm
    blk_start = jnp.arange(nblk, dtype=I32) * bm
    n_used = (pend[-1] // bm).astype(I32)
    be = jnp.minimum(jnp.sum(pend[None, :] <= blk_start[:, None], axis=1), N_EXPERTS - 1).astype(I32)
    be = jnp.where(jnp.arange(nblk) < n_used, be, jnp.take(be, n_used - 1))
    slot_flat = slot[0:TOP_K].reshape(m)
    buf = _dispatch(slot_flat, pend.astype(I32), counts, x1, tm=tm, bm=bm, nblk=nblk)
    eo = _experts(be, n_used.reshape(1), buf, w_g, w_u, w_d, bm=bm)
    gate_cols = gt[0:TOP_K].T
    return _combine(slot_flat, x1, gate_cols, ln_g, ln_b, eo, tm=tm, alpha=alpha)


def _t5_bucket(n):
    nf = jnp.maximum(n, 1).astype(F32)
    large = MAX_EXACT + (jnp.log(nf / MAX_EXACT) / math.log(MAX_DISTANCE / MAX_EXACT)
                         * (N_BUCKETS - MAX_EXACT)).astype(I32)
    large = jnp.minimum(large, N_BUCKETS - 1)
    return jnp.where(n < MAX_EXACT, n, large)


def _bias_per_group(rel_bias):
    offs = jnp.arange(N_KEYS, dtype=I32)[None, :] * jnp.array(DILATIONS, I32)[:, None]
    bucket = _t5_bucket(offs)
    table = rel_bias.reshape(N_BUCKETS, N_GROUPS, HEADS)
    b = table[bucket, jnp.arange(N_GROUPS)[:, None]]
    return jnp.transpose(b, (0, 2, 1)).astype(F32)


def _prompt_bias_tables(bias):
    width = 3 * Q_BLOCK
    neg = jnp.full((N_GROUPS, HEADS, Q_BLOCK - 1), NEG_INF, F32)
    r = jnp.concatenate([neg, bias[:, :, ::-1], neg, jnp.full((N_GROUPS, HEADS, 1), NEG_INF, F32)], axis=-1)
    flat = jnp.tile(r, (1, 1, Q_BLOCK))[:, :, :Q_BLOCK * (width - 1)]
    skew = flat.reshape(N_GROUPS, HEADS, Q_BLOCK, width - 1)
    later = skew[:, :, :, Q_BLOCK - 1:3 * Q_BLOCK - 1]
    has_prev = (np.arange(2 * Q_BLOCK) >= Q_BLOCK)[None, None, None, :]
    first = jnp.where(has_prev, later, NEG_INF)
    tb = jnp.stack([first, later], axis=1)
    return tb.reshape(N_GROUPS, 2, HEADS // 2, 2 * Q_BLOCK, 2 * Q_BLOCK)


def _sample_bias_tables(bias):
    bcs = []
    for g, d in enumerate(DILATIONS):
        rev = bias[g][:, ::-1][:, :WINDOW_KEYS]
        per_t = []
        for t in range(T_NEW):
            if d == 1:
                row = jnp.concatenate([jnp.full((HEADS, t), NEG_INF, F32), rev[:, :WINDOW_KEYS - t]], axis=1)
            else:
                cls = np.arange(d)[None, None, :] == t
                row = jnp.where(cls, rev[:, :, None], NEG_INF).reshape(HEADS, WINDOW_KEYS * d)
            per_t.append(row)
        bcs.append(jnp.stack(per_t, axis=1))
    bn = []
    for g, d in enumerate(DILATIONS):
        rows = []
        for t in range(T_NEW):
            cols = []
            for tn in range(T_NEW):
                ok = (tn <= t) if d == 1 else (tn == t)
                cols.append(bias[g][:, t - tn] if ok else jnp.full((HEADS,), NEG_INF, F32))
            rows.append(jnp.stack(cols, axis=-1))
        bn.append(jnp.stack(rows, axis=1))
    return bcs, jnp.stack(bn)


def _split_bf16(w):
    hi = w.astype(BF16)
    lo = (w - hi.astype(F32)).astype(BF16)
    return hi, lo


def kernel(x_prompt, x_sample, cache_attn_w128, cache_attn_w512, cache_attn_w2048, state_conv, rel_bias, w_in, w_conv, w_pa, w_pb, w_o, ln1_g, ln1_b, w_router_group, w_router_expert, w_expert_gate, w_expert_up, w_expert_down, ln2_g, ln2_b):
    depth = w_in.shape[0]
    assert depth == 1 and x_prompt.shape[0] == 1
    alpha = (2.0 * depth) ** 0.25
    s = x_prompt.shape[1]
    bd, t_len = x_sample.shape[0], x_sample.shape[1]
    assert t_len == T_NEW and s % (DILATIONS[-1] * Q_BLOCK) == 0

    bias = _bias_per_group(rel_bias)
    tb = _prompt_bias_tables(bias)
    bcs, bn = _sample_bias_tables(bias)

    w_in_bf = w_in[0].astype(BF16)
    w_pa_bf = w_pa[0].astype(BF16)
    w_pb_bf = w_pb[0].astype(BF16)
    w_o_bf = w_o[0].astype(BF16)
    wr = jnp.zeros((ROUTER_ROWS, D_MODEL), F32)
    wr = wr.at[0:N_EXPERT_GROUPS].set(w_router_group[0].T).at[8:8 + N_EXPERTS].set(w_router_expert[0].T)
    wr_hi, wr_lo = _split_bf16(wr)
    g1, b1 = ln1_g[0][None], ln1_b[0][None]
    g2, b2 = ln2_g[0][None], ln2_b[0][None]
    wg, wu, wd = w_expert_gate[0], w_expert_up[0], w_expert_down[0]

    xp = x_prompt[0]
    kv_tail = min(MAX_DISTANCE, s)
    q, kb, vb, (k32, v32, yb, sga, sgb, ut) = _proj(
        xp, w_in_bf, w_conv[0], None, tm=256, kv_tail=kv_tail, u_tail=8, q_dtype=BF16, dils=DILATIONS)
    o_l = [_attn_prompt_group(q[g], kb[g], vb[g], tb[g], g) for g in range(N_GROUPS)]
    x1, ei, gt = _mix(xp, [a[0] for a in o_l], [a[1] for a in o_l], yb, sga, sgb,
                      w_pa_bf, w_pb_bf, w_o_bf, g1, b1, wr_hi, wr_lo, tm=512, alpha=alpha, dils=DILATIONS)
    y_prompt = _hier_moe_ln(x1, ei, gt, wg, wu, wd, g2, b2, tl=512, tm=256, bm=256, alpha=alpha)[None]

    kv_prompt = []
    for g, d in enumerate(DILATIONS):
        length = min(WINDOW_KEYS * d, s)
        cols = slice(g * GROUP_WIDTH, (g + 1) * GROUP_WIDTH)
        kg = k32[kv_tail - length:, cols].reshape(length, HEADS, HEAD_DIM)
        vg = v32[kv_tail - length:, cols].reshape(length, HEADS, HEAD_DIM)
        kv_prompt.append(jnp.stack([kg, vg], axis=1)[None, None])
    conv_prompt = ut[6:8][None, None]

    ns = bd * t_len
    xs = x_sample.reshape(ns, D_MODEL)
    st = state_conv[0]
    s0 = jnp.repeat(st[:, 0], t_len, axis=0)
    s1 = jnp.repeat(st[:, 1], t_len, axis=0)
    qs, _, _, (k32s, v32s, ybs, sgas, sgbs, us) = _proj(
        xs, w_in_bf, w_conv[0], (s0, s1), tm=ns, kv_tail=ns, u_tail=ns, q_dtype=F32, dils=NO_DILATION)
    qs = jnp.concatenate([a[0] for a in qs], axis=1)
    packed = jnp.stack([qs, k32s, v32s]).reshape(3, bd, t_len, N_GROUPS, GROUP_WIDTH)
    qkv_t = jnp.transpose(packed, (1, 4, 0, 3, 2)).reshape(bd, GROUP_WIDTH, 3 * N_GROUPS * t_len)
    qkv_t = jnp.pad(qkv_t, ((0, 0), (0, 0), (0, 128 - 3 * N_GROUPS * t_len)))
    caches = (cache_attn_w128[0], cache_attn_w512[0], cache_attn_w2048[0])
    caches_t = [jnp.transpose(c, (0, 2, 3, 4, 1)).reshape(bd, 2, GROUP_WIDTH, c.shape[1]) for c in caches]
    pair = (bd, t_len, N_GROUPS, HEADS // 2, 2, HEAD_DIM)
    q6 = jnp.transpose(qs.reshape(pair), (0, 3, 2, 4, 1, 5))
    zeros = jnp.zeros_like(q6[:, :, :, 0])
    qbd = jnp.stack([jnp.concatenate([q6[:, :, :, 0], zeros], axis=-1),
                     jnp.concatenate([zeros, q6[:, :, :, 1]], axis=-1)], axis=3)
    qbd = qbd.reshape(bd, HEADS // 2, N_GROUPS, 2 * t_len, 128)
    new_rows = jnp.stack([k32s, v32s]).reshape(2, bd, t_len, N_GROUPS, HEADS // 2, 128)
    new_rows = jnp.transpose(new_rows, (1, 4, 3, 0, 2, 5))
    bcs = [t.reshape(HEADS // 2, 2 * t_len, t.shape[-1]) for t in bcs]
    bn = bn.reshape(N_GROUPS, HEADS // 2, 2 * t_len, t_len)
    n0, n1, n2, o_s, lse_s = _sample_cache(qbd, new_rows, qkv_t, caches_t, bcs, bn)

    def unpack(a):
        a = a[:, :N_GROUPS * t_len].reshape(bd, N_GROUPS, t_len, GROUP_WIDTH)
        return jnp.transpose(a, (1, 0, 2, 3)).reshape(N_GROUPS, 1, ns, GROUP_WIDTH)

    o_s, lse_s = unpack(o_s), unpack(lse_s)
    x1s, eis, gts = _mix(xs, o_s, lse_s, ybs, sgas, sgbs, w_pa_bf, w_pb_bf, w_o_bf, g1, b1, wr_hi, wr_lo,
                         tm=ns, alpha=alpha, dils=NO_DILATION)
    y_sample = _hier_moe_ln(x1s, eis, gts, wg, wu, wd, g2, b2, tl=ns, tm=ns, bm=128, alpha=alpha)
    y_sample = y_sample.reshape(bd, t_len, D_MODEL)

    kv_sample = [jnp.transpose(c.reshape(bd, 2, HEADS, HEAD_DIM, c.shape[-1]), (0, 4, 1, 2, 3))[None]
                 for c in (n0, n1, n2)]
    conv_sample = us.reshape(bd, t_len, CONV_CHANNELS)[:, t_len - 2:][None]

    return (y_prompt, y_sample, kv_prompt[0], kv_prompt[1], kv_prompt[2], conv_prompt,
            kv_sample[0], kv_sample[1], kv_sample[2], conv_sample)
```

```python
import functools
import math

import numpy as np
import jax
import jax.numpy as jnp
from jax import lax
from jax.experimental import pallas as pl
from jax.experimental.pallas import tpu as pltpu

F32 = jnp.float32
BF16 = jnp.bfloat16
I32 = jnp.int32

D_MODEL = 1024
N_GROUPS = 3
HEADS = 8
HEAD_DIM = 64
GROUP_WIDTH = HEADS * HEAD_DIM
ATTN_WIDTH = N_GROUPS * GROUP_WIDTH
DILATIONS = (1, 4, 16)
NO_DILATION = (1, 1, 1)
WINDOW_KEYS = 128
N_KEYS = WINDOW_KEYS + 1
N_BUCKETS = 32
MAX_EXACT = 16
MAX_DISTANCE = 2048
CONV_CHANNELS = 512
N_EXPERT_GROUPS = 4
EXPERTS_PER_GROUP = 8
N_EXPERTS = 32
TOP_K = 2
D_EXPERT = 512
LN_EPS = 1e-5
PROJ_WIDTH = 3 * ATTN_WIDTH + 3 * CONV_CHANNELS + 2 * D_MODEL
ROUTER_ROWS = 8 + N_EXPERTS
Q_BLOCK = 128
T_NEW = 4
NEG_INF = float("-inf")
VMEM_LIMIT = 56 * 1024 * 1024


def _sigmoid(x):
    return 1.0 / (1.0 + jnp.exp(-x))


def _params(limit=VMEM_LIMIT):
    return pltpu.CompilerParams(vmem_limit_bytes=limit)


def _proj_kernel(*refs, tm, tail_rows, sample_mode, dils):
    n_in = 5 if sample_mode else 3
    x_ref, w_ref, wc_ref = refs[0:3]
    outs = refs[n_in:]
    q_refs, k_refs, v_refs = outs[0:3], outs[3:6], outs[6:9]
    k32_ref, v32_ref, yb_ref, sga_ref, sgb_ref, ut_ref, cls_ref = outs[9:16]
    xb = x_ref[...].astype(BF16)

    def col(c0, width):
        return jnp.dot(xb, w_ref[:, c0:c0 + width], preferred_element_type=F32)

    def write_classes(val, group_refs):
        for g, d in enumerate(dils):
            part = val[:, g * GROUP_WIDTH:(g + 1) * GROUP_WIDTH]
            ref = group_refs[g]
            if d == 1:
                ref[0] = part.astype(ref.dtype)
            else:
                for kk in range(GROUP_WIDTH // 128):
                    lanes = slice(kk * 128, (kk + 1) * 128)
                    cls_ref[kk] = part[:, lanes]
                    for c in range(d):
                        ref[c, :, lanes] = cls_ref[kk, pl.ds(c, tm // d, stride=d), :].astype(ref.dtype)

    write_classes(col(0, ATTN_WIDTH), q_refs)
    k = col(ATTN_WIDTH, ATTN_WIDTH)
    k32_ref[...] = k
    write_classes(k, k_refs)
    v = col(2 * ATTN_WIDTH, ATTN_WIDTH)
    v32_ref[...] = v
    write_classes(v, v_refs)

    c0 = 3 * ATTN_WIDTH
    bg = col(c0, CONV_CHANNELS)
    u = col(c0 + CONV_CHANNELS, CONV_CHANNELS) * col(c0 + 2 * CONV_CHANNELS, CONV_CHANNELS)
    row = lax.broadcasted_iota(I32, (tm, CONV_CHANNELS), 0)
    r1 = pltpu.roll(u, 1, axis=0)
    r2 = pltpu.roll(u, 2, axis=0)
    if sample_mode:
        s0 = refs[3][...]
        s1 = refs[4][...]
        t = row & (T_NEW - 1)
        prev1 = jnp.where(t == 0, s1, r1)
        prev2 = jnp.where(t == 0, s0, jnp.where(t == 1, s1, r2))
    else:
        carry_ref = outs[16]

        @pl.when(pl.program_id(0) == 0)
        def _():
            carry_ref[...] = jnp.zeros_like(carry_ref)
        c6 = carry_ref[6:7, :]
        c7 = carry_ref[7:8, :]
        prev1 = jnp.where(row == 0, c7, r1)
        prev2 = jnp.where(row == 0, c6, jnp.where(row == 1, c7, r2))
        carry_ref[...] = u[tm - 8:tm, :]
    conv = prev2 * wc_ref[0:1, :] + prev1 * wc_ref[1:2, :] + u * wc_ref[2:3, :]
    yb_ref[...] = (bg * conv).astype(BF16)
    ut_ref[...] = u[tm - tail_rows:tm, :]

    c1 = c0 + 3 * CONV_CHANNELS
    sga_ref[...] = _sigmoid(col(c1, D_MODEL)).astype(BF16)
    sgb_ref[...] = _sigmoid(col(c1 + D_MODEL, D_MODEL)).astype(BF16)


def _proj(x, w_in_bf, w_conv, conv_prev, *, tm, kv_tail, u_tail, q_dtype, dils):
    n = x.shape[0]
    sample_mode = conv_prev is not None
    nt = n // tm
    tail_first = (n - kv_tail) // tm

    def row_spec(width):
        return pl.BlockSpec((tm, width), lambda i: (i, 0))

    def tail_spec(width):
        return pl.BlockSpec((tm, width), lambda i: (jnp.maximum(i - tail_first, 0), 0))

    def class_spec(d):
        return pl.BlockSpec((d, tm // d, GROUP_WIDTH), lambda i: (0, i, 0))

    def class_shape(d, dtype):
        return jax.ShapeDtypeStruct((d, n // d, GROUP_WIDTH), dtype)

    in_specs = [
        row_spec(D_MODEL),
        pl.BlockSpec((D_MODEL, PROJ_WIDTH), lambda i: (0, 0), pipeline_mode=pl.Buffered(1)),
        pl.BlockSpec((3, CONV_CHANNELS), lambda i: (0, 0)),
    ]
    args = [x, w_in_bf, w_conv]
    scratch = [pltpu.VMEM((GROUP_WIDTH // 128, tm, 128), F32)]
    if sample_mode:
        in_specs += [row_spec(CONV_CHANNELS), row_spec(CONV_CHANNELS)]
        args += [conv_prev[0], conv_prev[1]]
    else:
        scratch.append(pltpu.VMEM((8, CONV_CHANNELS), F32))
    out_shape = (
        [class_shape(d, q_dtype) for d in dils] + [class_shape(d, BF16) for d in dils] * 2
        + [jax.ShapeDtypeStruct((kv_tail, ATTN_WIDTH), F32),
           jax.ShapeDtypeStruct((kv_tail, ATTN_WIDTH), F32),
           jax.ShapeDtypeStruct((n, CONV_CHANNELS), BF16),
           jax.ShapeDtypeStruct((n, D_MODEL), BF16),
           jax.ShapeDtypeStruct((n, D_MODEL), BF16),
           jax.ShapeDtypeStruct((u_tail, CONV_CHANNELS), F32)])
    out_specs = (
        [class_spec(d) for d in dils] * 3
        + [tail_spec(ATTN_WIDTH), tail_spec(ATTN_WIDTH),
           row_spec(CONV_CHANNELS), row_spec(D_MODEL), row_spec(D_MODEL),
           pl.BlockSpec((u_tail, CONV_CHANNELS), lambda i: (0, 0))])
    res = pl.pallas_call(
        functools.partial(_proj_kernel, tm=tm, tail_rows=u_tail, sample_mode=sample_mode, dils=dils),
        grid=(nt,),
        in_specs=in_specs,
        out_specs=out_specs,
        out_shape=out_shape,
        scratch_shapes=scratch,
        compiler_params=_params(),
        name="proj",
    )(*args)
    return res[0:3], res[3:6], res[6:9], res[9:]


Q_BLOCKS_PER_STEP = 4


def _attn_kernel(q_ref, kp_ref, kc_ref, vp_ref, vc_ref, tb_ref, o_ref, lse_ref):
    lane = lax.broadcasted_iota(I32, (Q_BLOCK, 128), 1)
    first = lane < HEAD_DIM
    scale = HEAD_DIM ** -0.5
    has_prev = jnp.minimum(pl.program_id(1), 1)
    for sub in range(Q_BLOCKS_PER_STEP):
        rows = slice(sub * Q_BLOCK, (sub + 1) * Q_BLOCK)
        band = slice((sub - 1) * Q_BLOCK, (sub + 1) * Q_BLOCK)
        for pr in range(HEADS // 2):
            sl = slice(pr * 128, (pr + 1) * 128)
            if sub == 0:
                k = jnp.concatenate([kp_ref[:, sl], kc_ref[rows, sl]], axis=0)
                v = jnp.concatenate([vp_ref[:, sl], vc_ref[rows, sl]], axis=0)
                bias = tb_ref[has_prev, pr]
            else:
                k = kc_ref[band, sl]
                v = vc_ref[band, sl]
                bias = tb_ref[1, pr]
            qf = q_ref[rows, sl].astype(F32) * scale
            qq = jnp.concatenate([jnp.where(first, qf, 0.0), jnp.where(first, 0.0, qf)], axis=0).astype(BF16)
            s = lax.dot_general(qq, k, (((1,), (1,)), ((), ())), preferred_element_type=F32) + bias
            m = jnp.max(s, axis=-1, keepdims=True)
            p = jnp.exp(s - m)
            l = jnp.sum(p, axis=-1, keepdims=True)
            o = jnp.dot(p.astype(BF16), v, preferred_element_type=F32) / l
            lse = m + jnp.log(l)
            o_ref[rows, sl] = jnp.where(first, o[:Q_BLOCK], o[Q_BLOCK:]).astype(o_ref.dtype)
            lse_ref[rows, sl] = jnp.where(first, jnp.broadcast_to(lse[:Q_BLOCK], (Q_BLOCK, 128)),
                                          jnp.broadcast_to(lse[Q_BLOCK:], (Q_BLOCK, 128)))


def _attn_prompt_group(q, kb, vb, tb, g):
    d, rows = q.shape[0], q.shape[1]
    nq = Q_BLOCKS_PER_STEP
    cur = pl.BlockSpec((None, nq * Q_BLOCK, GROUP_WIDTH), lambda c, i: (c, i, 0))
    prev = pl.BlockSpec((None, Q_BLOCK, GROUP_WIDTH), lambda c, i: (c, jnp.maximum(i * nq - 1, 0), 0))
    return pl.pallas_call(
        _attn_kernel,
        grid=(d, rows // (nq * Q_BLOCK)),
        in_specs=[cur, prev, cur, prev, cur, pl.BlockSpec(tb.shape, lambda c, i: (0, 0, 0, 0))],
        out_specs=[cur, cur],
        out_shape=[jax.ShapeDtypeStruct((d, rows, GROUP_WIDTH), BF16),
                   jax.ShapeDtypeStruct((d, rows, GROUP_WIDTH), F32)],
        compiler_params=_params(),
        name=f"attn_prompt_g{g}",
    )(q, kb, kb, vb, vb, tb)


PACK_Q, PACK_K, PACK_V = 0, N_GROUPS * T_NEW, 2 * N_GROUPS * T_NEW


PAIRS_PER_STEP = 2


def _sample_cache_kernel(qbd_ref, nr_ref, qkv_ref, c0_ref, c1_ref, c2_ref, b0_ref, b1_ref, b2_ref, bn_ref,
                         n0_ref, n1_ref, n2_ref, o_ref, lse_ref):
    for pp in range(PAIRS_PER_STEP):
        rows = pl.ds(pp * 128, 128)
        kv = pl.ds(0, 2)
        _sample_pair(qbd_ref.at[pp], nr_ref.at[pp], qkv_ref.at[rows],
                     [c.at[kv, rows] for c in (c0_ref, c1_ref, c2_ref)],
                     [b.at[pp] for b in (b0_ref, b1_ref, b2_ref)], bn_ref.at[pl.ds(0, N_GROUPS), pp],
                     [c.at[kv, rows] for c in (n0_ref, n1_ref, n2_ref)],
                     o_ref.at[pl.ds(0, 16), rows], lse_ref.at[pl.ds(0, 16), rows])


def _sample_pair(qbd_ref, nr_ref, qkv_ref, c_refs, b_refs, bn_ref, n_refs, o_ref, lse_ref):
    scale = HEAD_DIM ** -0.5
    nt = (((1,), (1,)), ((), ()))
    lane = lax.broadcasted_iota(I32, (128, 128), 1)
    head0 = lane[0:T_NEW] < HEAD_DIM
    for g, (c_ref, b_ref, n_ref) in enumerate(zip(c_refs, b_refs, n_refs)):
        length = c_ref.shape[-1]
        k_new = qkv_ref[:, PACK_K + g * T_NEW:PACK_K + (g + 1) * T_NEW]
        v_new = qkv_ref[:, PACK_V + g * T_NEW:PACK_V + (g + 1) * T_NEW]
        qbd = qbd_ref[g]
        s_c = jnp.dot(qbd.astype(BF16), c_ref[0].astype(BF16), preferred_element_type=F32) * scale + b_ref[...]
        bn = bn_ref[g]
        s_n = [jnp.sum(qbd * nr_ref[g, 0, tn:tn + 1, :], axis=1, keepdims=True) * scale + bn[:, tn:tn + 1]
               for tn in range(T_NEW)]
        m = jnp.max(s_c, axis=1, keepdims=True)
        for x in s_n:
            m = jnp.maximum(m, x)
        p_c = jnp.exp(s_c - m)
        p_n = [jnp.exp(x - m) for x in s_n]
        l = jnp.sum(p_c, axis=1, keepdims=True)
        acc = lax.dot_general(p_c.astype(BF16), c_ref[1].astype(BF16), nt, preferred_element_type=F32)
        for tn in range(T_NEW):
            l = l + p_n[tn]
            acc = acc + p_n[tn] * nr_ref[g, 1, tn:tn + 1, :]
        o = acc / l
        lse = jnp.broadcast_to(m + jnp.log(l), (2 * T_NEW, 128))
        o_ref[g * T_NEW:(g + 1) * T_NEW, :] = jnp.where(head0, o[0:T_NEW], o[T_NEW:])
        lse_ref[g * T_NEW:(g + 1) * T_NEW, :] = jnp.where(head0, lse[0:T_NEW], lse[T_NEW:])

        for kv, new in ((0, k_new), (1, v_new)):
            rolled = pltpu.roll(c_ref[kv], length - T_NEW, axis=1)
            tail = rolled[:, length - 128:]
            for t in range(T_NEW):
                tail = jnp.where(lane == 128 - T_NEW + t, new[:, t:t + 1], tail)
            if length > 128:
                n_ref[kv, :, 0:length - 128] = rolled[:, 0:length - 128]
            n_ref[kv, :, length - 128:] = tail
    pad_rows = slice(N_GROUPS * T_NEW, 16)
    o_ref[pad_rows, :] = jnp.zeros((16 - N_GROUPS * T_NEW, 128), F32)
    lse_ref[pad_rows, :] = jnp.zeros((16 - N_GROUPS * T_NEW, 128), F32)


def _sample_cache(qbd, new_rows, qkv_t, caches_t, bcs, bn):
    b = qkv_t.shape[0]

    pp = PAIRS_PER_STEP

    def cache_spec(c):
        return pl.BlockSpec((None, 2, pp * 128, c.shape[-1]), lambda i, h: (i, 0, h, 0))

    def bias_spec(t):
        return pl.BlockSpec((pp, 2 * T_NEW, t.shape[-1]), lambda i, h: (h, 0, 0))

    out = pl.BlockSpec((None, 16, pp * 128), lambda i, h: (i, 0, h))
    return pl.pallas_call(
        _sample_cache_kernel,
        grid=(b, HEADS // 2 // pp),
        in_specs=[pl.BlockSpec((None, pp, N_GROUPS, 2 * T_NEW, 128), lambda i, h: (i, h, 0, 0, 0)),
                  pl.BlockSpec((None, pp, N_GROUPS, 2, T_NEW, 128), lambda i, h: (i, h, 0, 0, 0, 0)),
                  pl.BlockSpec((None, pp * 128, 128), lambda i, h: (i, h, 0))]
                 + [cache_spec(c) for c in caches_t] + [bias_spec(t) for t in bcs]
                 + [pl.BlockSpec((N_GROUPS, pp, 2 * T_NEW, T_NEW), lambda i, h: (0, h, 0, 0))],
        out_specs=[cache_spec(c) for c in caches_t] + [out, out],
        out_shape=[jax.ShapeDtypeStruct(c.shape, c.dtype) for c in caches_t]
                  + [jax.ShapeDtypeStruct((b, 16, GROUP_WIDTH), F32)] * 2,
        compiler_params=_params(),
        name="sample_cache",
    )(qbd, new_rows, qkv_t, *caches_t, *bcs, bn)


def _mix_kernel(*refs, tm, alpha, dils):
    (x_ref, o0_ref, o1_ref, o2_ref, l0_ref, l1_ref, l2_ref, yb_ref, sga_ref, sgb_ref,
     wpa_ref, wpb_ref, wo_ref, g_ref, b_ref, wrh_ref, wrl_ref, x1_ref, ei_ref, gt_ref) = refs[0:20]
    scratch = list(refs[20:])

    def natural(ref, d):
        if d == 1:
            return ref[0].astype(F32)
        scr = scratch.pop()
        for kk in range(GROUP_WIDTH // 128):
            for c in range(d):
                scr[kk, pl.ds(c, tm // d, stride=d), :] = ref[c, :, kk * 128:(kk + 1) * 128].astype(F32)
        return jnp.concatenate([scr[kk] for kk in range(GROUP_WIDTH // 128)], axis=1)

    l0, l1, l2 = natural(l0_ref, dils[0]), natural(l1_ref, dils[1]), natural(l2_ref, dils[2])
    mx = jnp.maximum(jnp.maximum(l0, l1), l2)
    e0 = jnp.exp(l0 - mx)
    e1 = jnp.exp(l1 - mx)
    e2 = jnp.exp(l2 - mx)
    ya = (e0 * natural(o0_ref, dils[0]) + e1 * natural(o1_ref, dils[1]) + e2 * natural(o2_ref, dils[2])) / (e0 + e1 + e2)
    pa = jnp.dot(ya.astype(BF16), wpa_ref[...], preferred_element_type=F32)
    pb = jnp.dot(yb_ref[...], wpb_ref[...], preferred_element_type=F32)
    gated = sga_ref[...].astype(F32) * pa + sgb_ref[...].astype(F32) * pb
    mix = jnp.dot(gated.astype(BF16), wo_ref[...], preferred_element_type=F32)
    z = alpha * x_ref[...] + mix
    mu = jnp.mean(z, axis=-1, keepdims=True)
    zc = z - mu
    var = jnp.mean(zc * zc, axis=-1, keepdims=True)
    x1 = zc * lax.rsqrt(var + LN_EPS) * g_ref[...] + b_ref[...]
    x1_ref[...] = x1

    xh = x1.astype(BF16)
    xl = (x1 - xh.astype(F32)).astype(BF16)
    nt = (((1,), (1,)), ((), ()))
    wrh = wrh_ref[...]
    lt = (lax.dot_general(wrh, xh, nt, preferred_element_type=F32)
          + lax.dot_general(wrh, xl, nt, preferred_element_type=F32)
          + lax.dot_general(wrl_ref[...], xh, nt, preferred_element_type=F32))

    gl = lt[0:N_EXPERT_GROUPS]
    gmax = jnp.max(gl, axis=0, keepdims=True)
    idx4 = lax.broadcasted_iota(I32, (N_EXPERT_GROUPS, tm), 0)
    g_idx = jnp.min(jnp.where(gl == gmax, idx4, N_EXPERT_GROUPS), axis=0, keepdims=True)
    g_prob = 1.0 / jnp.sum(jnp.exp(gl - gmax), axis=0, keepdims=True)
    e_sel = lt[8:16]
    for grp in range(1, N_EXPERT_GROUPS):
        e_sel = jnp.where(g_idx == grp, lt[8 + 8 * grp:16 + 8 * grp], e_sel)
    idx8 = lax.broadcasted_iota(I32, (EXPERTS_PER_GROUP, tm), 0)
    v1 = jnp.max(e_sel, axis=0, keepdims=True)
    i1 = jnp.min(jnp.where(e_sel == v1, idx8, EXPERTS_PER_GROUP), axis=0, keepdims=True)
    rest = jnp.where(idx8 == i1, NEG_INF, e_sel)
    v2 = jnp.max(rest, axis=0, keepdims=True)
    i2 = jnp.min(jnp.where(rest == v2, idx8, EXPERTS_PER_GROUP), axis=0, keepdims=True)
    r = jnp.exp(v2 - v1)
    gate1 = g_prob / (1.0 + r)
    gate2 = g_prob * r / (1.0 + r)
    ex1 = g_idx * EXPERTS_PER_GROUP + i1
    ex2 = g_idx * EXPERTS_PER_GROUP + i2
    ei_ref[...] = jnp.where(idx8 == 0, ex1, jnp.where(idx8 == 1, ex2, 0))
    gt_ref[...] = jnp.where(idx8 == 0, gate1, jnp.where(idx8 == 1, gate2, 0.0))


def _mix(x, o, lse, yb, sga, sgb, w_pa, w_pb, w_o, ln_g, ln_b, wr_hi, wr_lo, *, tm, alpha, dils):
    n = x.shape[0]

    def row_spec(width):
        return pl.BlockSpec((tm, width), lambda i: (i, 0))

    def class_spec(d):
        return pl.BlockSpec((d, tm // d, GROUP_WIDTH), lambda i: (0, i, 0))

    def full(a):
        return pl.BlockSpec(a.shape, lambda i: (0,) * a.ndim)

    lane_spec = pl.BlockSpec((8, tm), lambda i: (0, i))
    n_scratch = 2 * sum(1 for d in dils if d > 1)
    return pl.pallas_call(
        functools.partial(_mix_kernel, tm=tm, alpha=alpha, dils=dils),
        grid=(n // tm,),
        in_specs=[row_spec(D_MODEL)] + [class_spec(d) for d in dils] * 2 + [row_spec(CONV_CHANNELS)]
                 + [row_spec(D_MODEL)] * 2
                 + [full(w_pa), full(w_pb), full(w_o), full(ln_g), full(ln_b), full(wr_hi), full(wr_lo)],
        out_specs=[row_spec(D_MODEL), lane_spec, lane_spec],
        out_shape=[jax.ShapeDtypeStruct((n, D_MODEL), F32),
                   jax.ShapeDtypeStruct((8, n), I32),
                   jax.ShapeDtypeStruct((8, n), F32)],
        scratch_shapes=[pltpu.VMEM((GROUP_WIDTH // 128, tm, 128), F32)] * n_scratch,
        compiler_params=_params(),
        name="mix",
    )(x, o[0], o[1], o[2], lse[0], lse[1], lse[2], yb, sga, sgb, w_pa, w_pb, w_o, ln_g, ln_b, wr_hi, wr_lo)


def _slot_kernel(ei_ref, slot_ref, cnt_ref, carry_ref, start_ref, *, tl, bm):
    phase = pl.program_id(0)
    i = pl.program_id(1)

    @pl.when(jnp.logical_and(phase == 0, i == 0))
    def _():
        carry_ref[...] = jnp.zeros_like(carry_ref)

    ex = lax.broadcasted_iota(I32, (N_EXPERTS, tl), 0)
    oh0 = (ex == ei_ref[0:1, :]).astype(F32)
    oh1 = (ex == ei_ref[1:2, :]).astype(F32)
    cnt0 = jnp.sum(oh0, axis=1, keepdims=True)
    cnt1 = jnp.sum(oh1, axis=1, keepdims=True)

    @pl.when(phase == 0)
    def _():
        total = carry_ref[...] + cnt0 + cnt1
        carry_ref[...] = total
        cnt_ref[...] = total.astype(I32)
        slot_ref[...] = jnp.zeros_like(slot_ref)

    @pl.when(jnp.logical_and(phase == 1, i == 0))
    def _():
        blocks = jnp.floor((carry_ref[...] + (bm - 1)) * (1.0 / bm))
        a = lax.broadcasted_iota(I32, (N_EXPERTS, N_EXPERTS), 0)
        b = lax.broadcasted_iota(I32, (N_EXPERTS, N_EXPERTS), 1)
        before = (b < a).astype(BF16)
        start_ref[...] = jnp.dot(before, blocks.astype(BF16), preferred_element_type=F32) * bm
        carry_ref[...] = jnp.zeros_like(carry_ref)

    @pl.when(phase == 1)
    def _():
        a = lax.broadcasted_iota(I32, (tl, tl), 0)
        b = lax.broadcasted_iota(I32, (tl, tl), 1)
        upper = (a < b).astype(BF16)
        pre0 = jnp.dot(oh0.astype(BF16), upper, preferred_element_type=F32)
        pre1 = jnp.dot(oh1.astype(BF16), upper, preferred_element_type=F32)
        base = carry_ref[:, 0:1] + start_ref[:, 0:1]
        slot0 = jnp.sum(oh0 * (pre0 + base), axis=0, keepdims=True)
        slot1 = jnp.sum(oh1 * (pre1 + cnt0 + base), axis=0, keepdims=True)
        row = lax.broadcasted_iota(I32, (8, tl), 0)
        slot_ref[...] = jnp.where(row == 0, slot0.astype(I32), jnp.where(row == 1, slot1.astype(I32), 0))
        carry_ref[...] = carry_ref[...] + cnt0 + cnt1


def _slots(ei, *, tl, bm):
    n = ei.shape[1]
    return pl.pallas_call(
        functools.partial(_slot_kernel, tl=tl, bm=bm),
        grid=(2, n // tl),
        in_specs=[pl.BlockSpec((8, tl), lambda p, i: (0, i))],
        out_specs=[pl.BlockSpec((8, tl), lambda p, i: (0, i * p)),
                   pl.BlockSpec((N_EXPERTS, 128), lambda p, i: (0, 0))],
        out_shape=[jax.ShapeDtypeStruct((8, n), I32), jax.ShapeDtypeStruct((N_EXPERTS, 128), I32)],
        scratch_shapes=[pltpu.VMEM((N_EXPERTS, 128), F32), pltpu.VMEM((N_EXPERTS, 128), F32)],
        name="moe_slots",
    )(ei)


def _row_copy(src, src_row, dst, dst_row, sem):
    return pltpu.make_async_copy(src.at[pl.ds(src_row, 1)], dst.at[pl.ds(dst_row, 1)], sem)


ISSUE_UNROLL = 8


def _dispatch_kernel(slot_ref, pend_ref, cnt_ref, x1_ref, buf_ref, zero_ref, sem, zsem, *, n, tm, bm):
    base = pl.program_id(0) * tm

    @pl.when(pl.program_id(0) == 0)
    def _():
        zero_ref[...] = jnp.zeros_like(zero_ref)

        def zero_copy(e):
            start = pl.multiple_of(pend_ref[e] - bm, bm)
            return pltpu.make_async_copy(zero_ref, buf_ref.at[pl.ds(start, bm)], zsem)

        def tail_copy(blk):
            return pltpu.make_async_copy(zero_ref, buf_ref.at[pl.ds(pl.multiple_of(blk * bm, bm), bm)], zsem)

        def tail_start(blk, c):
            tail_copy(blk).start()
            return c

        def tail_wait(blk, c):
            tail_copy(blk).wait()
            return c

        for e in range(N_EXPERTS):
            @pl.when(cnt_ref[e] > 0)
            def _(e=e):
                zero_copy(e).start()
        first_unused = pend_ref[N_EXPERTS - 1] // bm
        lax.fori_loop(first_unused, buf_ref.shape[0] // bm, tail_start, 0)
        for e in range(N_EXPERTS):
            @pl.when(cnt_ref[e] > 0)
            def _(e=e):
                zero_copy(e).wait()
        lax.fori_loop(first_unused, buf_ref.shape[0] // bm, tail_wait, 0)

    def body(r, carry):
        for k in range(TOP_K):
            _row_copy(x1_ref, r, buf_ref, slot_ref[k * n + base + r], sem).start()
        return carry

    lax.fori_loop(0, tm, body, 0, unroll=ISSUE_UNROLL)
    for _ in range(TOP_K):
        pltpu.make_async_copy(x1_ref, buf_ref.at[pl.ds(0, tm)], sem).wait()


def _dispatch(slot_flat, pend, counts, x1, *, tm, bm, nblk):
    n = x1.shape[0]
    return pl.pallas_call(
        functools.partial(_dispatch_kernel, n=n, tm=tm, bm=bm),
        grid_spec=pltpu.PrefetchScalarGridSpec(
            num_scalar_prefetch=3,
            grid=(n // tm,),
            in_specs=[pl.BlockSpec((tm, D_MODEL), lambda i, s, p, c: (i, 0))],
            out_specs=pl.BlockSpec(memory_space=pl.ANY),
            scratch_shapes=[pltpu.VMEM((bm, D_MODEL), F32), pltpu.SemaphoreType.DMA(()),
                            pltpu.SemaphoreType.DMA(())],
        ),
        out_shape=jax.ShapeDtypeStruct((nblk * bm, D_MODEL), F32),
        name="moe_dispatch",
    )(slot_flat, pend, counts, x1)


def _expert_kernel(be_ref, nu_ref, xb_ref, wg_ref, wu_ref, wd_ref, out_ref, wg_bf, wu_bf, wd_bf):
    j = pl.program_id(0)
    used = j < nu_ref[0]
    changed = jnp.logical_or(j == 0, be_ref[j] != be_ref[jnp.maximum(j - 1, 0)])

    @pl.when(jnp.logical_and(used, changed))
    def _():
        wg_bf[...] = wg_ref[...].astype(BF16)
        wu_bf[...] = wu_ref[...].astype(BF16)
        wd_bf[...] = wd_ref[...].astype(BF16)

    @pl.when(used)
    def _():
        xb = xb_ref[...].astype(BF16)
        a = jnp.dot(xb, wg_bf[...], preferred_element_type=F32)
        b = jnp.dot(xb, wu_bf[...], preferred_element_type=F32)
        h = (a * _sigmoid(a)) * b
        out_ref[...] = jnp.dot(h.astype(BF16), wd_bf[...], preferred_element_type=F32)

    @pl.when(jnp.logical_not(used))
    def _():
        out_ref[...] = jnp.zeros_like(out_ref)


def _experts(block_expert, n_used, buf, w_g, w_u, w_d, *, bm):
    nblk = buf.shape[0] // bm

    def row_map(j, be, nu):
        return (jnp.minimum(j, nu[0] - 1), 0)

    def w_map(j, be, nu):
        return (be[j], 0, 0)

    return pl.pallas_call(
        _expert_kernel,
        grid_spec=pltpu.PrefetchScalarGridSpec(
            num_scalar_prefetch=2,
            grid=(nblk,),
            in_specs=[pl.BlockSpec((bm, D_MODEL), row_map),
                      pl.BlockSpec((None, D_MODEL, D_EXPERT), w_map),
                      pl.BlockSpec((None, D_MODEL, D_EXPERT), w_map),
                      pl.BlockSpec((None, D_EXPERT, D_MODEL), w_map)],
            out_specs=pl.BlockSpec((bm, D_MODEL), lambda j, be, nu: (j, 0)),
            scratch_shapes=[pltpu.VMEM((D_MODEL, D_EXPERT), BF16),
                            pltpu.VMEM((D_MODEL, D_EXPERT), BF16),
                            pltpu.VMEM((D_EXPERT, D_MODEL), BF16)],
        ),
        out_shape=jax.ShapeDtypeStruct(buf.shape, F32),
        compiler_params=_params(),
        name="moe_experts",
    )(block_expert, n_used, buf, w_g, w_u, w_d)


def _combine_kernel(slot_ref, x1_ref, gc_ref, g_ref, b_ref, eo_ref, y_ref, rows, sem, *, n, tm, alpha):
    i = pl.program_id(0)
    last = pl.num_programs(0) - 1
    cur = i % 2

    def start(tile, buf, r):
        for k in range(TOP_K):
            _row_copy(eo_ref, slot_ref[k * n + tile * tm + r], rows.at[buf, k], r, sem.at[buf]).start()

    def wait(buf):
        for k in range(TOP_K):
            pltpu.make_async_copy(eo_ref.at[pl.ds(0, tm)], rows.at[buf, k], sem.at[buf]).wait()

    @pl.when(i == 0)
    def _():
        def body(r, c):
            start(0, 0, r)
            return c
        lax.fori_loop(0, tm, body, 0, unroll=ISSUE_UNROLL)

    wait(cur)
    nxt = jnp.minimum(i + 1, last)
    for r in range(tm):
        start(nxt, 1 - cur, r)
    gc = gc_ref[...]
    z = alpha * x1_ref[...] + gc[:, 0:1] * rows[cur, 0] + gc[:, 1:2] * rows[cur, 1]
    mu = jnp.mean(z, axis=-1, keepdims=True)
    zc = z - mu
    var = jnp.mean(zc * zc, axis=-1, keepdims=True)
    y_ref[...] = zc * lax.rsqrt(var + LN_EPS) * g_ref[...] + b_ref[...]

    @pl.when(i == last)
    def _():
        wait(1 - cur)


def _combine(slot_flat, x1, gate_cols, ln_g, ln_b, expert_out, *, tm, alpha):
    n = x1.shape[0]
    return pl.pallas_call(
        functools.partial(_combine_kernel, n=n, tm=tm, alpha=alpha),
        grid_spec=pltpu.PrefetchScalarGridSpec(
            num_scalar_prefetch=1,
            grid=(n // tm,),
            in_specs=[pl.BlockSpec((tm, D_MODEL), lambda i, s: (i, 0)),
                      pl.BlockSpec((tm, TOP_K), lambda i, s: (i, 0)),
                      pl.BlockSpec((1, D_MODEL), lambda i, s: (0, 0)),
                      pl.BlockSpec((1, D_MODEL), lambda i, s: (0, 0)),
                      pl.BlockSpec(memory_space=pl.ANY)],
            out_specs=pl.BlockSpec((tm, D_MODEL), lambda i, s: (i, 0)),
            scratch_shapes=[pltpu.VMEM((2, TOP_K, tm, D_MODEL), F32), pltpu.SemaphoreType.DMA((2,))],
        ),
        out_shape=jax.ShapeDtypeStruct((n, D_MODEL), F32),
        compiler_params=_params(),
        name="moe_combine",
    )(slot_flat, x1, gate_cols, ln_g, ln_b, expert_out)


def _hier_moe_ln(x1, ei, gt, w_g, w_u, w_d, ln_g, ln_b, *, tl, tm, bm, alpha):
    n = x1.shape[0]
    m = n * TOP_K
    slot, cnt = _slots(ei, tl=tl, bm=bm)
    counts = cnt[:, 0]
    pend = jnp.cumsum((counts + bm - 1) // bm * bm)
    nblk = (m + N_EXPERTS * (bm - 1) + bm - 1) // bm
    blk_start = jnp.arange(nblk, dtype=I32) * bm
    n_used = (pend[-1] // bm).astype(I32)
    be = jnp.minimum(jnp.sum(pend[None, :] <= blk_start[:, None], axis=1), N_EXPERTS - 1).astype(I32)
    be = jnp.where(jnp.arange(nblk) < n_used, be, jnp.take(be, n_used - 1))
    slot_flat = slot[0:TOP_K].reshape(m)
    buf = _dispatch(slot_flat, pend.astype(I32), counts, x1, tm=tm, bm=bm, nblk=nblk)
    eo = _experts(be, n_used.reshape(1), buf, w_g, w_u, w_d, bm=bm)
    gate_cols = gt[0:TOP_K].T
    return _combine(slot_flat, x1, gate_cols, ln_g, ln_b, eo, tm=tm, alpha=alpha)


def _t5_bucket(n):
    nf = jnp.maximum(n, 1).astype(F32)
    large = MAX_EXACT + (jnp.log(nf / MAX_EXACT) / math.log(MAX_DISTANCE / MAX_EXACT)
                         * (N_BUCKETS - MAX_EXACT)).astype(I32)
    large = jnp.minimum(large, N_BUCKETS - 1)
    return jnp.where(n < MAX_EXACT, n, large)


def _bias_per_group(rel_bias):
    offs = jnp.arange(N_KEYS, dtype=I32)[None, :] * jnp.array(DILATIONS, I32)[:, None]
    bucket = _t5_bucket(offs)
    table = rel_bias.reshape(N_BUCKETS, N_GROUPS, HEADS)
    b = table[bucket, jnp.arange(N_GROUPS)[:, None]]
    return jnp.transpose(b, (0, 2, 1)).astype(F32)


def _prompt_bias_tables(bias):
    width = 3 * Q_BLOCK
    neg = jnp.full((N_GROUPS, HEADS, Q_BLOCK - 1), NEG_INF, F32)
    r = jnp.concatenate([neg, bias[:, :, ::-1], neg, jnp.full((N_GROUPS, HEADS, 1), NEG_INF, F32)], axis=-1)
    flat = jnp.tile(r, (1, 1, Q_BLOCK))[:, :, :Q_BLOCK * (width - 1)]
    skew = flat.reshape(N_GROUPS, HEADS, Q_BLOCK, width - 1)
    later = skew[:, :, :, Q_BLOCK - 1:3 * Q_BLOCK - 1]
    has_prev = (np.arange(2 * Q_BLOCK) >= Q_BLOCK)[None, None, None, :]
    first = jnp.where(has_prev, later, NEG_INF)
    tb = jnp.stack([first, later], axis=1)
    return tb.reshape(N_GROUPS, 2, HEADS // 2, 2 * Q_BLOCK, 2 * Q_BLOCK)


def _sample_bias_tables(bias):
    bcs = []
    for g, d in enumerate(DILATIONS):
        rev = bias[g][:, ::-1][:, :WINDOW_KEYS]
        per_t = []
        for t in range(T_NEW):
            if d == 1:
                row = jnp.concatenate([jnp.full((HEADS, t), NEG_INF, F32), rev[:, :WINDOW_KEYS - t]], axis=1)
            else:
                cls = np.arange(d)[None, None, :] == t
                row = jnp.where(cls, rev[:, :, None], NEG_INF).reshape(HEADS, WINDOW_KEYS * d)
            per_t.append(row)
        bcs.append(jnp.stack(per_t, axis=1))
    bn = []
    for g, d in enumerate(DILATIONS):
        rows = []
        for t in range(T_NEW):
            cols = []
            for tn in range(T_NEW):
                ok = (tn <= t) if d == 1 else (tn == t)
                cols.append(bias[g][:, t - tn] if ok else jnp.full((HEADS,), NEG_INF, F32))
            rows.append(jnp.stack(cols, axis=-1))
        bn.append(jnp.stack(rows, axis=1))
    return bcs, jnp.stack(bn)


def _split_bf16(w):
    hi = w.astype(BF16)
    lo = (w - hi.astype(F32)).astype(BF16)
    return hi, lo


def kernel(x_prompt, x_sample, cache_attn_w128, cache_attn_w512, cache_attn_w2048, state_conv, rel_bias, w_in, w_conv, w_pa, w_pb, w_o, ln1_g, ln1_b, w_router_group, w_router_expert, w_expert_gate, w_expert_up, w_expert_down, ln2_g, ln2_b):
    depth = w_in.shape[0]
    assert depth == 1 and x_prompt.shape[0] == 1
    alpha = (2.0 * depth) ** 0.25
    s = x_prompt.shape[1]
    bd, t_len = x_sample.shape[0], x_sample.shape[1]
    assert t_len == T_NEW and s % (DILATIONS[-1] * Q_BLOCK * Q_BLOCKS_PER_STEP) == 0

    bias = _bias_per_group(rel_bias)
    tb = _prompt_bias_tables(bias)
    bcs, bn = _sample_bias_tables(bias)

    w_in_bf = w_in[0].astype(BF16)
    w_pa_bf = w_pa[0].astype(BF16)
    w_pb_bf = w_pb[0].astype(BF16)
    w_o_bf = w_o[0].astype(BF16)
    wr = jnp.zeros((ROUTER_ROWS, D_MODEL), F32)
    wr = wr.at[0:N_EXPERT_GROUPS].set(w_router_group[0].T).at[8:8 + N_EXPERTS].set(w_router_expert[0].T)
    wr_hi, wr_lo = _split_bf16(wr)
    g1, b1 = ln1_g[0][None], ln1_b[0][None]
    g2, b2 = ln2_g[0][None], ln2_b[0][None]
    wg, wu, wd = w_expert_gate[0], w_expert_up[0], w_expert_down[0]

    xp = x_prompt[0]
    kv_tail = min(MAX_DISTANCE, s)
    q, kb, vb, (k32, v32, yb, sga, sgb, ut) = _proj(
        xp, w_in_bf, w_conv[0], None, tm=256, kv_tail=kv_tail, u_tail=8, q_dtype=BF16, dils=DILATIONS)
    o_l = [_attn_prompt_group(q[g], kb[g], vb[g], tb[g], g) for g in range(N_GROUPS)]
    x1, ei, gt = _mix(xp, [a[0] for a in o_l], [a[1] for a in o_l], yb, sga, sgb,
                      w_pa_bf, w_pb_bf, w_o_bf, g1, b1, wr_hi, wr_lo, tm=512, alpha=alpha, dils=DILATIONS)
    y_prompt = _hier_moe_ln(x1, ei, gt, wg, wu, wd, g2, b2, tl=512, tm=256, bm=256, alpha=alpha)[None]

    kv_prompt = []
    for g, d in enumerate(DILATIONS):
        length = min(WINDOW_KEYS * d, s)
        cols = slice(g * GROUP_WIDTH, (g + 1) * GROUP_WIDTH)
        kg = k32[kv_tail - length:, cols].reshape(length, HEADS, HEAD_DIM)
        vg = v32[kv_tail - length:, cols].reshape(length, HEADS, HEAD_DIM)
        kv_prompt.append(jnp.stack([kg, vg], axis=1)[None, None])
    conv_prompt = ut[6:8][None, None]

    ns = bd * t_len
    xs = x_sample.reshape(ns, D_MODEL)
    st = state_conv[0]
    s0 = jnp.repeat(st[:, 0], t_len, axis=0)
    s1 = jnp.repeat(st[:, 1], t_len, axis=0)
    qs, _, _, (k32s, v32s, ybs, sgas, sgbs, us) = _proj(
        xs, w_in_bf, w_conv[0], (s0, s1), tm=ns, kv_tail=ns, u_tail=ns, q_dtype=F32, dils=NO_DILATION)
    qs = jnp.concatenate([a[0] for a in qs], axis=1)
    packed = jnp.stack([qs, k32s, v32s]).reshape(3, bd, t_len, N_GROUPS, GROUP_WIDTH)
    qkv_t = jnp.transpose(packed, (1, 4, 0, 3, 2)).reshape(bd, GROUP_WIDTH, 3 * N_GROUPS * t_len)
    qkv_t = jnp.pad(qkv_t, ((0, 0), (0, 0), (0, 128 - 3 * N_GROUPS * t_len)))
    caches = (cache_attn_w128[0], cache_attn_w512[0], cache_attn_w2048[0])
    caches_t = [jnp.transpose(c, (0, 2, 3, 4, 1)).reshape(bd, 2, GROUP_WIDTH, c.shape[1]) for c in caches]
    pair = (bd, t_len, N_GROUPS, HEADS // 2, 2, HEAD_DIM)
    q6 = jnp.transpose(qs.reshape(pair), (0, 3, 2, 4, 1, 5))
    zeros = jnp.zeros_like(q6[:, :, :, 0])
    qbd = jnp.stack([jnp.concatenate([q6[:, :, :, 0], zeros], axis=-1),
                     jnp.concatenate([zeros, q6[:, :, :, 1]], axis=-1)], axis=3)
    qbd = qbd.reshape(bd, HEADS // 2, N_GROUPS, 2 * t_len, 128)
    new_rows = jnp.stack([k32s, v32s]).reshape(2, bd, t_len, N_GROUPS, HEADS // 2, 128)
    new_rows = jnp.transpose(new_rows, (1, 4, 3, 0, 2, 5))
    bcs = [t.reshape(HEADS // 2, 2 * t_len, t.shape[-1]) for t in bcs]
    bn = bn.reshape(N_GROUPS, HEADS // 2, 2 * t_len, t_len)
    n0, n1, n2, o_s, lse_s = _sample_cache(qbd, new_rows, qkv_t, caches_t, bcs, bn)

    def unpack(a):
        a = a[:, :N_GROUPS * t_len].reshape(bd, N_GROUPS, t_len, GROUP_WIDTH)
        return jnp.transpose(a, (1, 0, 2, 3)).reshape(N_GROUPS, 1, ns, GROUP_WIDTH)

    o_s, lse_s = unpack(o_s), unpack(lse_s)
    x1s, eis, gts = _mix(xs, o_s, lse_s, ybs, sgas, sgbs, w_pa_bf, w_pb_bf, w_o_bf, g1, b1, wr_hi, wr_lo,
                         tm=ns, alpha=alpha, dils=NO_DILATION)
    y_sample = _hier_moe_ln(x1s, eis, gts, wg, wu, wd, g2, b2, tl=ns, tm=ns, bm=128, alpha=alpha)
    y_sample = y_sample.reshape(bd, t_len, D_MODEL)

    kv_sample = [jnp.transpose(c.reshape(bd, 2, HEADS, HEAD_DIM, c.shape[-1]), (0, 4, 1, 2, 3))[None]
                 for c in (n0, n1, n2)]
    conv_sample = us.reshape(bd, t_len, CONV_CHANNELS)[:, t_len - 2:][None]

    return (y_prompt, y_sample, kv_prompt[0], kv_prompt[1], kv_prompt[2], conv_prompt,
            kv_sample[0], kv_sample[1], kv_sample[2], conv_sample)
```

```python
import functools
import math

import numpy as np
import jax
import jax.numpy as jnp
from jax import lax
from jax.experimental import pallas as pl
from jax.experimental.pallas import tpu as pltpu

F32 = jnp.float32
BF16 = jnp.bfloat16
I32 = jnp.int32

D_MODEL = 1024
N_GROUPS = 3
HEADS = 8
HEAD_DIM = 64
GROUP_WIDTH = HEADS * HEAD_DIM
ATTN_WIDTH = N_GROUPS * GROUP_WIDTH
DILATIONS = (1, 4, 16)
NO_DILATION = (1, 1, 1)
WINDOW_KEYS = 128
N_KEYS = WINDOW_KEYS + 1
N_BUCKETS = 32
MAX_EXACT = 16
MAX_DISTANCE = 2048
CONV_CHANNELS = 512
N_EXPERT_GROUPS = 4
EXPERTS_PER_GROUP = 8
N_EXPERTS = 32
TOP_K = 2
D_EXPERT = 512
LN_EPS = 1e-5
PROJ_WIDTH = 3 * ATTN_WIDTH + 3 * CONV_CHANNELS + 2 * D_MODEL
ROUTER_ROWS = 8 + N_EXPERTS
Q_BLOCK = 128
T_NEW = 4
NEG_INF = float("-inf")
VMEM_LIMIT = 56 * 1024 * 1024


def _sigmoid(x):
    return 1.0 / (1.0 + jnp.exp(-x))


def _params(limit=VMEM_LIMIT):
    return pltpu.CompilerParams(vmem_limit_bytes=limit)


def _proj_kernel(*refs, tm, tail_rows, sample_mode, dils, kv_f32):
    n_in = 5 if sample_mode else 3
    x_ref, w_ref, wc_ref = refs[0:3]
    outs = list(refs[n_in:])
    q_refs, k_refs, v_refs = outs[0:3], outs[3:6], outs[6:9]
    del outs[0:9]
    k32_ref, v32_ref = (outs.pop(0), outs.pop(0)) if kv_f32 else (None, None)
    yb_ref, sga_ref, sgb_ref, ut_ref, cls_ref = outs[0:5]
    xb = x_ref[...].astype(BF16)

    def col(c0, width):
        return jnp.dot(xb, w_ref[:, c0:c0 + width], preferred_element_type=F32)

    def write_classes(val, group_refs):
        for g, d in enumerate(dils):
            part = val[:, g * GROUP_WIDTH:(g + 1) * GROUP_WIDTH]
            ref = group_refs[g]
            if d == 1:
                ref[0] = part.astype(ref.dtype)
            else:
                for kk in range(GROUP_WIDTH // 128):
                    lanes = slice(kk * 128, (kk + 1) * 128)
                    cls_ref[kk] = part[:, lanes]
                    for c in range(d):
                        ref[c, :, lanes] = cls_ref[kk, pl.ds(c, tm // d, stride=d), :].astype(ref.dtype)

    write_classes(col(0, ATTN_WIDTH), q_refs)
    k = col(ATTN_WIDTH, ATTN_WIDTH)
    v = col(2 * ATTN_WIDTH, ATTN_WIDTH)
    if kv_f32:
        k32_ref[...] = k
        v32_ref[...] = v
    write_classes(k, k_refs)
    write_classes(v, v_refs)

    c0 = 3 * ATTN_WIDTH
    bg = col(c0, CONV_CHANNELS)
    u = col(c0 + CONV_CHANNELS, CONV_CHANNELS) * col(c0 + 2 * CONV_CHANNELS, CONV_CHANNELS)
    row = lax.broadcasted_iota(I32, (tm, CONV_CHANNELS), 0)
    r1 = pltpu.roll(u, 1, axis=0)
    r2 = pltpu.roll(u, 2, axis=0)
    if sample_mode:
        s0 = refs[3][...]
        s1 = refs[4][...]
        t = row & (T_NEW - 1)
        prev1 = jnp.where(t == 0, s1, r1)
        prev2 = jnp.where(t == 0, s0, jnp.where(t == 1, s1, r2))
    else:
        carry_ref = outs[5]

        @pl.when(pl.program_id(0) == 0)
        def _():
            carry_ref[...] = jnp.zeros_like(carry_ref)
        c6 = carry_ref[6:7, :]
        c7 = carry_ref[7:8, :]
        prev1 = jnp.where(row == 0, c7, r1)
        prev2 = jnp.where(row == 0, c6, jnp.where(row == 1, c7, r2))
        carry_ref[...] = u[tm - 8:tm, :]
    conv = prev2 * wc_ref[0:1, :] + prev1 * wc_ref[1:2, :] + u * wc_ref[2:3, :]
    yb_ref[...] = (bg * conv).astype(BF16)
    ut_ref[...] = u[tm - tail_rows:tm, :]

    c1 = c0 + 3 * CONV_CHANNELS
    sga_ref[...] = _sigmoid(col(c1, D_MODEL)).astype(BF16)
    sgb_ref[...] = _sigmoid(col(c1 + D_MODEL, D_MODEL)).astype(BF16)


def _proj(x, w_in_bf, w_conv, conv_prev, *, tm, u_tail, q_dtype, dils, kv_f32):
    n = x.shape[0]
    sample_mode = conv_prev is not None
    nt = n // tm

    def row_spec(width):
        return pl.BlockSpec((tm, width), lambda i: (i, 0))

    def class_spec(d):
        return pl.BlockSpec((d, tm // d, GROUP_WIDTH), lambda i: (0, i, 0))

    def class_shape(d, dtype):
        return jax.ShapeDtypeStruct((d, n // d, GROUP_WIDTH), dtype)

    in_specs = [
        row_spec(D_MODEL),
        pl.BlockSpec((D_MODEL, PROJ_WIDTH), lambda i: (0, 0), pipeline_mode=pl.Buffered(1)),
        pl.BlockSpec((3, CONV_CHANNELS), lambda i: (0, 0)),
    ]
    args = [x, w_in_bf, w_conv]
    scratch = [pltpu.VMEM((GROUP_WIDTH // 128, tm, 128), F32)]
    if sample_mode:
        in_specs += [row_spec(CONV_CHANNELS), row_spec(CONV_CHANNELS)]
        args += [conv_prev[0], conv_prev[1]]
    else:
        scratch.append(pltpu.VMEM((8, CONV_CHANNELS), F32))
    out_shape = (
        [class_shape(d, q_dtype) for d in dils] + [class_shape(d, BF16) for d in dils] * 2
        + [jax.ShapeDtypeStruct((n, ATTN_WIDTH), F32)] * (2 if kv_f32 else 0)
        + [jax.ShapeDtypeStruct((n, CONV_CHANNELS), BF16),
           jax.ShapeDtypeStruct((n, D_MODEL), BF16),
           jax.ShapeDtypeStruct((n, D_MODEL), BF16),
           jax.ShapeDtypeStruct((u_tail, CONV_CHANNELS), F32)])
    out_specs = (
        [class_spec(d) for d in dils] * 3
        + [row_spec(ATTN_WIDTH)] * (2 if kv_f32 else 0)
        + [row_spec(CONV_CHANNELS), row_spec(D_MODEL), row_spec(D_MODEL),
           pl.BlockSpec((u_tail, CONV_CHANNELS), lambda i: (0, 0))])
    res = pl.pallas_call(
        functools.partial(_proj_kernel, tm=tm, tail_rows=u_tail, sample_mode=sample_mode, dils=dils, kv_f32=kv_f32),
        grid=(nt,),
        in_specs=in_specs,
        out_specs=out_specs,
        out_shape=out_shape,
        scratch_shapes=scratch,
        compiler_params=_params(),
        name="proj",
    )(*args)
    return res[0:3], res[3:6], res[6:9], res[9:]


Q_BLOCKS_PER_STEP = 4


def _attn_kernel(q_ref, kp_ref, kc_ref, vp_ref, vc_ref, tb_ref, o_ref, lse_ref):
    lane = lax.broadcasted_iota(I32, (Q_BLOCK, 128), 1)
    first = lane < HEAD_DIM
    scale = HEAD_DIM ** -0.5
    has_prev = jnp.minimum(pl.program_id(1), 1)
    for sub in range(Q_BLOCKS_PER_STEP):
        rows = slice(sub * Q_BLOCK, (sub + 1) * Q_BLOCK)
        band = slice((sub - 1) * Q_BLOCK, (sub + 1) * Q_BLOCK)
        for pr in range(HEADS // 2):
            sl = slice(pr * 128, (pr + 1) * 128)
            if sub == 0:
                k = jnp.concatenate([kp_ref[:, sl], kc_ref[rows, sl]], axis=0)
                v = jnp.concatenate([vp_ref[:, sl], vc_ref[rows, sl]], axis=0)
                bias = tb_ref[has_prev, pr]
            else:
                k = kc_ref[band, sl]
                v = vc_ref[band, sl]
                bias = tb_ref[1, pr]
            qf = q_ref[rows, sl].astype(F32) * scale
            qq = jnp.concatenate([jnp.where(first, qf, 0.0), jnp.where(first, 0.0, qf)], axis=0).astype(BF16)
            s = lax.dot_general(qq, k, (((1,), (1,)), ((), ())), preferred_element_type=F32) + bias
            m = jnp.max(s, axis=-1, keepdims=True)
            p = jnp.exp(s - m)
            l = jnp.sum(p, axis=-1, keepdims=True)
            o = jnp.dot(p.astype(BF16), v, preferred_element_type=F32) / l
            lse = m + jnp.log(l)
            o_ref[rows, sl] = jnp.where(first, o[:Q_BLOCK], o[Q_BLOCK:]).astype(o_ref.dtype)
            lse_ref[rows, sl] = jnp.where(first, jnp.broadcast_to(lse[:Q_BLOCK], (Q_BLOCK, 128)),
                                          jnp.broadcast_to(lse[Q_BLOCK:], (Q_BLOCK, 128)))


def _attn_prompt_group(q, kb, vb, tb, g):
    d, rows = q.shape[0], q.shape[1]
    nq = Q_BLOCKS_PER_STEP
    cur = pl.BlockSpec((None, nq * Q_BLOCK, GROUP_WIDTH), lambda c, i: (c, i, 0))
    prev = pl.BlockSpec((None, Q_BLOCK, GROUP_WIDTH), lambda c, i: (c, jnp.maximum(i * nq - 1, 0), 0))
    return pl.pallas_call(
        _attn_kernel,
        grid=(d, rows // (nq * Q_BLOCK)),
        in_specs=[cur, prev, cur, prev, cur, pl.BlockSpec(tb.shape, lambda c, i: (0, 0, 0, 0))],
        out_specs=[cur, cur],
        out_shape=[jax.ShapeDtypeStruct((d, rows, GROUP_WIDTH), BF16),
                   jax.ShapeDtypeStruct((d, rows, GROUP_WIDTH), F32)],
        compiler_params=_params(),
        name=f"attn_prompt_g{g}",
    )(q, kb, kb, vb, vb, tb)


PACK_Q, PACK_K, PACK_V = 0, N_GROUPS * T_NEW, 2 * N_GROUPS * T_NEW


PAIRS_PER_STEP = 2


def _sample_cache_kernel(qbd_ref, nr_ref, qkv_ref, c0_ref, c1_ref, c2_ref, b0_ref, b1_ref, b2_ref, bn_ref,
                         n0_ref, n1_ref, n2_ref, o_ref, lse_ref):
    for pp in range(PAIRS_PER_STEP):
        rows = pl.ds(pp * 128, 128)
        kv = pl.ds(0, 2)
        _sample_pair(qbd_ref.at[pp], nr_ref.at[pp], qkv_ref.at[rows],
                     [c.at[kv, rows] for c in (c0_ref, c1_ref, c2_ref)],
                     [b.at[pp] for b in (b0_ref, b1_ref, b2_ref)], bn_ref.at[pl.ds(0, N_GROUPS), pp],
                     [c.at[kv, rows] for c in (n0_ref, n1_ref, n2_ref)],
                     o_ref.at[pl.ds(0, 16), rows], lse_ref.at[pl.ds(0, 16), rows])


def _sample_pair(qbd_ref, nr_ref, qkv_ref, c_refs, b_refs, bn_ref, n_refs, o_ref, lse_ref):
    scale = HEAD_DIM ** -0.5
    nt = (((1,), (1,)), ((), ()))
    lane = lax.broadcasted_iota(I32, (128, 128), 1)
    head0 = lane[0:T_NEW] < HEAD_DIM
    for g, (c_ref, b_ref, n_ref) in enumerate(zip(c_refs, b_refs, n_refs)):
        length = c_ref.shape[-1]
        k_new = qkv_ref[:, PACK_K + g * T_NEW:PACK_K + (g + 1) * T_NEW]
        v_new = qkv_ref[:, PACK_V + g * T_NEW:PACK_V + (g + 1) * T_NEW]
        qbd = qbd_ref[g]
        s_c = jnp.dot(qbd.astype(BF16), c_ref[0].astype(BF16), preferred_element_type=F32) * scale + b_ref[...]
        bn = bn_ref[g]
        s_n = [jnp.sum(qbd * nr_ref[g, 0, tn:tn + 1, :], axis=1, keepdims=True) * scale + bn[:, tn:tn + 1]
               for tn in range(T_NEW)]
        m = jnp.max(s_c, axis=1, keepdims=True)
        for x in s_n:
            m = jnp.maximum(m, x)
        p_c = jnp.exp(s_c - m)
        p_n = [jnp.exp(x - m) for x in s_n]
        l = jnp.sum(p_c, axis=1, keepdims=True)
        acc = lax.dot_general(p_c.astype(BF16), c_ref[1].astype(BF16), nt, preferred_element_type=F32)
        for tn in range(T_NEW):
            l = l + p_n[tn]
            acc = acc + p_n[tn] * nr_ref[g, 1, tn:tn + 1, :]
        o = acc / l
        lse = jnp.broadcast_to(m + jnp.log(l), (2 * T_NEW, 128))
        o_ref[g * T_NEW:(g + 1) * T_NEW, :] = jnp.where(head0, o[0:T_NEW], o[T_NEW:])
        lse_ref[g * T_NEW:(g + 1) * T_NEW, :] = jnp.where(head0, lse[0:T_NEW], lse[T_NEW:])

        for kv, new in ((0, k_new), (1, v_new)):
            rolled = pltpu.roll(c_ref[kv], length - T_NEW, axis=1)
            tail = rolled[:, length - 128:]
            for t in range(T_NEW):
                tail = jnp.where(lane == 128 - T_NEW + t, new[:, t:t + 1], tail)
            if length > 128:
                n_ref[kv, :, 0:length - 128] = rolled[:, 0:length - 128]
            n_ref[kv, :, length - 128:] = tail
    pad_rows = slice(N_GROUPS * T_NEW, 16)
    o_ref[pad_rows, :] = jnp.zeros((16 - N_GROUPS * T_NEW, 128), F32)
    lse_ref[pad_rows, :] = jnp.zeros((16 - N_GROUPS * T_NEW, 128), F32)


def _sample_cache(qbd, new_rows, qkv_t, caches_t, bcs, bn):
    b = qkv_t.shape[0]

    pp = PAIRS_PER_STEP

    def cache_spec(c):
        return pl.BlockSpec((None, 2, pp * 128, c.shape[-1]), lambda i, h: (i, 0, h, 0))

    def bias_spec(t):
        return pl.BlockSpec((pp, 2 * T_NEW, t.shape[-1]), lambda i, h: (h, 0, 0))

    out = pl.BlockSpec((None, 16, pp * 128), lambda i, h: (i, 0, h))
    return pl.pallas_call(
        _sample_cache_kernel,
        grid=(b, HEADS // 2 // pp),
        in_specs=[pl.BlockSpec((None, pp, N_GROUPS, 2 * T_NEW, 128), lambda i, h: (i, h, 0, 0, 0)),
                  pl.BlockSpec((None, pp, N_GROUPS, 2, T_NEW, 128), lambda i, h: (i, h, 0, 0, 0, 0)),
                  pl.BlockSpec((None, pp * 128, 128), lambda i, h: (i, h, 0))]
                 + [cache_spec(c) for c in caches_t] + [bias_spec(t) for t in bcs]
                 + [pl.BlockSpec((N_GROUPS, pp, 2 * T_NEW, T_NEW), lambda i, h: (0, h, 0, 0))],
        out_specs=[cache_spec(c) for c in caches_t] + [out, out],
        out_shape=[jax.ShapeDtypeStruct(c.shape, c.dtype) for c in caches_t]
                  + [jax.ShapeDtypeStruct((b, 16, GROUP_WIDTH), F32)] * 2,
        compiler_params=_params(),
        name="sample_cache",
    )(qbd, new_rows, qkv_t, *caches_t, *bcs, bn)


def _mix_kernel(*refs, tm, alpha, dils, n_alias):
    (x_ref, o0_ref, o1_ref, o2_ref, l0_ref, l1_ref, l2_ref, yb_ref, sga_ref, sgb_ref,
     wpa_ref, wpb_ref, wo_ref, g_ref, b_ref, wrh_ref, wrl_ref) = refs[0:17]
    x1_ref, ei_ref, gt_ref = refs[17 + n_alias:20 + n_alias]
    scratch = list(refs[20 + n_alias:])

    def natural(ref, d):
        if d == 1:
            return ref[0].astype(F32)
        scr = scratch.pop()
        for kk in range(GROUP_WIDTH // 128):
            for c in range(d):
                scr[kk, pl.ds(c, tm // d, stride=d), :] = ref[c, :, kk * 128:(kk + 1) * 128].astype(F32)
        return jnp.concatenate([scr[kk] for kk in range(GROUP_WIDTH // 128)], axis=1)

    l0, l1, l2 = natural(l0_ref, dils[0]), natural(l1_ref, dils[1]), natural(l2_ref, dils[2])
    mx = jnp.maximum(jnp.maximum(l0, l1), l2)
    e0 = jnp.exp(l0 - mx)
    e1 = jnp.exp(l1 - mx)
    e2 = jnp.exp(l2 - mx)
    ya = (e0 * natural(o0_ref, dils[0]) + e1 * natural(o1_ref, dils[1]) + e2 * natural(o2_ref, dils[2])) / (e0 + e1 + e2)
    pa = jnp.dot(ya.astype(BF16), wpa_ref[...], preferred_element_type=F32)
    pb = jnp.dot(yb_ref[...], wpb_ref[...], preferred_element_type=F32)
    gated = sga_ref[...].astype(F32) * pa + sgb_ref[...].astype(F32) * pb
    mix = jnp.dot(gated.astype(BF16), wo_ref[...], preferred_element_type=F32)
    z = alpha * x_ref[...] + mix
    mu = jnp.mean(z, axis=-1, keepdims=True)
    zc = z - mu
    var = jnp.mean(zc * zc, axis=-1, keepdims=True)
    x1 = zc * lax.rsqrt(var + LN_EPS) * g_ref[...] + b_ref[...]
    x1_ref[...] = x1

    xh = x1.astype(BF16)
    xl = (x1 - xh.astype(F32)).astype(BF16)
    nt = (((1,), (1,)), ((), ()))
    wrh = wrh_ref[...]
    lt = (lax.dot_general(wrh, xh, nt, preferred_element_type=F32)
          + lax.dot_general(wrh, xl, nt, preferred_element_type=F32)
          + lax.dot_general(wrl_ref[...], xh, nt, preferred_element_type=F32))

    gl = lt[0:N_EXPERT_GROUPS]
    gmax = jnp.max(gl, axis=0, keepdims=True)
    idx4 = lax.broadcasted_iota(I32, (N_EXPERT_GROUPS, tm), 0)
    g_idx = jnp.min(jnp.where(gl == gmax, idx4, N_EXPERT_GROUPS), axis=0, keepdims=True)
    g_prob = 1.0 / jnp.sum(jnp.exp(gl - gmax), axis=0, keepdims=True)
    e_sel = lt[8:16]
    for grp in range(1, N_EXPERT_GROUPS):
        e_sel = jnp.where(g_idx == grp, lt[8 + 8 * grp:16 + 8 * grp], e_sel)
    idx8 = lax.broadcasted_iota(I32, (EXPERTS_PER_GROUP, tm), 0)
    v1 = jnp.max(e_sel, axis=0, keepdims=True)
    i1 = jnp.min(jnp.where(e_sel == v1, idx8, EXPERTS_PER_GROUP), axis=0, keepdims=True)
    rest = jnp.where(idx8 == i1, NEG_INF, e_sel)
    v2 = jnp.max(rest, axis=0, keepdims=True)
    i2 = jnp.min(jnp.where(rest == v2, idx8, EXPERTS_PER_GROUP), axis=0, keepdims=True)
    r = jnp.exp(v2 - v1)
    gate1 = g_prob / (1.0 + r)
    gate2 = g_prob * r / (1.0 + r)
    ex1 = g_idx * EXPERTS_PER_GROUP + i1
    ex2 = g_idx * EXPERTS_PER_GROUP + i2
    ei_ref[...] = jnp.where(idx8 == 0, ex1, jnp.where(idx8 == 1, ex2, 0))
    gt_ref[...] = jnp.where(idx8 == 0, gate1, jnp.where(idx8 == 1, gate2, 0.0))


def _mix(x, o, lse, yb, sga, sgb, w_pa, w_pb, w_o, ln_g, ln_b, wr_hi, wr_lo, *, tm, alpha, dils,
         extra_tiles=0, into=None, into_tile=0):
    n = x.shape[0]
    nt = n // tm

    def src(i):
        return jnp.minimum(i, nt - 1)

    def row_spec(width):
        return pl.BlockSpec((tm, width), lambda i: (src(i), 0))

    def class_spec(d):
        return pl.BlockSpec((d, tm // d, GROUP_WIDTH), lambda i: (0, src(i), 0))

    def full(a):
        return pl.BlockSpec(a.shape, lambda i: (0,) * a.ndim)

    n_alias = 0 if into is None else 3
    rows_out = n + extra_tiles * tm if into is None else into[0].shape[0]
    any_spec = pl.BlockSpec(memory_space=pl.ANY)
    n_scratch = 2 * sum(1 for d in dils if d > 1)
    args = [x, o[0], o[1], o[2], lse[0], lse[1], lse[2], yb, sga, sgb, w_pa, w_pb, w_o, ln_g, ln_b, wr_hi, wr_lo]
    return pl.pallas_call(
        functools.partial(_mix_kernel, tm=tm, alpha=alpha, dils=dils, n_alias=n_alias),
        grid=(nt + extra_tiles,),
        in_specs=[row_spec(D_MODEL)] + [class_spec(d) for d in dils] * 2 + [row_spec(CONV_CHANNELS)]
                 + [row_spec(D_MODEL)] * 2
                 + [full(w_pa), full(w_pb), full(w_o), full(ln_g), full(ln_b), full(wr_hi), full(wr_lo)]
                 + [any_spec] * n_alias,
        out_specs=[pl.BlockSpec((tm, D_MODEL), lambda i: (i + into_tile, 0)),
                   pl.BlockSpec((8, tm), lambda i: (0, i + into_tile)),
                   pl.BlockSpec((8, tm), lambda i: (0, i + into_tile))],
        out_shape=[jax.ShapeDtypeStruct((rows_out, D_MODEL), F32),
                   jax.ShapeDtypeStruct((8, rows_out), I32),
                   jax.ShapeDtypeStruct((8, rows_out), F32)],
        input_output_aliases={len(args) + k: k for k in range(n_alias)},
        scratch_shapes=[pltpu.VMEM((GROUP_WIDTH // 128, tm, 128), F32)] * n_scratch,
        compiler_params=_params(),
        name="mix",
    )(*args, *(into or ()))


def _slot_kernel(ei_ref, slot_ref, cnt_ref, carry_ref, start_ref, *, tl, bm):
    phase = pl.program_id(0)
    i = pl.program_id(1)

    @pl.when(jnp.logical_and(phase == 0, i == 0))
    def _():
        carry_ref[...] = jnp.zeros_like(carry_ref)

    ex = lax.broadcasted_iota(I32, (N_EXPERTS, tl), 0)
    oh0 = (ex == ei_ref[0:1, :]).astype(F32)
    oh1 = (ex == ei_ref[1:2, :]).astype(F32)
    cnt0 = jnp.sum(oh0, axis=1, keepdims=True)
    cnt1 = jnp.sum(oh1, axis=1, keepdims=True)

    @pl.when(phase == 0)
    def _():
        total = carry_ref[...] + cnt0 + cnt1
        carry_ref[...] = total
        cnt_ref[...] = total.astype(I32)
        slot_ref[...] = jnp.zeros_like(slot_ref)

    @pl.when(jnp.logical_and(phase == 1, i == 0))
    def _():
        blocks = jnp.floor((carry_ref[...] + (bm - 1)) * (1.0 / bm))
        a = lax.broadcasted_iota(I32, (N_EXPERTS, N_EXPERTS), 0)
        b = lax.broadcasted_iota(I32, (N_EXPERTS, N_EXPERTS), 1)
        before = (b < a).astype(BF16)
        start_ref[...] = jnp.dot(before, blocks.astype(BF16), preferred_element_type=F32) * bm
        carry_ref[...] = jnp.zeros_like(carry_ref)

    @pl.when(phase == 1)
    def _():
        a = lax.broadcasted_iota(I32, (tl, tl), 0)
        b = lax.broadcasted_iota(I32, (tl, tl), 1)
        upper = (a < b).astype(BF16)
        pre0 = jnp.dot(oh0.astype(BF16), upper, preferred_element_type=F32)
        pre1 = jnp.dot(oh1.astype(BF16), upper, preferred_element_type=F32)
        base = carry_ref[:, 0:1] + start_ref[:, 0:1]
        slot0 = jnp.sum(oh0 * (pre0 + base), axis=0, keepdims=True)
        slot1 = jnp.sum(oh1 * (pre1 + cnt0 + base), axis=0, keepdims=True)
        row = lax.broadcasted_iota(I32, (8, tl), 0)
        slot_ref[...] = jnp.where(row == 0, slot0.astype(I32), jnp.where(row == 1, slot1.astype(I32), 0))
        carry_ref[...] = carry_ref[...] + cnt0 + cnt1


def _slots(ei, *, tl, bm):
    n = ei.shape[1]
    return pl.pallas_call(
        functools.partial(_slot_kernel, tl=tl, bm=bm),
        grid=(2, n // tl),
        in_specs=[pl.BlockSpec((8, tl), lambda p, i: (0, i))],
        out_specs=[pl.BlockSpec((8, tl), lambda p, i: (0, i * p)),
                   pl.BlockSpec((N_EXPERTS, 128), lambda p, i: (0, 0))],
        out_shape=[jax.ShapeDtypeStruct((8, n), I32), jax.ShapeDtypeStruct((N_EXPERTS, 128), I32)],
        scratch_shapes=[pltpu.VMEM((N_EXPERTS, 128), F32), pltpu.VMEM((N_EXPERTS, 128), F32)],
        name="moe_slots",
    )(ei)


def _row_copy(src, src_row, dst, dst_row, sem):
    return pltpu.make_async_copy(src.at[pl.ds(src_row, 1)], dst.at[pl.ds(dst_row, 1)], sem)


ISSUE_UNROLL = 8


def _dispatch_kernel(slot_ref, pend_ref, cnt_ref, x1_ref, buf_ref, zero_ref, sem, zsem, *, n, tm, bm):
    base = pl.program_id(0) * tm

    @pl.when(pl.program_id(0) == 0)
    def _():
        zero_ref[...] = jnp.zeros_like(zero_ref)

        def zero_copy(e):
            start = pl.multiple_of(pend_ref[e] - bm, bm)
            return pltpu.make_async_copy(zero_ref, buf_ref.at[pl.ds(start, bm)], zsem)

        def tail_copy(blk):
            return pltpu.make_async_copy(zero_ref, buf_ref.at[pl.ds(pl.multiple_of(blk * bm, bm), bm)], zsem)

        def tail_start(blk, c):
            tail_copy(blk).start()
            return c

        def tail_wait(blk, c):
            tail_copy(blk).wait()
            return c

        for e in range(N_EXPERTS):
            @pl.when(cnt_ref[e] > 0)
            def _(e=e):
                zero_copy(e).start()
        first_unused = pend_ref[N_EXPERTS - 1] // bm
        lax.fori_loop(first_unused, buf_ref.shape[0] // bm, tail_start, 0)
        for e in range(N_EXPERTS):
            @pl.when(cnt_ref[e] > 0)
            def _(e=e):
                zero_copy(e).wait()
        lax.fori_loop(first_unused, buf_ref.shape[0] // bm, tail_wait, 0)

    def body(r, carry):
        for k in range(TOP_K):
            _row_copy(x1_ref, r, buf_ref, slot_ref[k * n + base + r], sem).start()
        return carry

    lax.fori_loop(0, tm, body, 0, unroll=ISSUE_UNROLL)
    for _ in range(TOP_K):
        pltpu.make_async_copy(x1_ref, buf_ref.at[pl.ds(0, tm)], sem).wait()


def _dispatch(slot_flat, pend, counts, x1, *, tm, bm, nblk):
    n = x1.shape[0]
    return pl.pallas_call(
        functools.partial(_dispatch_kernel, n=n, tm=tm, bm=bm),
        grid_spec=pltpu.PrefetchScalarGridSpec(
            num_scalar_prefetch=3,
            grid=(n // tm,),
            in_specs=[pl.BlockSpec((tm, D_MODEL), lambda i, s, p, c: (i, 0))],
            out_specs=pl.BlockSpec(memory_space=pl.ANY),
            scratch_shapes=[pltpu.VMEM((bm, D_MODEL), F32), pltpu.SemaphoreType.DMA(()),
                            pltpu.SemaphoreType.DMA(())],
        ),
        out_shape=jax.ShapeDtypeStruct((nblk * bm, D_MODEL), F32),
        name="moe_dispatch",
    )(slot_flat, pend, counts, x1)


def _expert_kernel(be_ref, nu_ref, xb_ref, wg_ref, wu_ref, wd_ref, out_ref, wg_bf, wu_bf, wd_bf):
    j = pl.program_id(0)
    used = j < nu_ref[0]
    changed = jnp.logical_or(j == 0, be_ref[j] != be_ref[jnp.maximum(j - 1, 0)])

    @pl.when(jnp.logical_and(used, changed))
    def _():
        wg_bf[...] = wg_ref[...].astype(BF16)
        wu_bf[...] = wu_ref[...].astype(BF16)
        wd_bf[...] = wd_ref[...].astype(BF16)

    @pl.when(used)
    def _():
        xb = xb_ref[...].astype(BF16)
        a = jnp.dot(xb, wg_bf[...], preferred_element_type=F32)
        b = jnp.dot(xb, wu_bf[...], preferred_element_type=F32)
        h = (a * _sigmoid(a)) * b
        out_ref[...] = jnp.dot(h.astype(BF16), wd_bf[...], preferred_element_type=F32)

    @pl.when(jnp.logical_not(used))
    def _():
        out_ref[...] = jnp.zeros_like(out_ref)


def _experts(block_expert, n_used, buf, w_g, w_u, w_d, *, bm):
    nblk = buf.shape[0] // bm

    def row_map(j, be, nu):
        return (jnp.minimum(j, nu[0] - 1), 0)

    def w_map(j, be, nu):
        return (be[j], 0, 0)

    return pl.pallas_call(
        _expert_kernel,
        grid_spec=pltpu.PrefetchScalarGridSpec(
            num_scalar_prefetch=2,
            grid=(nblk,),
            in_specs=[pl.BlockSpec((bm, D_MODEL), row_map),
                      pl.BlockSpec((None, D_MODEL, D_EXPERT), w_map),
                      pl.BlockSpec((None, D_MODEL, D_EXPERT), w_map),
                      pl.BlockSpec((None, D_EXPERT, D_MODEL), w_map)],
            out_specs=pl.BlockSpec((bm, D_MODEL), lambda j, be, nu: (j, 0)),
            scratch_shapes=[pltpu.VMEM((D_MODEL, D_EXPERT), BF16),
                            pltpu.VMEM((D_MODEL, D_EXPERT), BF16),
                            pltpu.VMEM((D_EXPERT, D_MODEL), BF16)],
        ),
        out_shape=jax.ShapeDtypeStruct(buf.shape, F32),
        compiler_params=_params(),
        name="moe_experts",
    )(block_expert, n_used, buf, w_g, w_u, w_d)


def _combine_kernel(slot_ref, x1_ref, gc_ref, g_ref, b_ref, eo_ref, y_ref, side_ref, rows, sem,
                    *, n, tm, alpha, main_tiles):
    i = pl.program_id(0)
    last = pl.num_programs(0) - 1
    cur = i % 2

    def start(tile, buf, r):
        for k in range(TOP_K):
            _row_copy(eo_ref, slot_ref[k * n + tile * tm + r], rows.at[buf, k], r, sem.at[buf]).start()

    def wait(buf):
        for k in range(TOP_K):
            pltpu.make_async_copy(eo_ref.at[pl.ds(0, tm)], rows.at[buf, k], sem.at[buf]).wait()

    @pl.when(i == 0)
    def _():
        def body(r, c):
            start(0, 0, r)
            return c
        lax.fori_loop(0, tm, body, 0, unroll=ISSUE_UNROLL)

    wait(cur)
    nxt = jnp.minimum(i + 1, last)
    for r in range(tm):
        start(nxt, 1 - cur, r)
    gc = gc_ref[...]
    z = alpha * x1_ref[...] + gc[:, 0:1] * rows[cur, 0] + gc[:, 1:2] * rows[cur, 1]
    mu = jnp.mean(z, axis=-1, keepdims=True)
    zc = z - mu
    var = jnp.mean(zc * zc, axis=-1, keepdims=True)
    y = zc * lax.rsqrt(var + LN_EPS) * g_ref[...] + b_ref[...]

    @pl.when(i < main_tiles)
    def _():
        y_ref[...] = y

    @pl.when(i == main_tiles)
    def _():
        side_ref[...] = y[0:side_ref.shape[0]]

    @pl.when(i == last)
    def _():
        wait(1 - cur)


def _combine(slot_flat, x1, gate_cols, ln_g, ln_b, expert_out, *, tm, alpha, main_rows, side_rows):
    n = x1.shape[0]
    main_tiles = main_rows // tm
    return pl.pallas_call(
        functools.partial(_combine_kernel, n=n, tm=tm, alpha=alpha, main_tiles=main_tiles),
        grid_spec=pltpu.PrefetchScalarGridSpec(
            num_scalar_prefetch=1,
            grid=(n // tm,),
            in_specs=[pl.BlockSpec((tm, D_MODEL), lambda i, s: (i, 0)),
                      pl.BlockSpec((tm, TOP_K), lambda i, s: (i, 0)),
                      pl.BlockSpec((1, D_MODEL), lambda i, s: (0, 0)),
                      pl.BlockSpec((1, D_MODEL), lambda i, s: (0, 0)),
                      pl.BlockSpec(memory_space=pl.ANY)],
            out_specs=[pl.BlockSpec((tm, D_MODEL), lambda i, s: (jnp.minimum(i, main_tiles - 1), 0)),
                       pl.BlockSpec((side_rows, D_MODEL), lambda i, s: (0, 0))],
            scratch_shapes=[pltpu.VMEM((2, TOP_K, tm, D_MODEL), F32), pltpu.SemaphoreType.DMA((2,))],
        ),
        out_shape=[jax.ShapeDtypeStruct((main_rows, D_MODEL), F32),
                   jax.ShapeDtypeStruct((side_rows, D_MODEL), F32)],
        compiler_params=_params(),
        name="moe_combine",
    )(slot_flat, x1, gate_cols, ln_g, ln_b, expert_out)


def _hier_moe_ln(x1, ei, gt, w_g, w_u, w_d, ln_g, ln_b, *, tl, tm, bm, alpha, main_rows, side_rows):
    n = x1.shape[0]
    m = n * TOP_K
    slot, cnt = _slots(ei, tl=tl, bm=bm)
    counts = cnt[:, 0]
    pend = jnp.cumsum((counts + bm - 1) // bm * bm)
    nblk = (m + N_EXPERTS * (bm - 1) + bm - 1) // bm
    blk_start = jnp.arange(nblk, dtype=I32) * bm
    n_used = (pend[-1] // bm).astype(I32)
    be = jnp.minimum(jnp.sum(pend[None, :] <= blk_start[:, None], axis=1), N_EXPERTS - 1).astype(I32)
    be = jnp.where(jnp.arange(nblk) < n_used, be, jnp.take(be, n_used - 1))
    slot_flat = slot[0:TOP_K].reshape(m)
    buf = _dispatch(slot_flat, pend.astype(I32), counts, x1, tm=tm, bm=bm, nblk=nblk)
    eo = _experts(be, n_used.reshape(1), buf, w_g, w_u, w_d, bm=bm)
    gate_cols = gt[0:TOP_K].T
    return _combine(slot_flat, x1, gate_cols, ln_g, ln_b, eo, tm=tm, alpha=alpha,
                    main_rows=main_rows, side_rows=side_rows)


def _t5_bucket(n):
    nf = jnp.maximum(n, 1).astype(F32)
    large = MAX_EXACT + (jnp.log(nf / MAX_EXACT) / math.log(MAX_DISTANCE / MAX_EXACT)
                         * (N_BUCKETS - MAX_EXACT)).astype(I32)
    large = jnp.minimum(large, N_BUCKETS - 1)
    return jnp.where(n < MAX_EXACT, n, large)


def _bias_per_group(rel_bias):
    offs = jnp.arange(N_KEYS, dtype=I32)[None, :] * jnp.array(DILATIONS, I32)[:, None]
    bucket = _t5_bucket(offs)
    table = rel_bias.reshape(N_BUCKETS, N_GROUPS, HEADS)
    b = table[bucket, jnp.arange(N_GROUPS)[:, None]]
    return jnp.transpose(b, (0, 2, 1)).astype(F32)


def _prompt_bias_tables(bias):
    width = 3 * Q_BLOCK
    neg = jnp.full((N_GROUPS, HEADS, Q_BLOCK - 1), NEG_INF, F32)
    r = jnp.concatenate([neg, bias[:, :, ::-1], neg, jnp.full((N_GROUPS, HEADS, 1), NEG_INF, F32)], axis=-1)
    flat = jnp.tile(r, (1, 1, Q_BLOCK))[:, :, :Q_BLOCK * (width - 1)]
    skew = flat.reshape(N_GROUPS, HEADS, Q_BLOCK, width - 1)
    later = skew[:, :, :, Q_BLOCK - 1:3 * Q_BLOCK - 1]
    has_prev = (np.arange(2 * Q_BLOCK) >= Q_BLOCK)[None, None, None, :]
    first = jnp.where(has_prev, later, NEG_INF)
    tb = jnp.stack([first, later], axis=1)
    return tb.reshape(N_GROUPS, 2, HEADS // 2, 2 * Q_BLOCK, 2 * Q_BLOCK)


def _sample_bias_tables(bias):
    bcs = []
    for g, d in enumerate(DILATIONS):
        rev = bias[g][:, ::-1][:, :WINDOW_KEYS]
        per_t = []
        for t in range(T_NEW):
            if d == 1:
                row = jnp.concatenate([jnp.full((HEADS, t), NEG_INF, F32), rev[:, :WINDOW_KEYS - t]], axis=1)
            else:
                cls = np.arange(d)[None, None, :] == t
                row = jnp.where(cls, rev[:, :, None], NEG_INF).reshape(HEADS, WINDOW_KEYS * d)
            per_t.append(row)
        bcs.append(jnp.stack(per_t, axis=1))
    bn = []
    for g, d in enumerate(DILATIONS):
        rows = []
        for t in range(T_NEW):
            cols = []
            for tn in range(T_NEW):
                ok = (tn <= t) if d == 1 else (tn == t)
                cols.append(bias[g][:, t - tn] if ok else jnp.full((HEADS,), NEG_INF, F32))
            rows.append(jnp.stack(cols, axis=-1))
        bn.append(jnp.stack(rows, axis=1))
    return bcs, jnp.stack(bn)


def _split_bf16(w):
    hi = w.astype(BF16)
    lo = (w - hi.astype(F32)).astype(BF16)
    return hi, lo


def kernel(x_prompt, x_sample, cache_attn_w128, cache_attn_w512, cache_attn_w2048, state_conv, rel_bias, w_in, w_conv, w_pa, w_pb, w_o, ln1_g, ln1_b, w_router_group, w_router_expert, w_expert_gate, w_expert_up, w_expert_down, ln2_g, ln2_b):
    depth = w_in.shape[0]
    assert depth == 1 and x_prompt.shape[0] == 1
    alpha = (2.0 * depth) ** 0.25
    s = x_prompt.shape[1]
    bd, t_len = x_sample.shape[0], x_sample.shape[1]
    assert t_len == T_NEW and s % (DILATIONS[-1] * Q_BLOCK * Q_BLOCKS_PER_STEP) == 0

    bias = _bias_per_group(rel_bias)
    tb = _prompt_bias_tables(bias)
    bcs, bn = _sample_bias_tables(bias)

    w_in_bf = w_in[0].astype(BF16)
    w_pa_bf = w_pa[0].astype(BF16)
    w_pb_bf = w_pb[0].astype(BF16)
    w_o_bf = w_o[0].astype(BF16)
    wr = jnp.zeros((ROUTER_ROWS, D_MODEL), F32)
    wr = wr.at[0:N_EXPERT_GROUPS].set(w_router_group[0].T).at[8:8 + N_EXPERTS].set(w_router_expert[0].T)
    wr_hi, wr_lo = _split_bf16(wr)
    g1, b1 = ln1_g[0][None], ln1_b[0][None]
    g2, b2 = ln2_g[0][None], ln2_b[0][None]
    wg, wu, wd = w_expert_gate[0], w_expert_up[0], w_expert_down[0]

    xp = x_prompt[0]
    kv_tail = min(MAX_DISTANCE, s)
    q, kb, vb, (k32, v32, yb, sga, sgb, ut) = _proj(
        xp, w_in_bf, w_conv[0], None, tm=256, u_tail=8, q_dtype=BF16, dils=DILATIONS, kv_f32=True)
    k32, v32 = k32[s - kv_tail:], v32[s - kv_tail:]
    o_l = [_attn_prompt_group(q[g], kb[g], vb[g], tb[g], g) for g in range(N_GROUPS)]
    mix_tm = 512
    routed = _mix(xp, [a[0] for a in o_l], [a[1] for a in o_l], yb, sga, sgb,
                  w_pa_bf, w_pb_bf, w_o_bf, g1, b1, wr_hi, wr_lo, tm=mix_tm, alpha=alpha, dils=DILATIONS,
                  extra_tiles=1)

    kv_prompt = []
    for g, d in enumerate(DILATIONS):
        length = min(WINDOW_KEYS * d, s)
        cols = slice(g * GROUP_WIDTH, (g + 1) * GROUP_WIDTH)
        kg = k32[kv_tail - length:, cols].reshape(length, HEADS, HEAD_DIM)
        vg = v32[kv_tail - length:, cols].reshape(length, HEADS, HEAD_DIM)
        kv_prompt.append(jnp.stack([kg, vg], axis=1)[None, None])
    conv_prompt = ut[6:8][None, None]

    ns = bd * t_len
    xs = x_sample.reshape(ns, D_MODEL)
    st = state_conv[0]
    s0 = jnp.repeat(st[:, 0], t_len, axis=0)
    s1 = jnp.repeat(st[:, 1], t_len, axis=0)
    qs, _, _, (k32s, v32s, ybs, sgas, sgbs, us) = _proj(
        xs, w_in_bf, w_conv[0], (s0, s1), tm=ns, u_tail=ns, q_dtype=F32, dils=NO_DILATION, kv_f32=True)
    qs = jnp.concatenate([a[0] for a in qs], axis=1)
    packed = jnp.stack([qs, k32s, v32s]).reshape(3, bd, t_len, N_GROUPS, GROUP_WIDTH)
    qkv_t = jnp.transpose(packed, (1, 4, 0, 3, 2)).reshape(bd, GROUP_WIDTH, 3 * N_GROUPS * t_len)
    qkv_t = jnp.pad(qkv_t, ((0, 0), (0, 0), (0, 128 - 3 * N_GROUPS * t_len)))
    caches = (cache_attn_w128[0], cache_attn_w512[0], cache_attn_w2048[0])
    caches_t = [jnp.transpose(c, (0, 2, 3, 4, 1)).reshape(bd, 2, GROUP_WIDTH, c.shape[1]) for c in caches]
    pair = (bd, t_len, N_GROUPS, HEADS // 2, 2, HEAD_DIM)
    q6 = jnp.transpose(qs.reshape(pair), (0, 3, 2, 4, 1, 5))
    zeros = jnp.zeros_like(q6[:, :, :, 0])
    qbd = jnp.stack([jnp.concatenate([q6[:, :, :, 0], zeros], axis=-1),
                     jnp.concatenate([zeros, q6[:, :, :, 1]], axis=-1)], axis=3)
    qbd = qbd.reshape(bd, HEADS // 2, N_GROUPS, 2 * t_len, 128)
    new_rows = jnp.stack([k32s, v32s]).reshape(2, bd, t_len, N_GROUPS, HEADS // 2, 128)
    new_rows = jnp.transpose(new_rows, (1, 4, 3, 0, 2, 5))
    bcs = [t.reshape(HEADS // 2, 2 * t_len, t.shape[-1]) for t in bcs]
    bn = bn.reshape(N_GROUPS, HEADS // 2, 2 * t_len, t_len)
    n0, n1, n2, o_s, lse_s = _sample_cache(qbd, new_rows, qkv_t, caches_t, bcs, bn)

    def unpack(a):
        a = a[:, :N_GROUPS * t_len].reshape(bd, N_GROUPS, t_len, GROUP_WIDTH)
        return jnp.transpose(a, (1, 0, 2, 3)).reshape(N_GROUPS, 1, ns, GROUP_WIDTH)

    o_s, lse_s = unpack(o_s), unpack(lse_s)
    assert ns <= mix_tm and s % ns == 0
    x1, ei, gt = _mix(xs, o_s, lse_s, ybs, sgas, sgbs, w_pa_bf, w_pb_bf, w_o_bf, g1, b1, wr_hi, wr_lo,
                      tm=ns, alpha=alpha, dils=NO_DILATION, into=routed, into_tile=s // ns)
    y_prompt, y_sample = _hier_moe_ln(x1, ei, gt, wg, wu, wd, g2, b2, tl=512, tm=256, bm=256, alpha=alpha,
                                      main_rows=s, side_rows=ns)
    y_prompt = y_prompt[None]
    y_sample = y_sample.reshape(bd, t_len, D_MODEL)

    kv_sample = [jnp.transpose(c.reshape(bd, 2, HEADS, HEAD_DIM, c.shape[-1]), (0, 4, 1, 2, 3))[None]
                 for c in (n0, n1, n2)]
    conv_sample = us.reshape(bd, t_len, CONV_CHANNELS)[:, t_len - 2:][None]

    return (y_prompt, y_sample, kv_prompt[0], kv_prompt[1], kv_prompt[2], conv_prompt,
            kv_sample[0], kv_sample[1], kv_sample[2], conv_sample)
```

```python
import functools
import math

import numpy as np
import jax
import jax.numpy as jnp
from jax import lax
from jax.experimental import pallas as pl
from jax.experimental.pallas import tpu as pltpu

F32 = jnp.float32
BF16 = jnp.bfloat16
I32 = jnp.int32

D_MODEL = 1024
N_GROUPS = 3
HEADS = 8
HEAD_DIM = 64
GROUP_WIDTH = HEADS * HEAD_DIM
ATTN_WIDTH = N_GROUPS * GROUP_WIDTH
DILATIONS = (1, 4, 16)
NO_DILATION = (1, 1, 1)
WINDOW_KEYS = 128
N_KEYS = WINDOW_KEYS + 1
N_BUCKETS = 32
MAX_EXACT = 16
MAX_DISTANCE = 2048
CONV_CHANNELS = 512
N_EXPERT_GROUPS = 4
EXPERTS_PER_GROUP = 8
N_EXPERTS = 32
TOP_K = 2
D_EXPERT = 512
LN_EPS = 1e-5
PROJ_WIDTH = 3 * ATTN_WIDTH + 3 * CONV_CHANNELS + 2 * D_MODEL
ROUTER_ROWS = 8 + N_EXPERTS
Q_BLOCK = 128
T_NEW = 4
NEG_INF = float("-inf")
VMEM_LIMIT = 56 * 1024 * 1024


def _sigmoid(x):
    return 1.0 / (1.0 + jnp.exp(-x))


def _params(limit=VMEM_LIMIT):
    return pltpu.CompilerParams(vmem_limit_bytes=limit)


def _proj_kernel(*refs, tm, tail_rows, sample_mode, dils, kv_f32):
    n_in = 5 if sample_mode else 3
    x_ref, w_ref, wc_ref = refs[0:3]
    outs = list(refs[n_in:])
    q_refs, k_refs, v_refs = outs[0:3], outs[3:6], outs[6:9]
    del outs[0:9]
    k32_ref, v32_ref = (outs.pop(0), outs.pop(0)) if kv_f32 else (None, None)
    yb_ref, sga_ref, sgb_ref, ut_ref, cls_ref = outs[0:5]
    xb = x_ref[...].astype(BF16)

    def col(c0, width):
        return jnp.dot(xb, w_ref[:, c0:c0 + width], preferred_element_type=F32)

    def write_classes(val, group_refs):
        for g, d in enumerate(dils):
            part = val[:, g * GROUP_WIDTH:(g + 1) * GROUP_WIDTH]
            ref = group_refs[g]
            if d == 1:
                ref[0] = part.astype(ref.dtype)
            else:
                for kk in range(GROUP_WIDTH // 128):
                    lanes = slice(kk * 128, (kk + 1) * 128)
                    cls_ref[kk] = part[:, lanes]
                    for c in range(d):
                        ref[c, :, lanes] = cls_ref[kk, pl.ds(c, tm // d, stride=d), :].astype(ref.dtype)

    write_classes(col(0, ATTN_WIDTH), q_refs)
    k = col(ATTN_WIDTH, ATTN_WIDTH)
    v = col(2 * ATTN_WIDTH, ATTN_WIDTH)
    if kv_f32:
        k32_ref[...] = k
        v32_ref[...] = v
    write_classes(k, k_refs)
    write_classes(v, v_refs)

    c0 = 3 * ATTN_WIDTH
    bg = col(c0, CONV_CHANNELS)
    u = col(c0 + CONV_CHANNELS, CONV_CHANNELS) * col(c0 + 2 * CONV_CHANNELS, CONV_CHANNELS)
    row = lax.broadcasted_iota(I32, (tm, CONV_CHANNELS), 0)
    r1 = pltpu.roll(u, 1, axis=0)
    r2 = pltpu.roll(u, 2, axis=0)
    if sample_mode:
        s0 = refs[3][...]
        s1 = refs[4][...]
        t = row & (T_NEW - 1)
        prev1 = jnp.where(t == 0, s1, r1)
        prev2 = jnp.where(t == 0, s0, jnp.where(t == 1, s1, r2))
    else:
        carry_ref = outs[5]

        @pl.when(pl.program_id(0) == 0)
        def _():
            carry_ref[...] = jnp.zeros_like(carry_ref)
        c6 = carry_ref[6:7, :]
        c7 = carry_ref[7:8, :]
        prev1 = jnp.where(row == 0, c7, r1)
        prev2 = jnp.where(row == 0, c6, jnp.where(row == 1, c7, r2))
        carry_ref[...] = u[tm - 8:tm, :]
    conv = prev2 * wc_ref[0:1, :] + prev1 * wc_ref[1:2, :] + u * wc_ref[2:3, :]
    yb_ref[...] = (bg * conv).astype(BF16)
    ut_ref[...] = u[tm - tail_rows:tm, :]

    c1 = c0 + 3 * CONV_CHANNELS
    sga_ref[...] = _sigmoid(col(c1, D_MODEL)).astype(BF16)
    sgb_ref[...] = _sigmoid(col(c1 + D_MODEL, D_MODEL)).astype(BF16)


def _proj(x, w_in_bf, w_conv, conv_prev, *, tm, u_tail, q_dtype, dils, kv_f32):
    n = x.shape[0]
    sample_mode = conv_prev is not None
    nt = n // tm

    def row_spec(width):
        return pl.BlockSpec((tm, width), lambda i: (i, 0))

    def class_spec(d):
        return pl.BlockSpec((d, tm // d, GROUP_WIDTH), lambda i: (0, i, 0))

    def class_shape(d, dtype):
        return jax.ShapeDtypeStruct((d, n // d, GROUP_WIDTH), dtype)

    in_specs = [
        row_spec(D_MODEL),
        pl.BlockSpec((D_MODEL, PROJ_WIDTH), lambda i: (0, 0), pipeline_mode=pl.Buffered(1)),
        pl.BlockSpec((3, CONV_CHANNELS), lambda i: (0, 0)),
    ]
    args = [x, w_in_bf, w_conv]
    scratch = [pltpu.VMEM((GROUP_WIDTH // 128, tm, 128), F32)]
    if sample_mode:
        in_specs += [row_spec(CONV_CHANNELS), row_spec(CONV_CHANNELS)]
        args += [conv_prev[0], conv_prev[1]]
    else:
        scratch.append(pltpu.VMEM((8, CONV_CHANNELS), F32))
    out_shape = (
        [class_shape(d, q_dtype) for d in dils] + [class_shape(d, BF16) for d in dils] * 2
        + [jax.ShapeDtypeStruct((n, ATTN_WIDTH), F32)] * (2 if kv_f32 else 0)
        + [jax.ShapeDtypeStruct((n, CONV_CHANNELS), BF16),
           jax.ShapeDtypeStruct((n, D_MODEL), BF16),
           jax.ShapeDtypeStruct((n, D_MODEL), BF16),
           jax.ShapeDtypeStruct((u_tail, CONV_CHANNELS), F32)])
    out_specs = (
        [class_spec(d) for d in dils] * 3
        + [row_spec(ATTN_WIDTH)] * (2 if kv_f32 else 0)
        + [row_spec(CONV_CHANNELS), row_spec(D_MODEL), row_spec(D_MODEL),
           pl.BlockSpec((u_tail, CONV_CHANNELS), lambda i: (0, 0))])
    res = pl.pallas_call(
        functools.partial(_proj_kernel, tm=tm, tail_rows=u_tail, sample_mode=sample_mode, dils=dils, kv_f32=kv_f32),
        grid=(nt,),
        in_specs=in_specs,
        out_specs=out_specs,
        out_shape=out_shape,
        scratch_shapes=scratch,
        compiler_params=_params(),
        name="proj",
    )(*args)
    return res[0:3], res[3:6], res[6:9], res[9:]


Q_BLOCKS_PER_STEP = 4


def _attn_kernel(q_ref, kp_ref, kc_ref, vp_ref, vc_ref, tb_ref, o_ref, lse_ref):
    lane = lax.broadcasted_iota(I32, (Q_BLOCK, 128), 1)
    first = lane < HEAD_DIM
    scale = HEAD_DIM ** -0.5
    has_prev = jnp.minimum(pl.program_id(1), 1)
    for sub in range(Q_BLOCKS_PER_STEP):
        rows = slice(sub * Q_BLOCK, (sub + 1) * Q_BLOCK)
        band = slice((sub - 1) * Q_BLOCK, (sub + 1) * Q_BLOCK)
        for pr in range(HEADS // 2):
            sl = slice(pr * 128, (pr + 1) * 128)
            if sub == 0:
                k = jnp.concatenate([kp_ref[:, sl], kc_ref[rows, sl]], axis=0)
                v = jnp.concatenate([vp_ref[:, sl], vc_ref[rows, sl]], axis=0)
                bias = tb_ref[has_prev, pr]
            else:
                k = kc_ref[band, sl]
                v = vc_ref[band, sl]
                bias = tb_ref[1, pr]
            qf = q_ref[rows, sl].astype(F32) * scale
            qq = jnp.concatenate([jnp.where(first, qf, 0.0), jnp.where(first, 0.0, qf)], axis=0).astype(BF16)
            s = lax.dot_general(qq, k, (((1,), (1,)), ((), ())), preferred_element_type=F32) + bias
            m = jnp.max(s, axis=-1, keepdims=True)
            p = jnp.exp(s - m)
            l = jnp.sum(p, axis=-1, keepdims=True)
            o = jnp.dot(p.astype(BF16), v, preferred_element_type=F32) / l
            lse = m + jnp.log(l)
            o_ref[rows, sl] = jnp.where(first, o[:Q_BLOCK], o[Q_BLOCK:]).astype(o_ref.dtype)
            lse_ref[rows, sl] = jnp.where(first, jnp.broadcast_to(lse[:Q_BLOCK], (Q_BLOCK, 128)),
                                          jnp.broadcast_to(lse[Q_BLOCK:], (Q_BLOCK, 128)))


def _attn_prompt_group(q, kb, vb, tb, g):
    d, rows = q.shape[0], q.shape[1]
    nq = Q_BLOCKS_PER_STEP
    cur = pl.BlockSpec((None, nq * Q_BLOCK, GROUP_WIDTH), lambda c, i: (c, i, 0))
    prev = pl.BlockSpec((None, Q_BLOCK, GROUP_WIDTH), lambda c, i: (c, jnp.maximum(i * nq - 1, 0), 0))
    return pl.pallas_call(
        _attn_kernel,
        grid=(d, rows // (nq * Q_BLOCK)),
        in_specs=[cur, prev, cur, prev, cur, pl.BlockSpec(tb.shape, lambda c, i: (0, 0, 0, 0))],
        out_specs=[cur, cur],
        out_shape=[jax.ShapeDtypeStruct((d, rows, GROUP_WIDTH), BF16),
                   jax.ShapeDtypeStruct((d, rows, GROUP_WIDTH), F32)],
        compiler_params=_params(),
        name=f"attn_prompt_g{g}",
    )(q, kb, kb, vb, vb, tb)


PACK_Q, PACK_K, PACK_V = 0, N_GROUPS * T_NEW, 2 * N_GROUPS * T_NEW


PAIRS_PER_STEP = 2


def _sample_cache_kernel(qbd_ref, nr_ref, qkv_ref, c0_ref, c1_ref, c2_ref, b0_ref, b1_ref, b2_ref, bn_ref,
                         n0_ref, n1_ref, n2_ref, o_ref, lse_ref):
    for pp in range(PAIRS_PER_STEP):
        rows = pl.ds(pp * 128, 128)
        kv = pl.ds(0, 2)
        _sample_pair(qbd_ref.at[pp], nr_ref.at[pp], qkv_ref.at[rows],
                     [c.at[kv, rows] for c in (c0_ref, c1_ref, c2_ref)],
                     [b.at[pp] for b in (b0_ref, b1_ref, b2_ref)], bn_ref.at[pl.ds(0, N_GROUPS), pp],
                     [c.at[kv, rows] for c in (n0_ref, n1_ref, n2_ref)],
                     o_ref.at[pl.ds(0, 16), rows], lse_ref.at[pl.ds(0, 16), rows])


def _sample_pair(qbd_ref, nr_ref, qkv_ref, c_refs, b_refs, bn_ref, n_refs, o_ref, lse_ref):
    scale = HEAD_DIM ** -0.5
    nt = (((1,), (1,)), ((), ()))
    lane = lax.broadcasted_iota(I32, (128, 128), 1)
    head0 = lane[0:T_NEW] < HEAD_DIM
    for g, (c_ref, b_ref, n_ref) in enumerate(zip(c_refs, b_refs, n_refs)):
        length = c_ref.shape[-1]
        k_new = qkv_ref[:, PACK_K + g * T_NEW:PACK_K + (g + 1) * T_NEW]
        v_new = qkv_ref[:, PACK_V + g * T_NEW:PACK_V + (g + 1) * T_NEW]
        qbd = qbd_ref[g]
        s_c = jnp.dot(qbd.astype(BF16), c_ref[0].astype(BF16), preferred_element_type=F32) * scale + b_ref[...]
        bn = bn_ref[g]
        s_n = [jnp.sum(qbd * nr_ref[g, 0, tn:tn + 1, :], axis=1, keepdims=True) * scale + bn[:, tn:tn + 1]
               for tn in range(T_NEW)]
        m = jnp.max(s_c, axis=1, keepdims=True)
        for x in s_n:
            m = jnp.maximum(m, x)
        p_c = jnp.exp(s_c - m)
        p_n = [jnp.exp(x - m) for x in s_n]
        l = jnp.sum(p_c, axis=1, keepdims=True)
        acc = lax.dot_general(p_c.astype(BF16), c_ref[1].astype(BF16), nt, preferred_element_type=F32)
        for tn in range(T_NEW):
            l = l + p_n[tn]
            acc = acc + p_n[tn] * nr_ref[g, 1, tn:tn + 1, :]
        o = acc / l
        lse = jnp.broadcast_to(m + jnp.log(l), (2 * T_NEW, 128))
        o_ref[g * T_NEW:(g + 1) * T_NEW, :] = jnp.where(head0, o[0:T_NEW], o[T_NEW:])
        lse_ref[g * T_NEW:(g + 1) * T_NEW, :] = jnp.where(head0, lse[0:T_NEW], lse[T_NEW:])

        for kv, new in ((0, k_new), (1, v_new)):
            rolled = pltpu.roll(c_ref[kv], length - T_NEW, axis=1)
            tail = rolled[:, length - 128:]
            for t in range(T_NEW):
                tail = jnp.where(lane == 128 - T_NEW + t, new[:, t:t + 1], tail)
            if length > 128:
                n_ref[kv, :, 0:length - 128] = rolled[:, 0:length - 128]
            n_ref[kv, :, length - 128:] = tail
    pad_rows = slice(N_GROUPS * T_NEW, 16)
    o_ref[pad_rows, :] = jnp.zeros((16 - N_GROUPS * T_NEW, 128), F32)
    lse_ref[pad_rows, :] = jnp.zeros((16 - N_GROUPS * T_NEW, 128), F32)


def _sample_cache(qbd, new_rows, qkv_t, caches_t, bcs, bn):
    b = qkv_t.shape[0]

    pp = PAIRS_PER_STEP

    def cache_spec(c):
        return pl.BlockSpec((None, 2, pp * 128, c.shape[-1]), lambda i, h: (i, 0, h, 0))

    def bias_spec(t):
        return pl.BlockSpec((pp, 2 * T_NEW, t.shape[-1]), lambda i, h: (h, 0, 0))

    out = pl.BlockSpec((None, 16, pp * 128), lambda i, h: (i, 0, h))
    return pl.pallas_call(
        _sample_cache_kernel,
        grid=(b, HEADS // 2 // pp),
        in_specs=[pl.BlockSpec((None, pp, N_GROUPS, 2 * T_NEW, 128), lambda i, h: (i, h, 0, 0, 0)),
                  pl.BlockSpec((None, pp, N_GROUPS, 2, T_NEW, 128), lambda i, h: (i, h, 0, 0, 0, 0)),
                  pl.BlockSpec((None, pp * 128, 128), lambda i, h: (i, h, 0))]
                 + [cache_spec(c) for c in caches_t] + [bias_spec(t) for t in bcs]
                 + [pl.BlockSpec((N_GROUPS, pp, 2 * T_NEW, T_NEW), lambda i, h: (0, h, 0, 0))],
        out_specs=[cache_spec(c) for c in caches_t] + [out, out],
        out_shape=[jax.ShapeDtypeStruct(c.shape, c.dtype) for c in caches_t]
                  + [jax.ShapeDtypeStruct((b, 16, GROUP_WIDTH), F32)] * 2,
        compiler_params=_params(),
        name="sample_cache",
    )(qbd, new_rows, qkv_t, *caches_t, *bcs, bn)


def _mix_kernel(*refs, tm, alpha, dils, n_alias):
    (x_ref, o0_ref, o1_ref, o2_ref, l0_ref, l1_ref, l2_ref, yb_ref, sga_ref, sgb_ref,
     wpa_ref, wpb_ref, wo_ref, g_ref, b_ref, wrh_ref, wrl_ref) = refs[0:17]
    x1_ref, ei_ref, gt_ref = refs[17 + n_alias:20 + n_alias]
    scratch = list(refs[20 + n_alias:])

    def natural(ref, d):
        if d == 1:
            return ref[0].astype(F32)
        scr = scratch.pop()
        for kk in range(GROUP_WIDTH // 128):
            for c in range(d):
                scr[kk, pl.ds(c, tm // d, stride=d), :] = ref[c, :, kk * 128:(kk + 1) * 128].astype(F32)
        return jnp.concatenate([scr[kk] for kk in range(GROUP_WIDTH // 128)], axis=1)

    l0, l1, l2 = natural(l0_ref, dils[0]), natural(l1_ref, dils[1]), natural(l2_ref, dils[2])
    mx = jnp.maximum(jnp.maximum(l0, l1), l2)
    e0 = jnp.exp(l0 - mx)
    e1 = jnp.exp(l1 - mx)
    e2 = jnp.exp(l2 - mx)
    ya = (e0 * natural(o0_ref, dils[0]) + e1 * natural(o1_ref, dils[1]) + e2 * natural(o2_ref, dils[2])) / (e0 + e1 + e2)
    pa = jnp.dot(ya.astype(BF16), wpa_ref[...], preferred_element_type=F32)
    pb = jnp.dot(yb_ref[...], wpb_ref[...], preferred_element_type=F32)
    gated = sga_ref[...].astype(F32) * pa + sgb_ref[...].astype(F32) * pb
    mix = jnp.dot(gated.astype(BF16), wo_ref[...], preferred_element_type=F32)
    z = alpha * x_ref[...] + mix
    mu = jnp.mean(z, axis=-1, keepdims=True)
    zc = z - mu
    var = jnp.mean(zc * zc, axis=-1, keepdims=True)
    x1 = zc * lax.rsqrt(var + LN_EPS) * g_ref[...] + b_ref[...]
    x1_ref[...] = x1

    xh = x1.astype(BF16)
    xl = (x1 - xh.astype(F32)).astype(BF16)
    nt = (((1,), (1,)), ((), ()))
    wrh = wrh_ref[...]
    lt = (lax.dot_general(wrh, xh, nt, preferred_element_type=F32)
          + lax.dot_general(wrh, xl, nt, preferred_element_type=F32)
          + lax.dot_general(wrl_ref[...], xh, nt, preferred_element_type=F32))

    gl = lt[0:N_EXPERT_GROUPS]
    gmax = jnp.max(gl, axis=0, keepdims=True)
    idx4 = lax.broadcasted_iota(I32, (N_EXPERT_GROUPS, tm), 0)
    g_idx = jnp.min(jnp.where(gl == gmax, idx4, N_EXPERT_GROUPS), axis=0, keepdims=True)
    g_prob = 1.0 / jnp.sum(jnp.exp(gl - gmax), axis=0, keepdims=True)
    e_sel = lt[8:16]
    for grp in range(1, N_EXPERT_GROUPS):
        e_sel = jnp.where(g_idx == grp, lt[8 + 8 * grp:16 + 8 * grp], e_sel)
    idx8 = lax.broadcasted_iota(I32, (EXPERTS_PER_GROUP, tm), 0)
    v1 = jnp.max(e_sel, axis=0, keepdims=True)
    i1 = jnp.min(jnp.where(e_sel == v1, idx8, EXPERTS_PER_GROUP), axis=0, keepdims=True)
    rest = jnp.where(idx8 == i1, NEG_INF, e_sel)
    v2 = jnp.max(rest, axis=0, keepdims=True)
    i2 = jnp.min(jnp.where(rest == v2, idx8, EXPERTS_PER_GROUP), axis=0, keepdims=True)
    r = jnp.exp(v2 - v1)
    gate1 = g_prob / (1.0 + r)
    gate2 = g_prob * r / (1.0 + r)
    ex1 = g_idx * EXPERTS_PER_GROUP + i1
    ex2 = g_idx * EXPERTS_PER_GROUP + i2
    ei_ref[...] = jnp.where(idx8 == 0, ex1, jnp.where(idx8 == 1, ex2, 0))
    gt_ref[...] = jnp.where(idx8 == 0, gate1, jnp.where(idx8 == 1, gate2, 0.0))


def _mix(x, o, lse, yb, sga, sgb, w_pa, w_pb, w_o, ln_g, ln_b, wr_hi, wr_lo, *, tm, alpha, dils,
         extra_tiles=0, into=None, into_tile=0):
    n = x.shape[0]
    nt = n // tm

    def src(i):
        return jnp.minimum(i, nt - 1)

    def row_spec(width):
        return pl.BlockSpec((tm, width), lambda i: (src(i), 0))

    def class_spec(d):
        return pl.BlockSpec((d, tm // d, GROUP_WIDTH), lambda i: (0, src(i), 0))

    def full(a):
        return pl.BlockSpec(a.shape, lambda i: (0,) * a.ndim)

    n_alias = 0 if into is None else 3
    rows_out = n + extra_tiles * tm if into is None else into[0].shape[0]
    any_spec = pl.BlockSpec(memory_space=pl.ANY)
    n_scratch = 2 * sum(1 for d in dils if d > 1)
    args = [x, o[0], o[1], o[2], lse[0], lse[1], lse[2], yb, sga, sgb, w_pa, w_pb, w_o, ln_g, ln_b, wr_hi, wr_lo]
    return pl.pallas_call(
        functools.partial(_mix_kernel, tm=tm, alpha=alpha, dils=dils, n_alias=n_alias),
        grid=(nt + extra_tiles,),
        in_specs=[row_spec(D_MODEL)] + [class_spec(d) for d in dils] * 2 + [row_spec(CONV_CHANNELS)]
                 + [row_spec(D_MODEL)] * 2
                 + [full(w_pa), full(w_pb), full(w_o), full(ln_g), full(ln_b), full(wr_hi), full(wr_lo)]
                 + [any_spec] * n_alias,
        out_specs=[pl.BlockSpec((tm, D_MODEL), lambda i: (i + into_tile, 0)),
                   pl.BlockSpec((8, tm), lambda i: (0, i + into_tile)),
                   pl.BlockSpec((8, tm), lambda i: (0, i + into_tile))],
        out_shape=[jax.ShapeDtypeStruct((rows_out, D_MODEL), F32),
                   jax.ShapeDtypeStruct((8, rows_out), I32),
                   jax.ShapeDtypeStruct((8, rows_out), F32)],
        input_output_aliases={len(args) + k: k for k in range(n_alias)},
        scratch_shapes=[pltpu.VMEM((GROUP_WIDTH // 128, tm, 128), F32)] * n_scratch,
        compiler_params=_params(),
        name="mix",
    )(*args, *(into or ()))


def _slot_kernel(ei_ref, slot_ref, cnt_ref, carry_ref, start_ref, *, tl, bm):
    phase = pl.program_id(0)
    i = pl.program_id(1)

    @pl.when(jnp.logical_and(phase == 0, i == 0))
    def _():
        carry_ref[...] = jnp.zeros_like(carry_ref)

    ex = lax.broadcasted_iota(I32, (N_EXPERTS, tl), 0)
    oh0 = (ex == ei_ref[0:1, :]).astype(F32)
    oh1 = (ex == ei_ref[1:2, :]).astype(F32)
    cnt0 = jnp.sum(oh0, axis=1, keepdims=True)
    cnt1 = jnp.sum(oh1, axis=1, keepdims=True)

    @pl.when(phase == 0)
    def _():
        total = carry_ref[...] + cnt0 + cnt1
        carry_ref[...] = total
        cnt_ref[...] = total.astype(I32)
        slot_ref[...] = jnp.zeros_like(slot_ref)

    @pl.when(jnp.logical_and(phase == 1, i == 0))
    def _():
        blocks = jnp.floor((carry_ref[...] + (bm - 1)) * (1.0 / bm))
        a = lax.broadcasted_iota(I32, (N_EXPERTS, N_EXPERTS), 0)
        b = lax.broadcasted_iota(I32, (N_EXPERTS, N_EXPERTS), 1)
        before = (b < a).astype(BF16)
        start_ref[...] = jnp.dot(before, blocks.astype(BF16), preferred_element_type=F32) * bm
        carry_ref[...] = jnp.zeros_like(carry_ref)

    @pl.when(phase == 1)
    def _():
        a = lax.broadcasted_iota(I32, (tl, tl), 0)
        b = lax.broadcasted_iota(I32, (tl, tl), 1)
        upper = (a < b).astype(BF16)
        pre0 = jnp.dot(oh0.astype(BF16), upper, preferred_element_type=F32)
        pre1 = jnp.dot(oh1.astype(BF16), upper, preferred_element_type=F32)
        base = carry_ref[:, 0:1] + start_ref[:, 0:1]
        slot0 = jnp.sum(oh0 * (pre0 + base), axis=0, keepdims=True)
        slot1 = jnp.sum(oh1 * (pre1 + cnt0 + base), axis=0, keepdims=True)
        row = lax.broadcasted_iota(I32, (8, tl), 0)
        slot_ref[...] = jnp.where(row == 0, slot0.astype(I32), jnp.where(row == 1, slot1.astype(I32), 0))
        carry_ref[...] = carry_ref[...] + cnt0 + cnt1


def _slots(ei, *, tl, bm):
    n = ei.shape[1]
    return pl.pallas_call(
        functools.partial(_slot_kernel, tl=tl, bm=bm),
        grid=(2, n // tl),
        in_specs=[pl.BlockSpec((8, tl), lambda p, i: (0, i))],
        out_specs=[pl.BlockSpec((8, tl), lambda p, i: (0, i * p)),
                   pl.BlockSpec((N_EXPERTS, 128), lambda p, i: (0, 0))],
        out_shape=[jax.ShapeDtypeStruct((8, n), I32), jax.ShapeDtypeStruct((N_EXPERTS, 128), I32)],
        scratch_shapes=[pltpu.VMEM((N_EXPERTS, 128), F32), pltpu.VMEM((N_EXPERTS, 128), F32)],
        name="moe_slots",
    )(ei)


def _row_copy(src, src_row, dst, dst_row, sem):
    return pltpu.make_async_copy(src.at[pl.ds(src_row, 1)], dst.at[pl.ds(dst_row, 1)], sem)


ISSUE_UNROLL = 8


def _dispatch_kernel(slot_ref, pend_ref, cnt_ref, x1_ref, buf_ref, zero_ref, sem, zsem, *, n, tm, bm):
    base = pl.program_id(0) * tm

    @pl.when(pl.program_id(0) == 0)
    def _():
        zero_ref[...] = jnp.zeros_like(zero_ref)

        def zero_copy(e):
            start = pl.multiple_of(pend_ref[e] - bm, bm)
            return pltpu.make_async_copy(zero_ref, buf_ref.at[pl.ds(start, bm)], zsem)

        def tail_copy(blk):
            return pltpu.make_async_copy(zero_ref, buf_ref.at[pl.ds(pl.multiple_of(blk * bm, bm), bm)], zsem)

        def tail_start(blk, c):
            tail_copy(blk).start()
            return c

        def tail_wait(blk, c):
            tail_copy(blk).wait()
            return c

        for e in range(N_EXPERTS):
            @pl.when(cnt_ref[e] > 0)
            def _(e=e):
                zero_copy(e).start()
        first_unused = pend_ref[N_EXPERTS - 1] // bm
        lax.fori_loop(first_unused, buf_ref.shape[0] // bm, tail_start, 0)
        for e in range(N_EXPERTS):
            @pl.when(cnt_ref[e] > 0)
            def _(e=e):
                zero_copy(e).wait()
        lax.fori_loop(first_unused, buf_ref.shape[0] // bm, tail_wait, 0)

    def body(r, carry):
        for k in range(TOP_K):
            _row_copy(x1_ref, r, buf_ref, slot_ref[k * n + base + r], sem).start()
        return carry

    lax.fori_loop(0, tm, body, 0, unroll=ISSUE_UNROLL)
    for _ in range(TOP_K):
        pltpu.make_async_copy(x1_ref, buf_ref.at[pl.ds(0, tm)], sem).wait()


def _dispatch(slot_flat, pend, counts, x1, *, tm, bm, nblk):
    n = x1.shape[0]
    return pl.pallas_call(
        functools.partial(_dispatch_kernel, n=n, tm=tm, bm=bm),
        grid_spec=pltpu.PrefetchScalarGridSpec(
            num_scalar_prefetch=3,
            grid=(n // tm,),
            in_specs=[pl.BlockSpec((tm, D_MODEL), lambda i, s, p, c: (i, 0))],
            out_specs=pl.BlockSpec(memory_space=pl.ANY),
            scratch_shapes=[pltpu.VMEM((bm, D_MODEL), F32), pltpu.SemaphoreType.DMA(()),
                            pltpu.SemaphoreType.DMA(())],
        ),
        out_shape=jax.ShapeDtypeStruct((nblk * bm, D_MODEL), F32),
        name="moe_dispatch",
    )(slot_flat, pend, counts, x1)


def _expert_kernel(be_ref, nu_ref, xb_ref, wg_ref, wu_ref, wd_ref, out_ref, wg_bf, wu_bf, wd_bf):
    j = pl.program_id(0)
    used = j < nu_ref[0]
    changed = jnp.logical_or(j == 0, be_ref[j] != be_ref[jnp.maximum(j - 1, 0)])

    @pl.when(jnp.logical_and(used, changed))
    def _():
        wg_bf[...] = wg_ref[...].astype(BF16)
        wu_bf[...] = wu_ref[...].astype(BF16)
        wd_bf[...] = wd_ref[...].astype(BF16)

    @pl.when(used)
    def _():
        xb = xb_ref[...].astype(BF16)
        a = jnp.dot(xb, wg_bf[...], preferred_element_type=F32)
        b = jnp.dot(xb, wu_bf[...], preferred_element_type=F32)
        h = (a * _sigmoid(a)) * b
        out_ref[...] = jnp.dot(h.astype(BF16), wd_bf[...], preferred_element_type=F32)

    @pl.when(jnp.logical_not(used))
    def _():
        out_ref[...] = jnp.zeros_like(out_ref)


def _experts(block_expert, n_used, buf, w_g, w_u, w_d, *, bm):
    nblk = buf.shape[0] // bm

    def row_map(j, be, nu):
        return (jnp.minimum(j, nu[0] - 1), 0)

    def w_map(j, be, nu):
        return (be[j], 0, 0)

    return pl.pallas_call(
        _expert_kernel,
        grid_spec=pltpu.PrefetchScalarGridSpec(
            num_scalar_prefetch=2,
            grid=(nblk,),
            in_specs=[pl.BlockSpec((bm, D_MODEL), row_map),
                      pl.BlockSpec((None, D_MODEL, D_EXPERT), w_map),
                      pl.BlockSpec((None, D_MODEL, D_EXPERT), w_map),
                      pl.BlockSpec((None, D_EXPERT, D_MODEL), w_map)],
            out_specs=pl.BlockSpec((bm, D_MODEL), lambda j, be, nu: (j, 0)),
            scratch_shapes=[pltpu.VMEM((D_MODEL, D_EXPERT), BF16),
                            pltpu.VMEM((D_MODEL, D_EXPERT), BF16),
                            pltpu.VMEM((D_EXPERT, D_MODEL), BF16)],
        ),
        out_shape=jax.ShapeDtypeStruct(buf.shape, F32),
        compiler_params=_params(),
        name="moe_experts",
    )(block_expert, n_used, buf, w_g, w_u, w_d)


def _combine_kernel(slot_ref, x1_ref, gc_ref, g_ref, b_ref, eo_ref, y_ref, side_ref, rows, sem,
                    *, n, tm, alpha, main_tiles):
    i = pl.program_id(0)
    last = pl.num_programs(0) - 1
    cur = i % 2

    def start(tile, buf, r):
        for k in range(TOP_K):
            _row_copy(eo_ref, slot_ref[k * n + tile * tm + r], rows.at[buf, k], r, sem.at[buf]).start()

    def wait(buf):
        for k in range(TOP_K):
            pltpu.make_async_copy(eo_ref.at[pl.ds(0, tm)], rows.at[buf, k], sem.at[buf]).wait()

    @pl.when(i == 0)
    def _():
        def body(r, c):
            start(0, 0, r)
            return c
        lax.fori_loop(0, tm, body, 0, unroll=ISSUE_UNROLL)

    wait(cur)
    nxt = jnp.minimum(i + 1, last)
    for r in range(tm):
        start(nxt, 1 - cur, r)
    gc = gc_ref[...]
    z = alpha * x1_ref[...] + gc[:, 0:1] * rows[cur, 0] + gc[:, 1:2] * rows[cur, 1]
    mu = jnp.mean(z, axis=-1, keepdims=True)
    zc = z - mu
    var = jnp.mean(zc * zc, axis=-1, keepdims=True)
    y = zc * lax.rsqrt(var + LN_EPS) * g_ref[...] + b_ref[...]

    @pl.when(i < main_tiles)
    def _():
        y_ref[...] = y

    @pl.when(i == main_tiles)
    def _():
        side_ref[...] = y[0:side_ref.shape[0]]

    @pl.when(i == last)
    def _():
        wait(1 - cur)


def _combine(slot_flat, x1, gate_cols, ln_g, ln_b, expert_out, *, tm, alpha, main_rows, side_rows):
    n = x1.shape[0]
    main_tiles = main_rows // tm
    return pl.pallas_call(
        functools.partial(_combine_kernel, n=n, tm=tm, alpha=alpha, main_tiles=main_tiles),
        grid_spec=pltpu.PrefetchScalarGridSpec(
            num_scalar_prefetch=1,
            grid=(n // tm,),
            in_specs=[pl.BlockSpec((tm, D_MODEL), lambda i, s: (i, 0)),
                      pl.BlockSpec((tm, TOP_K), lambda i, s: (i, 0)),
                      pl.BlockSpec((1, D_MODEL), lambda i, s: (0, 0)),
                      pl.BlockSpec((1, D_MODEL), lambda i, s: (0, 0)),
                      pl.BlockSpec(memory_space=pl.ANY)],
            out_specs=[pl.BlockSpec((tm, D_MODEL), lambda i, s: (jnp.minimum(i, main_tiles - 1), 0)),
                       pl.BlockSpec((side_rows, D_MODEL), lambda i, s: (0, 0))],
            scratch_shapes=[pltpu.VMEM((2, TOP_K, tm, D_MODEL), F32), pltpu.SemaphoreType.DMA((2,))],
        ),
        out_shape=[jax.ShapeDtypeStruct((main_rows, D_MODEL), F32),
                   jax.ShapeDtypeStruct((side_rows, D_MODEL), F32)],
        compiler_params=_params(),
        name="moe_combine",
    )(slot_flat, x1, gate_cols, ln_g, ln_b, expert_out)


def _hier_moe_ln(x1, ei, gt, w_g, w_u, w_d, ln_g, ln_b, *, tl, tm, bm, alpha, main_rows, side_rows):
    n = x1.shape[0]
    m = n * TOP_K
    slot, cnt = _slots(ei, tl=tl, bm=bm)
    counts = cnt[:, 0]
    pend = jnp.cumsum((counts + bm - 1) // bm * bm)
    nblk = (m + N_EXPERTS * (bm - 1) + bm - 1) // bm
    blk_start = jnp.arange(nblk, dtype=I32) * bm
    n_used = (pend[-1] // bm).astype(I32)
    be = jnp.minimum(jnp.sum(pend[None, :] <= blk_start[:, None], axis=1), N_EXPERTS - 1).astype(I32)
    be = jnp.where(jnp.arange(nblk) < n_used, be, jnp.take(be, n_used - 1))
    slot_flat = slot[0:TOP_K].reshape(m)
    buf = _dispatch(slot_flat, pend.astype(I32), counts, x1, tm=tm, bm=bm, nblk=nblk)
    eo = _experts(be, n_used.reshape(1), buf, w_g, w_u, w_d, bm=bm)
    gate_cols = gt[0:TOP_K].T
    return _combine(slot_flat, x1, gate_cols, ln_g, ln_b, eo, tm=tm, alpha=alpha,
                    main_rows=main_rows, side_rows=side_rows)


def _t5_bucket(n):
    nf = jnp.maximum(n, 1).astype(F32)
    large = MAX_EXACT + (jnp.log(nf / MAX_EXACT) / math.log(MAX_DISTANCE / MAX_EXACT)
                         * (N_BUCKETS - MAX_EXACT)).astype(I32)
    large = jnp.minimum(large, N_BUCKETS - 1)
    return jnp.where(n < MAX_EXACT, n, large)


def _bias_per_group(rel_bias):
    offs = jnp.arange(N_KEYS, dtype=I32)[None, :] * jnp.array(DILATIONS, I32)[:, None]
    bucket = _t5_bucket(offs)
    table = rel_bias.reshape(N_BUCKETS, N_GROUPS, HEADS)
    b = table[bucket, jnp.arange(N_GROUPS)[:, None]]
    return jnp.transpose(b, (0, 2, 1)).astype(F32)


def _prompt_bias_tables(bias):
    width = 3 * Q_BLOCK
    neg = jnp.full((N_GROUPS, HEADS, Q_BLOCK - 1), NEG_INF, F32)
    r = jnp.concatenate([neg, bias[:, :, ::-1], neg, jnp.full((N_GROUPS, HEADS, 1), NEG_INF, F32)], axis=-1)
    flat = jnp.tile(r, (1, 1, Q_BLOCK))[:, :, :Q_BLOCK * (width - 1)]
    skew = flat.reshape(N_GROUPS, HEADS, Q_BLOCK, width - 1)
    later = skew[:, :, :, Q_BLOCK - 1:3 * Q_BLOCK - 1]
    has_prev = (np.arange(2 * Q_BLOCK) >= Q_BLOCK)[None, None, None, :]
    first = jnp.where(has_prev, later, NEG_INF)
    tb = jnp.stack([first, later], axis=1)
    return tb.reshape(N_GROUPS, 2, HEADS // 2, 2 * Q_BLOCK, 2 * Q_BLOCK)


def _sample_bias_tables(bias):
    t = np.arange(T_NEW)
    bcs = []
    for g, d in enumerate(DILATIONS):
        rev = bias[g][:, ::-1][:, :WINDOW_KEYS]
        if d == 1:
            dist = np.arange(WINDOW_KEYS)[None, :] - t[:, None]
            vals = jnp.take(rev, np.clip(dist, 0, WINDOW_KEYS - 1), axis=1)
            bcs.append(jnp.where(dist[None] >= 0, vals, NEG_INF))
        else:
            cls = np.arange(d)[None, :] == t[:, None]
            table = jnp.where(cls[None, :, None, :], rev[:, None, :, None], NEG_INF)
            bcs.append(table.reshape(HEADS, T_NEW, WINDOW_KEYS * d))
    back = t[:, None] - t[None, :]
    vals = jnp.take(bias, np.clip(back, 0, T_NEW - 1), axis=2)
    ok = np.stack([(back >= 0) if d == 1 else (back == 0) for d in DILATIONS])
    return bcs, jnp.where(ok[:, None], vals, NEG_INF)


def _split_bf16(w):
    hi = w.astype(BF16)
    lo = (w - hi.astype(F32)).astype(BF16)
    return hi, lo


def kernel(x_prompt, x_sample, cache_attn_w128, cache_attn_w512, cache_attn_w2048, state_conv, rel_bias, w_in, w_conv, w_pa, w_pb, w_o, ln1_g, ln1_b, w_router_group, w_router_expert, w_expert_gate, w_expert_up, w_expert_down, ln2_g, ln2_b):
    depth = w_in.shape[0]
    assert depth == 1 and x_prompt.shape[0] == 1
    alpha = (2.0 * depth) ** 0.25
    s = x_prompt.shape[1]
    bd, t_len = x_sample.shape[0], x_sample.shape[1]
    assert t_len == T_NEW and s % (DILATIONS[-1] * Q_BLOCK * Q_BLOCKS_PER_STEP) == 0

    bias = _bias_per_group(rel_bias)
    tb = _prompt_bias_tables(bias)
    bcs, bn = _sample_bias_tables(bias)

    w_in_bf = w_in[0].astype(BF16)
    w_pa_bf = w_pa[0].astype(BF16)
    w_pb_bf = w_pb[0].astype(BF16)
    w_o_bf = w_o[0].astype(BF16)
    wr = jnp.zeros((ROUTER_ROWS, D_MODEL), F32)
    wr = wr.at[0:N_EXPERT_GROUPS].set(w_router_group[0].T).at[8:8 + N_EXPERTS].set(w_router_expert[0].T)
    wr_hi, wr_lo = _split_bf16(wr)
    g1, b1 = ln1_g[0][None], ln1_b[0][None]
    g2, b2 = ln2_g[0][None], ln2_b[0][None]
    wg, wu, wd = w_expert_gate[0], w_expert_up[0], w_expert_down[0]

    xp = x_prompt[0]
    kv_tail = min(MAX_DISTANCE, s)
    q, kb, vb, (k32, v32, yb, sga, sgb, ut) = _proj(
        xp, w_in_bf, w_conv[0], None, tm=256, u_tail=8, q_dtype=BF16, dils=DILATIONS, kv_f32=True)
    k32, v32 = k32[s - kv_tail:], v32[s - kv_tail:]
    o_l = [_attn_prompt_group(q[g], kb[g], vb[g], tb[g], g) for g in range(N_GROUPS)]
    mix_tm = 512
    routed = _mix(xp, [a[0] for a in o_l], [a[1] for a in o_l], yb, sga, sgb,
                  w_pa_bf, w_pb_bf, w_o_bf, g1, b1, wr_hi, wr_lo, tm=mix_tm, alpha=alpha, dils=DILATIONS,
                  extra_tiles=1)

    kv_prompt = []
    for g, d in enumerate(DILATIONS):
        length = min(WINDOW_KEYS * d, s)
        cols = slice(g * GROUP_WIDTH, (g + 1) * GROUP_WIDTH)
        kg = k32[kv_tail - length:, cols].reshape(length, HEADS, HEAD_DIM)
        vg = v32[kv_tail - length:, cols].reshape(length, HEADS, HEAD_DIM)
        kv_prompt.append(jnp.stack([kg, vg], axis=1)[None, None])
    conv_prompt = ut[6:8][None, None]

    ns = bd * t_len
    xs = x_sample.reshape(ns, D_MODEL)
    st = state_conv[0]
    s0 = jnp.repeat(st[:, 0], t_len, axis=0)
    s1 = jnp.repeat(st[:, 1], t_len, axis=0)
    qs, _, _, (k32s, v32s, ybs, sgas, sgbs, us) = _proj(
        xs, w_in_bf, w_conv[0], (s0, s1), tm=ns, u_tail=ns, q_dtype=F32, dils=NO_DILATION, kv_f32=True)
    qs = jnp.concatenate([a[0] for a in qs], axis=1)
    packed = jnp.stack([qs, k32s, v32s]).reshape(3, bd, t_len, N_GROUPS, GROUP_WIDTH)
    qkv_t = jnp.transpose(packed, (1, 4, 0, 3, 2)).reshape(bd, GROUP_WIDTH, 3 * N_GROUPS * t_len)
    qkv_t = jnp.pad(qkv_t, ((0, 0), (0, 0), (0, 128 - 3 * N_GROUPS * t_len)))
    caches = (cache_attn_w128[0], cache_attn_w512[0], cache_attn_w2048[0])
    caches_t = [jnp.transpose(c, (0, 2, 3, 4, 1)).reshape(bd, 2, GROUP_WIDTH, c.shape[1]) for c in caches]
    pair = (bd, t_len, N_GROUPS, HEADS // 2, 2, HEAD_DIM)
    q6 = jnp.transpose(qs.reshape(pair), (0, 3, 2, 4, 1, 5))
    zeros = jnp.zeros_like(q6[:, :, :, 0])
    qbd = jnp.stack([jnp.concatenate([q6[:, :, :, 0], zeros], axis=-1),
                     jnp.concatenate([zeros, q6[:, :, :, 1]], axis=-1)], axis=3)
    qbd = qbd.reshape(bd, HEADS // 2, N_GROUPS, 2 * t_len, 128)
    new_rows = jnp.stack([k32s, v32s]).reshape(2, bd, t_len, N_GROUPS, HEADS // 2, 128)
    new_rows = jnp.transpose(new_rows, (1, 4, 3, 0, 2, 5))
    bcs = [t.reshape(HEADS // 2, 2 * t_len, t.shape[-1]) for t in bcs]
    bn = bn.reshape(N_GROUPS, HEADS // 2, 2 * t_len, t_len)
    n0, n1, n2, o_s, lse_s = _sample_cache(qbd, new_rows, qkv_t, caches_t, bcs, bn)

    def unpack(a):
        a = a[:, :N_GROUPS * t_len].reshape(bd, N_GROUPS, t_len, GROUP_WIDTH)
        return jnp.transpose(a, (1, 0, 2, 3)).reshape(N_GROUPS, 1, ns, GROUP_WIDTH)

    o_s, lse_s = unpack(o_s), unpack(lse_s)
    assert ns <= mix_tm and s % ns == 0
    x1, ei, gt = _mix(xs, o_s, lse_s, ybs, sgas, sgbs, w_pa_bf, w_pb_bf, w_o_bf, g1, b1, wr_hi, wr_lo,
                      tm=ns, alpha=alpha, dils=NO_DILATION, into=routed, into_tile=s // ns)
    y_prompt, y_sample = _hier_moe_ln(x1, ei, gt, wg, wu, wd, g2, b2, tl=512, tm=512, bm=256, alpha=alpha,
                                      main_rows=s, side_rows=ns)
    y_prompt = y_prompt[None]
    y_sample = y_sample.reshape(bd, t_len, D_MODEL)

    kv_sample = [jnp.transpose(c.reshape(bd, 2, HEADS, HEAD_DIM, c.shape[-1]), (0, 4, 1, 2, 3))[None]
                 for c in (n0, n1, n2)]
    conv_sample = us.reshape(bd, t_len, CONV_CHANNELS)[:, t_len - 2:][None]

    return (y_prompt, y_sample, kv_prompt[0], kv_prompt[1], kv_prompt[2], conv_prompt,
            kv_sample[0], kv_sample[1], kv_sample[2], conv_sample)
```

```python
import functools
import math

import numpy as np
import jax
import jax.numpy as jnp
from jax import lax
from jax.experimental import pallas as pl
from jax.experimental.pallas import tpu as pltpu

F32 = jnp.float32
BF16 = jnp.bfloat16
I32 = jnp.int32

D_MODEL = 1024
N_GROUPS = 3
HEADS = 8
HEAD_DIM = 64
GROUP_WIDTH = HEADS * HEAD_DIM
ATTN_WIDTH = N_GROUPS * GROUP_WIDTH
DILATIONS = (1, 4, 16)
NO_DILATION = (1, 1, 1)
WINDOW_KEYS = 128
N_KEYS = WINDOW_KEYS + 1
N_BUCKETS = 32
MAX_EXACT = 16
MAX_DISTANCE = 2048
CONV_CHANNELS = 512
N_EXPERT_GROUPS = 4
EXPERTS_PER_GROUP = 8
N_EXPERTS = 32
TOP_K = 2
D_EXPERT = 512
LN_EPS = 1e-5
PROJ_WIDTH = 3 * ATTN_WIDTH + 3 * CONV_CHANNELS + 2 * D_MODEL
ROUTER_ROWS = 8 + N_EXPERTS
Q_BLOCK = 128
T_NEW = 4
NEG_INF = float("-inf")
VMEM_LIMIT = 56 * 1024 * 1024


def _sigmoid(x):
    return 1.0 / (1.0 + jnp.exp(-x))


def _params(limit=VMEM_LIMIT):
    return pltpu.CompilerParams(vmem_limit_bytes=limit)


def _proj_kernel(*refs, tm, tail_rows, sample_mode, dils, kv_f32):
    n_in = 5 if sample_mode else 3
    x_ref, w_ref, wc_ref = refs[0:3]
    outs = list(refs[n_in:])
    q_refs, k_refs, v_refs = outs[0:3], outs[3:6], outs[6:9]
    del outs[0:9]
    k32_ref, v32_ref = (outs.pop(0), outs.pop(0)) if kv_f32 else (None, None)
    yb_ref, sga_ref, sgb_ref, ut_ref, cls_ref = outs[0:5]
    xb = x_ref[...].astype(BF16)

    def col(c0, width):
        return jnp.dot(xb, w_ref[:, c0:c0 + width], preferred_element_type=F32)

    def write_classes(val, group_refs):
        for g, d in enumerate(dils):
            part = val[:, g * GROUP_WIDTH:(g + 1) * GROUP_WIDTH]
            ref = group_refs[g]
            if d == 1:
                ref[0] = part.astype(ref.dtype)
            else:
                for kk in range(GROUP_WIDTH // 128):
                    lanes = slice(kk * 128, (kk + 1) * 128)
                    cls_ref[kk] = part[:, lanes]
                    for c in range(d):
                        ref[c, :, lanes] = cls_ref[kk, pl.ds(c, tm // d, stride=d), :].astype(ref.dtype)

    write_classes(col(0, ATTN_WIDTH), q_refs)
    k = col(ATTN_WIDTH, ATTN_WIDTH)
    v = col(2 * ATTN_WIDTH, ATTN_WIDTH)
    if kv_f32:
        k32_ref[...] = k
        v32_ref[...] = v
    write_classes(k, k_refs)
    write_classes(v, v_refs)

    c0 = 3 * ATTN_WIDTH
    bg = col(c0, CONV_CHANNELS)
    u = col(c0 + CONV_CHANNELS, CONV_CHANNELS) * col(c0 + 2 * CONV_CHANNELS, CONV_CHANNELS)
    row = lax.broadcasted_iota(I32, (tm, CONV_CHANNELS), 0)
    r1 = pltpu.roll(u, 1, axis=0)
    r2 = pltpu.roll(u, 2, axis=0)
    if sample_mode:
        s0 = refs[3][...]
        s1 = refs[4][...]
        t = row & (T_NEW - 1)
        prev1 = jnp.where(t == 0, s1, r1)
        prev2 = jnp.where(t == 0, s0, jnp.where(t == 1, s1, r2))
    else:
        carry_ref = outs[5]

        @pl.when(pl.program_id(0) == 0)
        def _():
            carry_ref[...] = jnp.zeros_like(carry_ref)
        c6 = carry_ref[6:7, :]
        c7 = carry_ref[7:8, :]
        prev1 = jnp.where(row == 0, c7, r1)
        prev2 = jnp.where(row == 0, c6, jnp.where(row == 1, c7, r2))
        carry_ref[...] = u[tm - 8:tm, :]
    conv = prev2 * wc_ref[0:1, :] + prev1 * wc_ref[1:2, :] + u * wc_ref[2:3, :]
    yb_ref[...] = (bg * conv).astype(BF16)
    ut_ref[...] = u[tm - tail_rows:tm, :]

    c1 = c0 + 3 * CONV_CHANNELS
    sga_ref[...] = _sigmoid(col(c1, D_MODEL)).astype(BF16)
    sgb_ref[...] = _sigmoid(col(c1 + D_MODEL, D_MODEL)).astype(BF16)


def _proj(x, w_in_bf, w_conv, conv_prev, *, tm, u_tail, q_dtype, dils, kv_f32):
    n = x.shape[0]
    sample_mode = conv_prev is not None
    nt = n // tm

    def row_spec(width):
        return pl.BlockSpec((tm, width), lambda i: (i, 0))

    def class_spec(d):
        return pl.BlockSpec((d, tm // d, GROUP_WIDTH), lambda i: (0, i, 0))

    def class_shape(d, dtype):
        return jax.ShapeDtypeStruct((d, n // d, GROUP_WIDTH), dtype)

    in_specs = [
        row_spec(D_MODEL),
        pl.BlockSpec((D_MODEL, PROJ_WIDTH), lambda i: (0, 0), pipeline_mode=pl.Buffered(1)),
        pl.BlockSpec((3, CONV_CHANNELS), lambda i: (0, 0)),
    ]
    args = [x, w_in_bf, w_conv]
    scratch = [pltpu.VMEM((GROUP_WIDTH // 128, tm, 128), F32)]
    if sample_mode:
        in_specs += [row_spec(CONV_CHANNELS), row_spec(CONV_CHANNELS)]
        args += [conv_prev[0], conv_prev[1]]
    else:
        scratch.append(pltpu.VMEM((8, CONV_CHANNELS), F32))
    out_shape = (
        [class_shape(d, q_dtype) for d in dils] + [class_shape(d, BF16) for d in dils] * 2
        + [jax.ShapeDtypeStruct((n, ATTN_WIDTH), F32)] * (2 if kv_f32 else 0)
        + [jax.ShapeDtypeStruct((n, CONV_CHANNELS), BF16),
           jax.ShapeDtypeStruct((n, D_MODEL), BF16),
           jax.ShapeDtypeStruct((n, D_MODEL), BF16),
           jax.ShapeDtypeStruct((u_tail, CONV_CHANNELS), F32)])
    out_specs = (
        [class_spec(d) for d in dils] * 3
        + [row_spec(ATTN_WIDTH)] * (2 if kv_f32 else 0)
        + [row_spec(CONV_CHANNELS), row_spec(D_MODEL), row_spec(D_MODEL),
           pl.BlockSpec((u_tail, CONV_CHANNELS), lambda i: (0, 0))])
    res = pl.pallas_call(
        functools.partial(_proj_kernel, tm=tm, tail_rows=u_tail, sample_mode=sample_mode, dils=dils, kv_f32=kv_f32),
        grid=(nt,),
        in_specs=in_specs,
        out_specs=out_specs,
        out_shape=out_shape,
        scratch_shapes=scratch,
        compiler_params=_params(),
        name="proj",
    )(*args)
    return res[0:3], res[3:6], res[6:9], res[9:]


Q_BLOCKS_PER_STEP = 4


def _attn_kernel(q_ref, kp_ref, kc_ref, vp_ref, vc_ref, tb_ref, o_ref, lse_ref):
    lane = lax.broadcasted_iota(I32, (Q_BLOCK, 128), 1)
    first = lane < HEAD_DIM
    scale = HEAD_DIM ** -0.5
    has_prev = jnp.minimum(pl.program_id(1), 1)
    for sub in range(Q_BLOCKS_PER_STEP):
        rows = slice(sub * Q_BLOCK, (sub + 1) * Q_BLOCK)
        band = slice((sub - 1) * Q_BLOCK, (sub + 1) * Q_BLOCK)
        for pr in range(HEADS // 2):
            sl = slice(pr * 128, (pr + 1) * 128)
            if sub == 0:
                k = jnp.concatenate([kp_ref[:, sl], kc_ref[rows, sl]], axis=0)
                v = jnp.concatenate([vp_ref[:, sl], vc_ref[rows, sl]], axis=0)
                bias = tb_ref[has_prev, pr]
            else:
                k = kc_ref[band, sl]
                v = vc_ref[band, sl]
                bias = tb_ref[1, pr]
            qf = q_ref[rows, sl].astype(F32) * scale
            qq = jnp.concatenate([jnp.where(first, qf, 0.0), jnp.where(first, 0.0, qf)], axis=0).astype(BF16)
            s = lax.dot_general(qq, k, (((1,), (1,)), ((), ())), preferred_element_type=F32) + bias
            m = jnp.max(s, axis=-1, keepdims=True)
            p = jnp.exp(s - m)
            l = jnp.sum(p, axis=-1, keepdims=True)
            o = jnp.dot(p.astype(BF16), v, preferred_element_type=F32) / l
            lse = m + jnp.log(l)
            o_ref[rows, sl] = jnp.where(first, o[:Q_BLOCK], o[Q_BLOCK:]).astype(o_ref.dtype)
            lse_ref[rows, sl] = jnp.where(first, jnp.broadcast_to(lse[:Q_BLOCK], (Q_BLOCK, 128)),
                                          jnp.broadcast_to(lse[Q_BLOCK:], (Q_BLOCK, 128)))


def _attn_prompt_group(q, kb, vb, tb, g):
    d, rows = q.shape[0], q.shape[1]
    nq = Q_BLOCKS_PER_STEP
    cur = pl.BlockSpec((None, nq * Q_BLOCK, GROUP_WIDTH), lambda c, i: (c, i, 0))
    prev = pl.BlockSpec((None, Q_BLOCK, GROUP_WIDTH), lambda c, i: (c, jnp.maximum(i * nq - 1, 0), 0))
    return pl.pallas_call(
        _attn_kernel,
        grid=(d, rows // (nq * Q_BLOCK)),
        in_specs=[cur, prev, cur, prev, cur, pl.BlockSpec(tb.shape, lambda c, i: (0, 0, 0, 0))],
        out_specs=[cur, cur],
        out_shape=[jax.ShapeDtypeStruct((d, rows, GROUP_WIDTH), BF16),
                   jax.ShapeDtypeStruct((d, rows, GROUP_WIDTH), F32)],
        compiler_params=_params(),
        name=f"attn_prompt_g{g}",
    )(q, kb, kb, vb, vb, tb)


PACK_Q, PACK_K, PACK_V = 0, N_GROUPS * T_NEW, 2 * N_GROUPS * T_NEW


PAIRS_PER_STEP = 2


def _sample_cache_kernel(qbd_ref, nr_ref, qkv_ref, c0_ref, c1_ref, c2_ref, b0_ref, b1_ref, b2_ref, bn_ref,
                         n0_ref, n1_ref, n2_ref, o_ref, lse_ref):
    for pp in range(PAIRS_PER_STEP):
        rows = pl.ds(pp * 128, 128)
        kv = pl.ds(0, 2)
        _sample_pair(qbd_ref.at[pp], nr_ref.at[pp], qkv_ref.at[rows],
                     [c.at[kv, rows] for c in (c0_ref, c1_ref, c2_ref)],
                     [b.at[pp] for b in (b0_ref, b1_ref, b2_ref)], bn_ref.at[pl.ds(0, N_GROUPS), pp],
                     [c.at[kv, rows] for c in (n0_ref, n1_ref, n2_ref)],
                     o_ref.at[pl.ds(0, 16), rows], lse_ref.at[pl.ds(0, 16), rows])


def _sample_pair(qbd_ref, nr_ref, qkv_ref, c_refs, b_refs, bn_ref, n_refs, o_ref, lse_ref):
    scale = HEAD_DIM ** -0.5
    nt = (((1,), (1,)), ((), ()))
    lane = lax.broadcasted_iota(I32, (128, 128), 1)
    head0 = lane[0:T_NEW] < HEAD_DIM
    for g, (c_ref, b_ref, n_ref) in enumerate(zip(c_refs, b_refs, n_refs)):
        length = c_ref.shape[-1]
        k_new = qkv_ref[:, PACK_K + g * T_NEW:PACK_K + (g + 1) * T_NEW]
        v_new = qkv_ref[:, PACK_V + g * T_NEW:PACK_V + (g + 1) * T_NEW]
        qbd = qbd_ref[g]
        s_c = jnp.dot(qbd.astype(BF16), c_ref[0].astype(BF16), preferred_element_type=F32) * scale + b_ref[...]
        bn = bn_ref[g]
        s_n = [jnp.sum(qbd * nr_ref[g, 0, tn:tn + 1, :], axis=1, keepdims=True) * scale + bn[:, tn:tn + 1]
               for tn in range(T_NEW)]
        m = jnp.max(s_c, axis=1, keepdims=True)
        for x in s_n:
            m = jnp.maximum(m, x)
        p_c = jnp.exp(s_c - m)
        p_n = [jnp.exp(x - m) for x in s_n]
        l = jnp.sum(p_c, axis=1, keepdims=True)
        acc = lax.dot_general(p_c.astype(BF16), c_ref[1].astype(BF16), nt, preferred_element_type=F32)
        for tn in range(T_NEW):
            l = l + p_n[tn]
            acc = acc + p_n[tn] * nr_ref[g, 1, tn:tn + 1, :]
        o = acc / l
        lse = jnp.broadcast_to(m + jnp.log(l), (2 * T_NEW, 128))
        o_ref[g * T_NEW:(g + 1) * T_NEW, :] = jnp.where(head0, o[0:T_NEW], o[T_NEW:])
        lse_ref[g * T_NEW:(g + 1) * T_NEW, :] = jnp.where(head0, lse[0:T_NEW], lse[T_NEW:])

        for kv, new in ((0, k_new), (1, v_new)):
            rolled = pltpu.roll(c_ref[kv], length - T_NEW, axis=1)
            tail = rolled[:, length - 128:]
            for t in range(T_NEW):
                tail = jnp.where(lane == 128 - T_NEW + t, new[:, t:t + 1], tail)
            if length > 128:
                n_ref[kv, :, 0:length - 128] = rolled[:, 0:length - 128]
            n_ref[kv, :, length - 128:] = tail
    pad_rows = slice(N_GROUPS * T_NEW, 16)
    o_ref[pad_rows, :] = jnp.zeros((16 - N_GROUPS * T_NEW, 128), F32)
    lse_ref[pad_rows, :] = jnp.zeros((16 - N_GROUPS * T_NEW, 128), F32)


def _sample_cache(qbd, new_rows, qkv_t, caches_t, bcs, bn):
    b = qkv_t.shape[0]

    pp = PAIRS_PER_STEP

    def cache_spec(c):
        return pl.BlockSpec((None, 2, pp * 128, c.shape[-1]), lambda i, h: (i, 0, h, 0))

    def bias_spec(t):
        return pl.BlockSpec((pp, 2 * T_NEW, t.shape[-1]), lambda i, h: (h, 0, 0))

    out = pl.BlockSpec((None, 16, pp * 128), lambda i, h: (i, 0, h))
    return pl.pallas_call(
        _sample_cache_kernel,
        grid=(b, HEADS // 2 // pp),
        in_specs=[pl.BlockSpec((None, pp, N_GROUPS, 2 * T_NEW, 128), lambda i, h: (i, h, 0, 0, 0)),
                  pl.BlockSpec((None, pp, N_GROUPS, 2, T_NEW, 128), lambda i, h: (i, h, 0, 0, 0, 0)),
                  pl.BlockSpec((None, pp * 128, 128), lambda i, h: (i, h, 0))]
                 + [cache_spec(c) for c in caches_t] + [bias_spec(t) for t in bcs]
                 + [pl.BlockSpec((N_GROUPS, pp, 2 * T_NEW, T_NEW), lambda i, h: (0, h, 0, 0))],
        out_specs=[cache_spec(c) for c in caches_t] + [out, out],
        out_shape=[jax.ShapeDtypeStruct(c.shape, c.dtype) for c in caches_t]
                  + [jax.ShapeDtypeStruct((b, 16, GROUP_WIDTH), F32)] * 2,
        compiler_params=_params(),
        name="sample_cache",
    )(qbd, new_rows, qkv_t, *caches_t, *bcs, bn)


def _mix_kernel(*refs, tm, alpha, dils, n_alias):
    (x_ref, o0_ref, o1_ref, o2_ref, l0_ref, l1_ref, l2_ref, yb_ref, sga_ref, sgb_ref,
     wpa_ref, wpb_ref, wo_ref, g_ref, b_ref, wrh_ref, wrl_ref) = refs[0:17]
    x1_ref, ei_ref, gt_ref = refs[17 + n_alias:20 + n_alias]
    scratch = list(refs[20 + n_alias:])

    def natural(ref, d):
        if d == 1:
            return ref[0].astype(F32)
        scr = scratch.pop()
        for kk in range(GROUP_WIDTH // 128):
            for c in range(d):
                scr[kk, pl.ds(c, tm // d, stride=d), :] = ref[c, :, kk * 128:(kk + 1) * 128].astype(F32)
        return jnp.concatenate([scr[kk] for kk in range(GROUP_WIDTH // 128)], axis=1)

    l0, l1, l2 = natural(l0_ref, dils[0]), natural(l1_ref, dils[1]), natural(l2_ref, dils[2])
    mx = jnp.maximum(jnp.maximum(l0, l1), l2)
    e0 = jnp.exp(l0 - mx)
    e1 = jnp.exp(l1 - mx)
    e2 = jnp.exp(l2 - mx)
    ya = (e0 * natural(o0_ref, dils[0]) + e1 * natural(o1_ref, dils[1]) + e2 * natural(o2_ref, dils[2])) / (e0 + e1 + e2)
    pa = jnp.dot(ya.astype(BF16), wpa_ref[...], preferred_element_type=F32)
    pb = jnp.dot(yb_ref[...], wpb_ref[...], preferred_element_type=F32)
    gated = sga_ref[...].astype(F32) * pa + sgb_ref[...].astype(F32) * pb
    mix = jnp.dot(gated.astype(BF16), wo_ref[...], preferred_element_type=F32)
    z = alpha * x_ref[...] + mix
    mu = jnp.mean(z, axis=-1, keepdims=True)
    zc = z - mu
    var = jnp.mean(zc * zc, axis=-1, keepdims=True)
    x1 = zc * lax.rsqrt(var + LN_EPS) * g_ref[...] + b_ref[...]
    x1_ref[...] = x1

    xh = x1.astype(BF16)
    xl = (x1 - xh.astype(F32)).astype(BF16)
    nt = (((1,), (1,)), ((), ()))
    wrh = wrh_ref[...]
    lt = (lax.dot_general(wrh, xh, nt, preferred_element_type=F32)
          + lax.dot_general(wrh, xl, nt, preferred_element_type=F32)
          + lax.dot_general(wrl_ref[...], xh, nt, preferred_element_type=F32))

    gl = lt[0:N_EXPERT_GROUPS]
    gmax = jnp.max(gl, axis=0, keepdims=True)
    idx4 = lax.broadcasted_iota(I32, (N_EXPERT_GROUPS, tm), 0)
    g_idx = jnp.min(jnp.where(gl == gmax, idx4, N_EXPERT_GROUPS), axis=0, keepdims=True)
    g_prob = 1.0 / jnp.sum(jnp.exp(gl - gmax), axis=0, keepdims=True)
    e_sel = lt[8:16]
    for grp in range(1, N_EXPERT_GROUPS):
        e_sel = jnp.where(g_idx == grp, lt[8 + 8 * grp:16 + 8 * grp], e_sel)
    idx8 = lax.broadcasted_iota(I32, (EXPERTS_PER_GROUP, tm), 0)
    v1 = jnp.max(e_sel, axis=0, keepdims=True)
    i1 = jnp.min(jnp.where(e_sel == v1, idx8, EXPERTS_PER_GROUP), axis=0, keepdims=True)
    rest = jnp.where(idx8 == i1, NEG_INF, e_sel)
    v2 = jnp.max(rest, axis=0, keepdims=True)
    i2 = jnp.min(jnp.where(rest == v2, idx8, EXPERTS_PER_GROUP), axis=0, keepdims=True)
    r = jnp.exp(v2 - v1)
    gate1 = g_prob / (1.0 + r)
    gate2 = g_prob * r / (1.0 + r)
    ex1 = g_idx * EXPERTS_PER_GROUP + i1
    ex2 = g_idx * EXPERTS_PER_GROUP + i2
    ei_ref[...] = jnp.where(idx8 == 0, ex1, jnp.where(idx8 == 1, ex2, 0))
    gt_ref[...] = jnp.where(idx8 == 0, gate1, jnp.where(idx8 == 1, gate2, 0.0))


def _mix(x, o, lse, yb, sga, sgb, w_pa, w_pb, w_o, ln_g, ln_b, wr_hi, wr_lo, *, tm, alpha, dils,
         extra_tiles=0, into=None, into_tile=0):
    n = x.shape[0]
    nt = n // tm

    def src(i):
        return jnp.minimum(i, nt - 1)

    def row_spec(width):
        return pl.BlockSpec((tm, width), lambda i: (src(i), 0))

    def class_spec(d):
        return pl.BlockSpec((d, tm // d, GROUP_WIDTH), lambda i: (0, src(i), 0))

    def full(a):
        return pl.BlockSpec(a.shape, lambda i: (0,) * a.ndim)

    n_alias = 0 if into is None else 3
    rows_out = n + extra_tiles * tm if into is None else into[0].shape[0]
    any_spec = pl.BlockSpec(memory_space=pl.ANY)
    n_scratch = 2 * sum(1 for d in dils if d > 1)
    args = [x, o[0], o[1], o[2], lse[0], lse[1], lse[2], yb, sga, sgb, w_pa, w_pb, w_o, ln_g, ln_b, wr_hi, wr_lo]
    return pl.pallas_call(
        functools.partial(_mix_kernel, tm=tm, alpha=alpha, dils=dils, n_alias=n_alias),
        grid=(nt + extra_tiles,),
        in_specs=[row_spec(D_MODEL)] + [class_spec(d) for d in dils] * 2 + [row_spec(CONV_CHANNELS)]
                 + [row_spec(D_MODEL)] * 2
                 + [full(w_pa), full(w_pb), full(w_o), full(ln_g), full(ln_b), full(wr_hi), full(wr_lo)]
                 + [any_spec] * n_alias,
        out_specs=[pl.BlockSpec((tm, D_MODEL), lambda i: (i + into_tile, 0)),
                   pl.BlockSpec((8, tm), lambda i: (0, i + into_tile)),
                   pl.BlockSpec((8, tm), lambda i: (0, i + into_tile))],
        out_shape=[jax.ShapeDtypeStruct((rows_out, D_MODEL), F32),
                   jax.ShapeDtypeStruct((8, rows_out), I32),
                   jax.ShapeDtypeStruct((8, rows_out), F32)],
        input_output_aliases={len(args) + k: k for k in range(n_alias)},
        scratch_shapes=[pltpu.VMEM((GROUP_WIDTH // 128, tm, 128), F32)] * n_scratch,
        compiler_params=_params(),
        name="mix",
    )(*args, *(into or ()))


def _slot_kernel(ei_ref, slot_ref, cnt_ref, carry_ref, start_ref, *, tl, bm):
    phase = pl.program_id(0)
    i = pl.program_id(1)

    @pl.when(jnp.logical_and(phase == 0, i == 0))
    def _():
        carry_ref[...] = jnp.zeros_like(carry_ref)

    ex = lax.broadcasted_iota(I32, (N_EXPERTS, tl), 0)
    oh0 = (ex == ei_ref[0:1, :]).astype(F32)
    oh1 = (ex == ei_ref[1:2, :]).astype(F32)
    cnt0 = jnp.sum(oh0, axis=1, keepdims=True)
    cnt1 = jnp.sum(oh1, axis=1, keepdims=True)

    @pl.when(phase == 0)
    def _():
        total = carry_ref[...] + cnt0 + cnt1
        carry_ref[...] = total
        cnt_ref[...] = total.astype(I32)
        slot_ref[...] = jnp.zeros_like(slot_ref)

    @pl.when(jnp.logical_and(phase == 1, i == 0))
    def _():
        blocks = jnp.floor((carry_ref[...] + (bm - 1)) * (1.0 / bm))
        a = lax.broadcasted_iota(I32, (N_EXPERTS, N_EXPERTS), 0)
        b = lax.broadcasted_iota(I32, (N_EXPERTS, N_EXPERTS), 1)
        before = (b < a).astype(BF16)
        start_ref[...] = jnp.dot(before, blocks.astype(BF16), preferred_element_type=F32) * bm
        carry_ref[...] = jnp.zeros_like(carry_ref)

    @pl.when(phase == 1)
    def _():
        a = lax.broadcasted_iota(I32, (tl, tl), 0)
        b = lax.broadcasted_iota(I32, (tl, tl), 1)
        upper = (a < b).astype(BF16)
        pre0 = jnp.dot(oh0.astype(BF16), upper, preferred_element_type=F32)
        pre1 = jnp.dot(oh1.astype(BF16), upper, preferred_element_type=F32)
        base = carry_ref[:, 0:1] + start_ref[:, 0:1]
        slot0 = jnp.sum(oh0 * (pre0 + base), axis=0, keepdims=True)
        slot1 = jnp.sum(oh1 * (pre1 + cnt0 + base), axis=0, keepdims=True)
        row = lax.broadcasted_iota(I32, (8, tl), 0)
        slot_ref[...] = jnp.where(row == 0, slot0.astype(I32), jnp.where(row == 1, slot1.astype(I32), 0))
        carry_ref[...] = carry_ref[...] + cnt0 + cnt1


def _slots(ei, *, tl, bm):
    n = ei.shape[1]
    return pl.pallas_call(
        functools.partial(_slot_kernel, tl=tl, bm=bm),
        grid=(2, n // tl),
        in_specs=[pl.BlockSpec((8, tl), lambda p, i: (0, i))],
        out_specs=[pl.BlockSpec((8, tl), lambda p, i: (0, i * p)),
                   pl.BlockSpec((N_EXPERTS, 128), lambda p, i: (0, 0))],
        out_shape=[jax.ShapeDtypeStruct((8, n), I32), jax.ShapeDtypeStruct((N_EXPERTS, 128), I32)],
        scratch_shapes=[pltpu.VMEM((N_EXPERTS, 128), F32), pltpu.VMEM((N_EXPERTS, 128), F32)],
        name="moe_slots",
    )(ei)


def _row_copy(src, src_row, dst, dst_row, sem):
    return pltpu.make_async_copy(src.at[pl.ds(src_row, 1)], dst.at[pl.ds(dst_row, 1)], sem)


ISSUE_UNROLL = 8


def _dispatch_kernel(slot_ref, pend_ref, cnt_ref, x1_ref, buf_ref, zero_ref, sem, zsem, *, n, tm, bm):
    base = pl.program_id(0) * tm

    @pl.when(pl.program_id(0) == 0)
    def _():
        zero_ref[...] = jnp.zeros_like(zero_ref)

        def zero_copy(e):
            start = pl.multiple_of(pend_ref[e] - bm, bm)
            return pltpu.make_async_copy(zero_ref, buf_ref.at[pl.ds(start, bm)], zsem)

        def tail_copy(blk):
            return pltpu.make_async_copy(zero_ref, buf_ref.at[pl.ds(pl.multiple_of(blk * bm, bm), bm)], zsem)

        def tail_start(blk, c):
            tail_copy(blk).start()
            return c

        def tail_wait(blk, c):
            tail_copy(blk).wait()
            return c

        for e in range(N_EXPERTS):
            @pl.when(cnt_ref[e] > 0)
            def _(e=e):
                zero_copy(e).start()
        first_unused = pend_ref[N_EXPERTS - 1] // bm
        lax.fori_loop(first_unused, buf_ref.shape[0] // bm, tail_start, 0)
        for e in range(N_EXPERTS):
            @pl.when(cnt_ref[e] > 0)
            def _(e=e):
                zero_copy(e).wait()
        lax.fori_loop(first_unused, buf_ref.shape[0] // bm, tail_wait, 0)

    def body(r, carry):
        for k in range(TOP_K):
            _row_copy(x1_ref, r, buf_ref, slot_ref[k * n + base + r], sem).start()
        return carry

    lax.fori_loop(0, tm, body, 0, unroll=ISSUE_UNROLL)
    for _ in range(TOP_K):
        pltpu.make_async_copy(x1_ref, buf_ref.at[pl.ds(0, tm)], sem).wait()


def _dispatch(slot_flat, pend, counts, x1, *, tm, bm, nblk):
    n = x1.shape[0]
    return pl.pallas_call(
        functools.partial(_dispatch_kernel, n=n, tm=tm, bm=bm),
        grid_spec=pltpu.PrefetchScalarGridSpec(
            num_scalar_prefetch=3,
            grid=(n // tm,),
            in_specs=[pl.BlockSpec((tm, D_MODEL), lambda i, s, p, c: (i, 0))],
            out_specs=pl.BlockSpec(memory_space=pl.ANY),
            scratch_shapes=[pltpu.VMEM((bm, D_MODEL), F32), pltpu.SemaphoreType.DMA(()),
                            pltpu.SemaphoreType.DMA(())],
        ),
        out_shape=jax.ShapeDtypeStruct((nblk * bm, D_MODEL), F32),
        name="moe_dispatch",
    )(slot_flat, pend, counts, x1)


def _expert_kernel(be_ref, nu_ref, nxt_ref, xb_ref, wg_hbm, wu_hbm, wd_hbm, out_ref,
                   wg_bf, wu_bf, wd_bf, wg_f32, wu_f32, wd_f32, run_ref, sem):
    j = pl.program_id(0)
    used = j < nu_ref[0]
    expert = be_ref[j]
    changed = jnp.logical_or(j == 0, expert != be_ref[jnp.maximum(j - 1, 0)])

    def weight_copies(e, slot):
        return [pltpu.make_async_copy(src.at[e], dst.at[slot], sem.at[slot])
                for src, dst in ((wg_hbm, wg_f32), (wu_hbm, wu_f32), (wd_hbm, wd_f32))]

    @pl.when(j == 0)
    def _():
        run_ref[0] = 0
        for c in weight_copies(expert, 0):
            c.start()

    @pl.when(jnp.logical_and(used, changed))
    def _():
        slot = run_ref[0] % 2
        for c in weight_copies(expert, slot):
            c.wait()
        following = nxt_ref[expert]

        @pl.when(following >= 0)
        def _():
            for c in weight_copies(following, 1 - slot):
                c.start()
        wg_bf[...] = wg_f32[slot].astype(BF16)
        wu_bf[...] = wu_f32[slot].astype(BF16)
        wd_bf[...] = wd_f32[slot].astype(BF16)
        run_ref[0] = run_ref[0] + 1

    @pl.when(used)
    def _():
        xb = xb_ref[...].astype(BF16)
        a = jnp.dot(xb, wg_bf[...], preferred_element_type=F32)
        b = jnp.dot(xb, wu_bf[...], preferred_element_type=F32)
        h = (a * _sigmoid(a)) * b
        out_ref[...] = jnp.dot(h.astype(BF16), wd_bf[...], preferred_element_type=F32)

    @pl.when(jnp.logical_not(used))
    def _():
        out_ref[...] = jnp.zeros_like(out_ref)


def _experts(block_expert, n_used, next_expert, buf, w_g, w_u, w_d, *, bm):
    nblk = buf.shape[0] // bm

    def row_map(j, be, nu, nx):
        return (jnp.minimum(j, nu[0] - 1), 0)

    any_spec = pl.BlockSpec(memory_space=pl.ANY)
    return pl.pallas_call(
        _expert_kernel,
        grid_spec=pltpu.PrefetchScalarGridSpec(
            num_scalar_prefetch=3,
            grid=(nblk,),
            in_specs=[pl.BlockSpec((bm, D_MODEL), row_map), any_spec, any_spec, any_spec],
            out_specs=pl.BlockSpec((bm, D_MODEL), lambda j, be, nu, nx: (j, 0)),
            scratch_shapes=[pltpu.VMEM((D_MODEL, D_EXPERT), BF16),
                            pltpu.VMEM((D_MODEL, D_EXPERT), BF16),
                            pltpu.VMEM((D_EXPERT, D_MODEL), BF16),
                            pltpu.VMEM((2, D_MODEL, D_EXPERT), F32),
                            pltpu.VMEM((2, D_MODEL, D_EXPERT), F32),
                            pltpu.VMEM((2, D_EXPERT, D_MODEL), F32),
                            pltpu.SMEM((1,), I32),
                            pltpu.SemaphoreType.DMA((2,))],
        ),
        out_shape=jax.ShapeDtypeStruct((buf.shape[0], D_MODEL), F32),
        compiler_params=_params(),
        name="moe_experts",
    )(block_expert, n_used, next_expert, buf, w_g, w_u, w_d)


def _combine_kernel(slot_ref, x1_ref, gc_ref, g_ref, b_ref, eo_ref, y_ref, side_ref, rows, sem,
                    *, n, tm, alpha, main_tiles):
    i = pl.program_id(0)
    last = pl.num_programs(0) - 1
    cur = i % 2

    def start(tile, buf, r):
        for k in range(TOP_K):
            _row_copy(eo_ref, slot_ref[k * n + tile * tm + r], rows.at[buf, k], r, sem.at[buf]).start(priority=k)

    def wait(buf):
        for k in range(TOP_K):
            pltpu.make_async_copy(eo_ref.at[pl.ds(0, tm)], rows.at[buf, k], sem.at[buf]).wait()

    @pl.when(i == 0)
    def _():
        def body(r, c):
            start(0, 0, r)
            return c
        lax.fori_loop(0, tm, body, 0, unroll=ISSUE_UNROLL)

    wait(cur)
    nxt = jnp.minimum(i + 1, last)
    for r in range(tm):
        start(nxt, 1 - cur, r)
    gc = gc_ref[...]
    z = alpha * x1_ref[...] + gc[:, 0:1] * rows[cur, 0] + gc[:, 1:2] * rows[cur, 1]
    mu = jnp.mean(z, axis=-1, keepdims=True)
    zc = z - mu
    var = jnp.mean(zc * zc, axis=-1, keepdims=True)
    y = zc * lax.rsqrt(var + LN_EPS) * g_ref[...] + b_ref[...]

    @pl.when(i < main_tiles)
    def _():
        y_ref[...] = y

    @pl.when(i == main_tiles)
    def _():
        side_ref[...] = y[0:side_ref.shape[0]]

    @pl.when(i == last)
    def _():
        wait(1 - cur)


def _combine(slot_flat, x1, gate_cols, ln_g, ln_b, expert_out, *, tm, alpha, main_rows, side_rows):
    n = x1.shape[0]
    main_tiles = main_rows // tm
    return pl.pallas_call(
        functools.partial(_combine_kernel, n=n, tm=tm, alpha=alpha, main_tiles=main_tiles),
        grid_spec=pltpu.PrefetchScalarGridSpec(
            num_scalar_prefetch=1,
            grid=(n // tm,),
            in_specs=[pl.BlockSpec((tm, D_MODEL), lambda i, s: (i, 0)),
                      pl.BlockSpec((tm, TOP_K), lambda i, s: (i, 0)),
                      pl.BlockSpec((1, D_MODEL), lambda i, s: (0, 0)),
                      pl.BlockSpec((1, D_MODEL), lambda i, s: (0, 0)),
                      pl.BlockSpec(memory_space=pl.ANY)],
            out_specs=[pl.BlockSpec((tm, D_MODEL), lambda i, s: (jnp.minimum(i, main_tiles - 1), 0)),
                       pl.BlockSpec((side_rows, D_MODEL), lambda i, s: (0, 0))],
            scratch_shapes=[pltpu.VMEM((2, TOP_K, tm, D_MODEL), F32), pltpu.SemaphoreType.DMA((2,))],
        ),
        out_shape=[jax.ShapeDtypeStruct((main_rows, D_MODEL), F32),
                   jax.ShapeDtypeStruct((side_rows, D_MODEL), F32)],
        compiler_params=_params(),
        name="moe_combine",
    )(slot_flat, x1, gate_cols, ln_g, ln_b, expert_out)


def _hier_moe_ln(x1, ei, gt, w_g, w_u, w_d, ln_g, ln_b, *, tl, tm, bm, alpha, main_rows, side_rows):
    n = x1.shape[0]
    m = n * TOP_K
    slot, cnt = _slots(ei, tl=tl, bm=bm)
    counts = cnt[:, 0]
    pend = jnp.cumsum((counts + bm - 1) // bm * bm)
    nblk = (m + N_EXPERTS * (bm - 1) + bm - 1) // bm
    blk_start = jnp.arange(nblk, dtype=I32) * bm
    n_used = (pend[-1] // bm).astype(I32)
    be = jnp.minimum(jnp.sum(pend[None, :] <= blk_start[:, None], axis=1), N_EXPERTS - 1).astype(I32)
    be = jnp.where(jnp.arange(nblk) < n_used, be, jnp.take(be, n_used - 1))
    slot_flat = slot[0:TOP_K].reshape(m)
    buf = _dispatch(slot_flat, pend.astype(I32), counts, x1, tm=tm, bm=bm, nblk=nblk)
    ids = jnp.arange(N_EXPERTS, dtype=I32)
    first_at_or_after = lax.cummin(jnp.where(counts > 0, ids, N_EXPERTS), reverse=True)
    next_expert = jnp.concatenate([first_at_or_after[1:], jnp.full((1,), N_EXPERTS, I32)])
    next_expert = jnp.where(next_expert == N_EXPERTS, -1, next_expert).astype(I32)
    eo = _experts(be, n_used.reshape(1), next_expert, buf, w_g, w_u, w_d, bm=bm)
    gate_cols = gt[0:TOP_K].T
    return _combine(slot_flat, x1, gate_cols, ln_g, ln_b, eo, tm=tm, alpha=alpha,
                    main_rows=main_rows, side_rows=side_rows)


def _t5_bucket(n):
    nf = jnp.maximum(n, 1).astype(F32)
    large = MAX_EXACT + (jnp.log(nf / MAX_EXACT) / math.log(MAX_DISTANCE / MAX_EXACT)
                         * (N_BUCKETS - MAX_EXACT)).astype(I32)
    large = jnp.minimum(large, N_BUCKETS - 1)
    return jnp.where(n < MAX_EXACT, n, large)


def _bias_per_group(rel_bias):
    offs = jnp.arange(N_KEYS, dtype=I32)[None, :] * jnp.array(DILATIONS, I32)[:, None]
    bucket = _t5_bucket(offs)
    table = rel_bias.reshape(N_BUCKETS, N_GROUPS, HEADS)
    b = table[bucket, jnp.arange(N_GROUPS)[:, None]]
    return jnp.transpose(b, (0, 2, 1)).astype(F32)


def _prompt_bias_tables(bias):
    width = 3 * Q_BLOCK
    neg = jnp.full((N_GROUPS, HEADS, Q_BLOCK - 1), NEG_INF, F32)
    r = jnp.concatenate([neg, bias[:, :, ::-1], neg, jnp.full((N_GROUPS, HEADS, 1), NEG_INF, F32)], axis=-1)
    flat = jnp.tile(r, (1, 1, Q_BLOCK))[:, :, :Q_BLOCK * (width - 1)]
    skew = flat.reshape(N_GROUPS, HEADS, Q_BLOCK, width - 1)
    later = skew[:, :, :, Q_BLOCK - 1:3 * Q_BLOCK - 1]
    has_prev = (np.arange(2 * Q_BLOCK) >= Q_BLOCK)[None, None, None, :]
    first = jnp.where(has_prev, later, NEG_INF)
    tb = jnp.stack([first, later], axis=1)
    return tb.reshape(N_GROUPS, 2, HEADS // 2, 2 * Q_BLOCK, 2 * Q_BLOCK)


def _sample_bias_tables(bias):
    t = np.arange(T_NEW)
    bcs = []
    for g, d in enumerate(DILATIONS):
        rev = bias[g][:, ::-1][:, :WINDOW_KEYS]
        if d == 1:
            dist = np.arange(WINDOW_KEYS)[None, :] - t[:, None]
            vals = jnp.take(rev, np.clip(dist, 0, WINDOW_KEYS - 1), axis=1)
            bcs.append(jnp.where(dist[None] >= 0, vals, NEG_INF))
        else:
            cls = np.arange(d)[None, :] == t[:, None]
            table = jnp.where(cls[None, :, None, :], rev[:, None, :, None], NEG_INF)
            bcs.append(table.reshape(HEADS, T_NEW, WINDOW_KEYS * d))
    back = t[:, None] - t[None, :]
    vals = jnp.take(bias, np.clip(back, 0, T_NEW - 1), axis=2)
    ok = np.stack([(back >= 0) if d == 1 else (back == 0) for d in DILATIONS])
    return bcs, jnp.where(ok[:, None], vals, NEG_INF)


def _split_bf16(w):
    hi = w.astype(BF16)
    lo = (w - hi.astype(F32)).astype(BF16)
    return hi, lo


def kernel(x_prompt, x_sample, cache_attn_w128, cache_attn_w512, cache_attn_w2048, state_conv, rel_bias, w_in, w_conv, w_pa, w_pb, w_o, ln1_g, ln1_b, w_router_group, w_router_expert, w_expert_gate, w_expert_up, w_expert_down, ln2_g, ln2_b):
    depth = w_in.shape[0]
    assert depth == 1 and x_prompt.shape[0] == 1
    alpha = (2.0 * depth) ** 0.25
    s = x_prompt.shape[1]
    bd, t_len = x_sample.shape[0], x_sample.shape[1]
    assert t_len == T_NEW and s % (DILATIONS[-1] * Q_BLOCK * Q_BLOCKS_PER_STEP) == 0

    bias = _bias_per_group(rel_bias)
    tb = _prompt_bias_tables(bias)
    bcs, bn = _sample_bias_tables(bias)

    w_in_bf = w_in[0].astype(BF16)
    w_pa_bf = w_pa[0].astype(BF16)
    w_pb_bf = w_pb[0].astype(BF16)
    w_o_bf = w_o[0].astype(BF16)
    wr = jnp.zeros((ROUTER_ROWS, D_MODEL), F32)
    wr = wr.at[0:N_EXPERT_GROUPS].set(w_router_group[0].T).at[8:8 + N_EXPERTS].set(w_router_expert[0].T)
    wr_hi, wr_lo = _split_bf16(wr)
    g1, b1 = ln1_g[0][None], ln1_b[0][None]
    g2, b2 = ln2_g[0][None], ln2_b[0][None]
    wg, wu, wd = w_expert_gate[0], w_expert_up[0], w_expert_down[0]

    xp = x_prompt[0]
    kv_tail = min(MAX_DISTANCE, s)
    q, kb, vb, (k32, v32, yb, sga, sgb, ut) = _proj(
        xp, w_in_bf, w_conv[0], None, tm=256, u_tail=8, q_dtype=BF16, dils=DILATIONS, kv_f32=True)
    k32, v32 = k32[s - kv_tail:], v32[s - kv_tail:]
    o_l = [_attn_prompt_group(q[g], kb[g], vb[g], tb[g], g) for g in range(N_GROUPS)]
    mix_tm = 512
    routed = _mix(xp, [a[0] for a in o_l], [a[1] for a in o_l], yb, sga, sgb,
                  w_pa_bf, w_pb_bf, w_o_bf, g1, b1, wr_hi, wr_lo, tm=mix_tm, alpha=alpha, dils=DILATIONS,
                  extra_tiles=1)

    kv_prompt = []
    for g, d in enumerate(DILATIONS):
        length = min(WINDOW_KEYS * d, s)
        cols = slice(g * GROUP_WIDTH, (g + 1) * GROUP_WIDTH)
        kg = k32[kv_tail - length:, cols].reshape(length, HEADS, HEAD_DIM)
        vg = v32[kv_tail - length:, cols].reshape(length, HEADS, HEAD_DIM)
        kv_prompt.append(jnp.stack([kg, vg], axis=1)[None, None])
    conv_prompt = ut[6:8][None, None]

    ns = bd * t_len
    xs = x_sample.reshape(ns, D_MODEL)
    st = state_conv[0]
    s0 = jnp.repeat(st[:, 0], t_len, axis=0)
    s1 = jnp.repeat(st[:, 1], t_len, axis=0)
    qs, _, _, (k32s, v32s, ybs, sgas, sgbs, us) = _proj(
        xs, w_in_bf, w_conv[0], (s0, s1), tm=ns, u_tail=ns, q_dtype=F32, dils=NO_DILATION, kv_f32=True)
    qs = jnp.concatenate([a[0] for a in qs], axis=1)
    packed = jnp.stack([qs, k32s, v32s]).reshape(3, bd, t_len, N_GROUPS, GROUP_WIDTH)
    qkv_t = jnp.transpose(packed, (1, 4, 0, 3, 2)).reshape(bd, GROUP_WIDTH, 3 * N_GROUPS * t_len)
    qkv_t = jnp.pad(qkv_t, ((0, 0), (0, 0), (0, 128 - 3 * N_GROUPS * t_len)))
    caches = (cache_attn_w128[0], cache_attn_w512[0], cache_attn_w2048[0])
    caches_t = [jnp.transpose(c, (0, 2, 3, 4, 1)).reshape(bd, 2, GROUP_WIDTH, c.shape[1]) for c in caches]
    pair = (bd, t_len, N_GROUPS, HEADS // 2, 2, HEAD_DIM)
    q6 = jnp.transpose(qs.reshape(pair), (0, 3, 2, 4, 1, 5))
    zeros = jnp.zeros_like(q6[:, :, :, 0])
    qbd = jnp.stack([jnp.concatenate([q6[:, :, :, 0], zeros], axis=-1),
                     jnp.concatenate([zeros, q6[:, :, :, 1]], axis=-1)], axis=3)
    qbd = qbd.reshape(bd, HEADS // 2, N_GROUPS, 2 * t_len, 128)
    new_rows = jnp.stack([k32s, v32s]).reshape(2, bd, t_len, N_GROUPS, HEADS // 2, 128)
    new_rows = jnp.transpose(new_rows, (1, 4, 3, 0, 2, 5))
    bcs = [t.reshape(HEADS // 2, 2 * t_len, t.shape[-1]) for t in bcs]
    bn = bn.reshape(N_GROUPS, HEADS // 2, 2 * t_len, t_len)
    n0, n1, n2, o_s, lse_s = _sample_cache(qbd, new_rows, qkv_t, caches_t, bcs, bn)

    def unpack(a):
        a = a[:, :N_GROUPS * t_len].reshape(bd, N_GROUPS, t_len, GROUP_WIDTH)
        return jnp.transpose(a, (1, 0, 2, 3)).reshape(N_GROUPS, 1, ns, GROUP_WIDTH)

    o_s, lse_s = unpack(o_s), unpack(lse_s)
    assert ns <= mix_tm and s % ns == 0
    x1, ei, gt = _mix(xs, o_s, lse_s, ybs, sgas, sgbs, w_pa_bf, w_pb_bf, w_o_bf, g1, b1, wr_hi, wr_lo,
                      tm=ns, alpha=alpha, dils=NO_DILATION, into=routed, into_tile=s // ns)
    y_prompt, y_sample = _hier_moe_ln(x1, ei, gt, wg, wu, wd, g2, b2, tl=512, tm=512, bm=512, alpha=alpha,
                                      main_rows=s, side_rows=ns)
    y_prompt = y_prompt[None]
    y_sample = y_sample.reshape(bd, t_len, D_MODEL)

    kv_sample = [jnp.transpose(c.reshape(bd, 2, HEADS, HEAD_DIM, c.shape[-1]), (0, 4, 1, 2, 3))[None]
                 for c in (n0, n1, n2)]
    conv_sample = us.reshape(bd, t_len, CONV_CHANNELS)[:, t_len - 2:][None]

    return (y_prompt, y_sample, kv_prompt[0], kv_prompt[1], kv_prompt[2], conv_prompt,
            kv_sample[0], kv_sample[1], kv_sample[2], conv_sample)
```

```python
import functools
import math

import numpy as np
import jax
import jax.numpy as jnp
from jax import lax
from jax.experimental import pallas as pl
from jax.experimental.pallas import tpu as pltpu

F32 = jnp.float32
BF16 = jnp.bfloat16
I32 = jnp.int32

D_MODEL = 1024
N_GROUPS = 3
HEADS = 8
HEAD_DIM = 64
GROUP_WIDTH = HEADS * HEAD_DIM
ATTN_WIDTH = N_GROUPS * GROUP_WIDTH
DILATIONS = (1, 4, 16)
NO_DILATION = (1, 1, 1)
WINDOW_KEYS = 128
N_KEYS = WINDOW_KEYS + 1
N_BUCKETS = 32
MAX_EXACT = 16
MAX_DISTANCE = 2048
CONV_CHANNELS = 512
N_EXPERT_GROUPS = 4
EXPERTS_PER_GROUP = 8
N_EXPERTS = 32
TOP_K = 2
D_EXPERT = 512
LN_EPS = 1e-5
PROJ_WIDTH = 3 * ATTN_WIDTH + 3 * CONV_CHANNELS + 2 * D_MODEL
ROUTER_ROWS = 8 + N_EXPERTS
Q_BLOCK = 128
T_NEW = 4
NEG_INF = float("-inf")
VMEM_LIMIT = 56 * 1024 * 1024


def _sigmoid(x):
    return 1.0 / (1.0 + jnp.exp(-x))


def _params(limit=VMEM_LIMIT):
    return pltpu.CompilerParams(vmem_limit_bytes=limit)


def _proj_kernel(*refs, tm, tail_rows, sample_mode, dils, kv_f32):
    n_in = 5 if sample_mode else 3
    x_ref, w_ref, wc_ref = refs[0:3]
    outs = list(refs[n_in:])
    q_refs, k_refs, v_refs = outs[0:3], outs[3:6], outs[6:9]
    del outs[0:9]
    k32_ref, v32_ref = (outs.pop(0), outs.pop(0)) if kv_f32 else (None, None)
    yb_ref, sga_ref, sgb_ref, ut_ref, cls_ref = outs[0:5]
    xb = x_ref[...].astype(BF16)

    def col(c0, width):
        return jnp.dot(xb, w_ref[:, c0:c0 + width], preferred_element_type=F32)

    def write_classes(val, group_refs):
        for g, d in enumerate(dils):
            part = val[:, g * GROUP_WIDTH:(g + 1) * GROUP_WIDTH]
            ref = group_refs[g]
            if d == 1:
                ref[0] = part.astype(ref.dtype)
            else:
                for kk in range(GROUP_WIDTH // 128):
                    lanes = slice(kk * 128, (kk + 1) * 128)
                    cls_ref[kk] = part[:, lanes]
                    for c in range(d):
                        ref[c, :, lanes] = cls_ref[kk, pl.ds(c, tm // d, stride=d), :].astype(ref.dtype)

    write_classes(col(0, ATTN_WIDTH), q_refs)
    k = col(ATTN_WIDTH, ATTN_WIDTH)
    v = col(2 * ATTN_WIDTH, ATTN_WIDTH)
    if kv_f32:
        k32_ref[...] = k
        v32_ref[...] = v
    write_classes(k, k_refs)
    write_classes(v, v_refs)

    c0 = 3 * ATTN_WIDTH
    bg = col(c0, CONV_CHANNELS)
    u = col(c0 + CONV_CHANNELS, CONV_CHANNELS) * col(c0 + 2 * CONV_CHANNELS, CONV_CHANNELS)
    row = lax.broadcasted_iota(I32, (tm, CONV_CHANNELS), 0)
    r1 = pltpu.roll(u, 1, axis=0)
    r2 = pltpu.roll(u, 2, axis=0)
    if sample_mode:
        s0 = refs[3][...]
        s1 = refs[4][...]
        t = row & (T_NEW - 1)
        prev1 = jnp.where(t == 0, s1, r1)
        prev2 = jnp.where(t == 0, s0, jnp.where(t == 1, s1, r2))
    else:
        carry_ref = outs[5]

        @pl.when(pl.program_id(0) == 0)
        def _():
            carry_ref[...] = jnp.zeros_like(carry_ref)
        c6 = carry_ref[6:7, :]
        c7 = carry_ref[7:8, :]
        prev1 = jnp.where(row == 0, c7, r1)
        prev2 = jnp.where(row == 0, c6, jnp.where(row == 1, c7, r2))
        carry_ref[...] = u[tm - 8:tm, :]
    conv = prev2 * wc_ref[0:1, :] + prev1 * wc_ref[1:2, :] + u * wc_ref[2:3, :]
    yb_ref[...] = (bg * conv).astype(BF16)
    ut_ref[...] = u[tm - tail_rows:tm, :]

    c1 = c0 + 3 * CONV_CHANNELS
    sga_ref[...] = _sigmoid(col(c1, D_MODEL)).astype(BF16)
    sgb_ref[...] = _sigmoid(col(c1 + D_MODEL, D_MODEL)).astype(BF16)


def _proj(x, w_in_bf, w_conv, conv_prev, *, tm, u_tail, q_dtype, dils, kv_f32):
    n = x.shape[0]
    sample_mode = conv_prev is not None
    nt = n // tm

    def row_spec(width):
        return pl.BlockSpec((tm, width), lambda i: (i, 0))

    def class_spec(d):
        return pl.BlockSpec((d, tm // d, GROUP_WIDTH), lambda i: (0, i, 0))

    def class_shape(d, dtype):
        return jax.ShapeDtypeStruct((d, n // d, GROUP_WIDTH), dtype)

    in_specs = [
        row_spec(D_MODEL),
        pl.BlockSpec((D_MODEL, PROJ_WIDTH), lambda i: (0, 0), pipeline_mode=pl.Buffered(1)),
        pl.BlockSpec((3, CONV_CHANNELS), lambda i: (0, 0)),
    ]
    args = [x, w_in_bf, w_conv]
    scratch = [pltpu.VMEM((GROUP_WIDTH // 128, tm, 128), F32)]
    if sample_mode:
        in_specs += [row_spec(CONV_CHANNELS), row_spec(CONV_CHANNELS)]
        args += [conv_prev[0], conv_prev[1]]
    else:
        scratch.append(pltpu.VMEM((8, CONV_CHANNELS), F32))
    out_shape = (
        [class_shape(d, q_dtype) for d in dils] + [class_shape(d, BF16) for d in dils] * 2
        + [jax.ShapeDtypeStruct((n, ATTN_WIDTH), F32)] * (2 if kv_f32 else 0)
        + [jax.ShapeDtypeStruct((n, CONV_CHANNELS), BF16),
           jax.ShapeDtypeStruct((n, D_MODEL), BF16),
           jax.ShapeDtypeStruct((n, D_MODEL), BF16),
           jax.ShapeDtypeStruct((u_tail, CONV_CHANNELS), F32)])
    out_specs = (
        [class_spec(d) for d in dils] * 3
        + [row_spec(ATTN_WIDTH)] * (2 if kv_f32 else 0)
        + [row_spec(CONV_CHANNELS), row_spec(D_MODEL), row_spec(D_MODEL),
           pl.BlockSpec((u_tail, CONV_CHANNELS), lambda i: (0, 0))])
    res = pl.pallas_call(
        functools.partial(_proj_kernel, tm=tm, tail_rows=u_tail, sample_mode=sample_mode, dils=dils, kv_f32=kv_f32),
        grid=(nt,),
        in_specs=in_specs,
        out_specs=out_specs,
        out_shape=out_shape,
        scratch_shapes=scratch,
        compiler_params=_params(),
        name="proj",
    )(*args)
    return res[0:3], res[3:6], res[6:9], res[9:]


Q_BLOCKS_PER_STEP = 8


def _attn_kernel(q_ref, kp_ref, kc_ref, vp_ref, vc_ref, tb_ref, o_ref, lse_ref):
    lane = lax.broadcasted_iota(I32, (Q_BLOCK, 128), 1)
    first = lane < HEAD_DIM
    scale = HEAD_DIM ** -0.5
    has_prev = jnp.minimum(pl.program_id(1), 1)
    for sub in range(Q_BLOCKS_PER_STEP):
        rows = slice(sub * Q_BLOCK, (sub + 1) * Q_BLOCK)
        band = slice((sub - 1) * Q_BLOCK, (sub + 1) * Q_BLOCK)
        for pr in range(HEADS // 2):
            sl = slice(pr * 128, (pr + 1) * 128)
            if sub == 0:
                k = jnp.concatenate([kp_ref[:, sl], kc_ref[rows, sl]], axis=0)
                v = jnp.concatenate([vp_ref[:, sl], vc_ref[rows, sl]], axis=0)
                bias = tb_ref[has_prev, pr]
            else:
                k = kc_ref[band, sl]
                v = vc_ref[band, sl]
                bias = tb_ref[1, pr]
            qf = q_ref[rows, sl].astype(F32) * scale
            qq = jnp.concatenate([jnp.where(first, qf, 0.0), jnp.where(first, 0.0, qf)], axis=0).astype(BF16)
            s = lax.dot_general(qq, k, (((1,), (1,)), ((), ())), preferred_element_type=F32) + bias
            m = jnp.max(s, axis=-1, keepdims=True)
            p = jnp.exp(s - m)
            l = jnp.sum(p, axis=-1, keepdims=True)
            o = jnp.dot(p.astype(BF16), v, preferred_element_type=F32) / l
            lse = m + jnp.log(l)
            o_ref[rows, sl] = jnp.where(first, o[:Q_BLOCK], o[Q_BLOCK:]).astype(o_ref.dtype)
            lse_ref[rows, sl] = jnp.where(first, jnp.broadcast_to(lse[:Q_BLOCK], (Q_BLOCK, 128)),
                                          jnp.broadcast_to(lse[Q_BLOCK:], (Q_BLOCK, 128)))


def _attn_prompt_group(q, kb, vb, tb, g):
    d, rows = q.shape[0], q.shape[1]
    nq = Q_BLOCKS_PER_STEP
    cur = pl.BlockSpec((None, nq * Q_BLOCK, GROUP_WIDTH), lambda c, i: (c, i, 0))
    prev = pl.BlockSpec((None, Q_BLOCK, GROUP_WIDTH), lambda c, i: (c, jnp.maximum(i * nq - 1, 0), 0))
    return pl.pallas_call(
        _attn_kernel,
        grid=(d, rows // (nq * Q_BLOCK)),
        in_specs=[cur, prev, cur, prev, cur, pl.BlockSpec(tb.shape, lambda c, i: (0, 0, 0, 0))],
        out_specs=[cur, cur],
        out_shape=[jax.ShapeDtypeStruct((d, rows, GROUP_WIDTH), BF16),
                   jax.ShapeDtypeStruct((d, rows, GROUP_WIDTH), F32)],
        compiler_params=_params(),
        name=f"attn_prompt_g{g}",
    )(q, kb, kb, vb, vb, tb)


PACK_Q, PACK_K, PACK_V = 0, N_GROUPS * T_NEW, 2 * N_GROUPS * T_NEW


PAIRS_PER_STEP = 2


def _sample_cache_kernel(qbd_ref, nr_ref, qkv_ref, c0_ref, c1_ref, c2_ref, b0_ref, b1_ref, b2_ref, bn_ref,
                         n0_ref, n1_ref, n2_ref, o_ref, lse_ref):
    for pp in range(PAIRS_PER_STEP):
        rows = pl.ds(pp * 128, 128)
        kv = pl.ds(0, 2)
        _sample_pair(qbd_ref.at[pp], nr_ref.at[pp], qkv_ref.at[rows],
                     [c.at[kv, rows] for c in (c0_ref, c1_ref, c2_ref)],
                     [b.at[pp] for b in (b0_ref, b1_ref, b2_ref)], bn_ref.at[pl.ds(0, N_GROUPS), pp],
                     [c.at[kv, rows] for c in (n0_ref, n1_ref, n2_ref)],
                     o_ref.at[pl.ds(0, 16), rows], lse_ref.at[pl.ds(0, 16), rows])


def _sample_pair(qbd_ref, nr_ref, qkv_ref, c_refs, b_refs, bn_ref, n_refs, o_ref, lse_ref):
    scale = HEAD_DIM ** -0.5
    nt = (((1,), (1,)), ((), ()))
    lane = lax.broadcasted_iota(I32, (128, 128), 1)
    head0 = lane[0:T_NEW] < HEAD_DIM
    for g, (c_ref, b_ref, n_ref) in enumerate(zip(c_refs, b_refs, n_refs)):
        length = c_ref.shape[-1]
        k_new = qkv_ref[:, PACK_K + g * T_NEW:PACK_K + (g + 1) * T_NEW]
        v_new = qkv_ref[:, PACK_V + g * T_NEW:PACK_V + (g + 1) * T_NEW]
        qbd = qbd_ref[g]
        s_c = jnp.dot(qbd.astype(BF16), c_ref[0].astype(BF16), preferred_element_type=F32) * scale + b_ref[...]
        bn = bn_ref[g]
        s_n = [jnp.sum(qbd * nr_ref[g, 0, tn:tn + 1, :], axis=1, keepdims=True) * scale + bn[:, tn:tn + 1]
               for tn in range(T_NEW)]
        m = jnp.max(s_c, axis=1, keepdims=True)
        for x in s_n:
            m = jnp.maximum(m, x)
        p_c = jnp.exp(s_c - m)
        p_n = [jnp.exp(x - m) for x in s_n]
        l = jnp.sum(p_c, axis=1, keepdims=True)
        acc = lax.dot_general(p_c.astype(BF16), c_ref[1].astype(BF16), nt, preferred_element_type=F32)
        for tn in range(T_NEW):
            l = l + p_n[tn]
            acc = acc + p_n[tn] * nr_ref[g, 1, tn:tn + 1, :]
        o = acc / l
        lse = jnp.broadcast_to(m + jnp.log(l), (2 * T_NEW, 128))
        o_ref[g * T_NEW:(g + 1) * T_NEW, :] = jnp.where(head0, o[0:T_NEW], o[T_NEW:])
        lse_ref[g * T_NEW:(g + 1) * T_NEW, :] = jnp.where(head0, lse[0:T_NEW], lse[T_NEW:])

        for kv, new in ((0, k_new), (1, v_new)):
            rolled = pltpu.roll(c_ref[kv], length - T_NEW, axis=1)
            tail = rolled[:, length - 128:]
            for t in range(T_NEW):
                tail = jnp.where(lane == 128 - T_NEW + t, new[:, t:t + 1], tail)
            if length > 128:
                n_ref[kv, :, 0:length - 128] = rolled[:, 0:length - 128]
            n_ref[kv, :, length - 128:] = tail
    pad_rows = slice(N_GROUPS * T_NEW, 16)
    o_ref[pad_rows, :] = jnp.zeros((16 - N_GROUPS * T_NEW, 128), F32)
    lse_ref[pad_rows, :] = jnp.zeros((16 - N_GROUPS * T_NEW, 128), F32)


def _sample_cache(qbd, new_rows, qkv_t, caches_t, bcs, bn):
    b = qkv_t.shape[0]

    pp = PAIRS_PER_STEP

    def cache_spec(c):
        return pl.BlockSpec((None, 2, pp * 128, c.shape[-1]), lambda i, h: (i, 0, h, 0))

    def bias_spec(t):
        return pl.BlockSpec((pp, 2 * T_NEW, t.shape[-1]), lambda i, h: (h, 0, 0))

    out = pl.BlockSpec((None, 16, pp * 128), lambda i, h: (i, 0, h))
    return pl.pallas_call(
        _sample_cache_kernel,
        grid=(b, HEADS // 2 // pp),
        in_specs=[pl.BlockSpec((None, pp, N_GROUPS, 2 * T_NEW, 128), lambda i, h: (i, h, 0, 0, 0)),
                  pl.BlockSpec((None, pp, N_GROUPS, 2, T_NEW, 128), lambda i, h: (i, h, 0, 0, 0, 0)),
                  pl.BlockSpec((None, pp * 128, 128), lambda i, h: (i, h, 0))]
                 + [cache_spec(c) for c in caches_t] + [bias_spec(t) for t in bcs]
                 + [pl.BlockSpec((N_GROUPS, pp, 2 * T_NEW, T_NEW), lambda i, h: (0, h, 0, 0))],
        out_specs=[cache_spec(c) for c in caches_t] + [out, out],
        out_shape=[jax.ShapeDtypeStruct(c.shape, c.dtype) for c in caches_t]
                  + [jax.ShapeDtypeStruct((b, 16, GROUP_WIDTH), F32)] * 2,
        compiler_params=_params(),
        name="sample_cache",
    )(qbd, new_rows, qkv_t, *caches_t, *bcs, bn)


def _mix_kernel(*refs, tm, alpha, dils, n_alias):
    (x_ref, o0_ref, o1_ref, o2_ref, l0_ref, l1_ref, l2_ref, yb_ref, sga_ref, sgb_ref,
     wpa_ref, wpb_ref, wo_ref, g_ref, b_ref, wrh_ref, wrl_ref) = refs[0:17]
    x1_ref, ei_ref, gt_ref = refs[17 + n_alias:20 + n_alias]
    scratch = list(refs[20 + n_alias:])

    def natural(ref, d):
        if d == 1:
            return ref[0].astype(F32)
        scr = scratch.pop()
        for kk in range(GROUP_WIDTH // 128):
            for c in range(d):
                scr[kk, pl.ds(c, tm // d, stride=d), :] = ref[c, :, kk * 128:(kk + 1) * 128].astype(F32)
        return jnp.concatenate([scr[kk] for kk in range(GROUP_WIDTH // 128)], axis=1)

    l0, l1, l2 = natural(l0_ref, dils[0]), natural(l1_ref, dils[1]), natural(l2_ref, dils[2])
    mx = jnp.maximum(jnp.maximum(l0, l1), l2)
    e0 = jnp.exp(l0 - mx)
    e1 = jnp.exp(l1 - mx)
    e2 = jnp.exp(l2 - mx)
    ya = (e0 * natural(o0_ref, dils[0]) + e1 * natural(o1_ref, dils[1]) + e2 * natural(o2_ref, dils[2])) / (e0 + e1 + e2)
    pa = jnp.dot(ya.astype(BF16), wpa_ref[...], preferred_element_type=F32)
    pb = jnp.dot(yb_ref[...], wpb_ref[...], preferred_element_type=F32)
    gated = sga_ref[...].astype(F32) * pa + sgb_ref[...].astype(F32) * pb
    mix = jnp.dot(gated.astype(BF16), wo_ref[...], preferred_element_type=F32)
    z = alpha * x_ref[...] + mix
    mu = jnp.mean(z, axis=-1, keepdims=True)
    zc = z - mu
    var = jnp.mean(zc * zc, axis=-1, keepdims=True)
    x1 = zc * lax.rsqrt(var + LN_EPS) * g_ref[...] + b_ref[...]
    x1_ref[...] = x1

    xh = x1.astype(BF16)
    xl = (x1 - xh.astype(F32)).astype(BF16)
    nt = (((1,), (1,)), ((), ()))
    wrh = wrh_ref[...]
    lt = (lax.dot_general(wrh, xh, nt, preferred_element_type=F32)
          + lax.dot_general(wrh, xl, nt, preferred_element_type=F32)
          + lax.dot_general(wrl_ref[...], xh, nt, preferred_element_type=F32))

    gl = lt[0:N_EXPERT_GROUPS]
    gmax = jnp.max(gl, axis=0, keepdims=True)
    idx4 = lax.broadcasted_iota(I32, (N_EXPERT_GROUPS, tm), 0)
    g_idx = jnp.min(jnp.where(gl == gmax, idx4, N_EXPERT_GROUPS), axis=0, keepdims=True)
    g_prob = 1.0 / jnp.sum(jnp.exp(gl - gmax), axis=0, keepdims=True)
    e_sel = lt[8:16]
    for grp in range(1, N_EXPERT_GROUPS):
        e_sel = jnp.where(g_idx == grp, lt[8 + 8 * grp:16 + 8 * grp], e_sel)
    idx8 = lax.broadcasted_iota(I32, (EXPERTS_PER_GROUP, tm), 0)
    v1 = jnp.max(e_sel, axis=0, keepdims=True)
    i1 = jnp.min(jnp.where(e_sel == v1, idx8, EXPERTS_PER_GROUP), axis=0, keepdims=True)
    rest = jnp.where(idx8 == i1, NEG_INF, e_sel)
    v2 = jnp.max(rest, axis=0, keepdims=True)
    i2 = jnp.min(jnp.where(rest == v2, idx8, EXPERTS_PER_GROUP), axis=0, keepdims=True)
    r = jnp.exp(v2 - v1)
    gate1 = g_prob / (1.0 + r)
    gate2 = g_prob * r / (1.0 + r)
    ex1 = g_idx * EXPERTS_PER_GROUP + i1
    ex2 = g_idx * EXPERTS_PER_GROUP + i2
    ei_ref[...] = jnp.where(idx8 == 0, ex1, jnp.where(idx8 == 1, ex2, 0))
    gt_ref[...] = jnp.where(idx8 == 0, gate1, jnp.where(idx8 == 1, gate2, 0.0))


def _mix(x, o, lse, yb, sga, sgb, w_pa, w_pb, w_o, ln_g, ln_b, wr_hi, wr_lo, *, tm, alpha, dils,
         extra_tiles=0, into=None, into_tile=0):
    n = x.shape[0]
    nt = n // tm

    def src(i):
        return jnp.minimum(i, nt - 1)

    def row_spec(width):
        return pl.BlockSpec((tm, width), lambda i: (src(i), 0))

    def class_spec(d):
        return pl.BlockSpec((d, tm // d, GROUP_WIDTH), lambda i: (0, src(i), 0))

    def full(a):
        return pl.BlockSpec(a.shape, lambda i: (0,) * a.ndim)

    n_alias = 0 if into is None else 3
    rows_out = n + extra_tiles * tm if into is None else into[0].shape[0]
    any_spec = pl.BlockSpec(memory_space=pl.ANY)
    n_scratch = 2 * sum(1 for d in dils if d > 1)
    args = [x, o[0], o[1], o[2], lse[0], lse[1], lse[2], yb, sga, sgb, w_pa, w_pb, w_o, ln_g, ln_b, wr_hi, wr_lo]
    return pl.pallas_call(
        functools.partial(_mix_kernel, tm=tm, alpha=alpha, dils=dils, n_alias=n_alias),
        grid=(nt + extra_tiles,),
        in_specs=[row_spec(D_MODEL)] + [class_spec(d) for d in dils] * 2 + [row_spec(CONV_CHANNELS)]
                 + [row_spec(D_MODEL)] * 2
                 + [full(w_pa), full(w_pb), full(w_o), full(ln_g), full(ln_b), full(wr_hi), full(wr_lo)]
                 + [any_spec] * n_alias,
        out_specs=[pl.BlockSpec((tm, D_MODEL), lambda i: (i + into_tile, 0)),
                   pl.BlockSpec((8, tm), lambda i: (0, i + into_tile)),
                   pl.BlockSpec((8, tm), lambda i: (0, i + into_tile))],
        out_shape=[jax.ShapeDtypeStruct((rows_out, D_MODEL), F32),
                   jax.ShapeDtypeStruct((8, rows_out), I32),
                   jax.ShapeDtypeStruct((8, rows_out), F32)],
        input_output_aliases={len(args) + k: k for k in range(n_alias)},
        scratch_shapes=[pltpu.VMEM((GROUP_WIDTH // 128, tm, 128), F32)] * n_scratch,
        compiler_params=_params(),
        name="mix",
    )(*args, *(into or ()))


def _slot_kernel(ei_ref, slot_ref, cnt_ref, carry_ref, start_ref, *, tl, bm):
    phase = pl.program_id(0)
    i = pl.program_id(1)

    @pl.when(jnp.logical_and(phase == 0, i == 0))
    def _():
        carry_ref[...] = jnp.zeros_like(carry_ref)

    ex = lax.broadcasted_iota(I32, (N_EXPERTS, tl), 0)
    oh0 = (ex == ei_ref[0:1, :]).astype(F32)
    oh1 = (ex == ei_ref[1:2, :]).astype(F32)
    cnt0 = jnp.sum(oh0, axis=1, keepdims=True)
    cnt1 = jnp.sum(oh1, axis=1, keepdims=True)

    @pl.when(phase == 0)
    def _():
        total = carry_ref[...] + cnt0 + cnt1
        carry_ref[...] = total
        cnt_ref[...] = total.astype(I32)
        slot_ref[...] = jnp.zeros_like(slot_ref)

    @pl.when(jnp.logical_and(phase == 1, i == 0))
    def _():
        blocks = jnp.floor((carry_ref[...] + (bm - 1)) * (1.0 / bm))
        a = lax.broadcasted_iota(I32, (N_EXPERTS, N_EXPERTS), 0)
        b = lax.broadcasted_iota(I32, (N_EXPERTS, N_EXPERTS), 1)
        before = (b < a).astype(BF16)
        start_ref[...] = jnp.dot(before, blocks.astype(BF16), preferred_element_type=F32) * bm
        carry_ref[...] = jnp.zeros_like(carry_ref)

    @pl.when(phase == 1)
    def _():
        a = lax.broadcasted_iota(I32, (tl, tl), 0)
        b = lax.broadcasted_iota(I32, (tl, tl), 1)
        upper = (a < b).astype(BF16)
        pre0 = jnp.dot(oh0.astype(BF16), upper, preferred_element_type=F32)
        pre1 = jnp.dot(oh1.astype(BF16), upper, preferred_element_type=F32)
        base = carry_ref[:, 0:1] + start_ref[:, 0:1]
        slot0 = jnp.sum(oh0 * (pre0 + base), axis=0, keepdims=True)
        slot1 = jnp.sum(oh1 * (pre1 + cnt0 + base), axis=0, keepdims=True)
        row = lax.broadcasted_iota(I32, (8, tl), 0)
        slot_ref[...] = jnp.where(row == 0, slot0.astype(I32), jnp.where(row == 1, slot1.astype(I32), 0))
        carry_ref[...] = carry_ref[...] + cnt0 + cnt1


def _slots(ei, *, tl, bm):
    n = ei.shape[1]
    return pl.pallas_call(
        functools.partial(_slot_kernel, tl=tl, bm=bm),
        grid=(2, n // tl),
        in_specs=[pl.BlockSpec((8, tl), lambda p, i: (0, i))],
        out_specs=[pl.BlockSpec((8, tl), lambda p, i: (0, i * p)),
                   pl.BlockSpec((N_EXPERTS, 128), lambda p, i: (0, 0))],
        out_shape=[jax.ShapeDtypeStruct((8, n), I32), jax.ShapeDtypeStruct((N_EXPERTS, 128), I32)],
        scratch_shapes=[pltpu.VMEM((N_EXPERTS, 128), F32), pltpu.VMEM((N_EXPERTS, 128), F32)],
        name="moe_slots",
    )(ei)


def _row_copy(src, src_row, dst, dst_row, sem):
    return pltpu.make_async_copy(src.at[pl.ds(src_row, 1)], dst.at[pl.ds(dst_row, 1)], sem)


ISSUE_UNROLL = 8


def _dispatch_kernel(slot_ref, pend_ref, cnt_ref, x1_ref, buf_ref, zero_ref, sem, zsem, *, n, tm, bm):
    base = pl.program_id(0) * tm

    @pl.when(pl.program_id(0) == 0)
    def _():
        zero_ref[...] = jnp.zeros_like(zero_ref)

        def zero_copy(e):
            start = pl.multiple_of(pend_ref[e] - bm, bm)
            return pltpu.make_async_copy(zero_ref, buf_ref.at[pl.ds(start, bm)], zsem)

        def tail_copy(blk):
            return pltpu.make_async_copy(zero_ref, buf_ref.at[pl.ds(pl.multiple_of(blk * bm, bm), bm)], zsem)

        def tail_start(blk, c):
            tail_copy(blk).start()
            return c

        def tail_wait(blk, c):
            tail_copy(blk).wait()
            return c

        for e in range(N_EXPERTS):
            @pl.when(cnt_ref[e] > 0)
            def _(e=e):
                zero_copy(e).start()
        first_unused = pend_ref[N_EXPERTS - 1] // bm
        lax.fori_loop(first_unused, buf_ref.shape[0] // bm, tail_start, 0)
        for e in range(N_EXPERTS):
            @pl.when(cnt_ref[e] > 0)
            def _(e=e):
                zero_copy(e).wait()
        lax.fori_loop(first_unused, buf_ref.shape[0] // bm, tail_wait, 0)

    def body(r, carry):
        for k in range(TOP_K):
            _row_copy(x1_ref, r, buf_ref, slot_ref[k * n + base + r], sem).start()
        return carry

    lax.fori_loop(0, tm, body, 0, unroll=ISSUE_UNROLL)
    for _ in range(TOP_K):
        pltpu.make_async_copy(x1_ref, buf_ref.at[pl.ds(0, tm)], sem).wait()


def _dispatch(slot_flat, pend, counts, x1, *, tm, bm, nblk):
    n = x1.shape[0]
    return pl.pallas_call(
        functools.partial(_dispatch_kernel, n=n, tm=tm, bm=bm),
        grid_spec=pltpu.PrefetchScalarGridSpec(
            num_scalar_prefetch=3,
            grid=(n // tm,),
            in_specs=[pl.BlockSpec((tm, D_MODEL), lambda i, s, p, c: (i, 0))],
            out_specs=pl.BlockSpec(memory_space=pl.ANY),
            scratch_shapes=[pltpu.VMEM((bm, D_MODEL), F32), pltpu.SemaphoreType.DMA(()),
                            pltpu.SemaphoreType.DMA(())],
        ),
        out_shape=jax.ShapeDtypeStruct((nblk * bm, D_MODEL), F32),
        name="moe_dispatch",
    )(slot_flat, pend, counts, x1)


def _expert_kernel(be_ref, nu_ref, nxt_ref, xb_ref, wg_hbm, wu_hbm, wd_hbm, out_ref,
                   wg_bf, wu_bf, wd_bf, wg_f32, wu_f32, wd_f32, run_ref, sem):
    j = pl.program_id(0)
    used = j < nu_ref[0]
    expert = be_ref[j]
    changed = jnp.logical_or(j == 0, expert != be_ref[jnp.maximum(j - 1, 0)])

    def weight_copies(e, slot):
        return [pltpu.make_async_copy(src.at[e], dst.at[slot], sem.at[slot])
                for src, dst in ((wg_hbm, wg_f32), (wu_hbm, wu_f32), (wd_hbm, wd_f32))]

    @pl.when(j == 0)
    def _():
        run_ref[0] = 0
        for c in weight_copies(expert, 0):
            c.start()

    @pl.when(jnp.logical_and(used, changed))
    def _():
        slot = run_ref[0] % 2
        for c in weight_copies(expert, slot):
            c.wait()
        following = nxt_ref[expert]

        @pl.when(following >= 0)
        def _():
            for c in weight_copies(following, 1 - slot):
                c.start()
        wg_bf[...] = wg_f32[slot].astype(BF16)
        wu_bf[...] = wu_f32[slot].astype(BF16)
        wd_bf[...] = wd_f32[slot].astype(BF16)
        run_ref[0] = run_ref[0] + 1

    @pl.when(used)
    def _():
        xb = xb_ref[...].astype(BF16)
        a = jnp.dot(xb, wg_bf[...], preferred_element_type=F32)
        b = jnp.dot(xb, wu_bf[...], preferred_element_type=F32)
        h = (a * _sigmoid(a)) * b
        out_ref[...] = jnp.dot(h.astype(BF16), wd_bf[...], preferred_element_type=F32)

    @pl.when(jnp.logical_not(used))
    def _():
        out_ref[...] = jnp.zeros_like(out_ref)


def _experts(block_expert, n_used, next_expert, buf, w_g, w_u, w_d, *, bm):
    nblk = buf.shape[0] // bm

    def row_map(j, be, nu, nx):
        return (jnp.minimum(j, nu[0] - 1), 0)

    any_spec = pl.BlockSpec(memory_space=pl.ANY)
    return pl.pallas_call(
        _expert_kernel,
        grid_spec=pltpu.PrefetchScalarGridSpec(
            num_scalar_prefetch=3,
            grid=(nblk,),
            in_specs=[pl.BlockSpec((bm, D_MODEL), row_map), any_spec, any_spec, any_spec],
            out_specs=pl.BlockSpec((bm, D_MODEL), lambda j, be, nu, nx: (j, 0)),
            scratch_shapes=[pltpu.VMEM((D_MODEL, D_EXPERT), BF16),
                            pltpu.VMEM((D_MODEL, D_EXPERT), BF16),
                            pltpu.VMEM((D_EXPERT, D_MODEL), BF16),
                            pltpu.VMEM((2, D_MODEL, D_EXPERT), F32),
                            pltpu.VMEM((2, D_MODEL, D_EXPERT), F32),
                            pltpu.VMEM((2, D_EXPERT, D_MODEL), F32),
                            pltpu.SMEM((1,), I32),
                            pltpu.SemaphoreType.DMA((2,))],
        ),
        out_shape=jax.ShapeDtypeStruct((buf.shape[0], D_MODEL), F32),
        compiler_params=_params(),
        name="moe_experts",
    )(block_expert, n_used, next_expert, buf, w_g, w_u, w_d)


def _combine_kernel(slot_ref, x1_ref, gc_ref, g_ref, b_ref, eo_ref, y_ref, side_ref, rows, sem,
                    *, n, tm, alpha, main_tiles):
    i = pl.program_id(0)
    last = pl.num_programs(0) - 1
    cur = i % 2

    def start(tile, buf, r):
        for k in range(TOP_K):
            _row_copy(eo_ref, slot_ref[k * n + tile * tm + r], rows.at[buf, k], r, sem.at[buf]).start(priority=k)

    def wait(buf):
        for k in range(TOP_K):
            pltpu.make_async_copy(eo_ref.at[pl.ds(0, tm)], rows.at[buf, k], sem.at[buf]).wait()

    @pl.when(i == 0)
    def _():
        def body(r, c):
            start(0, 0, r)
            return c
        lax.fori_loop(0, tm, body, 0, unroll=ISSUE_UNROLL)

    wait(cur)
    nxt = jnp.minimum(i + 1, last)
    for r in range(tm):
        start(nxt, 1 - cur, r)
    gc = gc_ref[...]
    z = alpha * x1_ref[...] + gc[:, 0:1] * rows[cur, 0] + gc[:, 1:2] * rows[cur, 1]
    mu = jnp.mean(z, axis=-1, keepdims=True)
    zc = z - mu
    var = jnp.mean(zc * zc, axis=-1, keepdims=True)
    y = zc * lax.rsqrt(var + LN_EPS) * g_ref[...] + b_ref[...]

    @pl.when(i < main_tiles)
    def _():
        y_ref[...] = y

    @pl.when(i == main_tiles)
    def _():
        side_ref[...] = y[0:side_ref.shape[0]]

    @pl.when(i == last)
    def _():
        wait(1 - cur)


def _combine(slot_flat, x1, gate_cols, ln_g, ln_b, expert_out, *, tm, alpha, main_rows, side_rows):
    n = x1.shape[0]
    main_tiles = main_rows // tm
    return pl.pallas_call(
        functools.partial(_combine_kernel, n=n, tm=tm, alpha=alpha, main_tiles=main_tiles),
        grid_spec=pltpu.PrefetchScalarGridSpec(
            num_scalar_prefetch=1,
            grid=(n // tm,),
            in_specs=[pl.BlockSpec((tm, D_MODEL), lambda i, s: (i, 0)),
                      pl.BlockSpec((tm, TOP_K), lambda i, s: (i, 0)),
                      pl.BlockSpec((1, D_MODEL), lambda i, s: (0, 0)),
                      pl.BlockSpec((1, D_MODEL), lambda i, s: (0, 0)),
                      pl.BlockSpec(memory_space=pl.ANY)],
            out_specs=[pl.BlockSpec((tm, D_MODEL), lambda i, s: (jnp.minimum(i, main_tiles - 1), 0)),
                       pl.BlockSpec((side_rows, D_MODEL), lambda i, s: (0, 0))],
            scratch_shapes=[pltpu.VMEM((2, TOP_K, tm, D_MODEL), F32), pltpu.SemaphoreType.DMA((2,))],
        ),
        out_shape=[jax.ShapeDtypeStruct((main_rows, D_MODEL), F32),
                   jax.ShapeDtypeStruct((side_rows, D_MODEL), F32)],
        compiler_params=_params(),
        name="moe_combine",
    )(slot_flat, x1, gate_cols, ln_g, ln_b, expert_out)


def _hier_moe_ln(x1, ei, gt, w_g, w_u, w_d, ln_g, ln_b, *, tl, tm, bm, alpha, main_rows, side_rows):
    n = x1.shape[0]
    m = n * TOP_K
    slot, cnt = _slots(ei, tl=tl, bm=bm)
    counts = cnt[:, 0]
    pend = jnp.cumsum((counts + bm - 1) // bm * bm)
    nblk = (m + N_EXPERTS * (bm - 1) + bm - 1) // bm
    blk_start = jnp.arange(nblk, dtype=I32) * bm
    n_used = (pend[-1] // bm).astype(I32)
    be = jnp.minimum(jnp.sum(pend[None, :] <= blk_start[:, None], axis=1), N_EXPERTS - 1).astype(I32)
    be = jnp.where(jnp.arange(nblk) < n_used, be, jnp.take(be, n_used - 1))
    slot_flat = slot[0:TOP_K].reshape(m)
    buf = _dispatch(slot_flat, pend.astype(I32), counts, x1, tm=tm, bm=bm, nblk=nblk)
    ids = jnp.arange(N_EXPERTS, dtype=I32)
    first_at_or_after = lax.cummin(jnp.where(counts > 0, ids, N_EXPERTS), reverse=True)
    next_expert = jnp.concatenate([first_at_or_after[1:], jnp.full((1,), N_EXPERTS, I32)])
    next_expert = jnp.where(next_expert == N_EXPERTS, -1, next_expert).astype(I32)
    eo = _experts(be, n_used.reshape(1), next_expert, buf, w_g, w_u, w_d, bm=bm)
    gate_cols = gt[0:TOP_K].T
    return _combine(slot_flat, x1, gate_cols, ln_g, ln_b, eo, tm=tm, alpha=alpha,
                    main_rows=main_rows, side_rows=side_rows)


def _t5_bucket(n):
    nf = jnp.maximum(n, 1).astype(F32)
    large = MAX_EXACT + (jnp.log(nf / MAX_EXACT) / math.log(MAX_DISTANCE / MAX_EXACT)
                         * (N_BUCKETS - MAX_EXACT)).astype(I32)
    large = jnp.minimum(large, N_BUCKETS - 1)
    return jnp.where(n < MAX_EXACT, n, large)


def _bias_per_group(rel_bias):
    offs = jnp.arange(N_KEYS, dtype=I32)[None, :] * jnp.array(DILATIONS, I32)[:, None]
    bucket = _t5_bucket(offs)
    table = rel_bias.reshape(N_BUCKETS, N_GROUPS, HEADS)
    b = table[bucket, jnp.arange(N_GROUPS)[:, None]]
    return jnp.transpose(b, (0, 2, 1)).astype(F32)


def _prompt_bias_tables(bias):
    width = 3 * Q_BLOCK
    neg = jnp.full((N_GROUPS, HEADS, Q_BLOCK - 1), NEG_INF, F32)
    r = jnp.concatenate([neg, bias[:, :, ::-1], neg, jnp.full((N_GROUPS, HEADS, 1), NEG_INF, F32)], axis=-1)
    flat = jnp.tile(r, (1, 1, Q_BLOCK))[:, :, :Q_BLOCK * (width - 1)]
    skew = flat.reshape(N_GROUPS, HEADS, Q_BLOCK, width - 1)
    later = skew[:, :, :, Q_BLOCK - 1:3 * Q_BLOCK - 1]
    has_prev = (np.arange(2 * Q_BLOCK) >= Q_BLOCK)[None, None, None, :]
    first = jnp.where(has_prev, later, NEG_INF)
    tb = jnp.stack([first, later], axis=1)
    return tb.reshape(N_GROUPS, 2, HEADS // 2, 2 * Q_BLOCK, 2 * Q_BLOCK)


def _sample_bias_tables(bias):
    t = np.arange(T_NEW)
    bcs = []
    for g, d in enumerate(DILATIONS):
        rev = bias[g][:, ::-1][:, :WINDOW_KEYS]
        if d == 1:
            dist = np.arange(WINDOW_KEYS)[None, :] - t[:, None]
            vals = jnp.take(rev, np.clip(dist, 0, WINDOW_KEYS - 1), axis=1)
            bcs.append(jnp.where(dist[None] >= 0, vals, NEG_INF))
        else:
            cls = np.arange(d)[None, :] == t[:, None]
            table = jnp.where(cls[None, :, None, :], rev[:, None, :, None], NEG_INF)
            bcs.append(table.reshape(HEADS, T_NEW, WINDOW_KEYS * d))
    back = t[:, None] - t[None, :]
    vals = jnp.take(bias, np.clip(back, 0, T_NEW - 1), axis=2)
    ok = np.stack([(back >= 0) if d == 1 else (back == 0) for d in DILATIONS])
    return bcs, jnp.where(ok[:, None], vals, NEG_INF)


def _split_bf16(w):
    hi = w.astype(BF16)
    lo = (w - hi.astype(F32)).astype(BF16)
    return hi, lo


def kernel(x_prompt, x_sample, cache_attn_w128, cache_attn_w512, cache_attn_w2048, state_conv, rel_bias, w_in, w_conv, w_pa, w_pb, w_o, ln1_g, ln1_b, w_router_group, w_router_expert, w_expert_gate, w_expert_up, w_expert_down, ln2_g, ln2_b):
    depth = w_in.shape[0]
    assert depth == 1 and x_prompt.shape[0] == 1
    alpha = (2.0 * depth) ** 0.25
    s = x_prompt.shape[1]
    bd, t_len = x_sample.shape[0], x_sample.shape[1]
    assert t_len == T_NEW and s % (DILATIONS[-1] * Q_BLOCK * Q_BLOCKS_PER_STEP) == 0

    bias = _bias_per_group(rel_bias)
    tb = _prompt_bias_tables(bias)
    bcs, bn = _sample_bias_tables(bias)

    w_in_bf = w_in[0].astype(BF16)
    w_pa_bf = w_pa[0].astype(BF16)
    w_pb_bf = w_pb[0].astype(BF16)
    w_o_bf = w_o[0].astype(BF16)
    wr = jnp.zeros((ROUTER_ROWS, D_MODEL), F32)
    wr = wr.at[0:N_EXPERT_GROUPS].set(w_router_group[0].T).at[8:8 + N_EXPERTS].set(w_router_expert[0].T)
    wr_hi, wr_lo = _split_bf16(wr)
    g1, b1 = ln1_g[0][None], ln1_b[0][None]
    g2, b2 = ln2_g[0][None], ln2_b[0][None]
    wg, wu, wd = w_expert_gate[0], w_expert_up[0], w_expert_down[0]

    xp = x_prompt[0]
    kv_tail = min(MAX_DISTANCE, s)
    q, kb, vb, (k32, v32, yb, sga, sgb, ut) = _proj(
        xp, w_in_bf, w_conv[0], None, tm=256, u_tail=8, q_dtype=BF16, dils=DILATIONS, kv_f32=True)
    k32, v32 = k32[s - kv_tail:], v32[s - kv_tail:]
    o_l = [_attn_prompt_group(q[g], kb[g], vb[g], tb[g], g) for g in range(N_GROUPS)]
    mix_tm = 512
    routed = _mix(xp, [a[0] for a in o_l], [a[1] for a in o_l], yb, sga, sgb,
                  w_pa_bf, w_pb_bf, w_o_bf, g1, b1, wr_hi, wr_lo, tm=mix_tm, alpha=alpha, dils=DILATIONS,
                  extra_tiles=1)

    kv_prompt = []
    for g, d in enumerate(DILATIONS):
        length = min(WINDOW_KEYS * d, s)
        cols = slice(g * GROUP_WIDTH, (g + 1) * GROUP_WIDTH)
        kg = k32[kv_tail - length:, cols].reshape(length, HEADS, HEAD_DIM)
        vg = v32[kv_tail - length:, cols].reshape(length, HEADS, HEAD_DIM)
        kv_prompt.append(jnp.stack([kg, vg], axis=1)[None, None])
    conv_prompt = ut[6:8][None, None]

    ns = bd * t_len
    xs = x_sample.reshape(ns, D_MODEL)
    st = state_conv[0]
    s0 = jnp.repeat(st[:, 0], t_len, axis=0)
    s1 = jnp.repeat(st[:, 1], t_len, axis=0)
    qs, _, _, (k32s, v32s, ybs, sgas, sgbs, us) = _proj(
        xs, w_in_bf, w_conv[0], (s0, s1), tm=ns, u_tail=ns, q_dtype=F32, dils=NO_DILATION, kv_f32=True)
    qs = jnp.concatenate([a[0] for a in qs], axis=1)
    packed = jnp.stack([qs, k32s, v32s]).reshape(3, bd, t_len, N_GROUPS, GROUP_WIDTH)
    qkv_t = jnp.transpose(packed, (1, 4, 0, 3, 2)).reshape(bd, GROUP_WIDTH, 3 * N_GROUPS * t_len)
    qkv_t = jnp.pad(qkv_t, ((0, 0), (0, 0), (0, 128 - 3 * N_GROUPS * t_len)))
    caches = (cache_attn_w128[0], cache_attn_w512[0], cache_attn_w2048[0])
    caches_t = [jnp.transpose(c, (0, 2, 3, 4, 1)).reshape(bd, 2, GROUP_WIDTH, c.shape[1]) for c in caches]
    pair = (bd, t_len, N_GROUPS, HEADS // 2, 2, HEAD_DIM)
    q6 = jnp.transpose(qs.reshape(pair), (0, 3, 2, 4, 1, 5))
    zeros = jnp.zeros_like(q6[:, :, :, 0])
    qbd = jnp.stack([jnp.concatenate([q6[:, :, :, 0], zeros], axis=-1),
                     jnp.concatenate([zeros, q6[:, :, :, 1]], axis=-1)], axis=3)
    qbd = qbd.reshape(bd, HEADS // 2, N_GROUPS, 2 * t_len, 128)
    new_rows = jnp.stack([k32s, v32s]).reshape(2, bd, t_len, N_GROUPS, HEADS // 2, 128)
    new_rows = jnp.transpose(new_rows, (1, 4, 3, 0, 2, 5))
    bcs = [t.reshape(HEADS // 2, 2 * t_len, t.shape[-1]) for t in bcs]
    bn = bn.reshape(N_GROUPS, HEADS // 2, 2 * t_len, t_len)
    n0, n1, n2, o_s, lse_s = _sample_cache(qbd, new_rows, qkv_t, caches_t, bcs, bn)

    def unpack(a):
        a = a[:, :N_GROUPS * t_len].reshape(bd, N_GROUPS, t_len, GROUP_WIDTH)
        return jnp.transpose(a, (1, 0, 2, 3)).reshape(N_GROUPS, 1, ns, GROUP_WIDTH)

    o_s, lse_s = unpack(o_s), unpack(lse_s)
    assert ns <= mix_tm and s % ns == 0
    x1, ei, gt = _mix(xs, o_s, lse_s, ybs, sgas, sgbs, w_pa_bf, w_pb_bf, w_o_bf, g1, b1, wr_hi, wr_lo,
                      tm=ns, alpha=alpha, dils=NO_DILATION, into=routed, into_tile=s // ns)
    y_prompt, y_sample = _hier_moe_ln(x1, ei, gt, wg, wu, wd, g2, b2, tl=512, tm=512, bm=512, alpha=alpha,
                                      main_rows=s, side_rows=ns)
    y_prompt = y_prompt[None]
    y_sample = y_sample.reshape(bd, t_len, D_MODEL)

    kv_sample = [jnp.transpose(c.reshape(bd, 2, HEADS, HEAD_DIM, c.shape[-1]), (0, 4, 1, 2, 3))[None]
                 for c in (n0, n1, n2)]
    conv_sample = us.reshape(bd, t_len, CONV_CHANNELS)[:, t_len - 2:][None]

    return (y_prompt, y_sample, kv_prompt[0], kv_prompt[1], kv_prompt[2], conv_prompt,
            kv_sample[0], kv_sample[1], kv_sample[2], conv_sample)
```

```python
import functools
import math

import numpy as np
import jax
import jax.numpy as jnp
from jax import lax
from jax.experimental import pallas as pl
from jax.experimental.pallas import tpu as pltpu

F32 = jnp.float32
BF16 = jnp.bfloat16
I32 = jnp.int32

D_MODEL = 1024
N_GROUPS = 3
HEADS = 8
HEAD_DIM = 64
GROUP_WIDTH = HEADS * HEAD_DIM
ATTN_WIDTH = N_GROUPS * GROUP_WIDTH
DILATIONS = (1, 4, 16)
NO_DILATION = (1, 1, 1)
WINDOW_KEYS = 128
N_KEYS = WINDOW_KEYS + 1
N_BUCKETS = 32
MAX_EXACT = 16
MAX_DISTANCE = 2048
CONV_CHANNELS = 512
N_EXPERT_GROUPS = 4
EXPERTS_PER_GROUP = 8
N_EXPERTS = 32
TOP_K = 2
D_EXPERT = 512
LN_EPS = 1e-5
PROJ_WIDTH = 3 * ATTN_WIDTH + 3 * CONV_CHANNELS + 2 * D_MODEL
ROUTER_ROWS = 8 + N_EXPERTS
Q_BLOCK = 128
T_NEW = 4
NEG_INF = float("-inf")
VMEM_LIMIT = 56 * 1024 * 1024

PROJ_TILE = 256
MIX_TILE = 512
SLOT_TILE = 512
MOE_TILE = 512
EXPERT_BLOCK = 512


def _sigmoid(x):
    return 1.0 / (1.0 + jnp.exp(-x))


def _params(limit=VMEM_LIMIT):
    return pltpu.CompilerParams(vmem_limit_bytes=limit)


def _proj_kernel(*refs, tm, tail_rows, sample_mode, dils, kv_f32):
    n_in = 5 if sample_mode else 3
    x_ref, w_ref, wc_ref = refs[0:3]
    outs = list(refs[n_in:])
    q_refs, k_refs, v_refs = outs[0:3], outs[3:6], outs[6:9]
    del outs[0:9]
    k32_ref, v32_ref = (outs.pop(0), outs.pop(0)) if kv_f32 else (None, None)
    yb_ref, sga_ref, sgb_ref, ut_ref, cls_ref = outs[0:5]
    xb = x_ref[...].astype(BF16)

    def col(c0, width):
        return jnp.dot(xb, w_ref[:, c0:c0 + width], preferred_element_type=F32)

    def write_classes(val, group_refs):
        for g, d in enumerate(dils):
            part = val[:, g * GROUP_WIDTH:(g + 1) * GROUP_WIDTH]
            ref = group_refs[g]
            if d == 1:
                ref[0] = part.astype(ref.dtype)
            else:
                for kk in range(GROUP_WIDTH // 128):
                    lanes = slice(kk * 128, (kk + 1) * 128)
                    cls_ref[kk] = part[:, lanes]
                    for c in range(d):
                        ref[c, :, lanes] = cls_ref[kk, pl.ds(c, tm // d, stride=d), :].astype(ref.dtype)

    write_classes(col(0, ATTN_WIDTH), q_refs)
    k = col(ATTN_WIDTH, ATTN_WIDTH)
    v = col(2 * ATTN_WIDTH, ATTN_WIDTH)
    if kv_f32:
        k32_ref[...] = k
        v32_ref[...] = v
    write_classes(k, k_refs)
    write_classes(v, v_refs)

    c0 = 3 * ATTN_WIDTH
    bg = col(c0, CONV_CHANNELS)
    u = col(c0 + CONV_CHANNELS, CONV_CHANNELS) * col(c0 + 2 * CONV_CHANNELS, CONV_CHANNELS)
    row = lax.broadcasted_iota(I32, (tm, CONV_CHANNELS), 0)
    r1 = pltpu.roll(u, 1, axis=0)
    r2 = pltpu.roll(u, 2, axis=0)
    if sample_mode:
        s0 = refs[3][...]
        s1 = refs[4][...]
        t = row & (T_NEW - 1)
        prev1 = jnp.where(t == 0, s1, r1)
        prev2 = jnp.where(t == 0, s0, jnp.where(t == 1, s1, r2))
    else:
        carry_ref = outs[5]

        @pl.when(pl.program_id(0) == 0)
        def _():
            carry_ref[...] = jnp.zeros_like(carry_ref)
        c6 = carry_ref[6:7, :]
        c7 = carry_ref[7:8, :]
        prev1 = jnp.where(row == 0, c7, r1)
        prev2 = jnp.where(row == 0, c6, jnp.where(row == 1, c7, r2))
        carry_ref[...] = u[tm - 8:tm, :]
    conv = prev2 * wc_ref[0:1, :] + prev1 * wc_ref[1:2, :] + u * wc_ref[2:3, :]
    yb_ref[...] = (bg * conv).astype(BF16)
    ut_ref[...] = u[tm - tail_rows:tm, :]

    c1 = c0 + 3 * CONV_CHANNELS
    sga_ref[...] = _sigmoid(col(c1, D_MODEL)).astype(BF16)
    sgb_ref[...] = _sigmoid(col(c1 + D_MODEL, D_MODEL)).astype(BF16)


def _proj(x, w_in_bf, w_conv, conv_prev, *, tm, u_tail, q_dtype, dils, kv_f32):
    n = x.shape[0]
    sample_mode = conv_prev is not None
    nt = n // tm

    def row_spec(width):
        return pl.BlockSpec((tm, width), lambda i: (i, 0))

    def class_spec(d):
        return pl.BlockSpec((d, tm // d, GROUP_WIDTH), lambda i: (0, i, 0))

    def class_shape(d, dtype):
        return jax.ShapeDtypeStruct((d, n // d, GROUP_WIDTH), dtype)

    in_specs = [
        row_spec(D_MODEL),
        pl.BlockSpec((D_MODEL, PROJ_WIDTH), lambda i: (0, 0), pipeline_mode=pl.Buffered(1)),
        pl.BlockSpec((3, CONV_CHANNELS), lambda i: (0, 0)),
    ]
    args = [x, w_in_bf, w_conv]
    scratch = [pltpu.VMEM((GROUP_WIDTH // 128, tm, 128), F32)]
    if sample_mode:
        in_specs += [row_spec(CONV_CHANNELS), row_spec(CONV_CHANNELS)]
        args += [conv_prev[0], conv_prev[1]]
    else:
        scratch.append(pltpu.VMEM((8, CONV_CHANNELS), F32))
    out_shape = (
        [class_shape(d, q_dtype) for d in dils] + [class_shape(d, BF16) for d in dils] * 2
        + [jax.ShapeDtypeStruct((n, ATTN_WIDTH), F32)] * (2 if kv_f32 else 0)
        + [jax.ShapeDtypeStruct((n, CONV_CHANNELS), BF16),
           jax.ShapeDtypeStruct((n, D_MODEL), BF16),
           jax.ShapeDtypeStruct((n, D_MODEL), BF16),
           jax.ShapeDtypeStruct((u_tail, CONV_CHANNELS), F32)])
    out_specs = (
        [class_spec(d) for d in dils] * 3
        + [row_spec(ATTN_WIDTH)] * (2 if kv_f32 else 0)
        + [row_spec(CONV_CHANNELS), row_spec(D_MODEL), row_spec(D_MODEL),
           pl.BlockSpec((u_tail, CONV_CHANNELS), lambda i: (0, 0))])
    res = pl.pallas_call(
        functools.partial(_proj_kernel, tm=tm, tail_rows=u_tail, sample_mode=sample_mode, dils=dils, kv_f32=kv_f32),
        grid=(nt,),
        in_specs=in_specs,
        out_specs=out_specs,
        out_shape=out_shape,
        scratch_shapes=scratch,
        compiler_params=_params(),
        name="proj",
    )(*args)
    return res[0:3], res[3:6], res[6:9], res[9:]


Q_BLOCKS_PER_STEP = 8


def _attn_kernel(q_ref, kp_ref, kc_ref, vp_ref, vc_ref, tb_ref, o_ref, lse_ref):
    lane = lax.broadcasted_iota(I32, (Q_BLOCK, 128), 1)
    first = lane < HEAD_DIM
    scale = HEAD_DIM ** -0.5
    has_prev = jnp.minimum(pl.program_id(1), 1)
    for sub in range(Q_BLOCKS_PER_STEP):
        rows = slice(sub * Q_BLOCK, (sub + 1) * Q_BLOCK)
        band = slice((sub - 1) * Q_BLOCK, (sub + 1) * Q_BLOCK)
        for pr in range(HEADS // 2):
            sl = slice(pr * 128, (pr + 1) * 128)
            if sub == 0:
                k = jnp.concatenate([kp_ref[:, sl], kc_ref[rows, sl]], axis=0)
                v = jnp.concatenate([vp_ref[:, sl], vc_ref[rows, sl]], axis=0)
                bias = tb_ref[has_prev, pr]
            else:
                k = kc_ref[band, sl]
                v = vc_ref[band, sl]
                bias = tb_ref[1, pr]
            qf = q_ref[rows, sl].astype(F32) * scale
            qq = jnp.concatenate([jnp.where(first, qf, 0.0), jnp.where(first, 0.0, qf)], axis=0).astype(BF16)
            s = lax.dot_general(qq, k, (((1,), (1,)), ((), ())), preferred_element_type=F32) + bias
            m = jnp.max(s, axis=-1, keepdims=True)
            p = jnp.exp(s - m)
            l = jnp.sum(p, axis=-1, keepdims=True)
            o = jnp.dot(p.astype(BF16), v, preferred_element_type=F32) / l
            lse = m + jnp.log(l)
            o_ref[rows, sl] = jnp.where(first, o[:Q_BLOCK], o[Q_BLOCK:]).astype(o_ref.dtype)
            lse_ref[rows, sl] = jnp.where(first, jnp.broadcast_to(lse[:Q_BLOCK], (Q_BLOCK, 128)),
                                          jnp.broadcast_to(lse[Q_BLOCK:], (Q_BLOCK, 128)))


def _attn_prompt_group(q, kb, vb, tb, g):
    d, rows = q.shape[0], q.shape[1]
    nq = Q_BLOCKS_PER_STEP
    cur = pl.BlockSpec((None, nq * Q_BLOCK, GROUP_WIDTH), lambda c, i: (c, i, 0))
    prev = pl.BlockSpec((None, Q_BLOCK, GROUP_WIDTH), lambda c, i: (c, jnp.maximum(i * nq - 1, 0), 0))
    return pl.pallas_call(
        _attn_kernel,
        grid=(d, rows // (nq * Q_BLOCK)),
        in_specs=[cur, prev, cur, prev, cur, pl.BlockSpec(tb.shape, lambda c, i: (0, 0, 0, 0))],
        out_specs=[cur, cur],
        out_shape=[jax.ShapeDtypeStruct((d, rows, GROUP_WIDTH), BF16),
                   jax.ShapeDtypeStruct((d, rows, GROUP_WIDTH), F32)],
        compiler_params=_params(),
        name=f"attn_prompt_g{g}",
    )(q, kb, kb, vb, vb, tb)


PACK_Q, PACK_K, PACK_V = 0, N_GROUPS * T_NEW, 2 * N_GROUPS * T_NEW


PAIRS_PER_STEP = 2


def _sample_cache_kernel(qbd_ref, nr_ref, qkv_ref, c0_ref, c1_ref, c2_ref, b0_ref, b1_ref, b2_ref, bn_ref,
                         n0_ref, n1_ref, n2_ref, o_ref, lse_ref):
    for pp in range(PAIRS_PER_STEP):
        rows = pl.ds(pp * 128, 128)
        kv = pl.ds(0, 2)
        _sample_pair(qbd_ref.at[pp], nr_ref.at[pp], qkv_ref.at[rows],
                     [c.at[kv, rows] for c in (c0_ref, c1_ref, c2_ref)],
                     [b.at[pp] for b in (b0_ref, b1_ref, b2_ref)], bn_ref.at[pl.ds(0, N_GROUPS), pp],
                     [c.at[kv, rows] for c in (n0_ref, n1_ref, n2_ref)],
                     o_ref.at[pl.ds(0, 16), rows], lse_ref.at[pl.ds(0, 16), rows])


def _sample_pair(qbd_ref, nr_ref, qkv_ref, c_refs, b_refs, bn_ref, n_refs, o_ref, lse_ref):
    scale = HEAD_DIM ** -0.5
    nt = (((1,), (1,)), ((), ()))
    lane = lax.broadcasted_iota(I32, (128, 128), 1)
    head0 = lane[0:T_NEW] < HEAD_DIM
    for g, (c_ref, b_ref, n_ref) in enumerate(zip(c_refs, b_refs, n_refs)):
        length = c_ref.shape[-1]
        k_new = qkv_ref[:, PACK_K + g * T_NEW:PACK_K + (g + 1) * T_NEW]
        v_new = qkv_ref[:, PACK_V + g * T_NEW:PACK_V + (g + 1) * T_NEW]
        qbd = qbd_ref[g]
        s_c = jnp.dot(qbd.astype(BF16), c_ref[0].astype(BF16), preferred_element_type=F32) * scale + b_ref[...]
        bn = bn_ref[g]
        s_n = [jnp.sum(qbd * nr_ref[g, 0, tn:tn + 1, :], axis=1, keepdims=True) * scale + bn[:, tn:tn + 1]
               for tn in range(T_NEW)]
        m = jnp.max(s_c, axis=1, keepdims=True)
        for x in s_n:
            m = jnp.maximum(m, x)
        p_c = jnp.exp(s_c - m)
        p_n = [jnp.exp(x - m) for x in s_n]
        l = jnp.sum(p_c, axis=1, keepdims=True)
        acc = lax.dot_general(p_c.astype(BF16), c_ref[1].astype(BF16), nt, preferred_element_type=F32)
        for tn in range(T_NEW):
            l = l + p_n[tn]
            acc = acc + p_n[tn] * nr_ref[g, 1, tn:tn + 1, :]
        o = acc / l
        lse = jnp.broadcast_to(m + jnp.log(l), (2 * T_NEW, 128))
        o_ref[g * T_NEW:(g + 1) * T_NEW, :] = jnp.where(head0, o[0:T_NEW], o[T_NEW:])
        lse_ref[g * T_NEW:(g + 1) * T_NEW, :] = jnp.where(head0, lse[0:T_NEW], lse[T_NEW:])

        for kv, new in ((0, k_new), (1, v_new)):
            rolled = pltpu.roll(c_ref[kv], length - T_NEW, axis=1)
            tail = rolled[:, length - 128:]
            for t in range(T_NEW):
                tail = jnp.where(lane == 128 - T_NEW + t, new[:, t:t + 1], tail)
            if length > 128:
                n_ref[kv, :, 0:length - 128] = rolled[:, 0:length - 128]
            n_ref[kv, :, length - 128:] = tail
    pad_rows = slice(N_GROUPS * T_NEW, 16)
    o_ref[pad_rows, :] = jnp.zeros((16 - N_GROUPS * T_NEW, 128), F32)
    lse_ref[pad_rows, :] = jnp.zeros((16 - N_GROUPS * T_NEW, 128), F32)


def _sample_cache(qbd, new_rows, qkv_t, caches_t, bcs, bn):
    b = qkv_t.shape[0]

    pp = PAIRS_PER_STEP

    def cache_spec(c):
        return pl.BlockSpec((None, 2, pp * 128, c.shape[-1]), lambda i, h: (i, 0, h, 0))

    def bias_spec(t):
        return pl.BlockSpec((pp, 2 * T_NEW, t.shape[-1]), lambda i, h: (h, 0, 0))

    out = pl.BlockSpec((None, 16, pp * 128), lambda i, h: (i, 0, h))
    return pl.pallas_call(
        _sample_cache_kernel,
        grid=(b, HEADS // 2 // pp),
        in_specs=[pl.BlockSpec((None, pp, N_GROUPS, 2 * T_NEW, 128), lambda i, h: (i, h, 0, 0, 0)),
                  pl.BlockSpec((None, pp, N_GROUPS, 2, T_NEW, 128), lambda i, h: (i, h, 0, 0, 0, 0)),
                  pl.BlockSpec((None, pp * 128, 128), lambda i, h: (i, h, 0))]
                 + [cache_spec(c) for c in caches_t] + [bias_spec(t) for t in bcs]
                 + [pl.BlockSpec((N_GROUPS, pp, 2 * T_NEW, T_NEW), lambda i, h: (0, h, 0, 0))],
        out_specs=[cache_spec(c) for c in caches_t] + [out, out],
        out_shape=[jax.ShapeDtypeStruct(c.shape, c.dtype) for c in caches_t]
                  + [jax.ShapeDtypeStruct((b, 16, GROUP_WIDTH), F32)] * 2,
        compiler_params=_params(),
        name="sample_cache",
    )(qbd, new_rows, qkv_t, *caches_t, *bcs, bn)


def _mix_kernel(*refs, tm, alpha, dils, n_alias):
    (x_ref, o0_ref, o1_ref, o2_ref, l0_ref, l1_ref, l2_ref, yb_ref, sga_ref, sgb_ref,
     wpa_ref, wpb_ref, wo_ref, g_ref, b_ref, wrh_ref, wrl_ref) = refs[0:17]
    x1_ref, ei_ref, gt_ref = refs[17 + n_alias:20 + n_alias]
    scratch = list(refs[20 + n_alias:])

    def natural(ref, d):
        if d == 1:
            return ref[0].astype(F32)
        scr = scratch.pop()
        for kk in range(GROUP_WIDTH // 128):
            for c in range(d):
                scr[kk, pl.ds(c, tm // d, stride=d), :] = ref[c, :, kk * 128:(kk + 1) * 128].astype(F32)
        return jnp.concatenate([scr[kk] for kk in range(GROUP_WIDTH // 128)], axis=1)

    l0, l1, l2 = natural(l0_ref, dils[0]), natural(l1_ref, dils[1]), natural(l2_ref, dils[2])
    mx = jnp.maximum(jnp.maximum(l0, l1), l2)
    e0 = jnp.exp(l0 - mx)
    e1 = jnp.exp(l1 - mx)
    e2 = jnp.exp(l2 - mx)
    ya = (e0 * natural(o0_ref, dils[0]) + e1 * natural(o1_ref, dils[1]) + e2 * natural(o2_ref, dils[2])) / (e0 + e1 + e2)
    pa = jnp.dot(ya.astype(BF16), wpa_ref[...], preferred_element_type=F32)
    pb = jnp.dot(yb_ref[...], wpb_ref[...], preferred_element_type=F32)
    gated = sga_ref[...].astype(F32) * pa + sgb_ref[...].astype(F32) * pb
    mix = jnp.dot(gated.astype(BF16), wo_ref[...], preferred_element_type=F32)
    z = alpha * x_ref[...] + mix
    mu = jnp.mean(z, axis=-1, keepdims=True)
    zc = z - mu
    var = jnp.mean(zc * zc, axis=-1, keepdims=True)
    x1 = zc * lax.rsqrt(var + LN_EPS) * g_ref[...] + b_ref[...]
    x1_ref[...] = x1

    xh = x1.astype(BF16)
    xl = (x1 - xh.astype(F32)).astype(BF16)
    nt = (((1,), (1,)), ((), ()))
    wrh = wrh_ref[...]
    lt = (lax.dot_general(wrh, xh, nt, preferred_element_type=F32)
          + lax.dot_general(wrh, xl, nt, preferred_element_type=F32)
          + lax.dot_general(wrl_ref[...], xh, nt, preferred_element_type=F32))

    gl = lt[0:N_EXPERT_GROUPS]
    gmax = jnp.max(gl, axis=0, keepdims=True)
    idx4 = lax.broadcasted_iota(I32, (N_EXPERT_GROUPS, tm), 0)
    g_idx = jnp.min(jnp.where(gl == gmax, idx4, N_EXPERT_GROUPS), axis=0, keepdims=True)
    g_prob = 1.0 / jnp.sum(jnp.exp(gl - gmax), axis=0, keepdims=True)
    e_sel = lt[8:16]
    for grp in range(1, N_EXPERT_GROUPS):
        e_sel = jnp.where(g_idx == grp, lt[8 + 8 * grp:16 + 8 * grp], e_sel)
    idx8 = lax.broadcasted_iota(I32, (EXPERTS_PER_GROUP, tm), 0)
    v1 = jnp.max(e_sel, axis=0, keepdims=True)
    i1 = jnp.min(jnp.where(e_sel == v1, idx8, EXPERTS_PER_GROUP), axis=0, keepdims=True)
    rest = jnp.where(idx8 == i1, NEG_INF, e_sel)
    v2 = jnp.max(rest, axis=0, keepdims=True)
    i2 = jnp.min(jnp.where(rest == v2, idx8, EXPERTS_PER_GROUP), axis=0, keepdims=True)
    r = jnp.exp(v2 - v1)
    gate1 = g_prob / (1.0 + r)
    gate2 = g_prob * r / (1.0 + r)
    ex1 = g_idx * EXPERTS_PER_GROUP + i1
    ex2 = g_idx * EXPERTS_PER_GROUP + i2
    ei_ref[...] = jnp.where(idx8 == 0, ex1, jnp.where(idx8 == 1, ex2, 0))
    gt_ref[...] = jnp.where(idx8 == 0, gate1, jnp.where(idx8 == 1, gate2, 0.0))


def _mix(x, o, lse, yb, sga, sgb, w_pa, w_pb, w_o, ln_g, ln_b, wr_hi, wr_lo, *, tm, alpha, dils,
         extra_tiles=0, into=None, into_tile=0):
    n = x.shape[0]
    nt = n // tm

    def src(i):
        return jnp.minimum(i, nt - 1)

    def row_spec(width):
        return pl.BlockSpec((tm, width), lambda i: (src(i), 0))

    def class_spec(d):
        return pl.BlockSpec((d, tm // d, GROUP_WIDTH), lambda i: (0, src(i), 0))

    def full(a):
        return pl.BlockSpec(a.shape, lambda i: (0,) * a.ndim)

    n_alias = 0 if into is None else 3
    rows_out = n + extra_tiles * tm if into is None else into[0].shape[0]
    any_spec = pl.BlockSpec(memory_space=pl.ANY)
    n_scratch = 2 * sum(1 for d in dils if d > 1)
    args = [x, o[0], o[1], o[2], lse[0], lse[1], lse[2], yb, sga, sgb, w_pa, w_pb, w_o, ln_g, ln_b, wr_hi, wr_lo]
    return pl.pallas_call(
        functools.partial(_mix_kernel, tm=tm, alpha=alpha, dils=dils, n_alias=n_alias),
        grid=(nt + extra_tiles,),
        in_specs=[row_spec(D_MODEL)] + [class_spec(d) for d in dils] * 2 + [row_spec(CONV_CHANNELS)]
                 + [row_spec(D_MODEL)] * 2
                 + [full(w_pa), full(w_pb), full(w_o), full(ln_g), full(ln_b), full(wr_hi), full(wr_lo)]
                 + [any_spec] * n_alias,
        out_specs=[pl.BlockSpec((tm, D_MODEL), lambda i: (i + into_tile, 0)),
                   pl.BlockSpec((8, tm), lambda i: (0, i + into_tile)),
                   pl.BlockSpec((8, tm), lambda i: (0, i + into_tile))],
        out_shape=[jax.ShapeDtypeStruct((rows_out, D_MODEL), F32),
                   jax.ShapeDtypeStruct((8, rows_out), I32),
                   jax.ShapeDtypeStruct((8, rows_out), F32)],
        input_output_aliases={len(args) + k: k for k in range(n_alias)},
        scratch_shapes=[pltpu.VMEM((GROUP_WIDTH // 128, tm, 128), F32)] * n_scratch,
        compiler_params=_params(),
        name="mix",
    )(*args, *(into or ()))


def _slot_kernel(all_ref, ei_ref, slot_ref, cnt_ref, carry_ref, start_ref, *, tl, bm):
    step = pl.program_id(0)

    def one_hot(rows):
        ex = lax.broadcasted_iota(I32, (N_EXPERTS, rows.shape[1]), 0)
        return (ex == rows).astype(F32)

    @pl.when(step == 0)
    def _():
        total = (jnp.sum(one_hot(all_ref[0:1, :]), axis=1, keepdims=True)
                 + jnp.sum(one_hot(all_ref[1:2, :]), axis=1, keepdims=True))
        total = jnp.broadcast_to(total, (N_EXPERTS, 128))
        cnt_ref[...] = total.astype(I32)
        blocks = jnp.floor((total + (bm - 1)) * (1.0 / bm))
        a = lax.broadcasted_iota(I32, (N_EXPERTS, N_EXPERTS), 0)
        b = lax.broadcasted_iota(I32, (N_EXPERTS, N_EXPERTS), 1)
        before = (b < a).astype(BF16)
        start_ref[...] = jnp.dot(before, blocks.astype(BF16), preferred_element_type=F32) * bm
        carry_ref[...] = jnp.zeros_like(carry_ref)
        slot_ref[...] = jnp.zeros_like(slot_ref)

    @pl.when(step > 0)
    def _():
        oh0 = one_hot(ei_ref[0:1, :])
        oh1 = one_hot(ei_ref[1:2, :])
        cnt0 = jnp.sum(oh0, axis=1, keepdims=True)
        cnt1 = jnp.sum(oh1, axis=1, keepdims=True)
        a = lax.broadcasted_iota(I32, (tl, tl), 0)
        b = lax.broadcasted_iota(I32, (tl, tl), 1)
        upper = (a < b).astype(BF16)
        pre0 = jnp.dot(oh0.astype(BF16), upper, preferred_element_type=F32)
        pre1 = jnp.dot(oh1.astype(BF16), upper, preferred_element_type=F32)
        base = carry_ref[:, 0:1] + start_ref[:, 0:1]
        slot0 = jnp.sum(oh0 * (pre0 + base), axis=0, keepdims=True)
        slot1 = jnp.sum(oh1 * (pre1 + cnt0 + base), axis=0, keepdims=True)
        row = lax.broadcasted_iota(I32, (8, tl), 0)
        slot_ref[...] = jnp.where(row == 0, slot0.astype(I32), jnp.where(row == 1, slot1.astype(I32), 0))
        carry_ref[...] = carry_ref[...] + cnt0 + cnt1


def _slots(ei, *, tl, bm):
    n = ei.shape[1]
    return pl.pallas_call(
        functools.partial(_slot_kernel, tl=tl, bm=bm),
        grid=(n // tl + 1,),
        in_specs=[pl.BlockSpec((8, n), lambda i: (0, 0)),
                  pl.BlockSpec((8, tl), lambda i: (0, jnp.maximum(i - 1, 0)))],
        out_specs=[pl.BlockSpec((8, tl), lambda i: (0, jnp.maximum(i - 1, 0))),
                   pl.BlockSpec((N_EXPERTS, 128), lambda i: (0, 0))],
        out_shape=[jax.ShapeDtypeStruct((8, n), I32), jax.ShapeDtypeStruct((N_EXPERTS, 128), I32)],
        scratch_shapes=[pltpu.VMEM((N_EXPERTS, 128), F32), pltpu.VMEM((N_EXPERTS, 128), F32)],
        name="moe_slots",
    )(ei, ei)


def _row_copy(src, src_row, dst, dst_row, sem):
    return pltpu.make_async_copy(src.at[pl.ds(src_row, 1)], dst.at[pl.ds(dst_row, 1)], sem)


ISSUE_UNROLL = 8


def _dispatch_kernel(slot_ref, pend_ref, cnt_ref, x1_ref, buf_ref, zero_ref, sem, zsem, *, n, tm, bm):
    base = pl.program_id(0) * tm

    @pl.when(pl.program_id(0) == 0)
    def _():
        zero_ref[...] = jnp.zeros_like(zero_ref)

        def zero_copy(e):
            start = pl.multiple_of(pend_ref[e] - bm, bm)
            return pltpu.make_async_copy(zero_ref, buf_ref.at[pl.ds(start, bm)], zsem)

        def tail_copy(blk):
            return pltpu.make_async_copy(zero_ref, buf_ref.at[pl.ds(pl.multiple_of(blk * bm, bm), bm)], zsem)

        def tail_start(blk, c):
            tail_copy(blk).start()
            return c

        def tail_wait(blk, c):
            tail_copy(blk).wait()
            return c

        for e in range(N_EXPERTS):
            @pl.when(cnt_ref[e] > 0)
            def _(e=e):
                zero_copy(e).start()
        first_unused = pend_ref[N_EXPERTS - 1] // bm
        lax.fori_loop(first_unused, buf_ref.shape[0] // bm, tail_start, 0)
        for e in range(N_EXPERTS):
            @pl.when(cnt_ref[e] > 0)
            def _(e=e):
                zero_copy(e).wait()
        lax.fori_loop(first_unused, buf_ref.shape[0] // bm, tail_wait, 0)

    def body(r, carry):
        for k in range(TOP_K):
            _row_copy(x1_ref, r, buf_ref, slot_ref[k * n + base + r], sem).start()
        return carry

    lax.fori_loop(0, tm, body, 0, unroll=ISSUE_UNROLL)
    for _ in range(TOP_K):
        pltpu.make_async_copy(x1_ref, buf_ref.at[pl.ds(0, tm)], sem).wait()


def _dispatch(slot_flat, pend, counts, x1, *, tm, bm, nblk):
    n = x1.shape[0]
    return pl.pallas_call(
        functools.partial(_dispatch_kernel, n=n, tm=tm, bm=bm),
        grid_spec=pltpu.PrefetchScalarGridSpec(
            num_scalar_prefetch=3,
            grid=(n // tm,),
            in_specs=[pl.BlockSpec((tm, D_MODEL), lambda i, s, p, c: (i, 0))],
            out_specs=pl.BlockSpec(memory_space=pl.ANY),
            scratch_shapes=[pltpu.VMEM((bm, D_MODEL), F32), pltpu.SemaphoreType.DMA(()),
                            pltpu.SemaphoreType.DMA(())],
        ),
        out_shape=jax.ShapeDtypeStruct((nblk * bm, D_MODEL), F32),
        name="moe_dispatch",
    )(slot_flat, pend, counts, x1)


def _expert_kernel(be_ref, nu_ref, nxt_ref, xb_ref, wg_hbm, wu_hbm, wd_hbm, out_ref,
                   wg_bf, wu_bf, wd_bf, wg_f32, wu_f32, wd_f32, run_ref, sem):
    j = pl.program_id(0)
    used = j < nu_ref[0]
    expert = be_ref[j]
    changed = jnp.logical_or(j == 0, expert != be_ref[jnp.maximum(j - 1, 0)])

    def weight_copies(e, slot):
        return [pltpu.make_async_copy(src.at[e], dst.at[slot], sem.at[slot])
                for src, dst in ((wg_hbm, wg_f32), (wu_hbm, wu_f32), (wd_hbm, wd_f32))]

    @pl.when(j == 0)
    def _():
        run_ref[0] = 0
        for c in weight_copies(expert, 0):
            c.start()

    @pl.when(jnp.logical_and(used, changed))
    def _():
        slot = run_ref[0] % 2
        for c in weight_copies(expert, slot):
            c.wait()
        following = nxt_ref[expert]

        @pl.when(following >= 0)
        def _():
            for c in weight_copies(following, 1 - slot):
                c.start()
        wg_bf[...] = wg_f32[slot].astype(BF16)
        wu_bf[...] = wu_f32[slot].astype(BF16)
        wd_bf[...] = wd_f32[slot].astype(BF16)
        run_ref[0] = run_ref[0] + 1

    @pl.when(used)
    def _():
        xb = xb_ref[...].astype(BF16)
        a = jnp.dot(xb, wg_bf[...], preferred_element_type=F32)
        b = jnp.dot(xb, wu_bf[...], preferred_element_type=F32)
        h = (a * _sigmoid(a)) * b
        out_ref[...] = jnp.dot(h.astype(BF16), wd_bf[...], preferred_element_type=F32)

    @pl.when(jnp.logical_not(used))
    def _():
        out_ref[...] = jnp.zeros_like(out_ref)


def _experts(block_expert, n_used, next_expert, buf, w_g, w_u, w_d, *, bm):
    nblk = buf.shape[0] // bm

    def row_map(j, be, nu, nx):
        return (jnp.minimum(j, nu[0] - 1), 0)

    any_spec = pl.BlockSpec(memory_space=pl.ANY)
    return pl.pallas_call(
        _expert_kernel,
        grid_spec=pltpu.PrefetchScalarGridSpec(
            num_scalar_prefetch=3,
            grid=(nblk,),
            in_specs=[pl.BlockSpec((bm, D_MODEL), row_map), any_spec, any_spec, any_spec],
            out_specs=pl.BlockSpec((bm, D_MODEL), lambda j, be, nu, nx: (j, 0)),
            scratch_shapes=[pltpu.VMEM((D_MODEL, D_EXPERT), BF16),
                            pltpu.VMEM((D_MODEL, D_EXPERT), BF16),
                            pltpu.VMEM((D_EXPERT, D_MODEL), BF16),
                            pltpu.VMEM((2, D_MODEL, D_EXPERT), F32),
                            pltpu.VMEM((2, D_MODEL, D_EXPERT), F32),
                            pltpu.VMEM((2, D_EXPERT, D_MODEL), F32),
                            pltpu.SMEM((1,), I32),
                            pltpu.SemaphoreType.DMA((2,))],
        ),
        out_shape=jax.ShapeDtypeStruct((buf.shape[0], D_MODEL), F32),
        compiler_params=_params(),
        name="moe_experts",
    )(block_expert, n_used, next_expert, buf, w_g, w_u, w_d)


def _combine_kernel(slot_ref, x1_ref, gc_ref, g_ref, b_ref, eo_ref, y_ref, side_ref, rows, sem,
                    *, n, tm, alpha, main_tiles):
    i = pl.program_id(0)
    last = pl.num_programs(0) - 1
    cur = i % 2

    def start(tile, buf, r):
        for k in range(TOP_K):
            _row_copy(eo_ref, slot_ref[k * n + tile * tm + r], rows.at[buf, k], r, sem.at[buf]).start(priority=k)

    def wait(buf):
        for k in range(TOP_K):
            pltpu.make_async_copy(eo_ref.at[pl.ds(0, tm)], rows.at[buf, k], sem.at[buf]).wait()

    @pl.when(i == 0)
    def _():
        def body(r, c):
            start(0, 0, r)
            return c
        lax.fori_loop(0, tm, body, 0, unroll=ISSUE_UNROLL)

    wait(cur)
    nxt = jnp.minimum(i + 1, last)
    for r in range(tm):
        start(nxt, 1 - cur, r)
    gc = gc_ref[...]
    z = alpha * x1_ref[...] + gc[:, 0:1] * rows[cur, 0] + gc[:, 1:2] * rows[cur, 1]
    mu = jnp.mean(z, axis=-1, keepdims=True)
    zc = z - mu
    var = jnp.mean(zc * zc, axis=-1, keepdims=True)
    y = zc * lax.rsqrt(var + LN_EPS) * g_ref[...] + b_ref[...]

    @pl.when(i < main_tiles)
    def _():
        y_ref[...] = y

    @pl.when(i == main_tiles)
    def _():
        side_ref[...] = y[0:side_ref.shape[0]]

    @pl.when(i == last)
    def _():
        wait(1 - cur)


def _combine(slot_flat, x1, gate_cols, ln_g, ln_b, expert_out, *, tm, alpha, main_rows, side_rows):
    n = x1.shape[0]
    main_tiles = main_rows // tm
    return pl.pallas_call(
        functools.partial(_combine_kernel, n=n, tm=tm, alpha=alpha, main_tiles=main_tiles),
        grid_spec=pltpu.PrefetchScalarGridSpec(
            num_scalar_prefetch=1,
            grid=(n // tm,),
            in_specs=[pl.BlockSpec((tm, D_MODEL), lambda i, s: (i, 0)),
                      pl.BlockSpec((tm, TOP_K), lambda i, s: (i, 0)),
                      pl.BlockSpec((1, D_MODEL), lambda i, s: (0, 0)),
                      pl.BlockSpec((1, D_MODEL), lambda i, s: (0, 0)),
                      pl.BlockSpec(memory_space=pl.ANY)],
            out_specs=[pl.BlockSpec((tm, D_MODEL), lambda i, s: (jnp.minimum(i, main_tiles - 1), 0)),
                       pl.BlockSpec((side_rows, D_MODEL), lambda i, s: (0, 0))],
            scratch_shapes=[pltpu.VMEM((2, TOP_K, tm, D_MODEL), F32), pltpu.SemaphoreType.DMA((2,))],
        ),
        out_shape=[jax.ShapeDtypeStruct((main_rows, D_MODEL), F32),
                   jax.ShapeDtypeStruct((side_rows, D_MODEL), F32)],
        compiler_params=_params(),
        name="moe_combine",
    )(slot_flat, x1, gate_cols, ln_g, ln_b, expert_out)


def _hier_moe_ln(x1, ei, gt, w_g, w_u, w_d, ln_g, ln_b, *, tl, tm, bm, alpha, main_rows, side_rows):
    n = x1.shape[0]
    m = n * TOP_K
    slot, cnt = _slots(ei, tl=tl, bm=bm)
    counts = cnt[:, 0]
    pend = jnp.cumsum((counts + bm - 1) // bm * bm)
    nblk = (m + N_EXPERTS * (bm - 1) + bm - 1) // bm
    blk_start = jnp.arange(nblk, dtype=I32) * bm
    n_used = (pend[-1] // bm).astype(I32)
    be = jnp.minimum(jnp.sum(pend[None, :] <= blk_start[:, None], axis=1), N_EXPERTS - 1).astype(I32)
    be = jnp.where(jnp.arange(nblk) < n_used, be, jnp.take(be, n_used - 1))
    slot_flat = slot[0:TOP_K].reshape(m)
    buf = _dispatch(slot_flat, pend.astype(I32), counts, x1, tm=tm, bm=bm, nblk=nblk)
    ids = jnp.arange(N_EXPERTS, dtype=I32)
    first_at_or_after = lax.cummin(jnp.where(counts > 0, ids, N_EXPERTS), reverse=True)
    next_expert = jnp.concatenate([first_at_or_after[1:], jnp.full((1,), N_EXPERTS, I32)])
    next_expert = jnp.where(next_expert == N_EXPERTS, -1, next_expert).astype(I32)
    eo = _experts(be, n_used.reshape(1), next_expert, buf, w_g, w_u, w_d, bm=bm)
    gate_cols = gt[0:TOP_K].T
    return _combine(slot_flat, x1, gate_cols, ln_g, ln_b, eo, tm=tm, alpha=alpha,
                    main_rows=main_rows, side_rows=side_rows)


def _t5_bucket(n):
    nf = jnp.maximum(n, 1).astype(F32)
    large = MAX_EXACT + (jnp.log(nf / MAX_EXACT) / math.log(MAX_DISTANCE / MAX_EXACT)
                         * (N_BUCKETS - MAX_EXACT)).astype(I32)
    large = jnp.minimum(large, N_BUCKETS - 1)
    return jnp.where(n < MAX_EXACT, n, large)


def _bias_per_group(rel_bias):
    offs = jnp.arange(N_KEYS, dtype=I32)[None, :] * jnp.array(DILATIONS, I32)[:, None]
    bucket = _t5_bucket(offs)
    table = rel_bias.reshape(N_BUCKETS, N_GROUPS, HEADS)
    b = table[bucket, jnp.arange(N_GROUPS)[:, None]]
    return jnp.transpose(b, (0, 2, 1)).astype(F32)


def _prompt_bias_tables(bias):
    width = 3 * Q_BLOCK
    neg = jnp.full((N_GROUPS, HEADS, Q_BLOCK - 1), NEG_INF, F32)
    r = jnp.concatenate([neg, bias[:, :, ::-1], neg, jnp.full((N_GROUPS, HEADS, 1), NEG_INF, F32)], axis=-1)
    flat = jnp.tile(r, (1, 1, Q_BLOCK))[:, :, :Q_BLOCK * (width - 1)]
    skew = flat.reshape(N_GROUPS, HEADS, Q_BLOCK, width - 1)
    later = skew[:, :, :, Q_BLOCK - 1:3 * Q_BLOCK - 1]
    has_prev = (np.arange(2 * Q_BLOCK) >= Q_BLOCK)[None, None, None, :]
    first = jnp.where(has_prev, later, NEG_INF)
    tb = jnp.stack([first, later], axis=1)
    return tb.reshape(N_GROUPS, 2, HEADS // 2, 2 * Q_BLOCK, 2 * Q_BLOCK)


def _sample_bias_tables(bias):
    t = np.arange(T_NEW)
    bcs = []
    for g, d in enumerate(DILATIONS):
        rev = bias[g][:, ::-1][:, :WINDOW_KEYS]
        if d == 1:
            dist = np.arange(WINDOW_KEYS)[None, :] - t[:, None]
            vals = jnp.take(rev, np.clip(dist, 0, WINDOW_KEYS - 1), axis=1)
            bcs.append(jnp.where(dist[None] >= 0, vals, NEG_INF))
        else:
            cls = np.arange(d)[None, :] == t[:, None]
            table = jnp.where(cls[None, :, None, :], rev[:, None, :, None], NEG_INF)
            bcs.append(table.reshape(HEADS, T_NEW, WINDOW_KEYS * d))
    back = t[:, None] - t[None, :]
    vals = jnp.take(bias, np.clip(back, 0, T_NEW - 1), axis=2)
    ok = np.stack([(back >= 0) if d == 1 else (back == 0) for d in DILATIONS])
    return bcs, jnp.where(ok[:, None], vals, NEG_INF)


def _split_bf16(w):
    hi = w.astype(BF16)
    lo = (w - hi.astype(F32)).astype(BF16)
    return hi, lo


def kernel(x_prompt, x_sample, cache_attn_w128, cache_attn_w512, cache_attn_w2048, state_conv, rel_bias, w_in, w_conv, w_pa, w_pb, w_o, ln1_g, ln1_b, w_router_group, w_router_expert, w_expert_gate, w_expert_up, w_expert_down, ln2_g, ln2_b):
    depth = w_in.shape[0]
    assert depth == 1 and x_prompt.shape[0] == 1
    alpha = (2.0 * depth) ** 0.25
    s = x_prompt.shape[1]
    bd, t_len = x_sample.shape[0], x_sample.shape[1]
    assert t_len == T_NEW and s % (DILATIONS[-1] * Q_BLOCK * Q_BLOCKS_PER_STEP) == 0
    assert s % PROJ_TILE == 0 and s % MIX_TILE == 0 and s % MOE_TILE == 0
    assert (s + MIX_TILE) % SLOT_TILE == 0 and (s + MIX_TILE) % MOE_TILE == 0

    bias = _bias_per_group(rel_bias)
    tb = _prompt_bias_tables(bias)
    bcs, bn = _sample_bias_tables(bias)

    w_in_bf = w_in[0].astype(BF16)
    w_pa_bf = w_pa[0].astype(BF16)
    w_pb_bf = w_pb[0].astype(BF16)
    w_o_bf = w_o[0].astype(BF16)
    wr = jnp.zeros((ROUTER_ROWS, D_MODEL), F32)
    wr = wr.at[0:N_EXPERT_GROUPS].set(w_router_group[0].T).at[8:8 + N_EXPERTS].set(w_router_expert[0].T)
    wr_hi, wr_lo = _split_bf16(wr)
    g1, b1 = ln1_g[0][None], ln1_b[0][None]
    g2, b2 = ln2_g[0][None], ln2_b[0][None]
    wg, wu, wd = w_expert_gate[0], w_expert_up[0], w_expert_down[0]

    xp = x_prompt[0]
    kv_tail = min(MAX_DISTANCE, s)
    q, kb, vb, (k32, v32, yb, sga, sgb, ut) = _proj(
        xp, w_in_bf, w_conv[0], None, tm=PROJ_TILE, u_tail=8, q_dtype=BF16, dils=DILATIONS, kv_f32=True)
    k32, v32 = k32[s - kv_tail:], v32[s - kv_tail:]
    o_l = [_attn_prompt_group(q[g], kb[g], vb[g], tb[g], g) for g in range(N_GROUPS)]
    routed = _mix(xp, [a[0] for a in o_l], [a[1] for a in o_l], yb, sga, sgb,
                  w_pa_bf, w_pb_bf, w_o_bf, g1, b1, wr_hi, wr_lo, tm=MIX_TILE, alpha=alpha, dils=DILATIONS,
                  extra_tiles=1)

    kv_prompt = []
    for g, d in enumerate(DILATIONS):
        length = min(WINDOW_KEYS * d, s)
        cols = slice(g * GROUP_WIDTH, (g + 1) * GROUP_WIDTH)
        kg = k32[kv_tail - length:, cols].reshape(length, HEADS, HEAD_DIM)
        vg = v32[kv_tail - length:, cols].reshape(length, HEADS, HEAD_DIM)
        kv_prompt.append(jnp.stack([kg, vg], axis=1)[None, None])
    conv_prompt = ut[6:8][None, None]

    ns = bd * t_len
    xs = x_sample.reshape(ns, D_MODEL)
    st = state_conv[0]
    s0 = jnp.repeat(st[:, 0], t_len, axis=0)
    s1 = jnp.repeat(st[:, 1], t_len, axis=0)
    qs, _, _, (k32s, v32s, ybs, sgas, sgbs, us) = _proj(
        xs, w_in_bf, w_conv[0], (s0, s1), tm=ns, u_tail=ns, q_dtype=F32, dils=NO_DILATION, kv_f32=True)
    qs = jnp.concatenate([a[0] for a in qs], axis=1)
    packed = jnp.stack([qs, k32s, v32s]).reshape(3, bd, t_len, N_GROUPS, GROUP_WIDTH)
    qkv_t = jnp.transpose(packed, (1, 4, 0, 3, 2)).reshape(bd, GROUP_WIDTH, 3 * N_GROUPS * t_len)
    qkv_t = jnp.pad(qkv_t, ((0, 0), (0, 0), (0, 128 - 3 * N_GROUPS * t_len)))
    caches = (cache_attn_w128[0], cache_attn_w512[0], cache_attn_w2048[0])
    caches_t = [jnp.transpose(c, (0, 2, 3, 4, 1)).reshape(bd, 2, GROUP_WIDTH, c.shape[1]) for c in caches]
    pair = (bd, t_len, N_GROUPS, HEADS // 2, 2, HEAD_DIM)
    q6 = jnp.transpose(qs.reshape(pair), (0, 3, 2, 4, 1, 5))
    zeros = jnp.zeros_like(q6[:, :, :, 0])
    qbd = jnp.stack([jnp.concatenate([q6[:, :, :, 0], zeros], axis=-1),
                     jnp.concatenate([zeros, q6[:, :, :, 1]], axis=-1)], axis=3)
    qbd = qbd.reshape(bd, HEADS // 2, N_GROUPS, 2 * t_len, 128)
    new_rows = jnp.stack([k32s, v32s]).reshape(2, bd, t_len, N_GROUPS, HEADS // 2, 128)
    new_rows = jnp.transpose(new_rows, (1, 4, 3, 0, 2, 5))
    bcs = [t.reshape(HEADS // 2, 2 * t_len, t.shape[-1]) for t in bcs]
    bn = bn.reshape(N_GROUPS, HEADS // 2, 2 * t_len, t_len)
    n0, n1, n2, o_s, lse_s = _sample_cache(qbd, new_rows, qkv_t, caches_t, bcs, bn)

    def unpack(a):
        a = a[:, :N_GROUPS * t_len].reshape(bd, N_GROUPS, t_len, GROUP_WIDTH)
        return jnp.transpose(a, (1, 0, 2, 3)).reshape(N_GROUPS, 1, ns, GROUP_WIDTH)

    o_s, lse_s = unpack(o_s), unpack(lse_s)
    assert ns <= min(MIX_TILE, MOE_TILE) and s % ns == 0
    x1, ei, gt = _mix(xs, o_s, lse_s, ybs, sgas, sgbs, w_pa_bf, w_pb_bf, w_o_bf, g1, b1, wr_hi, wr_lo,
                      tm=ns, alpha=alpha, dils=NO_DILATION, into=routed, into_tile=s // ns)
    y_prompt, y_sample = _hier_moe_ln(x1, ei, gt, wg, wu, wd, g2, b2, tl=SLOT_TILE, tm=MOE_TILE, bm=EXPERT_BLOCK,
                                      alpha=alpha, main_rows=s, side_rows=ns)
    y_prompt = y_prompt[None]
    y_sample = y_sample.reshape(bd, t_len, D_MODEL)

    kv_sample = [jnp.transpose(c.reshape(bd, 2, HEADS, HEAD_DIM, c.shape[-1]), (0, 4, 1, 2, 3))[None]
                 for c in (n0, n1, n2)]
    conv_sample = us.reshape(bd, t_len, CONV_CHANNELS)[:, t_len - 2:][None]

    return (y_prompt, y_sample, kv_prompt[0], kv_prompt[1], kv_prompt[2], conv_prompt,
            kv_sample[0], kv_sample[1], kv_sample[2], conv_sample)
```

```python
import functools
import math

import numpy as np
import jax
import jax.numpy as jnp
from jax import lax
from jax.experimental import pallas as pl
from jax.experimental.pallas import tpu as pltpu

F32 = jnp.float32
BF16 = jnp.bfloat16
I32 = jnp.int32

D_MODEL = 1024
N_GROUPS = 3
HEADS = 8
HEAD_DIM = 64
GROUP_WIDTH = HEADS * HEAD_DIM
ATTN_WIDTH = N_GROUPS * GROUP_WIDTH
DILATIONS = (1, 4, 16)
NO_DILATION = (1, 1, 1)
WINDOW_KEYS = 128
N_KEYS = WINDOW_KEYS + 1
N_BUCKETS = 32
MAX_EXACT = 16
MAX_DISTANCE = 2048
CONV_CHANNELS = 512
N_EXPERT_GROUPS = 4
EXPERTS_PER_GROUP = 8
N_EXPERTS = 32
TOP_K = 2
D_EXPERT = 512
LN_EPS = 1e-5
PROJ_WIDTH = 3 * ATTN_WIDTH + 3 * CONV_CHANNELS + 2 * D_MODEL
ROUTER_ROWS = 8 + N_EXPERTS
Q_BLOCK = 128
T_NEW = 4
NEG_INF = float("-inf")
VMEM_LIMIT = 56 * 1024 * 1024

PROJ_TILE = 256
MIX_TILE = 512
SLOT_TILE = 512
MOE_TILE = 512
EXPERT_BLOCK = 512


def _sigmoid(x):
    return 1.0 / (1.0 + jnp.exp(-x))


def _params(limit=VMEM_LIMIT):
    return pltpu.CompilerParams(vmem_limit_bytes=limit)


def _proj_kernel(*refs, tm, tail_rows, sample_mode, dils, kv_f32):
    n_in = 5 if sample_mode else 3
    x_ref, w_ref, wc_ref = refs[0:3]
    outs = list(refs[n_in:])
    q_refs, k_refs, v_refs = outs[0:3], outs[3:6], outs[6:9]
    del outs[0:9]
    k32_ref, v32_ref = (outs.pop(0), outs.pop(0)) if kv_f32 else (None, None)
    yb_ref, sga_ref, sgb_ref, ut_ref, cls_ref = outs[0:5]
    xb = x_ref[...].astype(BF16)

    def col(c0, width):
        return jnp.dot(xb, w_ref[:, c0:c0 + width], preferred_element_type=F32)

    def write_classes(val, group_refs):
        for g, d in enumerate(dils):
            part = val[:, g * GROUP_WIDTH:(g + 1) * GROUP_WIDTH]
            ref = group_refs[g]
            if d == 1:
                ref[0] = part.astype(ref.dtype)
            else:
                for kk in range(GROUP_WIDTH // 128):
                    lanes = slice(kk * 128, (kk + 1) * 128)
                    cls_ref[kk] = part[:, lanes]
                    for c in range(d):
                        ref[c, :, lanes] = cls_ref[kk, pl.ds(c, tm // d, stride=d), :].astype(ref.dtype)

    write_classes(col(0, ATTN_WIDTH), q_refs)
    k = col(ATTN_WIDTH, ATTN_WIDTH)
    v = col(2 * ATTN_WIDTH, ATTN_WIDTH)
    if kv_f32:
        k32_ref[...] = k
        v32_ref[...] = v
    write_classes(k, k_refs)
    write_classes(v, v_refs)

    c0 = 3 * ATTN_WIDTH
    bg = col(c0, CONV_CHANNELS)
    u = col(c0 + CONV_CHANNELS, CONV_CHANNELS) * col(c0 + 2 * CONV_CHANNELS, CONV_CHANNELS)
    row = lax.broadcasted_iota(I32, (tm, CONV_CHANNELS), 0)
    r1 = pltpu.roll(u, 1, axis=0)
    r2 = pltpu.roll(u, 2, axis=0)
    if sample_mode:
        s0 = refs[3][...]
        s1 = refs[4][...]
        t = row & (T_NEW - 1)
        prev1 = jnp.where(t == 0, s1, r1)
        prev2 = jnp.where(t == 0, s0, jnp.where(t == 1, s1, r2))
    else:
        carry_ref = outs[5]

        @pl.when(pl.program_id(0) == 0)
        def _():
            carry_ref[...] = jnp.zeros_like(carry_ref)
        c6 = carry_ref[6:7, :]
        c7 = carry_ref[7:8, :]
        prev1 = jnp.where(row == 0, c7, r1)
        prev2 = jnp.where(row == 0, c6, jnp.where(row == 1, c7, r2))
        carry_ref[...] = u[tm - 8:tm, :]
    conv = prev2 * wc_ref[0:1, :] + prev1 * wc_ref[1:2, :] + u * wc_ref[2:3, :]
    yb_ref[...] = (bg * conv).astype(BF16)
    ut_ref[...] = u[tm - tail_rows:tm, :]

    c1 = c0 + 3 * CONV_CHANNELS
    sga_ref[...] = _sigmoid(col(c1, D_MODEL)).astype(BF16)
    sgb_ref[...] = _sigmoid(col(c1 + D_MODEL, D_MODEL)).astype(BF16)


def _proj(x, w_in_bf, w_conv, conv_prev, *, tm, u_tail, q_dtype, dils, kv_f32):
    n = x.shape[0]
    sample_mode = conv_prev is not None
    nt = n // tm

    def row_spec(width):
        return pl.BlockSpec((tm, width), lambda i: (i, 0))

    def class_spec(d):
        return pl.BlockSpec((d, tm // d, GROUP_WIDTH), lambda i: (0, i, 0))

    def class_shape(d, dtype):
        return jax.ShapeDtypeStruct((d, n // d, GROUP_WIDTH), dtype)

    in_specs = [
        row_spec(D_MODEL),
        pl.BlockSpec((D_MODEL, PROJ_WIDTH), lambda i: (0, 0), pipeline_mode=pl.Buffered(1)),
        pl.BlockSpec((3, CONV_CHANNELS), lambda i: (0, 0)),
    ]
    args = [x, w_in_bf, w_conv]
    scratch = [pltpu.VMEM((GROUP_WIDTH // 128, tm, 128), F32)]
    if sample_mode:
        in_specs += [row_spec(CONV_CHANNELS), row_spec(CONV_CHANNELS)]
        args += [conv_prev[0], conv_prev[1]]
    else:
        scratch.append(pltpu.VMEM((8, CONV_CHANNELS), F32))
    out_shape = (
        [class_shape(d, q_dtype) for d in dils] + [class_shape(d, BF16) for d in dils] * 2
        + [jax.ShapeDtypeStruct((n, ATTN_WIDTH), F32)] * (2 if kv_f32 else 0)
        + [jax.ShapeDtypeStruct((n, CONV_CHANNELS), BF16),
           jax.ShapeDtypeStruct((n, D_MODEL), BF16),
           jax.ShapeDtypeStruct((n, D_MODEL), BF16),
           jax.ShapeDtypeStruct((u_tail, CONV_CHANNELS), F32)])
    out_specs = (
        [class_spec(d) for d in dils] * 3
        + [row_spec(ATTN_WIDTH)] * (2 if kv_f32 else 0)
        + [row_spec(CONV_CHANNELS), row_spec(D_MODEL), row_spec(D_MODEL),
           pl.BlockSpec((u_tail, CONV_CHANNELS), lambda i: (0, 0))])
    res = pl.pallas_call(
        functools.partial(_proj_kernel, tm=tm, tail_rows=u_tail, sample_mode=sample_mode, dils=dils, kv_f32=kv_f32),
        grid=(nt,),
        in_specs=in_specs,
        out_specs=out_specs,
        out_shape=out_shape,
        scratch_shapes=scratch,
        compiler_params=_params(),
        name="proj",
    )(*args)
    return res[0:3], res[3:6], res[6:9], res[9:]


Q_BLOCKS_PER_STEP = 8


def _attn_kernel(q_ref, kp_ref, kc_ref, vp_ref, vc_ref, tb_ref, o_ref, lse_ref):
    lane = lax.broadcasted_iota(I32, (Q_BLOCK, 128), 1)
    first = lane < HEAD_DIM
    scale = HEAD_DIM ** -0.5
    has_prev = jnp.minimum(pl.program_id(1), 1)
    for sub in range(Q_BLOCKS_PER_STEP):
        rows = slice(sub * Q_BLOCK, (sub + 1) * Q_BLOCK)
        band = slice((sub - 1) * Q_BLOCK, (sub + 1) * Q_BLOCK)
        for pr in range(HEADS // 2):
            sl = slice(pr * 128, (pr + 1) * 128)
            if sub == 0:
                k = jnp.concatenate([kp_ref[:, sl], kc_ref[rows, sl]], axis=0)
                v = jnp.concatenate([vp_ref[:, sl], vc_ref[rows, sl]], axis=0)
                bias = tb_ref[has_prev, pr]
            else:
                k = kc_ref[band, sl]
                v = vc_ref[band, sl]
                bias = tb_ref[1, pr]
            qf = q_ref[rows, sl].astype(F32) * scale
            qq = jnp.concatenate([jnp.where(first, qf, 0.0), jnp.where(first, 0.0, qf)], axis=0).astype(BF16)
            s = lax.dot_general(qq, k, (((1,), (1,)), ((), ())), preferred_element_type=F32) + bias
            m = jnp.max(s, axis=-1, keepdims=True)
            p = jnp.exp(s - m)
            l = jnp.sum(p, axis=-1, keepdims=True)
            o = jnp.dot(p.astype(BF16), v, preferred_element_type=F32) / l
            lse = m + jnp.log(l)
            o_ref[rows, sl] = jnp.where(first, o[:Q_BLOCK], o[Q_BLOCK:]).astype(o_ref.dtype)
            lse_ref[rows, sl] = jnp.where(first, jnp.broadcast_to(lse[:Q_BLOCK], (Q_BLOCK, 128)),
                                          jnp.broadcast_to(lse[Q_BLOCK:], (Q_BLOCK, 128)))


def _attn_prompt_group(q, kb, vb, tb, g):
    d, rows = q.shape[0], q.shape[1]
    nq = Q_BLOCKS_PER_STEP
    cur = pl.BlockSpec((None, nq * Q_BLOCK, GROUP_WIDTH), lambda c, i: (c, i, 0))
    prev = pl.BlockSpec((None, Q_BLOCK, GROUP_WIDTH), lambda c, i: (c, jnp.maximum(i * nq - 1, 0), 0))
    return pl.pallas_call(
        _attn_kernel,
        grid=(d, rows // (nq * Q_BLOCK)),
        in_specs=[cur, prev, cur, prev, cur, pl.BlockSpec(tb.shape, lambda c, i: (0, 0, 0, 0))],
        out_specs=[cur, cur],
        out_shape=[jax.ShapeDtypeStruct((d, rows, GROUP_WIDTH), BF16),
                   jax.ShapeDtypeStruct((d, rows, GROUP_WIDTH), F32)],
        compiler_params=_params(),
        name=f"attn_prompt_g{g}",
    )(q, kb, kb, vb, vb, tb)


PACK_Q, PACK_K, PACK_V = 0, N_GROUPS * T_NEW, 2 * N_GROUPS * T_NEW


PAIRS_PER_STEP = 2


def _sample_cache_kernel(qbd_ref, nr_ref, qkv_ref, c0_ref, c1_ref, c2_ref, b0_ref, b1_ref, b2_ref, bn_ref,
                         n0_ref, n1_ref, n2_ref, o_ref, lse_ref):
    for pp in range(PAIRS_PER_STEP):
        rows = pl.ds(pp * 128, 128)
        kv = pl.ds(0, 2)
        _sample_pair(qbd_ref.at[pp], nr_ref.at[pp], qkv_ref.at[rows],
                     [c.at[kv, rows] for c in (c0_ref, c1_ref, c2_ref)],
                     [b.at[pp] for b in (b0_ref, b1_ref, b2_ref)], bn_ref.at[pl.ds(0, N_GROUPS), pp],
                     [c.at[kv, rows] for c in (n0_ref, n1_ref, n2_ref)],
                     o_ref.at[pl.ds(0, 16), rows], lse_ref.at[pl.ds(0, 16), rows])


def _sample_pair(qbd_ref, nr_ref, qkv_ref, c_refs, b_refs, bn_ref, n_refs, o_ref, lse_ref):
    scale = HEAD_DIM ** -0.5
    nt = (((1,), (1,)), ((), ()))
    lane = lax.broadcasted_iota(I32, (128, 128), 1)
    head0 = lane[0:T_NEW] < HEAD_DIM
    for g, (c_ref, b_ref, n_ref) in enumerate(zip(c_refs, b_refs, n_refs)):
        length = c_ref.shape[-1]
        qbd = qbd_ref[g]
        s_c = jnp.dot(qbd.astype(BF16), c_ref[0].astype(BF16), preferred_element_type=F32) * scale + b_ref[...]
        bn = bn_ref[g]
        s_n = [jnp.sum(qbd * nr_ref[g, 0, tn:tn + 1, :], axis=1, keepdims=True) * scale + bn[:, tn:tn + 1]
               for tn in range(T_NEW)]
        m = jnp.max(s_c, axis=1, keepdims=True)
        for x in s_n:
            m = jnp.maximum(m, x)
        p_c = jnp.exp(s_c - m)
        p_n = [jnp.exp(x - m) for x in s_n]
        l = jnp.sum(p_c, axis=1, keepdims=True)
        acc = lax.dot_general(p_c.astype(BF16), c_ref[1].astype(BF16), nt, preferred_element_type=F32)
        for tn in range(T_NEW):
            l = l + p_n[tn]
            acc = acc + p_n[tn] * nr_ref[g, 1, tn:tn + 1, :]
        o = acc / l
        lse = jnp.broadcast_to(m + jnp.log(l), (2 * T_NEW, 128))
        o_ref[g * T_NEW:(g + 1) * T_NEW, :] = jnp.where(head0, o[0:T_NEW], o[T_NEW:])
        lse_ref[g * T_NEW:(g + 1) * T_NEW, :] = jnp.where(head0, lse[0:T_NEW], lse[T_NEW:])

        for kv, pack in ((0, PACK_K), (1, PACK_V)):
            rolled = pltpu.roll(c_ref[kv], length - T_NEW, axis=1)
            new_tail = pltpu.roll(qkv_ref[...], 128 - T_NEW - (pack + g * T_NEW), axis=1)
            tail = jnp.where(lane >= 128 - T_NEW, new_tail, rolled[:, length - 128:])
            if length > 128:
                n_ref[kv, :, 0:length - 128] = rolled[:, 0:length - 128]
            n_ref[kv, :, length - 128:] = tail
    pad_rows = slice(N_GROUPS * T_NEW, 16)
    o_ref[pad_rows, :] = jnp.zeros((16 - N_GROUPS * T_NEW, 128), F32)
    lse_ref[pad_rows, :] = jnp.zeros((16 - N_GROUPS * T_NEW, 128), F32)


def _sample_cache(qbd, new_rows, qkv_t, caches_t, bcs, bn):
    b = qkv_t.shape[0]

    pp = PAIRS_PER_STEP

    def cache_spec(c):
        return pl.BlockSpec((None, 2, pp * 128, c.shape[-1]), lambda i, h: (i, 0, h, 0))

    def bias_spec(t):
        return pl.BlockSpec((pp, 2 * T_NEW, t.shape[-1]), lambda i, h: (h, 0, 0))

    out = pl.BlockSpec((None, 16, pp * 128), lambda i, h: (i, 0, h))
    return pl.pallas_call(
        _sample_cache_kernel,
        grid=(b, HEADS // 2 // pp),
        in_specs=[pl.BlockSpec((None, pp, N_GROUPS, 2 * T_NEW, 128), lambda i, h: (i, h, 0, 0, 0)),
                  pl.BlockSpec((None, pp, N_GROUPS, 2, T_NEW, 128), lambda i, h: (i, h, 0, 0, 0, 0)),
                  pl.BlockSpec((None, pp * 128, 128), lambda i, h: (i, h, 0))]
                 + [cache_spec(c) for c in caches_t] + [bias_spec(t) for t in bcs]
                 + [pl.BlockSpec((N_GROUPS, pp, 2 * T_NEW, T_NEW), lambda i, h: (0, h, 0, 0))],
        out_specs=[cache_spec(c) for c in caches_t] + [out, out],
        out_shape=[jax.ShapeDtypeStruct(c.shape, c.dtype) for c in caches_t]
                  + [jax.ShapeDtypeStruct((b, 16, GROUP_WIDTH), F32)] * 2,
        compiler_params=_params(),
        name="sample_cache",
    )(qbd, new_rows, qkv_t, *caches_t, *bcs, bn)


def _mix_kernel(*refs, tm, alpha, dils, n_alias):
    (x_ref, o0_ref, o1_ref, o2_ref, l0_ref, l1_ref, l2_ref, yb_ref, sga_ref, sgb_ref,
     wpa_ref, wpb_ref, wo_ref, g_ref, b_ref, wrh_ref, wrl_ref) = refs[0:17]
    x1_ref, ei_ref, gt_ref = refs[17 + n_alias:20 + n_alias]
    scratch = list(refs[20 + n_alias:])

    def natural(ref, d):
        if d == 1:
            return ref[0].astype(F32)
        scr = scratch.pop()
        for kk in range(GROUP_WIDTH // 128):
            for c in range(d):
                scr[kk, pl.ds(c, tm // d, stride=d), :] = ref[c, :, kk * 128:(kk + 1) * 128].astype(F32)
        return jnp.concatenate([scr[kk] for kk in range(GROUP_WIDTH // 128)], axis=1)

    l0, l1, l2 = natural(l0_ref, dils[0]), natural(l1_ref, dils[1]), natural(l2_ref, dils[2])
    mx = jnp.maximum(jnp.maximum(l0, l1), l2)
    e0 = jnp.exp(l0 - mx)
    e1 = jnp.exp(l1 - mx)
    e2 = jnp.exp(l2 - mx)
    ya = (e0 * natural(o0_ref, dils[0]) + e1 * natural(o1_ref, dils[1]) + e2 * natural(o2_ref, dils[2])) / (e0 + e1 + e2)
    pa = jnp.dot(ya.astype(BF16), wpa_ref[...], preferred_element_type=F32)
    pb = jnp.dot(yb_ref[...], wpb_ref[...], preferred_element_type=F32)
    gated = sga_ref[...].astype(F32) * pa + sgb_ref[...].astype(F32) * pb
    mix = jnp.dot(gated.astype(BF16), wo_ref[...], preferred_element_type=F32)
    z = alpha * x_ref[...] + mix
    mu = jnp.mean(z, axis=-1, keepdims=True)
    zc = z - mu
    var = jnp.mean(zc * zc, axis=-1, keepdims=True)
    x1 = zc * lax.rsqrt(var + LN_EPS) * g_ref[...] + b_ref[...]
    x1_ref[...] = x1

    xh = x1.astype(BF16)
    xl = (x1 - xh.astype(F32)).astype(BF16)
    nt = (((1,), (1,)), ((), ()))
    wrh = wrh_ref[...]
    lt = (lax.dot_general(wrh, xh, nt, preferred_element_type=F32)
          + lax.dot_general(wrh, xl, nt, preferred_element_type=F32)
          + lax.dot_general(wrl_ref[...], xh, nt, preferred_element_type=F32))

    gl = lt[0:N_EXPERT_GROUPS]
    gmax = jnp.max(gl, axis=0, keepdims=True)
    idx4 = lax.broadcasted_iota(I32, (N_EXPERT_GROUPS, tm), 0)
    g_idx = jnp.min(jnp.where(gl == gmax, idx4, N_EXPERT_GROUPS), axis=0, keepdims=True)
    g_prob = 1.0 / jnp.sum(jnp.exp(gl - gmax), axis=0, keepdims=True)
    e_sel = lt[8:16]
    for grp in range(1, N_EXPERT_GROUPS):
        e_sel = jnp.where(g_idx == grp, lt[8 + 8 * grp:16 + 8 * grp], e_sel)
    idx8 = lax.broadcasted_iota(I32, (EXPERTS_PER_GROUP, tm), 0)
    v1 = jnp.max(e_sel, axis=0, keepdims=True)
    i1 = jnp.min(jnp.where(e_sel == v1, idx8, EXPERTS_PER_GROUP), axis=0, keepdims=True)
    rest = jnp.where(idx8 == i1, NEG_INF, e_sel)
    v2 = jnp.max(rest, axis=0, keepdims=True)
    i2 = jnp.min(jnp.where(rest == v2, idx8, EXPERTS_PER_GROUP), axis=0, keepdims=True)
    r = jnp.exp(v2 - v1)
    gate1 = g_prob / (1.0 + r)
    gate2 = g_prob * r / (1.0 + r)
    ex1 = g_idx * EXPERTS_PER_GROUP + i1
    ex2 = g_idx * EXPERTS_PER_GROUP + i2
    ei_ref[...] = jnp.where(idx8 == 0, ex1, jnp.where(idx8 == 1, ex2, 0))
    gt_ref[...] = jnp.where(idx8 == 0, gate1, jnp.where(idx8 == 1, gate2, 0.0))


def _mix(x, o, lse, yb, sga, sgb, w_pa, w_pb, w_o, ln_g, ln_b, wr_hi, wr_lo, *, tm, alpha, dils,
         extra_tiles=0, into=None, into_tile=0):
    n = x.shape[0]
    nt = n // tm

    def src(i):
        return jnp.minimum(i, nt - 1)

    def row_spec(width):
        return pl.BlockSpec((tm, width), lambda i: (src(i), 0))

    def class_spec(d):
        return pl.BlockSpec((d, tm // d, GROUP_WIDTH), lambda i: (0, src(i), 0))

    def full(a):
        return pl.BlockSpec(a.shape, lambda i: (0,) * a.ndim)

    n_alias = 0 if into is None else 3
    rows_out = n + extra_tiles * tm if into is None else into[0].shape[0]
    any_spec = pl.BlockSpec(memory_space=pl.ANY)
    n_scratch = 2 * sum(1 for d in dils if d > 1)
    args = [x, o[0], o[1], o[2], lse[0], lse[1], lse[2], yb, sga, sgb, w_pa, w_pb, w_o, ln_g, ln_b, wr_hi, wr_lo]
    return pl.pallas_call(
        functools.partial(_mix_kernel, tm=tm, alpha=alpha, dils=dils, n_alias=n_alias),
        grid=(nt + extra_tiles,),
        in_specs=[row_spec(D_MODEL)] + [class_spec(d) for d in dils] * 2 + [row_spec(CONV_CHANNELS)]
                 + [row_spec(D_MODEL)] * 2
                 + [full(w_pa), full(w_pb), full(w_o), full(ln_g), full(ln_b), full(wr_hi), full(wr_lo)]
                 + [any_spec] * n_alias,
        out_specs=[pl.BlockSpec((tm, D_MODEL), lambda i: (i + into_tile, 0)),
                   pl.BlockSpec((8, tm), lambda i: (0, i + into_tile)),
                   pl.BlockSpec((8, tm), lambda i: (0, i + into_tile))],
        out_shape=[jax.ShapeDtypeStruct((rows_out, D_MODEL), F32),
                   jax.ShapeDtypeStruct((8, rows_out), I32),
                   jax.ShapeDtypeStruct((8, rows_out), F32)],
        input_output_aliases={len(args) + k: k for k in range(n_alias)},
        scratch_shapes=[pltpu.VMEM((GROUP_WIDTH // 128, tm, 128), F32)] * n_scratch,
        compiler_params=_params(),
        name="mix",
    )(*args, *(into or ()))


def _slot_kernel(all_ref, ei_ref, slot_ref, cnt_ref, carry_ref, start_ref, *, tl, bm):
    step = pl.program_id(0)

    def one_hot(rows):
        ex = lax.broadcasted_iota(I32, (N_EXPERTS, rows.shape[1]), 0)
        return (ex == rows).astype(F32)

    @pl.when(step == 0)
    def _():
        total = (jnp.sum(one_hot(all_ref[0:1, :]), axis=1, keepdims=True)
                 + jnp.sum(one_hot(all_ref[1:2, :]), axis=1, keepdims=True))
        total = jnp.broadcast_to(total, (N_EXPERTS, 128))
        cnt_ref[...] = total.astype(I32)
        blocks = jnp.floor((total + (bm - 1)) * (1.0 / bm))
        a = lax.broadcasted_iota(I32, (N_EXPERTS, N_EXPERTS), 0)
        b = lax.broadcasted_iota(I32, (N_EXPERTS, N_EXPERTS), 1)
        before = (b < a).astype(BF16)
        start_ref[...] = jnp.dot(before, blocks.astype(BF16), preferred_element_type=F32) * bm
        carry_ref[...] = jnp.zeros_like(carry_ref)
        slot_ref[...] = jnp.zeros_like(slot_ref)

    @pl.when(step > 0)
    def _():
        oh0 = one_hot(ei_ref[0:1, :])
        oh1 = one_hot(ei_ref[1:2, :])
        cnt0 = jnp.sum(oh0, axis=1, keepdims=True)
        cnt1 = jnp.sum(oh1, axis=1, keepdims=True)
        a = lax.broadcasted_iota(I32, (tl, tl), 0)
        b = lax.broadcasted_iota(I32, (tl, tl), 1)
        upper = (a < b).astype(BF16)
        pre0 = jnp.dot(oh0.astype(BF16), upper, preferred_element_type=F32)
        pre1 = jnp.dot(oh1.astype(BF16), upper, preferred_element_type=F32)
        base = carry_ref[:, 0:1] + start_ref[:, 0:1]
        slot0 = jnp.sum(oh0 * (pre0 + base), axis=0, keepdims=True)
        slot1 = jnp.sum(oh1 * (pre1 + cnt0 + base), axis=0, keepdims=True)
        row = lax.broadcasted_iota(I32, (8, tl), 0)
        slot_ref[...] = jnp.where(row == 0, slot0.astype(I32), jnp.where(row == 1, slot1.astype(I32), 0))
        carry_ref[...] = carry_ref[...] + cnt0 + cnt1


def _slots(ei, *, tl, bm):
    n = ei.shape[1]
    return pl.pallas_call(
        functools.partial(_slot_kernel, tl=tl, bm=bm),
        grid=(n // tl + 1,),
        in_specs=[pl.BlockSpec((8, n), lambda i: (0, 0)),
                  pl.BlockSpec((8, tl), lambda i: (0, jnp.maximum(i - 1, 0)))],
        out_specs=[pl.BlockSpec((8, tl), lambda i: (0, jnp.maximum(i - 1, 0))),
                   pl.BlockSpec((N_EXPERTS, 128), lambda i: (0, 0))],
        out_shape=[jax.ShapeDtypeStruct((8, n), I32), jax.ShapeDtypeStruct((N_EXPERTS, 128), I32)],
        scratch_shapes=[pltpu.VMEM((N_EXPERTS, 128), F32), pltpu.VMEM((N_EXPERTS, 128), F32)],
        name="moe_slots",
    )(ei, ei)


def _row_copy(src, src_row, dst, dst_row, sem):
    return pltpu.make_async_copy(src.at[pl.ds(src_row, 1)], dst.at[pl.ds(dst_row, 1)], sem)


ISSUE_UNROLL = 8


def _dispatch_kernel(slot_ref, pend_ref, cnt_ref, x1_ref, buf_ref, zero_ref, sem, zsem, *, n, tm, bm):
    base = pl.program_id(0) * tm

    @pl.when(pl.program_id(0) == 0)
    def _():
        zero_ref[...] = jnp.zeros_like(zero_ref)

        def zero_copy(e):
            start = pl.multiple_of(pend_ref[e] - bm, bm)
            return pltpu.make_async_copy(zero_ref, buf_ref.at[pl.ds(start, bm)], zsem)

        def tail_copy(blk):
            return pltpu.make_async_copy(zero_ref, buf_ref.at[pl.ds(pl.multiple_of(blk * bm, bm), bm)], zsem)

        def tail_start(blk, c):
            tail_copy(blk).start()
            return c

        def tail_wait(blk, c):
            tail_copy(blk).wait()
            return c

        for e in range(N_EXPERTS):
            @pl.when(cnt_ref[e] > 0)
            def _(e=e):
                zero_copy(e).start()
        first_unused = pend_ref[N_EXPERTS - 1] // bm
        lax.fori_loop(first_unused, buf_ref.shape[0] // bm, tail_start, 0)
        for e in range(N_EXPERTS):
            @pl.when(cnt_ref[e] > 0)
            def _(e=e):
                zero_copy(e).wait()
        lax.fori_loop(first_unused, buf_ref.shape[0] // bm, tail_wait, 0)

    def body(r, carry):
        for k in range(TOP_K):
            _row_copy(x1_ref, r, buf_ref, slot_ref[k * n + base + r], sem).start()
        return carry

    lax.fori_loop(0, tm, body, 0, unroll=ISSUE_UNROLL)
    for _ in range(TOP_K):
        pltpu.make_async_copy(x1_ref, buf_ref.at[pl.ds(0, tm)], sem).wait()


def _dispatch(slot_flat, pend, counts, x1, *, tm, bm, nblk):
    n = x1.shape[0]
    return pl.pallas_call(
        functools.partial(_dispatch_kernel, n=n, tm=tm, bm=bm),
        grid_spec=pltpu.PrefetchScalarGridSpec(
            num_scalar_prefetch=3,
            grid=(n // tm,),
            in_specs=[pl.BlockSpec((tm, D_MODEL), lambda i, s, p, c: (i, 0))],
            out_specs=pl.BlockSpec(memory_space=pl.ANY),
            scratch_shapes=[pltpu.VMEM((bm, D_MODEL), F32), pltpu.SemaphoreType.DMA(()),
                            pltpu.SemaphoreType.DMA(())],
        ),
        out_shape=jax.ShapeDtypeStruct((nblk * bm, D_MODEL), F32),
        name="moe_dispatch",
    )(slot_flat, pend, counts, x1)


def _expert_kernel(be_ref, nu_ref, nxt_ref, xb_ref, wg_hbm, wu_hbm, wd_hbm, out_ref,
                   wg_bf, wu_bf, wd_bf, wg_f32, wu_f32, wd_f32, run_ref, sem):
    j = pl.program_id(0)
    used = j < nu_ref[0]
    expert = be_ref[j]
    changed = jnp.logical_or(j == 0, expert != be_ref[jnp.maximum(j - 1, 0)])

    def weight_copies(e, slot):
        return [pltpu.make_async_copy(src.at[e], dst.at[slot], sem.at[slot])
                for src, dst in ((wg_hbm, wg_f32), (wu_hbm, wu_f32), (wd_hbm, wd_f32))]

    @pl.when(j == 0)
    def _():
        run_ref[0] = 0
        for c in weight_copies(expert, 0):
            c.start()

    @pl.when(jnp.logical_and(used, changed))
    def _():
        slot = run_ref[0] % 2
        for c in weight_copies(expert, slot):
            c.wait()
        following = nxt_ref[expert]

        @pl.when(following >= 0)
        def _():
            for c in weight_copies(following, 1 - slot):
                c.start()
        wg_bf[...] = wg_f32[slot].astype(BF16)
        wu_bf[...] = wu_f32[slot].astype(BF16)
        wd_bf[...] = wd_f32[slot].astype(BF16)
        run_ref[0] = run_ref[0] + 1

    @pl.when(used)
    def _():
        xb = xb_ref[...].astype(BF16)
        a = jnp.dot(xb, wg_bf[...], preferred_element_type=F32)
        b = jnp.dot(xb, wu_bf[...], preferred_element_type=F32)
        h = (a * _sigmoid(a)) * b
        out_ref[...] = jnp.dot(h.astype(BF16), wd_bf[...], preferred_element_type=F32)

    @pl.when(jnp.logical_not(used))
    def _():
        out_ref[...] = jnp.zeros_like(out_ref)


def _experts(block_expert, n_used, next_expert, buf, w_g, w_u, w_d, *, bm):
    nblk = buf.shape[0] // bm

    def row_map(j, be, nu, nx):
        return (jnp.minimum(j, nu[0] - 1), 0)

    any_spec = pl.BlockSpec(memory_space=pl.ANY)
    return pl.pallas_call(
        _expert_kernel,
        grid_spec=pltpu.PrefetchScalarGridSpec(
            num_scalar_prefetch=3,
            grid=(nblk,),
            in_specs=[pl.BlockSpec((bm, D_MODEL), row_map), any_spec, any_spec, any_spec],
            out_specs=pl.BlockSpec((bm, D_MODEL), lambda j, be, nu, nx: (j, 0)),
            scratch_shapes=[pltpu.VMEM((D_MODEL, D_EXPERT), BF16),
                            pltpu.VMEM((D_MODEL, D_EXPERT), BF16),
                            pltpu.VMEM((D_EXPERT, D_MODEL), BF16),
                            pltpu.VMEM((2, D_MODEL, D_EXPERT), F32),
                            pltpu.VMEM((2, D_MODEL, D_EXPERT), F32),
                            pltpu.VMEM((2, D_EXPERT, D_MODEL), F32),
                            pltpu.SMEM((1,), I32),
                            pltpu.SemaphoreType.DMA((2,))],
        ),
        out_shape=jax.ShapeDtypeStruct((buf.shape[0], D_MODEL), F32),
        compiler_params=_params(),
        name="moe_experts",
    )(block_expert, n_used, next_expert, buf, w_g, w_u, w_d)


def _combine_kernel(slot_ref, x1_ref, gc_ref, g_ref, b_ref, eo_ref, y_ref, side_ref, rows, sem,
                    *, n, tm, alpha, main_tiles):
    i = pl.program_id(0)
    last = pl.num_programs(0) - 1
    cur = i % 2

    def start(tile, buf, r):
        for k in range(TOP_K):
            _row_copy(eo_ref, slot_ref[k * n + tile * tm + r], rows.at[buf, k], r, sem.at[buf]).start(priority=k)

    def wait(buf):
        for k in range(TOP_K):
            pltpu.make_async_copy(eo_ref.at[pl.ds(0, tm)], rows.at[buf, k], sem.at[buf]).wait()

    @pl.when(i == 0)
    def _():
        def body(r, c):
            start(0, 0, r)
            return c
        lax.fori_loop(0, tm, body, 0, unroll=ISSUE_UNROLL)

    wait(cur)
    nxt = jnp.minimum(i + 1, last)
    for r in range(tm):
        start(nxt, 1 - cur, r)
    gc = gc_ref[...]
    z = alpha * x1_ref[...] + gc[:, 0:1] * rows[cur, 0] + gc[:, 1:2] * rows[cur, 1]
    mu = jnp.mean(z, axis=-1, keepdims=True)
    zc = z - mu
    var = jnp.mean(zc * zc, axis=-1, keepdims=True)
    y = zc * lax.rsqrt(var + LN_EPS) * g_ref[...] + b_ref[...]

    @pl.when(i < main_tiles)
    def _():
        y_ref[...] = y

    @pl.when(i == main_tiles)
    def _():
        side_ref[...] = y[0:side_ref.shape[0]]

    @pl.when(i == last)
    def _():
        wait(1 - cur)


def _combine(slot_flat, x1, gate_cols, ln_g, ln_b, expert_out, *, tm, alpha, main_rows, side_rows):
    n = x1.shape[0]
    main_tiles = main_rows // tm
    return pl.pallas_call(
        functools.partial(_combine_kernel, n=n, tm=tm, alpha=alpha, main_tiles=main_tiles),
        grid_spec=pltpu.PrefetchScalarGridSpec(
            num_scalar_prefetch=1,
            grid=(n // tm,),
            in_specs=[pl.BlockSpec((tm, D_MODEL), lambda i, s: (i, 0)),
                      pl.BlockSpec((tm, TOP_K), lambda i, s: (i, 0)),
                      pl.BlockSpec((1, D_MODEL), lambda i, s: (0, 0)),
                      pl.BlockSpec((1, D_MODEL), lambda i, s: (0, 0)),
                      pl.BlockSpec(memory_space=pl.ANY)],
            out_specs=[pl.BlockSpec((tm, D_MODEL), lambda i, s: (jnp.minimum(i, main_tiles - 1), 0)),
                       pl.BlockSpec((side_rows, D_MODEL), lambda i, s: (0, 0))],
            scratch_shapes=[pltpu.VMEM((2, TOP_K, tm, D_MODEL), F32), pltpu.SemaphoreType.DMA((2,))],
        ),
        out_shape=[jax.ShapeDtypeStruct((main_rows, D_MODEL), F32),
                   jax.ShapeDtypeStruct((side_rows, D_MODEL), F32)],
        compiler_params=_params(),
        name="moe_combine",
    )(slot_flat, x1, gate_cols, ln_g, ln_b, expert_out)


def _hier_moe_ln(x1, ei, gt, w_g, w_u, w_d, ln_g, ln_b, *, tl, tm, bm, alpha, main_rows, side_rows):
    n = x1.shape[0]
    m = n * TOP_K
    slot, cnt = _slots(ei, tl=tl, bm=bm)
    counts = cnt[:, 0]
    pend = jnp.cumsum((counts + bm - 1) // bm * bm)
    nblk = (m + N_EXPERTS * (bm - 1) + bm - 1) // bm
    blk_start = jnp.arange(nblk, dtype=I32) * bm
    n_used = (pend[-1] // bm).astype(I32)
    be = jnp.minimum(jnp.sum(pend[None, :] <= blk_start[:, None], axis=1), N_EXPERTS - 1).astype(I32)
    be = jnp.where(jnp.arange(nblk) < n_used, be, jnp.take(be, n_used - 1))
    slot_flat = slot[0:TOP_K].reshape(m)
    buf = _dispatch(slot_flat, pend.astype(I32), counts, x1, tm=tm, bm=bm, nblk=nblk)
    ids = jnp.arange(N_EXPERTS, dtype=I32)
    first_at_or_after = lax.cummin(jnp.where(counts > 0, ids, N_EXPERTS), reverse=True)
    next_expert = jnp.concatenate([first_at_or_after[1:], jnp.full((1,), N_EXPERTS, I32)])
    next_expert = jnp.where(next_expert == N_EXPERTS, -1, next_expert).astype(I32)
    eo = _experts(be, n_used.reshape(1), next_expert, buf, w_g, w_u, w_d, bm=bm)
    gate_cols = gt[0:TOP_K].T
    return _combine(slot_flat, x1, gate_cols, ln_g, ln_b, eo, tm=tm, alpha=alpha,
                    main_rows=main_rows, side_rows=side_rows)


def _t5_bucket(n):
    nf = jnp.maximum(n, 1).astype(F32)
    large = MAX_EXACT + (jnp.log(nf / MAX_EXACT) / math.log(MAX_DISTANCE / MAX_EXACT)
                         * (N_BUCKETS - MAX_EXACT)).astype(I32)
    large = jnp.minimum(large, N_BUCKETS - 1)
    return jnp.where(n < MAX_EXACT, n, large)


def _bias_per_group(rel_bias):
    offs = jnp.arange(N_KEYS, dtype=I32)[None, :] * jnp.array(DILATIONS, I32)[:, None]
    bucket = _t5_bucket(offs)
    table = rel_bias.reshape(N_BUCKETS, N_GROUPS, HEADS)
    b = table[bucket, jnp.arange(N_GROUPS)[:, None]]
    return jnp.transpose(b, (0, 2, 1)).astype(F32)


def _prompt_bias_tables(bias):
    width = 3 * Q_BLOCK
    neg = jnp.full((N_GROUPS, HEADS, Q_BLOCK - 1), NEG_INF, F32)
    r = jnp.concatenate([neg, bias[:, :, ::-1], neg, jnp.full((N_GROUPS, HEADS, 1), NEG_INF, F32)], axis=-1)
    flat = jnp.tile(r, (1, 1, Q_BLOCK))[:, :, :Q_BLOCK * (width - 1)]
    skew = flat.reshape(N_GROUPS, HEADS, Q_BLOCK, width - 1)
    later = skew[:, :, :, Q_BLOCK - 1:3 * Q_BLOCK - 1]
    has_prev = (np.arange(2 * Q_BLOCK) >= Q_BLOCK)[None, None, None, :]
    first = jnp.where(has_prev, later, NEG_INF)
    tb = jnp.stack([first, later], axis=1)
    return tb.reshape(N_GROUPS, 2, HEADS // 2, 2 * Q_BLOCK, 2 * Q_BLOCK)


def _sample_bias_tables(bias):
    t = np.arange(T_NEW)
    bcs = []
    for g, d in enumerate(DILATIONS):
        rev = bias[g][:, ::-1][:, :WINDOW_KEYS]
        if d == 1:
            dist = np.arange(WINDOW_KEYS)[None, :] - t[:, None]
            vals = jnp.take(rev, np.clip(dist, 0, WINDOW_KEYS - 1), axis=1)
            bcs.append(jnp.where(dist[None] >= 0, vals, NEG_INF))
        else:
            cls = np.arange(d)[None, :] == t[:, None]
            table = jnp.where(cls[None, :, None, :], rev[:, None, :, None], NEG_INF)
            bcs.append(table.reshape(HEADS, T_NEW, WINDOW_KEYS * d))
    back = t[:, None] - t[None, :]
    vals = jnp.take(bias, np.clip(back, 0, T_NEW - 1), axis=2)
    ok = np.stack([(back >= 0) if d == 1 else (back == 0) for d in DILATIONS])
    return bcs, jnp.where(ok[:, None], vals, NEG_INF)


def _split_bf16(w):
    hi = w.astype(BF16)
    lo = (w - hi.astype(F32)).astype(BF16)
    return hi, lo


def kernel(x_prompt, x_sample, cache_attn_w128, cache_attn_w512, cache_attn_w2048, state_conv, rel_bias, w_in, w_conv, w_pa, w_pb, w_o, ln1_g, ln1_b, w_router_group, w_router_expert, w_expert_gate, w_expert_up, w_expert_down, ln2_g, ln2_b):
    depth = w_in.shape[0]
    assert depth == 1 and x_prompt.shape[0] == 1
    alpha = (2.0 * depth) ** 0.25
    s = x_prompt.shape[1]
    bd, t_len = x_sample.shape[0], x_sample.shape[1]
    assert t_len == T_NEW and s % (DILATIONS[-1] * Q_BLOCK * Q_BLOCKS_PER_STEP) == 0
    assert s % PROJ_TILE == 0 and s % MIX_TILE == 0 and s % MOE_TILE == 0
    assert (s + MIX_TILE) % SLOT_TILE == 0 and (s + MIX_TILE) % MOE_TILE == 0

    bias = _bias_per_group(rel_bias)
    tb = _prompt_bias_tables(bias)
    bcs, bn = _sample_bias_tables(bias)

    w_in_bf = w_in[0].astype(BF16)
    w_pa_bf = w_pa[0].astype(BF16)
    w_pb_bf = w_pb[0].astype(BF16)
    w_o_bf = w_o[0].astype(BF16)
    wr = jnp.zeros((ROUTER_ROWS, D_MODEL), F32)
    wr = wr.at[0:N_EXPERT_GROUPS].set(w_router_group[0].T).at[8:8 + N_EXPERTS].set(w_router_expert[0].T)
    wr_hi, wr_lo = _split_bf16(wr)
    g1, b1 = ln1_g[0][None], ln1_b[0][None]
    g2, b2 = ln2_g[0][None], ln2_b[0][None]
    wg, wu, wd = w_expert_gate[0], w_expert_up[0], w_expert_down[0]

    xp = x_prompt[0]
    kv_tail = min(MAX_DISTANCE, s)
    q, kb, vb, (k32, v32, yb, sga, sgb, ut) = _proj(
        xp, w_in_bf, w_conv[0], None, tm=PROJ_TILE, u_tail=8, q_dtype=BF16, dils=DILATIONS, kv_f32=True)
    k32, v32 = k32[s - kv_tail:], v32[s - kv_tail:]
    o_l = [_attn_prompt_group(q[g], kb[g], vb[g], tb[g], g) for g in range(N_GROUPS)]
    routed = _mix(xp, [a[0] for a in o_l], [a[1] for a in o_l], yb, sga, sgb,
                  w_pa_bf, w_pb_bf, w_o_bf, g1, b1, wr_hi, wr_lo, tm=MIX_TILE, alpha=alpha, dils=DILATIONS,
                  extra_tiles=1)

    kv_prompt = []
    for g, d in enumerate(DILATIONS):
        length = min(WINDOW_KEYS * d, s)
        cols = slice(g * GROUP_WIDTH, (g + 1) * GROUP_WIDTH)
        kg = k32[kv_tail - length:, cols].reshape(length, HEADS, HEAD_DIM)
        vg = v32[kv_tail - length:, cols].reshape(length, HEADS, HEAD_DIM)
        kv_prompt.append(jnp.stack([kg, vg], axis=1)[None, None])
    conv_prompt = ut[6:8][None, None]

    ns = bd * t_len
    xs = x_sample.reshape(ns, D_MODEL)
    st = state_conv[0]
    s0 = jnp.repeat(st[:, 0], t_len, axis=0)
    s1 = jnp.repeat(st[:, 1], t_len, axis=0)
    qs, _, _, (k32s, v32s, ybs, sgas, sgbs, us) = _proj(
        xs, w_in_bf, w_conv[0], (s0, s1), tm=ns, u_tail=ns, q_dtype=F32, dils=NO_DILATION, kv_f32=True)
    qs = jnp.concatenate([a[0] for a in qs], axis=1)
    packed = jnp.stack([qs, k32s, v32s]).reshape(3, bd, t_len, N_GROUPS, GROUP_WIDTH)
    qkv_t = jnp.transpose(packed, (1, 4, 0, 3, 2)).reshape(bd, GROUP_WIDTH, 3 * N_GROUPS * t_len)
    qkv_t = jnp.pad(qkv_t, ((0, 0), (0, 0), (0, 128 - 3 * N_GROUPS * t_len)))
    caches = (cache_attn_w128[0], cache_attn_w512[0], cache_attn_w2048[0])
    caches_t = [jnp.transpose(c, (0, 2, 3, 4, 1)).reshape(bd, 2, GROUP_WIDTH, c.shape[1]) for c in caches]
    pair = (bd, t_len, N_GROUPS, HEADS // 2, 2, HEAD_DIM)
    q6 = jnp.transpose(qs.reshape(pair), (0, 3, 2, 4, 1, 5))
    zeros = jnp.zeros_like(q6[:, :, :, 0])
    qbd = jnp.stack([jnp.concatenate([q6[:, :, :, 0], zeros], axis=-1),
                     jnp.concatenate([zeros, q6[:, :, :, 1]], axis=-1)], axis=3)
    qbd = qbd.reshape(bd, HEADS // 2, N_GROUPS, 2 * t_len, 128)
    new_rows = jnp.stack([k32s, v32s]).reshape(2, bd, t_len, N_GROUPS, HEADS // 2, 128)
    new_rows = jnp.transpose(new_rows, (1, 4, 3, 0, 2, 5))
    bcs = [t.reshape(HEADS // 2, 2 * t_len, t.shape[-1]) for t in bcs]
    bn = bn.reshape(N_GROUPS, HEADS // 2, 2 * t_len, t_len)
    n0, n1, n2, o_s, lse_s = _sample_cache(qbd, new_rows, qkv_t, caches_t, bcs, bn)

    def unpack(a):
        a = a[:, :N_GROUPS * t_len].reshape(bd, N_GROUPS, t_len, GROUP_WIDTH)
        return jnp.transpose(a, (1, 0, 2, 3)).reshape(N_GROUPS, 1, ns, GROUP_WIDTH)

    o_s, lse_s = unpack(o_s), unpack(lse_s)
    assert ns <= min(MIX_TILE, MOE_TILE) and s % ns == 0
    x1, ei, gt = _mix(xs, o_s, lse_s, ybs, sgas, sgbs, w_pa_bf, w_pb_bf, w_o_bf, g1, b1, wr_hi, wr_lo,
                      tm=ns, alpha=alpha, dils=NO_DILATION, into=routed, into_tile=s // ns)
    y_prompt, y_sample = _hier_moe_ln(x1, ei, gt, wg, wu, wd, g2, b2, tl=SLOT_TILE, tm=MOE_TILE, bm=EXPERT_BLOCK,
                                      alpha=alpha, main_rows=s, side_rows=ns)
    y_prompt = y_prompt[None]
    y_sample = y_sample.reshape(bd, t_len, D_MODEL)

    kv_sample = [jnp.transpose(c.reshape(bd, 2, HEADS, HEAD_DIM, c.shape[-1]), (0, 4, 1, 2, 3))[None]
                 for c in (n0, n1, n2)]
    conv_sample = us.reshape(bd, t_len, CONV_CHANNELS)[:, t_len - 2:][None]

    return (y_prompt, y_sample, kv_prompt[0], kv_prompt[1], kv_prompt[2], conv_prompt,
            kv_sample[0], kv_sample[1], kv_sample[2], conv_sample)
```

```python
import functools
import math

import numpy as np
import jax
import jax.numpy as jnp
from jax import lax
from jax.experimental import pallas as pl
from jax.experimental.pallas import tpu as pltpu

F32 = jnp.float32
BF16 = jnp.bfloat16
I32 = jnp.int32

D_MODEL = 1024
N_GROUPS = 3
HEADS = 8
HEAD_DIM = 64
GROUP_WIDTH = HEADS * HEAD_DIM
ATTN_WIDTH = N_GROUPS * GROUP_WIDTH
DILATIONS = (1, 4, 16)
NO_DILATION = (1, 1, 1)
WINDOW_KEYS = 128
N_KEYS = WINDOW_KEYS + 1
N_BUCKETS = 32
MAX_EXACT = 16
MAX_DISTANCE = 2048
CONV_CHANNELS = 512
N_EXPERT_GROUPS = 4
EXPERTS_PER_GROUP = 8
N_EXPERTS = 32
TOP_K = 2
D_EXPERT = 512
LN_EPS = 1e-5
PROJ_WIDTH = 3 * ATTN_WIDTH + 3 * CONV_CHANNELS + 2 * D_MODEL
ROUTER_ROWS = 8 + N_EXPERTS
Q_BLOCK = 128
T_NEW = 4
NEG_INF = float("-inf")
VMEM_LIMIT = 56 * 1024 * 1024

PROJ_TILE = 256
MIX_TILE = 512
SLOT_TILE = 512
MOE_TILE = 512
EXPERT_BLOCK = 512


def _sigmoid(x):
    return 1.0 / (1.0 + jnp.exp(-x))


def _params(limit=VMEM_LIMIT):
    return pltpu.CompilerParams(vmem_limit_bytes=limit)


def _proj_kernel(*refs, tm, tail_rows, sample_mode, dils, kv_f32):
    n_in = 5 if sample_mode else 3
    x_ref, w_ref, wc_ref = refs[0:3]
    outs = list(refs[n_in:])
    q_refs, k_refs, v_refs = outs[0:3], outs[3:6], outs[6:9]
    del outs[0:9]
    k32_ref, v32_ref = (outs.pop(0), outs.pop(0)) if kv_f32 else (None, None)
    yb_ref, sga_ref, sgb_ref, ut_ref, cls_ref = outs[0:5]
    xb = x_ref[...].astype(BF16)

    def col(c0, width):
        return jnp.dot(xb, w_ref[:, c0:c0 + width], preferred_element_type=F32)

    def write_classes(val, group_refs):
        for g, d in enumerate(dils):
            part = val[:, g * GROUP_WIDTH:(g + 1) * GROUP_WIDTH]
            ref = group_refs[g]
            if d == 1:
                ref[0] = part.astype(ref.dtype)
            else:
                for kk in range(GROUP_WIDTH // 128):
                    lanes = slice(kk * 128, (kk + 1) * 128)
                    cls_ref[kk] = part[:, lanes]
                    for c in range(d):
                        ref[c, :, lanes] = cls_ref[kk, pl.ds(c, tm // d, stride=d), :].astype(ref.dtype)

    write_classes(col(0, ATTN_WIDTH), q_refs)
    k = col(ATTN_WIDTH, ATTN_WIDTH)
    v = col(2 * ATTN_WIDTH, ATTN_WIDTH)
    if kv_f32:
        k32_ref[...] = k
        v32_ref[...] = v
    write_classes(k, k_refs)
    write_classes(v, v_refs)

    c0 = 3 * ATTN_WIDTH
    bg = col(c0, CONV_CHANNELS)
    u = col(c0 + CONV_CHANNELS, CONV_CHANNELS) * col(c0 + 2 * CONV_CHANNELS, CONV_CHANNELS)
    row = lax.broadcasted_iota(I32, (tm, CONV_CHANNELS), 0)
    r1 = pltpu.roll(u, 1, axis=0)
    r2 = pltpu.roll(u, 2, axis=0)
    if sample_mode:
        s0 = refs[3][...]
        s1 = refs[4][...]
        t = row & (T_NEW - 1)
        prev1 = jnp.where(t == 0, s1, r1)
        prev2 = jnp.where(t == 0, s0, jnp.where(t == 1, s1, r2))
    else:
        carry_ref = outs[5]

        @pl.when(pl.program_id(0) == 0)
        def _():
            carry_ref[...] = jnp.zeros_like(carry_ref)
        c6 = carry_ref[6:7, :]
        c7 = carry_ref[7:8, :]
        prev1 = jnp.where(row == 0, c7, r1)
        prev2 = jnp.where(row == 0, c6, jnp.where(row == 1, c7, r2))
        carry_ref[...] = u[tm - 8:tm, :]
    conv = prev2 * wc_ref[0:1, :] + prev1 * wc_ref[1:2, :] + u * wc_ref[2:3, :]
    yb_ref[...] = (bg * conv).astype(BF16)
    ut_ref[...] = u[tm - tail_rows:tm, :]

    c1 = c0 + 3 * CONV_CHANNELS
    sga_ref[...] = _sigmoid(col(c1, D_MODEL)).astype(BF16)
    sgb_ref[...] = _sigmoid(col(c1 + D_MODEL, D_MODEL)).astype(BF16)


def _proj(x, w_in_bf, w_conv, conv_prev, *, tm, u_tail, q_dtype, dils, kv_f32):
    n = x.shape[0]
    sample_mode = conv_prev is not None
    nt = n // tm

    def row_spec(width):
        return pl.BlockSpec((tm, width), lambda i: (i, 0))

    def class_spec(d):
        return pl.BlockSpec((d, tm // d, GROUP_WIDTH), lambda i: (0, i, 0))

    def class_shape(d, dtype):
        return jax.ShapeDtypeStruct((d, n // d, GROUP_WIDTH), dtype)

    in_specs = [
        row_spec(D_MODEL),
        pl.BlockSpec((D_MODEL, PROJ_WIDTH), lambda i: (0, 0), pipeline_mode=pl.Buffered(1)),
        pl.BlockSpec((3, CONV_CHANNELS), lambda i: (0, 0)),
    ]
    args = [x, w_in_bf, w_conv]
    scratch = [pltpu.VMEM((GROUP_WIDTH // 128, tm, 128), F32)]
    if sample_mode:
        in_specs += [row_spec(CONV_CHANNELS), row_spec(CONV_CHANNELS)]
        args += [conv_prev[0], conv_prev[1]]
    else:
        scratch.append(pltpu.VMEM((8, CONV_CHANNELS), F32))
    out_shape = (
        [class_shape(d, q_dtype) for d in dils] + [class_shape(d, BF16) for d in dils] * 2
        + [jax.ShapeDtypeStruct((n, ATTN_WIDTH), F32)] * (2 if kv_f32 else 0)
        + [jax.ShapeDtypeStruct((n, CONV_CHANNELS), BF16),
           jax.ShapeDtypeStruct((n, D_MODEL), BF16),
           jax.ShapeDtypeStruct((n, D_MODEL), BF16),
           jax.ShapeDtypeStruct((u_tail, CONV_CHANNELS), F32)])
    out_specs = (
        [class_spec(d) for d in dils] * 3
        + [row_spec(ATTN_WIDTH)] * (2 if kv_f32 else 0)
        + [row_spec(CONV_CHANNELS), row_spec(D_MODEL), row_spec(D_MODEL),
           pl.BlockSpec((u_tail, CONV_CHANNELS), lambda i: (0, 0))])
    res = pl.pallas_call(
        functools.partial(_proj_kernel, tm=tm, tail_rows=u_tail, sample_mode=sample_mode, dils=dils, kv_f32=kv_f32),
        grid=(nt,),
        in_specs=in_specs,
        out_specs=out_specs,
        out_shape=out_shape,
        scratch_shapes=scratch,
        compiler_params=_params(),
        name="proj",
    )(*args)
    return res[0:3], res[3:6], res[6:9], res[9:]


Q_BLOCKS_PER_STEP = 8


def _attn_kernel(q_ref, kp_ref, kc_ref, vp_ref, vc_ref, tb_ref, o_ref, lse_ref):
    lane = lax.broadcasted_iota(I32, (Q_BLOCK, 128), 1)
    first = lane < HEAD_DIM
    scale = HEAD_DIM ** -0.5
    has_prev = jnp.minimum(pl.program_id(1), 1)
    for sub in range(Q_BLOCKS_PER_STEP):
        rows = slice(sub * Q_BLOCK, (sub + 1) * Q_BLOCK)
        band = slice((sub - 1) * Q_BLOCK, (sub + 1) * Q_BLOCK)
        for pr in range(HEADS // 2):
            sl = slice(pr * 128, (pr + 1) * 128)
            if sub == 0:
                k = jnp.concatenate([kp_ref[:, sl], kc_ref[rows, sl]], axis=0)
                v = jnp.concatenate([vp_ref[:, sl], vc_ref[rows, sl]], axis=0)
                bias = tb_ref[has_prev, pr]
            else:
                k = kc_ref[band, sl]
                v = vc_ref[band, sl]
                bias = tb_ref[1, pr]
            qf = q_ref[rows, sl].astype(F32) * scale
            qq = jnp.concatenate([jnp.where(first, qf, 0.0), jnp.where(first, 0.0, qf)], axis=0).astype(BF16)
            s = lax.dot_general(qq, k, (((1,), (1,)), ((), ())), preferred_element_type=F32) + bias
            m = jnp.max(s, axis=-1, keepdims=True)
            p = jnp.exp(s - m)
            l = jnp.sum(p, axis=-1, keepdims=True)
            o = jnp.dot(p.astype(BF16), v, preferred_element_type=F32) / l
            lse = m + jnp.log(l)
            o_ref[rows, sl] = jnp.where(first, o[:Q_BLOCK], o[Q_BLOCK:]).astype(o_ref.dtype)
            lse_ref[rows, sl] = jnp.where(first, jnp.broadcast_to(lse[:Q_BLOCK], (Q_BLOCK, 128)),
                                          jnp.broadcast_to(lse[Q_BLOCK:], (Q_BLOCK, 128)))


def _attn_prompt_group(q, kb, vb, tb, g):
    d, rows = q.shape[0], q.shape[1]
    nq = Q_BLOCKS_PER_STEP
    cur = pl.BlockSpec((None, nq * Q_BLOCK, GROUP_WIDTH), lambda c, i: (c, i, 0))
    prev = pl.BlockSpec((None, Q_BLOCK, GROUP_WIDTH), lambda c, i: (c, jnp.maximum(i * nq - 1, 0), 0))
    return pl.pallas_call(
        _attn_kernel,
        grid=(d, rows // (nq * Q_BLOCK)),
        in_specs=[cur, prev, cur, prev, cur, pl.BlockSpec(tb.shape, lambda c, i: (0, 0, 0, 0))],
        out_specs=[cur, cur],
        out_shape=[jax.ShapeDtypeStruct((d, rows, GROUP_WIDTH), BF16),
                   jax.ShapeDtypeStruct((d, rows, GROUP_WIDTH), F32)],
        compiler_params=_params(),
        name=f"attn_prompt_g{g}",
    )(q, kb, kb, vb, vb, tb)


PACK_Q, PACK_K, PACK_V = 0, N_GROUPS * T_NEW, 2 * N_GROUPS * T_NEW


PAIRS_PER_STEP = 2


def _sample_cache_kernel(qbd_ref, nr_ref, qkv_ref, c0_ref, c1_ref, c2_ref, b0_ref, b1_ref, b2_ref, bn_ref,
                         n0_ref, n1_ref, n2_ref, o_ref, lse_ref):
    for pp in range(PAIRS_PER_STEP):
        rows = pl.ds(pp * 128, 128)
        kv = pl.ds(0, 2)
        _sample_pair(qbd_ref.at[pp], nr_ref.at[pp], qkv_ref.at[rows],
                     [c.at[kv, rows] for c in (c0_ref, c1_ref, c2_ref)],
                     [b.at[pp] for b in (b0_ref, b1_ref, b2_ref)], bn_ref.at[pl.ds(0, N_GROUPS), pp],
                     [c.at[kv, rows] for c in (n0_ref, n1_ref, n2_ref)],
                     o_ref.at[pl.ds(0, 16), rows], lse_ref.at[pl.ds(0, 16), rows])


def _sample_pair(qbd_ref, nr_ref, qkv_ref, c_refs, b_refs, bn_ref, n_refs, o_ref, lse_ref):
    scale = HEAD_DIM ** -0.5
    nt = (((1,), (1,)), ((), ()))
    lane = lax.broadcasted_iota(I32, (128, 128), 1)
    head0 = lane[0:T_NEW] < HEAD_DIM
    for g, (c_ref, b_ref, n_ref) in enumerate(zip(c_refs, b_refs, n_refs)):
        length = c_ref.shape[-1]
        qbd = qbd_ref[g]
        s_c = jnp.dot(qbd.astype(BF16), c_ref[0].astype(BF16), preferred_element_type=F32) * scale + b_ref[...]
        bn = bn_ref[g]
        s_n = [jnp.sum(qbd * nr_ref[g, 0, tn:tn + 1, :], axis=1, keepdims=True) * scale + bn[:, tn:tn + 1]
               for tn in range(T_NEW)]
        m = jnp.max(s_c, axis=1, keepdims=True)
        for x in s_n:
            m = jnp.maximum(m, x)
        p_c = jnp.exp(s_c - m)
        p_n = [jnp.exp(x - m) for x in s_n]
        l = jnp.sum(p_c, axis=1, keepdims=True)
        acc = lax.dot_general(p_c.astype(BF16), c_ref[1].astype(BF16), nt, preferred_element_type=F32)
        for tn in range(T_NEW):
            l = l + p_n[tn]
            acc = acc + p_n[tn] * nr_ref[g, 1, tn:tn + 1, :]
        o = acc / l
        lse = jnp.broadcast_to(m + jnp.log(l), (2 * T_NEW, 128))
        o_ref[g * T_NEW:(g + 1) * T_NEW, :] = jnp.where(head0, o[0:T_NEW], o[T_NEW:])
        lse_ref[g * T_NEW:(g + 1) * T_NEW, :] = jnp.where(head0, lse[0:T_NEW], lse[T_NEW:])

        for kv, pack in ((0, PACK_K), (1, PACK_V)):
            rolled = pltpu.roll(c_ref[kv], length - T_NEW, axis=1)
            new_tail = pltpu.roll(qkv_ref[...], 128 - T_NEW - (pack + g * T_NEW), axis=1)
            tail = jnp.where(lane >= 128 - T_NEW, new_tail, rolled[:, length - 128:])
            if length > 128:
                n_ref[kv, :, 0:length - 128] = rolled[:, 0:length - 128]
            n_ref[kv, :, length - 128:] = tail
    pad_rows = slice(N_GROUPS * T_NEW, 16)
    o_ref[pad_rows, :] = jnp.zeros((16 - N_GROUPS * T_NEW, 128), F32)
    lse_ref[pad_rows, :] = jnp.zeros((16 - N_GROUPS * T_NEW, 128), F32)


def _sample_cache(qbd, new_rows, qkv_t, caches_t, bcs, bn):
    b = qkv_t.shape[0]

    pp = PAIRS_PER_STEP

    def cache_spec(c):
        return pl.BlockSpec((None, 2, pp * 128, c.shape[-1]), lambda i, h: (i, 0, h, 0))

    def bias_spec(t):
        return pl.BlockSpec((pp, 2 * T_NEW, t.shape[-1]), lambda i, h: (h, 0, 0))

    out = pl.BlockSpec((None, 16, pp * 128), lambda i, h: (i, 0, h))
    return pl.pallas_call(
        _sample_cache_kernel,
        grid=(b, HEADS // 2 // pp),
        in_specs=[pl.BlockSpec((None, pp, N_GROUPS, 2 * T_NEW, 128), lambda i, h: (i, h, 0, 0, 0)),
                  pl.BlockSpec((None, pp, N_GROUPS, 2, T_NEW, 128), lambda i, h: (i, h, 0, 0, 0, 0)),
                  pl.BlockSpec((None, pp * 128, 128), lambda i, h: (i, h, 0))]
                 + [cache_spec(c) for c in caches_t] + [bias_spec(t) for t in bcs]
                 + [pl.BlockSpec((N_GROUPS, pp, 2 * T_NEW, T_NEW), lambda i, h: (0, h, 0, 0))],
        out_specs=[cache_spec(c) for c in caches_t] + [out, out],
        out_shape=[jax.ShapeDtypeStruct(c.shape, c.dtype) for c in caches_t]
                  + [jax.ShapeDtypeStruct((b, 16, GROUP_WIDTH), F32)] * 2,
        compiler_params=_params(),
        name="sample_cache",
    )(qbd, new_rows, qkv_t, *caches_t, *bcs, bn)


def _mix_kernel(*refs, tm, alpha, dils, n_alias):
    (x_ref, o0_ref, o1_ref, o2_ref, l0_ref, l1_ref, l2_ref, yb_ref, sga_ref, sgb_ref,
     wpa_ref, wpb_ref, wo_ref, g_ref, b_ref, wrh_ref, wrl_ref) = refs[0:17]
    x1_ref, ei_ref, gt_ref = refs[17 + n_alias:20 + n_alias]
    scratch = list(refs[20 + n_alias:])

    def natural(ref, d):
        if d == 1:
            return ref[0].astype(F32)
        scr = scratch.pop()
        for kk in range(GROUP_WIDTH // 128):
            for c in range(d):
                scr[kk, pl.ds(c, tm // d, stride=d), :] = ref[c, :, kk * 128:(kk + 1) * 128].astype(F32)
        return jnp.concatenate([scr[kk] for kk in range(GROUP_WIDTH // 128)], axis=1)

    l0, l1, l2 = natural(l0_ref, dils[0]), natural(l1_ref, dils[1]), natural(l2_ref, dils[2])
    mx = jnp.maximum(jnp.maximum(l0, l1), l2)
    e0 = jnp.exp(l0 - mx)
    e1 = jnp.exp(l1 - mx)
    e2 = jnp.exp(l2 - mx)
    ya = (e0 * natural(o0_ref, dils[0]) + e1 * natural(o1_ref, dils[1]) + e2 * natural(o2_ref, dils[2])) / (e0 + e1 + e2)
    pa = jnp.dot(ya.astype(BF16), wpa_ref[...], preferred_element_type=F32)
    pb = jnp.dot(yb_ref[...], wpb_ref[...], preferred_element_type=F32)
    gated = sga_ref[...].astype(F32) * pa + sgb_ref[...].astype(F32) * pb
    mix = jnp.dot(gated.astype(BF16), wo_ref[...], preferred_element_type=F32)
    z = alpha * x_ref[...] + mix
    mu = jnp.mean(z, axis=-1, keepdims=True)
    zc = z - mu
    var = jnp.mean(zc * zc, axis=-1, keepdims=True)
    x1 = zc * lax.rsqrt(var + LN_EPS) * g_ref[...] + b_ref[...]
    x1_ref[...] = x1

    xh = x1.astype(BF16)
    xl = (x1 - xh.astype(F32)).astype(BF16)
    nt = (((1,), (1,)), ((), ()))
    wrh = wrh_ref[...]
    lt = (lax.dot_general(wrh, xh, nt, preferred_element_type=F32)
          + lax.dot_general(wrh, xl, nt, preferred_element_type=F32)
          + lax.dot_general(wrl_ref[...], xh, nt, preferred_element_type=F32))

    gl = lt[0:N_EXPERT_GROUPS]
    gmax = jnp.max(gl, axis=0, keepdims=True)
    idx4 = lax.broadcasted_iota(I32, (N_EXPERT_GROUPS, tm), 0)
    g_idx = jnp.min(jnp.where(gl == gmax, idx4, N_EXPERT_GROUPS), axis=0, keepdims=True)
    g_prob = 1.0 / jnp.sum(jnp.exp(gl - gmax), axis=0, keepdims=True)
    e_sel = lt[8:16]
    for grp in range(1, N_EXPERT_GROUPS):
        e_sel = jnp.where(g_idx == grp, lt[8 + 8 * grp:16 + 8 * grp], e_sel)
    idx8 = lax.broadcasted_iota(I32, (EXPERTS_PER_GROUP, tm), 0)
    v1 = jnp.max(e_sel, axis=0, keepdims=True)
    i1 = jnp.min(jnp.where(e_sel == v1, idx8, EXPERTS_PER_GROUP), axis=0, keepdims=True)
    rest = jnp.where(idx8 == i1, NEG_INF, e_sel)
    v2 = jnp.max(rest, axis=0, keepdims=True)
    i2 = jnp.min(jnp.where(rest == v2, idx8, EXPERTS_PER_GROUP), axis=0, keepdims=True)
    r = jnp.exp(v2 - v1)
    gate1 = g_prob / (1.0 + r)
    gate2 = g_prob * r / (1.0 + r)
    ex1 = g_idx * EXPERTS_PER_GROUP + i1
    ex2 = g_idx * EXPERTS_PER_GROUP + i2
    ei_ref[...] = jnp.where(idx8 == 0, ex1, jnp.where(idx8 == 1, ex2, 0))
    gt_ref[...] = jnp.where(idx8 == 0, gate1, jnp.where(idx8 == 1, gate2, 0.0))


def _mix(x, o, lse, yb, sga, sgb, w_pa, w_pb, w_o, ln_g, ln_b, wr_hi, wr_lo, *, tm, alpha, dils,
         extra_tiles=0, into=None, into_tile=0):
    n = x.shape[0]
    nt = n // tm

    def src(i):
        return jnp.minimum(i, nt - 1)

    def row_spec(width):
        return pl.BlockSpec((tm, width), lambda i: (src(i), 0))

    def class_spec(d):
        return pl.BlockSpec((d, tm // d, GROUP_WIDTH), lambda i: (0, src(i), 0))

    def full(a):
        return pl.BlockSpec(a.shape, lambda i: (0,) * a.ndim)

    n_alias = 0 if into is None else 3
    rows_out = n + extra_tiles * tm if into is None else into[0].shape[0]
    any_spec = pl.BlockSpec(memory_space=pl.ANY)
    n_scratch = 2 * sum(1 for d in dils if d > 1)
    args = [x, o[0], o[1], o[2], lse[0], lse[1], lse[2], yb, sga, sgb, w_pa, w_pb, w_o, ln_g, ln_b, wr_hi, wr_lo]
    return pl.pallas_call(
        functools.partial(_mix_kernel, tm=tm, alpha=alpha, dils=dils, n_alias=n_alias),
        grid=(nt + extra_tiles,),
        in_specs=[row_spec(D_MODEL)] + [class_spec(d) for d in dils] * 2 + [row_spec(CONV_CHANNELS)]
                 + [row_spec(D_MODEL)] * 2
                 + [full(w_pa), full(w_pb), full(w_o), full(ln_g), full(ln_b), full(wr_hi), full(wr_lo)]
                 + [any_spec] * n_alias,
        out_specs=[pl.BlockSpec((tm, D_MODEL), lambda i: (i + into_tile, 0)),
                   pl.BlockSpec((8, tm), lambda i: (0, i + into_tile)),
                   pl.BlockSpec((8, tm), lambda i: (0, i + into_tile))],
        out_shape=[jax.ShapeDtypeStruct((rows_out, D_MODEL), F32),
                   jax.ShapeDtypeStruct((8, rows_out), I32),
                   jax.ShapeDtypeStruct((8, rows_out), F32)],
        input_output_aliases={len(args) + k: k for k in range(n_alias)},
        scratch_shapes=[pltpu.VMEM((GROUP_WIDTH // 128, tm, 128), F32)] * n_scratch,
        compiler_params=_params(),
        name="mix",
    )(*args, *(into or ()))


def _slot_kernel(all_ref, ei_ref, slot_ref, cnt_ref, carry_ref, start_ref, *, tl, bm):
    step = pl.program_id(0)

    def one_hot(rows):
        ex = lax.broadcasted_iota(I32, (N_EXPERTS, rows.shape[1]), 0)
        return (ex == rows).astype(F32)

    @pl.when(step == 0)
    def _():
        total = (jnp.sum(one_hot(all_ref[0:1, :]), axis=1, keepdims=True)
                 + jnp.sum(one_hot(all_ref[1:2, :]), axis=1, keepdims=True))
        total = jnp.broadcast_to(total, (N_EXPERTS, 128))
        cnt_ref[...] = total.astype(I32)
        blocks = jnp.floor((total + (bm - 1)) * (1.0 / bm))
        a = lax.broadcasted_iota(I32, (N_EXPERTS, N_EXPERTS), 0)
        b = lax.broadcasted_iota(I32, (N_EXPERTS, N_EXPERTS), 1)
        before = (b < a).astype(BF16)
        start_ref[...] = jnp.dot(before, blocks.astype(BF16), preferred_element_type=F32) * bm
        carry_ref[...] = jnp.zeros_like(carry_ref)
        slot_ref[...] = jnp.zeros_like(slot_ref)

    @pl.when(step > 0)
    def _():
        oh0 = one_hot(ei_ref[0:1, :])
        oh1 = one_hot(ei_ref[1:2, :])
        cnt0 = jnp.sum(oh0, axis=1, keepdims=True)
        cnt1 = jnp.sum(oh1, axis=1, keepdims=True)
        a = lax.broadcasted_iota(I32, (tl, tl), 0)
        b = lax.broadcasted_iota(I32, (tl, tl), 1)
        upper = (a < b).astype(BF16)
        pre0 = jnp.dot(oh0.astype(BF16), upper, preferred_element_type=F32)
        pre1 = jnp.dot(oh1.astype(BF16), upper, preferred_element_type=F32)
        base = carry_ref[:, 0:1] + start_ref[:, 0:1]
        slot0 = jnp.sum(oh0 * (pre0 + base), axis=0, keepdims=True)
        slot1 = jnp.sum(oh1 * (pre1 + cnt0 + base), axis=0, keepdims=True)
        row = lax.broadcasted_iota(I32, (8, tl), 0)
        slot_ref[...] = jnp.where(row == 0, slot0.astype(I32), jnp.where(row == 1, slot1.astype(I32), 0))
        carry_ref[...] = carry_ref[...] + cnt0 + cnt1


def _slots(ei, *, tl, bm):
    n = ei.shape[1]
    return pl.pallas_call(
        functools.partial(_slot_kernel, tl=tl, bm=bm),
        grid=(n // tl + 1,),
        in_specs=[pl.BlockSpec((8, n), lambda i: (0, 0)),
                  pl.BlockSpec((8, tl), lambda i: (0, jnp.maximum(i - 1, 0)))],
        out_specs=[pl.BlockSpec((8, tl), lambda i: (0, jnp.maximum(i - 1, 0))),
                   pl.BlockSpec((N_EXPERTS, 128), lambda i: (0, 0))],
        out_shape=[jax.ShapeDtypeStruct((8, n), I32), jax.ShapeDtypeStruct((N_EXPERTS, 128), I32)],
        scratch_shapes=[pltpu.VMEM((N_EXPERTS, 128), F32), pltpu.VMEM((N_EXPERTS, 128), F32)],
        name="moe_slots",
    )(ei, ei)


def _row_copy(src, src_row, dst, dst_row, sem):
    return pltpu.make_async_copy(src.at[pl.ds(src_row, 1)], dst.at[pl.ds(dst_row, 1)], sem)


ISSUE_UNROLL = 128


def _dispatch_kernel(slot_ref, pend_ref, cnt_ref, x1_ref, buf_ref, zero_ref, sem, zsem, *, n, tm, bm):
    base = pl.program_id(0) * tm

    @pl.when(pl.program_id(0) == 0)
    def _():
        zero_ref[...] = jnp.zeros_like(zero_ref)

        def zero_copy(e):
            start = pl.multiple_of(pend_ref[e] - bm, bm)
            return pltpu.make_async_copy(zero_ref, buf_ref.at[pl.ds(start, bm)], zsem)

        def tail_copy(blk):
            return pltpu.make_async_copy(zero_ref, buf_ref.at[pl.ds(pl.multiple_of(blk * bm, bm), bm)], zsem)

        def tail_start(blk, c):
            tail_copy(blk).start()
            return c

        def tail_wait(blk, c):
            tail_copy(blk).wait()
            return c

        for e in range(N_EXPERTS):
            @pl.when(cnt_ref[e] > 0)
            def _(e=e):
                zero_copy(e).start()
        first_unused = pend_ref[N_EXPERTS - 1] // bm
        lax.fori_loop(first_unused, buf_ref.shape[0] // bm, tail_start, 0)
        for e in range(N_EXPERTS):
            @pl.when(cnt_ref[e] > 0)
            def _(e=e):
                zero_copy(e).wait()
        lax.fori_loop(first_unused, buf_ref.shape[0] // bm, tail_wait, 0)

    def body(r, carry):
        for k in range(TOP_K):
            _row_copy(x1_ref, r, buf_ref, slot_ref[k * n + base + r], sem).start()
        return carry

    lax.fori_loop(0, tm, body, 0, unroll=ISSUE_UNROLL)
    for _ in range(TOP_K):
        pltpu.make_async_copy(x1_ref, buf_ref.at[pl.ds(0, tm)], sem).wait()


def _dispatch(slot_flat, pend, counts, x1, *, tm, bm, nblk):
    n = x1.shape[0]
    return pl.pallas_call(
        functools.partial(_dispatch_kernel, n=n, tm=tm, bm=bm),
        grid_spec=pltpu.PrefetchScalarGridSpec(
            num_scalar_prefetch=3,
            grid=(n // tm,),
            in_specs=[pl.BlockSpec((tm, D_MODEL), lambda i, s, p, c: (i, 0))],
            out_specs=pl.BlockSpec(memory_space=pl.ANY),
            scratch_shapes=[pltpu.VMEM((bm, D_MODEL), F32), pltpu.SemaphoreType.DMA(()),
                            pltpu.SemaphoreType.DMA(())],
        ),
        out_shape=jax.ShapeDtypeStruct((nblk * bm, D_MODEL), F32),
        name="moe_dispatch",
    )(slot_flat, pend, counts, x1)


def _expert_kernel(be_ref, nu_ref, nxt_ref, xb_ref, wg_hbm, wu_hbm, wd_hbm, out_ref,
                   wg_bf, wu_bf, wd_bf, wg_f32, wu_f32, wd_f32, run_ref, sem):
    j = pl.program_id(0)
    used = j < nu_ref[0]
    expert = be_ref[j]
    changed = jnp.logical_or(j == 0, expert != be_ref[jnp.maximum(j - 1, 0)])

    def weight_copies(e, slot):
        return [pltpu.make_async_copy(src.at[e], dst.at[slot], sem.at[slot])
                for src, dst in ((wg_hbm, wg_f32), (wu_hbm, wu_f32), (wd_hbm, wd_f32))]

    @pl.when(j == 0)
    def _():
        run_ref[0] = 0
        for c in weight_copies(expert, 0):
            c.start()

    @pl.when(jnp.logical_and(used, changed))
    def _():
        slot = run_ref[0] % 2
        for c in weight_copies(expert, slot):
            c.wait()
        following = nxt_ref[expert]

        @pl.when(following >= 0)
        def _():
            for c in weight_copies(following, 1 - slot):
                c.start()
        wg_bf[...] = wg_f32[slot].astype(BF16)
        wu_bf[...] = wu_f32[slot].astype(BF16)
        wd_bf[...] = wd_f32[slot].astype(BF16)
        run_ref[0] = run_ref[0] + 1

    @pl.when(used)
    def _():
        xb = xb_ref[...].astype(BF16)
        a = jnp.dot(xb, wg_bf[...], preferred_element_type=F32)
        b = jnp.dot(xb, wu_bf[...], preferred_element_type=F32)
        h = (a * _sigmoid(a)) * b
        out_ref[...] = jnp.dot(h.astype(BF16), wd_bf[...], preferred_element_type=F32)

    @pl.when(jnp.logical_not(used))
    def _():
        out_ref[...] = jnp.zeros_like(out_ref)


def _experts(block_expert, n_used, next_expert, buf, w_g, w_u, w_d, *, bm):
    nblk = buf.shape[0] // bm

    def row_map(j, be, nu, nx):
        return (jnp.minimum(j, nu[0] - 1), 0)

    any_spec = pl.BlockSpec(memory_space=pl.ANY)
    return pl.pallas_call(
        _expert_kernel,
        grid_spec=pltpu.PrefetchScalarGridSpec(
            num_scalar_prefetch=3,
            grid=(nblk,),
            in_specs=[pl.BlockSpec((bm, D_MODEL), row_map), any_spec, any_spec, any_spec],
            out_specs=pl.BlockSpec((bm, D_MODEL), lambda j, be, nu, nx: (j, 0)),
            scratch_shapes=[pltpu.VMEM((D_MODEL, D_EXPERT), BF16),
                            pltpu.VMEM((D_MODEL, D_EXPERT), BF16),
                            pltpu.VMEM((D_EXPERT, D_MODEL), BF16),
                            pltpu.VMEM((2, D_MODEL, D_EXPERT), F32),
                            pltpu.VMEM((2, D_MODEL, D_EXPERT), F32),
                            pltpu.VMEM((2, D_EXPERT, D_MODEL), F32),
                            pltpu.SMEM((1,), I32),
                            pltpu.SemaphoreType.DMA((2,))],
        ),
        out_shape=jax.ShapeDtypeStruct((buf.shape[0], D_MODEL), F32),
        compiler_params=_params(),
        name="moe_experts",
    )(block_expert, n_used, next_expert, buf, w_g, w_u, w_d)


def _combine_kernel(slot_ref, x1_ref, gc_ref, g_ref, b_ref, eo_ref, y_ref, side_ref, rows, sem,
                    *, n, tm, alpha, main_tiles):
    i = pl.program_id(0)
    last = pl.num_programs(0) - 1
    cur = i % 2

    def start(tile, buf, r):
        for k in range(TOP_K):
            _row_copy(eo_ref, slot_ref[k * n + tile * tm + r], rows.at[buf, k], r, sem.at[buf]).start(priority=k)

    def wait(buf):
        for k in range(TOP_K):
            pltpu.make_async_copy(eo_ref.at[pl.ds(0, tm)], rows.at[buf, k], sem.at[buf]).wait()

    @pl.when(i == 0)
    def _():
        def body(r, c):
            start(0, 0, r)
            return c
        lax.fori_loop(0, tm, body, 0, unroll=ISSUE_UNROLL)

    wait(cur)
    nxt = jnp.minimum(i + 1, last)
    for r in range(tm):
        start(nxt, 1 - cur, r)
    gc = gc_ref[...]
    z = alpha * x1_ref[...] + gc[:, 0:1] * rows[cur, 0] + gc[:, 1:2] * rows[cur, 1]
    mu = jnp.mean(z, axis=-1, keepdims=True)
    zc = z - mu
    var = jnp.mean(zc * zc, axis=-1, keepdims=True)
    y = zc * lax.rsqrt(var + LN_EPS) * g_ref[...] + b_ref[...]

    @pl.when(i < main_tiles)
    def _():
        y_ref[...] = y

    @pl.when(i == main_tiles)
    def _():
        side_ref[...] = y[0:side_ref.shape[0]]

    @pl.when(i == last)
    def _():
        wait(1 - cur)


def _combine(slot_flat, x1, gate_cols, ln_g, ln_b, expert_out, *, tm, alpha, main_rows, side_rows):
    n = x1.shape[0]
    main_tiles = main_rows // tm
    return pl.pallas_call(
        functools.partial(_combine_kernel, n=n, tm=tm, alpha=alpha, main_tiles=main_tiles),
        grid_spec=pltpu.PrefetchScalarGridSpec(
            num_scalar_prefetch=1,
            grid=(n // tm,),
            in_specs=[pl.BlockSpec((tm, D_MODEL), lambda i, s: (i, 0)),
                      pl.BlockSpec((tm, TOP_K), lambda i, s: (i, 0)),
                      pl.BlockSpec((1, D_MODEL), lambda i, s: (0, 0)),
                      pl.BlockSpec((1, D_MODEL), lambda i, s: (0, 0)),
                      pl.BlockSpec(memory_space=pl.ANY)],
            out_specs=[pl.BlockSpec((tm, D_MODEL), lambda i, s: (jnp.minimum(i, main_tiles - 1), 0)),
                       pl.BlockSpec((side_rows, D_MODEL), lambda i, s: (0, 0))],
            scratch_shapes=[pltpu.VMEM((2, TOP_K, tm, D_MODEL), F32), pltpu.SemaphoreType.DMA((2,))],
        ),
        out_shape=[jax.ShapeDtypeStruct((main_rows, D_MODEL), F32),
                   jax.ShapeDtypeStruct((side_rows, D_MODEL), F32)],
        compiler_params=_params(),
        name="moe_combine",
    )(slot_flat, x1, gate_cols, ln_g, ln_b, expert_out)


def _hier_moe_ln(x1, ei, gt, w_g, w_u, w_d, ln_g, ln_b, *, tl, tm, bm, alpha, main_rows, side_rows):
    n = x1.shape[0]
    m = n * TOP_K
    slot, cnt = _slots(ei, tl=tl, bm=bm)
    counts = cnt[:, 0]
    pend = jnp.cumsum((counts + bm - 1) // bm * bm)
    nblk = (m + N_EXPERTS * (bm - 1) + bm - 1) // bm
    blk_start = jnp.arange(nblk, dtype=I32) * bm
    n_used = (pend[-1] // bm).astype(I32)
    be = jnp.minimum(jnp.sum(pend[None, :] <= blk_start[:, None], axis=1), N_EXPERTS - 1).astype(I32)
    be = jnp.where(jnp.arange(nblk) < n_used, be, jnp.take(be, n_used - 1))
    slot_flat = slot[0:TOP_K].reshape(m)
    buf = _dispatch(slot_flat, pend.astype(I32), counts, x1, tm=tm, bm=bm, nblk=nblk)
    ids = jnp.arange(N_EXPERTS, dtype=I32)
    first_at_or_after = lax.cummin(jnp.where(counts > 0, ids, N_EXPERTS), reverse=True)
    next_expert = jnp.concatenate([first_at_or_after[1:], jnp.full((1,), N_EXPERTS, I32)])
    next_expert = jnp.where(next_expert == N_EXPERTS, -1, next_expert).astype(I32)
    eo = _experts(be, n_used.reshape(1), next_expert, buf, w_g, w_u, w_d, bm=bm)
    gate_cols = gt[0:TOP_K].T
    return _combine(slot_flat, x1, gate_cols, ln_g, ln_b, eo, tm=tm, alpha=alpha,
                    main_rows=main_rows, side_rows=side_rows)


def _t5_bucket(n):
    nf = jnp.maximum(n, 1).astype(F32)
    large = MAX_EXACT + (jnp.log(nf / MAX_EXACT) / math.log(MAX_DISTANCE / MAX_EXACT)
                         * (N_BUCKETS - MAX_EXACT)).astype(I32)
    large = jnp.minimum(large, N_BUCKETS - 1)
    return jnp.where(n < MAX_EXACT, n, large)


def _bias_per_group(rel_bias):
    offs = jnp.arange(N_KEYS, dtype=I32)[None, :] * jnp.array(DILATIONS, I32)[:, None]
    bucket = _t5_bucket(offs)
    table = rel_bias.reshape(N_BUCKETS, N_GROUPS, HEADS)
    b = table[bucket, jnp.arange(N_GROUPS)[:, None]]
    return jnp.transpose(b, (0, 2, 1)).astype(F32)


def _prompt_bias_tables(bias):
    width = 3 * Q_BLOCK
    neg = jnp.full((N_GROUPS, HEADS, Q_BLOCK - 1), NEG_INF, F32)
    r = jnp.concatenate([neg, bias[:, :, ::-1], neg, jnp.full((N_GROUPS, HEADS, 1), NEG_INF, F32)], axis=-1)
    flat = jnp.tile(r, (1, 1, Q_BLOCK))[:, :, :Q_BLOCK * (width - 1)]
    skew = flat.reshape(N_GROUPS, HEADS, Q_BLOCK, width - 1)
    later = skew[:, :, :, Q_BLOCK - 1:3 * Q_BLOCK - 1]
    has_prev = (np.arange(2 * Q_BLOCK) >= Q_BLOCK)[None, None, None, :]
    first = jnp.where(has_prev, later, NEG_INF)
    tb = jnp.stack([first, later], axis=1)
    return tb.reshape(N_GROUPS, 2, HEADS // 2, 2 * Q_BLOCK, 2 * Q_BLOCK)


def _sample_bias_tables(bias):
    t = np.arange(T_NEW)
    bcs = []
    for g, d in enumerate(DILATIONS):
        rev = bias[g][:, ::-1][:, :WINDOW_KEYS]
        if d == 1:
            dist = np.arange(WINDOW_KEYS)[None, :] - t[:, None]
            vals = jnp.take(rev, np.clip(dist, 0, WINDOW_KEYS - 1), axis=1)
            bcs.append(jnp.where(dist[None] >= 0, vals, NEG_INF))
        else:
            cls = np.arange(d)[None, :] == t[:, None]
            table = jnp.where(cls[None, :, None, :], rev[:, None, :, None], NEG_INF)
            bcs.append(table.reshape(HEADS, T_NEW, WINDOW_KEYS * d))
    back = t[:, None] - t[None, :]
    vals = jnp.take(bias, np.clip(back, 0, T_NEW - 1), axis=2)
    ok = np.stack([(back >= 0) if d == 1 else (back == 0) for d in DILATIONS])
    return bcs, jnp.where(ok[:, None], vals, NEG_INF)


def _split_bf16(w):
    hi = w.astype(BF16)
    lo = (w - hi.astype(F32)).astype(BF16)
    return hi, lo


def kernel(x_prompt, x_sample, cache_attn_w128, cache_attn_w512, cache_attn_w2048, state_conv, rel_bias, w_in, w_conv, w_pa, w_pb, w_o, ln1_g, ln1_b, w_router_group, w_router_expert, w_expert_gate, w_expert_up, w_expert_down, ln2_g, ln2_b):
    depth = w_in.shape[0]
    assert depth == 1 and x_prompt.shape[0] == 1
    alpha = (2.0 * depth) ** 0.25
    s = x_prompt.shape[1]
    bd, t_len = x_sample.shape[0], x_sample.shape[1]
    assert t_len == T_NEW and s % (DILATIONS[-1] * Q_BLOCK * Q_BLOCKS_PER_STEP) == 0
    assert s % PROJ_TILE == 0 and s % MIX_TILE == 0 and s % MOE_TILE == 0
    assert (s + MIX_TILE) % SLOT_TILE == 0 and (s + MIX_TILE) % MOE_TILE == 0

    bias = _bias_per_group(rel_bias)
    tb = _prompt_bias_tables(bias)
    bcs, bn = _sample_bias_tables(bias)

    w_in_bf = w_in[0].astype(BF16)
    w_pa_bf = w_pa[0].astype(BF16)
    w_pb_bf = w_pb[0].astype(BF16)
    w_o_bf = w_o[0].astype(BF16)
    wr = jnp.zeros((ROUTER_ROWS, D_MODEL), F32)
    wr = wr.at[0:N_EXPERT_GROUPS].set(w_router_group[0].T).at[8:8 + N_EXPERTS].set(w_router_expert[0].T)
    wr_hi, wr_lo = _split_bf16(wr)
    g1, b1 = ln1_g[0][None], ln1_b[0][None]
    g2, b2 = ln2_g[0][None], ln2_b[0][None]
    wg, wu, wd = w_expert_gate[0], w_expert_up[0], w_expert_down[0]

    xp = x_prompt[0]
    kv_tail = min(MAX_DISTANCE, s)
    q, kb, vb, (k32, v32, yb, sga, sgb, ut) = _proj(
        xp, w_in_bf, w_conv[0], None, tm=PROJ_TILE, u_tail=8, q_dtype=BF16, dils=DILATIONS, kv_f32=True)
    k32, v32 = k32[s - kv_tail:], v32[s - kv_tail:]
    o_l = [_attn_prompt_group(q[g], kb[g], vb[g], tb[g], g) for g in range(N_GROUPS)]
    routed = _mix(xp, [a[0] for a in o_l], [a[1] for a in o_l], yb, sga, sgb,
                  w_pa_bf, w_pb_bf, w_o_bf, g1, b1, wr_hi, wr_lo, tm=MIX_TILE, alpha=alpha, dils=DILATIONS,
                  extra_tiles=1)

    kv_prompt = []
    for g, d in enumerate(DILATIONS):
        length = min(WINDOW_KEYS * d, s)
        cols = slice(g * GROUP_WIDTH, (g + 1) * GROUP_WIDTH)
        kg = k32[kv_tail - length:, cols].reshape(length, HEADS, HEAD_DIM)
        vg = v32[kv_tail - length:, cols].reshape(length, HEADS, HEAD_DIM)
        kv_prompt.append(jnp.stack([kg, vg], axis=1)[None, None])
    conv_prompt = ut[6:8][None, None]

    ns = bd * t_len
    xs = x_sample.reshape(ns, D_MODEL)
    st = state_conv[0]
    s0 = jnp.repeat(st[:, 0], t_len, axis=0)
    s1 = jnp.repeat(st[:, 1], t_len, axis=0)
    qs, _, _, (k32s, v32s, ybs, sgas, sgbs, us) = _proj(
        xs, w_in_bf, w_conv[0], (s0, s1), tm=ns, u_tail=ns, q_dtype=F32, dils=NO_DILATION, kv_f32=True)
    qs = jnp.concatenate([a[0] for a in qs], axis=1)
    packed = jnp.stack([qs, k32s, v32s]).reshape(3, bd, t_len, N_GROUPS, GROUP_WIDTH)
    qkv_t = jnp.transpose(packed, (1, 4, 0, 3, 2)).reshape(bd, GROUP_WIDTH, 3 * N_GROUPS * t_len)
    qkv_t = jnp.pad(qkv_t, ((0, 0), (0, 0), (0, 128 - 3 * N_GROUPS * t_len)))
    caches = (cache_attn_w128[0], cache_attn_w512[0], cache_attn_w2048[0])
    caches_t = [jnp.transpose(c, (0, 2, 3, 4, 1)).reshape(bd, 2, GROUP_WIDTH, c.shape[1]) for c in caches]
    pair = (bd, t_len, N_GROUPS, HEADS // 2, 2, HEAD_DIM)
    q6 = jnp.transpose(qs.reshape(pair), (0, 3, 2, 4, 1, 5))
    zeros = jnp.zeros_like(q6[:, :, :, 0])
    qbd = jnp.stack([jnp.concatenate([q6[:, :, :, 0], zeros], axis=-1),
                     jnp.concatenate([zeros, q6[:, :, :, 1]], axis=-1)], axis=3)
    qbd = qbd.reshape(bd, HEADS // 2, N_GROUPS, 2 * t_len, 128)
    new_rows = jnp.stack([k32s, v32s]).reshape(2, bd, t_len, N_GROUPS, HEADS // 2, 128)
    new_rows = jnp.transpose(new_rows, (1, 4, 3, 0, 2, 5))
    bcs = [t.reshape(HEADS // 2, 2 * t_len, t.shape[-1]) for t in bcs]
    bn = bn.reshape(N_GROUPS, HEADS // 2, 2 * t_len, t_len)
    n0, n1, n2, o_s, lse_s = _sample_cache(qbd, new_rows, qkv_t, caches_t, bcs, bn)

    def unpack(a):
        a = a[:, :N_GROUPS * t_len].reshape(bd, N_GROUPS, t_len, GROUP_WIDTH)
        return jnp.transpose(a, (1, 0, 2, 3)).reshape(N_GROUPS, 1, ns, GROUP_WIDTH)

    o_s, lse_s = unpack(o_s), unpack(lse_s)
    assert ns <= min(MIX_TILE, MOE_TILE) and s % ns == 0
    x1, ei, gt = _mix(xs, o_s, lse_s, ybs, sgas, sgbs, w_pa_bf, w_pb_bf, w_o_bf, g1, b1, wr_hi, wr_lo,
                      tm=ns, alpha=alpha, dils=NO_DILATION, into=routed, into_tile=s // ns)
    y_prompt, y_sample = _hier_moe_ln(x1, ei, gt, wg, wu, wd, g2, b2, tl=SLOT_TILE, tm=MOE_TILE, bm=EXPERT_BLOCK,
                                      alpha=alpha, main_rows=s, side_rows=ns)
    y_prompt = y_prompt[None]
    y_sample = y_sample.reshape(bd, t_len, D_MODEL)

    kv_sample = [jnp.transpose(c.reshape(bd, 2, HEADS, HEAD_DIM, c.shape[-1]), (0, 4, 1, 2, 3))[None]
                 for c in (n0, n1, n2)]
    conv_sample = us.reshape(bd, t_len, CONV_CHANNELS)[:, t_len - 2:][None]

    return (y_prompt, y_sample, kv_prompt[0], kv_prompt[1], kv_prompt[2], conv_prompt,
            kv_sample[0], kv_sample[1], kv_sample[2], conv_sample)
```

```python
import functools
import math

import numpy as np
import jax
import jax.numpy as jnp
from jax import lax
from jax.experimental import pallas as pl
from jax.experimental.pallas import tpu as pltpu

F32 = jnp.float32
BF16 = jnp.bfloat16
I32 = jnp.int32

D_MODEL = 1024
N_GROUPS = 3
HEADS = 8
HEAD_DIM = 64
GROUP_WIDTH = HEADS * HEAD_DIM
ATTN_WIDTH = N_GROUPS * GROUP_WIDTH
DILATIONS = (1, 4, 16)
NO_DILATION = (1, 1, 1)
WINDOW_KEYS = 128
N_KEYS = WINDOW_KEYS + 1
N_BUCKETS = 32
MAX_EXACT = 16
MAX_DISTANCE = 2048
CONV_CHANNELS = 512
N_EXPERT_GROUPS = 4
EXPERTS_PER_GROUP = 8
N_EXPERTS = 32
TOP_K = 2
D_EXPERT = 512
LN_EPS = 1e-5
PROJ_WIDTH = 3 * ATTN_WIDTH + 3 * CONV_CHANNELS + 2 * D_MODEL
ROUTER_ROWS = 8 + N_EXPERTS
Q_BLOCK = 128
T_NEW = 4
NEG_INF = float("-inf")
VMEM_LIMIT = 56 * 1024 * 1024

PROJ_TILE = 256
MIX_TILE = 512
SLOT_TILE = 512
MOE_TILE = 512
EXPERT_BLOCK = 512
ZERO_ROWS = 128


def _sigmoid(x):
    return 1.0 / (1.0 + jnp.exp(-x))


def _params(limit=VMEM_LIMIT):
    return pltpu.CompilerParams(vmem_limit_bytes=limit)


def _proj_kernel(*refs, tm, tail_rows, sample_mode, dils, kv_f32):
    n_in = 5 if sample_mode else 3
    x_ref, w_ref, wc_ref = refs[0:3]
    outs = list(refs[n_in:])
    q_refs, k_refs, v_refs = outs[0:3], outs[3:6], outs[6:9]
    del outs[0:9]
    k32_ref, v32_ref = (outs.pop(0), outs.pop(0)) if kv_f32 else (None, None)
    yb_ref, sga_ref, sgb_ref, ut_ref, cls_ref = outs[0:5]
    xb = x_ref[...].astype(BF16)

    def col(c0, width):
        return jnp.dot(xb, w_ref[:, c0:c0 + width], preferred_element_type=F32)

    def write_classes(val, group_refs):
        for g, d in enumerate(dils):
            part = val[:, g * GROUP_WIDTH:(g + 1) * GROUP_WIDTH]
            ref = group_refs[g]
            if d == 1:
                ref[0] = part.astype(ref.dtype)
            else:
                for kk in range(GROUP_WIDTH // 128):
                    lanes = slice(kk * 128, (kk + 1) * 128)
                    cls_ref[kk] = part[:, lanes]
                    for c in range(d):
                        ref[c, :, lanes] = cls_ref[kk, pl.ds(c, tm // d, stride=d), :].astype(ref.dtype)

    write_classes(col(0, ATTN_WIDTH), q_refs)
    k = col(ATTN_WIDTH, ATTN_WIDTH)
    v = col(2 * ATTN_WIDTH, ATTN_WIDTH)
    if kv_f32:
        k32_ref[...] = k
        v32_ref[...] = v
    write_classes(k, k_refs)
    write_classes(v, v_refs)

    c0 = 3 * ATTN_WIDTH
    bg = col(c0, CONV_CHANNELS)
    u = col(c0 + CONV_CHANNELS, CONV_CHANNELS) * col(c0 + 2 * CONV_CHANNELS, CONV_CHANNELS)
    row = lax.broadcasted_iota(I32, (tm, CONV_CHANNELS), 0)
    r1 = pltpu.roll(u, 1, axis=0)
    r2 = pltpu.roll(u, 2, axis=0)
    if sample_mode:
        s0 = refs[3][...]
        s1 = refs[4][...]
        t = row & (T_NEW - 1)
        prev1 = jnp.where(t == 0, s1, r1)
        prev2 = jnp.where(t == 0, s0, jnp.where(t == 1, s1, r2))
    else:
        carry_ref = outs[5]

        @pl.when(pl.program_id(0) == 0)
        def _():
            carry_ref[...] = jnp.zeros_like(carry_ref)
        c6 = carry_ref[6:7, :]
        c7 = carry_ref[7:8, :]
        prev1 = jnp.where(row == 0, c7, r1)
        prev2 = jnp.where(row == 0, c6, jnp.where(row == 1, c7, r2))
        carry_ref[...] = u[tm - 8:tm, :]
    conv = prev2 * wc_ref[0:1, :] + prev1 * wc_ref[1:2, :] + u * wc_ref[2:3, :]
    yb_ref[...] = (bg * conv).astype(BF16)
    ut_ref[...] = u[tm - tail_rows:tm, :]

    c1 = c0 + 3 * CONV_CHANNELS
    sga_ref[...] = _sigmoid(col(c1, D_MODEL)).astype(BF16)
    sgb_ref[...] = _sigmoid(col(c1 + D_MODEL, D_MODEL)).astype(BF16)


def _proj(x, w_in_bf, w_conv, conv_prev, *, tm, u_tail, q_dtype, dils, kv_f32):
    n = x.shape[0]
    sample_mode = conv_prev is not None
    nt = n // tm

    def row_spec(width):
        return pl.BlockSpec((tm, width), lambda i: (i, 0))

    def class_spec(d):
        return pl.BlockSpec((d, tm // d, GROUP_WIDTH), lambda i: (0, i, 0))

    def class_shape(d, dtype):
        return jax.ShapeDtypeStruct((d, n // d, GROUP_WIDTH), dtype)

    in_specs = [
        row_spec(D_MODEL),
        pl.BlockSpec((D_MODEL, PROJ_WIDTH), lambda i: (0, 0), pipeline_mode=pl.Buffered(1)),
        pl.BlockSpec((3, CONV_CHANNELS), lambda i: (0, 0)),
    ]
    args = [x, w_in_bf, w_conv]
    scratch = [pltpu.VMEM((GROUP_WIDTH // 128, tm, 128), F32)]
    if sample_mode:
        in_specs += [row_spec(CONV_CHANNELS), row_spec(CONV_CHANNELS)]
        args += [conv_prev[0], conv_prev[1]]
    else:
        scratch.append(pltpu.VMEM((8, CONV_CHANNELS), F32))
    out_shape = (
        [class_shape(d, q_dtype) for d in dils] + [class_shape(d, BF16) for d in dils] * 2
        + [jax.ShapeDtypeStruct((n, ATTN_WIDTH), F32)] * (2 if kv_f32 else 0)
        + [jax.ShapeDtypeStruct((n, CONV_CHANNELS), BF16),
           jax.ShapeDtypeStruct((n, D_MODEL), BF16),
           jax.ShapeDtypeStruct((n, D_MODEL), BF16),
           jax.ShapeDtypeStruct((u_tail, CONV_CHANNELS), F32)])
    out_specs = (
        [class_spec(d) for d in dils] * 3
        + [row_spec(ATTN_WIDTH)] * (2 if kv_f32 else 0)
        + [row_spec(CONV_CHANNELS), row_spec(D_MODEL), row_spec(D_MODEL),
           pl.BlockSpec((u_tail, CONV_CHANNELS), lambda i: (0, 0))])
    res = pl.pallas_call(
        functools.partial(_proj_kernel, tm=tm, tail_rows=u_tail, sample_mode=sample_mode, dils=dils, kv_f32=kv_f32),
        grid=(nt,),
        in_specs=in_specs,
        out_specs=out_specs,
        out_shape=out_shape,
        scratch_shapes=scratch,
        compiler_params=_params(),
        name="proj",
    )(*args)
    return res[0:3], res[3:6], res[6:9], res[9:]


Q_BLOCKS_PER_STEP = 8


def _attn_kernel(q_ref, kp_ref, kc_ref, vp_ref, vc_ref, tb_ref, o_ref, lse_ref):
    lane = lax.broadcasted_iota(I32, (Q_BLOCK, 128), 1)
    first = lane < HEAD_DIM
    scale = HEAD_DIM ** -0.5
    has_prev = jnp.minimum(pl.program_id(1), 1)
    for sub in range(Q_BLOCKS_PER_STEP):
        rows = slice(sub * Q_BLOCK, (sub + 1) * Q_BLOCK)
        band = slice((sub - 1) * Q_BLOCK, (sub + 1) * Q_BLOCK)
        for pr in range(HEADS // 2):
            sl = slice(pr * 128, (pr + 1) * 128)
            if sub == 0:
                k = jnp.concatenate([kp_ref[:, sl], kc_ref[rows, sl]], axis=0)
                v = jnp.concatenate([vp_ref[:, sl], vc_ref[rows, sl]], axis=0)
                bias = tb_ref[has_prev, pr]
            else:
                k = kc_ref[band, sl]
                v = vc_ref[band, sl]
                bias = tb_ref[1, pr]
            qf = q_ref[rows, sl].astype(F32) * scale
            qq = jnp.concatenate([jnp.where(first, qf, 0.0), jnp.where(first, 0.0, qf)], axis=0).astype(BF16)
            s = lax.dot_general(qq, k, (((1,), (1,)), ((), ())), preferred_element_type=F32) + bias
            m = jnp.max(s, axis=-1, keepdims=True)
            p = jnp.exp(s - m)
            l = jnp.sum(p, axis=-1, keepdims=True)
            o = jnp.dot(p.astype(BF16), v, preferred_element_type=F32) / l
            lse = m + jnp.log(l)
            o_ref[rows, sl] = jnp.where(first, o[:Q_BLOCK], o[Q_BLOCK:]).astype(o_ref.dtype)
            lse_ref[rows, sl] = jnp.where(first, jnp.broadcast_to(lse[:Q_BLOCK], (Q_BLOCK, 128)),
                                          jnp.broadcast_to(lse[Q_BLOCK:], (Q_BLOCK, 128)))


def _attn_prompt_group(q, kb, vb, tb, g):
    d, rows = q.shape[0], q.shape[1]
    nq = Q_BLOCKS_PER_STEP
    cur = pl.BlockSpec((None, nq * Q_BLOCK, GROUP_WIDTH), lambda c, i: (c, i, 0))
    prev = pl.BlockSpec((None, Q_BLOCK, GROUP_WIDTH), lambda c, i: (c, jnp.maximum(i * nq - 1, 0), 0))
    return pl.pallas_call(
        _attn_kernel,
        grid=(d, rows // (nq * Q_BLOCK)),
        in_specs=[cur, prev, cur, prev, cur, pl.BlockSpec(tb.shape, lambda c, i: (0, 0, 0, 0))],
        out_specs=[cur, cur],
        out_shape=[jax.ShapeDtypeStruct((d, rows, GROUP_WIDTH), BF16),
                   jax.ShapeDtypeStruct((d, rows, GROUP_WIDTH), F32)],
        compiler_params=_params(),
        name=f"attn_prompt_g{g}",
    )(q, kb, kb, vb, vb, tb)


PACK_Q, PACK_K, PACK_V = 0, N_GROUPS * T_NEW, 2 * N_GROUPS * T_NEW


PAIRS_PER_STEP = 2


def _sample_cache_kernel(qbd_ref, nr_ref, qkv_ref, c0_ref, c1_ref, c2_ref, b0_ref, b1_ref, b2_ref, bn_ref,
                         n0_ref, n1_ref, n2_ref, o_ref, lse_ref):
    for pp in range(PAIRS_PER_STEP):
        rows = pl.ds(pp * 128, 128)
        kv = pl.ds(0, 2)
        _sample_pair(qbd_ref.at[pp], nr_ref.at[pp], qkv_ref.at[rows],
                     [c.at[kv, rows] for c in (c0_ref, c1_ref, c2_ref)],
                     [b.at[pp] for b in (b0_ref, b1_ref, b2_ref)], bn_ref.at[pl.ds(0, N_GROUPS), pp],
                     [c.at[kv, rows] for c in (n0_ref, n1_ref, n2_ref)],
                     o_ref.at[pl.ds(0, 16), rows], lse_ref.at[pl.ds(0, 16), rows])


def _sample_pair(qbd_ref, nr_ref, qkv_ref, c_refs, b_refs, bn_ref, n_refs, o_ref, lse_ref):
    scale = HEAD_DIM ** -0.5
    nt = (((1,), (1,)), ((), ()))
    lane = lax.broadcasted_iota(I32, (128, 128), 1)
    head0 = lane[0:T_NEW] < HEAD_DIM
    for g, (c_ref, b_ref, n_ref) in enumerate(zip(c_refs, b_refs, n_refs)):
        length = c_ref.shape[-1]
        qbd = qbd_ref[g]
        s_c = jnp.dot(qbd.astype(BF16), c_ref[0].astype(BF16), preferred_element_type=F32) * scale + b_ref[...]
        bn = bn_ref[g]
        s_n = [jnp.sum(qbd * nr_ref[g, 0, tn:tn + 1, :], axis=1, keepdims=True) * scale + bn[:, tn:tn + 1]
               for tn in range(T_NEW)]
        m = jnp.max(s_c, axis=1, keepdims=True)
        for x in s_n:
            m = jnp.maximum(m, x)
        p_c = jnp.exp(s_c - m)
        p_n = [jnp.exp(x - m) for x in s_n]
        l = jnp.sum(p_c, axis=1, keepdims=True)
        acc = lax.dot_general(p_c.astype(BF16), c_ref[1].astype(BF16), nt, preferred_element_type=F32)
        for tn in range(T_NEW):
            l = l + p_n[tn]
            acc = acc + p_n[tn] * nr_ref[g, 1, tn:tn + 1, :]
        o = acc / l
        lse = jnp.broadcast_to(m + jnp.log(l), (2 * T_NEW, 128))
        o_ref[g * T_NEW:(g + 1) * T_NEW, :] = jnp.where(head0, o[0:T_NEW], o[T_NEW:])
        lse_ref[g * T_NEW:(g + 1) * T_NEW, :] = jnp.where(head0, lse[0:T_NEW], lse[T_NEW:])

        for kv, pack in ((0, PACK_K), (1, PACK_V)):
            rolled = pltpu.roll(c_ref[kv], length - T_NEW, axis=1)
            new_tail = pltpu.roll(qkv_ref[...], 128 - T_NEW - (pack + g * T_NEW), axis=1)
            tail = jnp.where(lane >= 128 - T_NEW, new_tail, rolled[:, length - 128:])
            if length > 128:
                n_ref[kv, :, 0:length - 128] = rolled[:, 0:length - 128]
            n_ref[kv, :, length - 128:] = tail
    pad_rows = slice(N_GROUPS * T_NEW, 16)
    o_ref[pad_rows, :] = jnp.zeros((16 - N_GROUPS * T_NEW, 128), F32)
    lse_ref[pad_rows, :] = jnp.zeros((16 - N_GROUPS * T_NEW, 128), F32)


def _sample_cache(qbd, new_rows, qkv_t, caches_t, bcs, bn):
    b = qkv_t.shape[0]

    pp = PAIRS_PER_STEP

    def cache_spec(c):
        return pl.BlockSpec((None, 2, pp * 128, c.shape[-1]), lambda i, h: (i, 0, h, 0))

    def bias_spec(t):
        return pl.BlockSpec((pp, 2 * T_NEW, t.shape[-1]), lambda i, h: (h, 0, 0))

    out = pl.BlockSpec((None, 16, pp * 128), lambda i, h: (i, 0, h))
    return pl.pallas_call(
        _sample_cache_kernel,
        grid=(b, HEADS // 2 // pp),
        in_specs=[pl.BlockSpec((None, pp, N_GROUPS, 2 * T_NEW, 128), lambda i, h: (i, h, 0, 0, 0)),
                  pl.BlockSpec((None, pp, N_GROUPS, 2, T_NEW, 128), lambda i, h: (i, h, 0, 0, 0, 0)),
                  pl.BlockSpec((None, pp * 128, 128), lambda i, h: (i, h, 0))]
                 + [cache_spec(c) for c in caches_t] + [bias_spec(t) for t in bcs]
                 + [pl.BlockSpec((N_GROUPS, pp, 2 * T_NEW, T_NEW), lambda i, h: (0, h, 0, 0))],
        out_specs=[cache_spec(c) for c in caches_t] + [out, out],
        out_shape=[jax.ShapeDtypeStruct(c.shape, c.dtype) for c in caches_t]
                  + [jax.ShapeDtypeStruct((b, 16, GROUP_WIDTH), F32)] * 2,
        compiler_params=_params(),
        name="sample_cache",
    )(qbd, new_rows, qkv_t, *caches_t, *bcs, bn)


def _mix_kernel(*refs, tm, alpha, dils, n_alias):
    (x_ref, o0_ref, o1_ref, o2_ref, l0_ref, l1_ref, l2_ref, yb_ref, sga_ref, sgb_ref,
     wpa_ref, wpb_ref, wo_ref, g_ref, b_ref, wrh_ref, wrl_ref) = refs[0:17]
    x1_ref, ei_ref, gt_ref = refs[17 + n_alias:20 + n_alias]
    scratch = list(refs[20 + n_alias:])

    def natural(ref, d):
        if d == 1:
            return ref[0].astype(F32)
        scr = scratch.pop()
        for kk in range(GROUP_WIDTH // 128):
            for c in range(d):
                scr[kk, pl.ds(c, tm // d, stride=d), :] = ref[c, :, kk * 128:(kk + 1) * 128].astype(F32)
        return jnp.concatenate([scr[kk] for kk in range(GROUP_WIDTH // 128)], axis=1)

    l0, l1, l2 = natural(l0_ref, dils[0]), natural(l1_ref, dils[1]), natural(l2_ref, dils[2])
    mx = jnp.maximum(jnp.maximum(l0, l1), l2)
    e0 = jnp.exp(l0 - mx)
    e1 = jnp.exp(l1 - mx)
    e2 = jnp.exp(l2 - mx)
    ya = (e0 * natural(o0_ref, dils[0]) + e1 * natural(o1_ref, dils[1]) + e2 * natural(o2_ref, dils[2])) / (e0 + e1 + e2)
    pa = jnp.dot(ya.astype(BF16), wpa_ref[...], preferred_element_type=F32)
    pb = jnp.dot(yb_ref[...], wpb_ref[...], preferred_element_type=F32)
    gated = sga_ref[...].astype(F32) * pa + sgb_ref[...].astype(F32) * pb
    mix = jnp.dot(gated.astype(BF16), wo_ref[...], preferred_element_type=F32)
    z = alpha * x_ref[...] + mix
    mu = jnp.mean(z, axis=-1, keepdims=True)
    zc = z - mu
    var = jnp.mean(zc * zc, axis=-1, keepdims=True)
    x1 = zc * lax.rsqrt(var + LN_EPS) * g_ref[...] + b_ref[...]
    x1_ref[...] = x1

    xh = x1.astype(BF16)
    xl = (x1 - xh.astype(F32)).astype(BF16)
    nt = (((1,), (1,)), ((), ()))
    wrh = wrh_ref[...]
    lt = (lax.dot_general(wrh, xh, nt, preferred_element_type=F32)
          + lax.dot_general(wrh, xl, nt, preferred_element_type=F32)
          + lax.dot_general(wrl_ref[...], xh, nt, preferred_element_type=F32))

    gl = lt[0:N_EXPERT_GROUPS]
    gmax = jnp.max(gl, axis=0, keepdims=True)
    idx4 = lax.broadcasted_iota(I32, (N_EXPERT_GROUPS, tm), 0)
    g_idx = jnp.min(jnp.where(gl == gmax, idx4, N_EXPERT_GROUPS), axis=0, keepdims=True)
    g_prob = 1.0 / jnp.sum(jnp.exp(gl - gmax), axis=0, keepdims=True)
    e_sel = lt[8:16]
    for grp in range(1, N_EXPERT_GROUPS):
        e_sel = jnp.where(g_idx == grp, lt[8 + 8 * grp:16 + 8 * grp], e_sel)
    idx8 = lax.broadcasted_iota(I32, (EXPERTS_PER_GROUP, tm), 0)
    v1 = jnp.max(e_sel, axis=0, keepdims=True)
    i1 = jnp.min(jnp.where(e_sel == v1, idx8, EXPERTS_PER_GROUP), axis=0, keepdims=True)
    rest = jnp.where(idx8 == i1, NEG_INF, e_sel)
    v2 = jnp.max(rest, axis=0, keepdims=True)
    i2 = jnp.min(jnp.where(rest == v2, idx8, EXPERTS_PER_GROUP), axis=0, keepdims=True)
    r = jnp.exp(v2 - v1)
    gate1 = g_prob / (1.0 + r)
    gate2 = g_prob * r / (1.0 + r)
    ex1 = g_idx * EXPERTS_PER_GROUP + i1
    ex2 = g_idx * EXPERTS_PER_GROUP + i2
    ei_ref[...] = jnp.where(idx8 == 0, ex1, jnp.where(idx8 == 1, ex2, 0))
    gt_ref[...] = jnp.where(idx8 == 0, gate1, jnp.where(idx8 == 1, gate2, 0.0))


def _mix(x, o, lse, yb, sga, sgb, w_pa, w_pb, w_o, ln_g, ln_b, wr_hi, wr_lo, *, tm, alpha, dils,
         extra_tiles=0, into=None, into_tile=0):
    n = x.shape[0]
    nt = n // tm

    def src(i):
        return jnp.minimum(i, nt - 1)

    def row_spec(width):
        return pl.BlockSpec((tm, width), lambda i: (src(i), 0))

    def class_spec(d):
        return pl.BlockSpec((d, tm // d, GROUP_WIDTH), lambda i: (0, src(i), 0))

    def full(a):
        return pl.BlockSpec(a.shape, lambda i: (0,) * a.ndim)

    n_alias = 0 if into is None else 3
    rows_out = n + extra_tiles * tm if into is None else into[0].shape[0]
    any_spec = pl.BlockSpec(memory_space=pl.ANY)
    n_scratch = 2 * sum(1 for d in dils if d > 1)
    args = [x, o[0], o[1], o[2], lse[0], lse[1], lse[2], yb, sga, sgb, w_pa, w_pb, w_o, ln_g, ln_b, wr_hi, wr_lo]
    return pl.pallas_call(
        functools.partial(_mix_kernel, tm=tm, alpha=alpha, dils=dils, n_alias=n_alias),
        grid=(nt + extra_tiles,),
        in_specs=[row_spec(D_MODEL)] + [class_spec(d) for d in dils] * 2 + [row_spec(CONV_CHANNELS)]
                 + [row_spec(D_MODEL)] * 2
                 + [full(w_pa), full(w_pb), full(w_o), full(ln_g), full(ln_b), full(wr_hi), full(wr_lo)]
                 + [any_spec] * n_alias,
        out_specs=[pl.BlockSpec((tm, D_MODEL), lambda i: (i + into_tile, 0)),
                   pl.BlockSpec((8, tm), lambda i: (0, i + into_tile)),
                   pl.BlockSpec((8, tm), lambda i: (0, i + into_tile))],
        out_shape=[jax.ShapeDtypeStruct((rows_out, D_MODEL), F32),
                   jax.ShapeDtypeStruct((8, rows_out), I32),
                   jax.ShapeDtypeStruct((8, rows_out), F32)],
        input_output_aliases={len(args) + k: k for k in range(n_alias)},
        scratch_shapes=[pltpu.VMEM((GROUP_WIDTH // 128, tm, 128), F32)] * n_scratch,
        compiler_params=_params(),
        name="mix",
    )(*args, *(into or ()))


def _slot_kernel(all_ref, ei_ref, slot_ref, cnt_ref, carry_ref, start_ref, *, tl, bm):
    step = pl.program_id(0)

    def one_hot(rows):
        ex = lax.broadcasted_iota(I32, (N_EXPERTS, rows.shape[1]), 0)
        return (ex == rows).astype(F32)

    @pl.when(step == 0)
    def _():
        total = (jnp.sum(one_hot(all_ref[0:1, :]), axis=1, keepdims=True)
                 + jnp.sum(one_hot(all_ref[1:2, :]), axis=1, keepdims=True))
        total = jnp.broadcast_to(total, (N_EXPERTS, 128))
        cnt_ref[...] = total.astype(I32)
        blocks = jnp.floor((total + (bm - 1)) * (1.0 / bm))
        a = lax.broadcasted_iota(I32, (N_EXPERTS, N_EXPERTS), 0)
        b = lax.broadcasted_iota(I32, (N_EXPERTS, N_EXPERTS), 1)
        before = (b < a).astype(BF16)
        start_ref[...] = jnp.dot(before, blocks.astype(BF16), preferred_element_type=F32) * bm
        carry_ref[...] = jnp.zeros_like(carry_ref)
        slot_ref[...] = jnp.zeros_like(slot_ref)

    @pl.when(step > 0)
    def _():
        oh0 = one_hot(ei_ref[0:1, :])
        oh1 = one_hot(ei_ref[1:2, :])
        cnt0 = jnp.sum(oh0, axis=1, keepdims=True)
        cnt1 = jnp.sum(oh1, axis=1, keepdims=True)
        a = lax.broadcasted_iota(I32, (tl, tl), 0)
        b = lax.broadcasted_iota(I32, (tl, tl), 1)
        upper = (a < b).astype(BF16)
        pre0 = jnp.dot(oh0.astype(BF16), upper, preferred_element_type=F32)
        pre1 = jnp.dot(oh1.astype(BF16), upper, preferred_element_type=F32)
        base = carry_ref[:, 0:1] + start_ref[:, 0:1]
        slot0 = jnp.sum(oh0 * (pre0 + base), axis=0, keepdims=True)
        slot1 = jnp.sum(oh1 * (pre1 + cnt0 + base), axis=0, keepdims=True)
        row = lax.broadcasted_iota(I32, (8, tl), 0)
        slot_ref[...] = jnp.where(row == 0, slot0.astype(I32), jnp.where(row == 1, slot1.astype(I32), 0))
        carry_ref[...] = carry_ref[...] + cnt0 + cnt1


def _slots(ei, *, tl, bm):
    n = ei.shape[1]
    return pl.pallas_call(
        functools.partial(_slot_kernel, tl=tl, bm=bm),
        grid=(n // tl + 1,),
        in_specs=[pl.BlockSpec((8, n), lambda i: (0, 0)),
                  pl.BlockSpec((8, tl), lambda i: (0, jnp.maximum(i - 1, 0)))],
        out_specs=[pl.BlockSpec((8, tl), lambda i: (0, jnp.maximum(i - 1, 0))),
                   pl.BlockSpec((N_EXPERTS, 128), lambda i: (0, 0))],
        out_shape=[jax.ShapeDtypeStruct((8, n), I32), jax.ShapeDtypeStruct((N_EXPERTS, 128), I32)],
        scratch_shapes=[pltpu.VMEM((N_EXPERTS, 128), F32), pltpu.VMEM((N_EXPERTS, 128), F32)],
        name="moe_slots",
    )(ei, ei)


def _row_copy(src, src_row, dst, dst_row, sem):
    return pltpu.make_async_copy(src.at[pl.ds(src_row, 1)], dst.at[pl.ds(dst_row, 1)], sem)


ISSUE_UNROLL = 128


def _dispatch_kernel(slot_ref, pend_ref, cnt_ref, x1_ref, buf_ref, zero_ref, sem, zsem, *, n, tm, bm):
    base = pl.program_id(0) * tm
    zero_rows = zero_ref.shape[0]

    @pl.when(pl.program_id(0) == 0)
    def _():
        zero_ref[...] = jnp.zeros_like(zero_ref)

        def chunk(row0):
            return pltpu.make_async_copy(zero_ref, buf_ref.at[pl.ds(pl.multiple_of(row0, zero_rows), zero_rows)], zsem)

        def for_each_chunk(action):
            for e in range(N_EXPERTS):
                end = pend_ref[e]
                padding = end - (pend_ref[e - 1] if e > 0 else 0) - cnt_ref[e]

                def back(j, c, end=end):
                    action(chunk(end - (j + 1) * zero_rows))
                    return c
                lax.fori_loop(0, (padding + zero_rows - 1) // zero_rows, back, 0)
            used_end = pend_ref[N_EXPERTS - 1]

            def forward(j, c):
                action(chunk(used_end + j * zero_rows))
                return c
            lax.fori_loop(0, (buf_ref.shape[0] - used_end) // zero_rows, forward, 0)

        for_each_chunk(lambda dma: dma.start())
        for_each_chunk(lambda dma: dma.wait())

    def body(r, carry):
        for k in range(TOP_K):
            _row_copy(x1_ref, r, buf_ref, slot_ref[k * n + base + r], sem).start()
        return carry

    lax.fori_loop(0, tm, body, 0, unroll=ISSUE_UNROLL)
    for _ in range(TOP_K):
        pltpu.make_async_copy(x1_ref, buf_ref.at[pl.ds(0, tm)], sem).wait()


def _dispatch(slot_flat, pend, counts, x1, *, tm, bm, nblk):
    n = x1.shape[0]
    return pl.pallas_call(
        functools.partial(_dispatch_kernel, n=n, tm=tm, bm=bm),
        grid_spec=pltpu.PrefetchScalarGridSpec(
            num_scalar_prefetch=3,
            grid=(n // tm,),
            in_specs=[pl.BlockSpec((tm, D_MODEL), lambda i, s, p, c: (i, 0))],
            out_specs=pl.BlockSpec(memory_space=pl.ANY),
            scratch_shapes=[pltpu.VMEM((ZERO_ROWS, D_MODEL), F32), pltpu.SemaphoreType.DMA(()),
                            pltpu.SemaphoreType.DMA(())],
        ),
        out_shape=jax.ShapeDtypeStruct((nblk * bm, D_MODEL), F32),
        name="moe_dispatch",
    )(slot_flat, pend, counts, x1)


def _expert_kernel(be_ref, nu_ref, nxt_ref, xb_ref, wg_hbm, wu_hbm, wd_hbm, out_ref,
                   wg_bf, wu_bf, wd_bf, wg_f32, wu_f32, wd_f32, run_ref, sem):
    j = pl.program_id(0)
    used = j < nu_ref[0]
    expert = be_ref[j]
    changed = jnp.logical_or(j == 0, expert != be_ref[jnp.maximum(j - 1, 0)])

    def weight_copies(e, slot):
        return [pltpu.make_async_copy(src.at[e], dst.at[slot], sem.at[slot])
                for src, dst in ((wg_hbm, wg_f32), (wu_hbm, wu_f32), (wd_hbm, wd_f32))]

    @pl.when(j == 0)
    def _():
        run_ref[0] = 0
        for c in weight_copies(expert, 0):
            c.start()

    @pl.when(jnp.logical_and(used, changed))
    def _():
        slot = run_ref[0] % 2
        for c in weight_copies(expert, slot):
            c.wait()
        following = nxt_ref[expert]

        @pl.when(following >= 0)
        def _():
            for c in weight_copies(following, 1 - slot):
                c.start()
        wg_bf[...] = wg_f32[slot].astype(BF16)
        wu_bf[...] = wu_f32[slot].astype(BF16)
        wd_bf[...] = wd_f32[slot].astype(BF16)
        run_ref[0] = run_ref[0] + 1

    @pl.when(used)
    def _():
        xb = xb_ref[...].astype(BF16)
        a = jnp.dot(xb, wg_bf[...], preferred_element_type=F32)
        b = jnp.dot(xb, wu_bf[...], preferred_element_type=F32)
        h = (a * _sigmoid(a)) * b
        out_ref[...] = jnp.dot(h.astype(BF16), wd_bf[...], preferred_element_type=F32)

    @pl.when(jnp.logical_not(used))
    def _():
        out_ref[...] = jnp.zeros_like(out_ref)


def _experts(block_expert, n_used, next_expert, buf, w_g, w_u, w_d, *, bm):
    nblk = buf.shape[0] // bm

    def row_map(j, be, nu, nx):
        return (jnp.minimum(j, nu[0] - 1), 0)

    any_spec = pl.BlockSpec(memory_space=pl.ANY)
    return pl.pallas_call(
        _expert_kernel,
        grid_spec=pltpu.PrefetchScalarGridSpec(
            num_scalar_prefetch=3,
            grid=(nblk,),
            in_specs=[pl.BlockSpec((bm, D_MODEL), row_map), any_spec, any_spec, any_spec],
            out_specs=pl.BlockSpec((bm, D_MODEL), lambda j, be, nu, nx: (j, 0)),
            scratch_shapes=[pltpu.VMEM((D_MODEL, D_EXPERT), BF16),
                            pltpu.VMEM((D_MODEL, D_EXPERT), BF16),
                            pltpu.VMEM((D_EXPERT, D_MODEL), BF16),
                            pltpu.VMEM((2, D_MODEL, D_EXPERT), F32),
                            pltpu.VMEM((2, D_MODEL, D_EXPERT), F32),
                            pltpu.VMEM((2, D_EXPERT, D_MODEL), F32),
                            pltpu.SMEM((1,), I32),
                            pltpu.SemaphoreType.DMA((2,))],
        ),
        out_shape=jax.ShapeDtypeStruct((buf.shape[0], D_MODEL), F32),
        compiler_params=_params(),
        name="moe_experts",
    )(block_expert, n_used, next_expert, buf, w_g, w_u, w_d)


def _combine_kernel(slot_ref, x1_ref, gc_ref, g_ref, b_ref, eo_ref, y_ref, side_ref, rows, sem,
                    *, n, tm, alpha, main_tiles):
    i = pl.program_id(0)
    last = pl.num_programs(0) - 1
    cur = i % 2

    def start(tile, buf, r):
        for k in range(TOP_K):
            _row_copy(eo_ref, slot_ref[k * n + tile * tm + r], rows.at[buf, k], r, sem.at[buf]).start(priority=k)

    def wait(buf):
        for k in range(TOP_K):
            pltpu.make_async_copy(eo_ref.at[pl.ds(0, tm)], rows.at[buf, k], sem.at[buf]).wait()

    @pl.when(i == 0)
    def _():
        def body(r, c):
            start(0, 0, r)
            return c
        lax.fori_loop(0, tm, body, 0, unroll=ISSUE_UNROLL)

    wait(cur)
    nxt = jnp.minimum(i + 1, last)
    for r in range(tm):
        start(nxt, 1 - cur, r)
    gc = gc_ref[...]
    z = alpha * x1_ref[...] + gc[:, 0:1] * rows[cur, 0] + gc[:, 1:2] * rows[cur, 1]
    mu = jnp.mean(z, axis=-1, keepdims=True)
    zc = z - mu
    var = jnp.mean(zc * zc, axis=-1, keepdims=True)
    y = zc * lax.rsqrt(var + LN_EPS) * g_ref[...] + b_ref[...]

    @pl.when(i < main_tiles)
    def _():
        y_ref[...] = y

    @pl.when(i == main_tiles)
    def _():
        side_ref[...] = y[0:side_ref.shape[0]]

    @pl.when(i == last)
    def _():
        wait(1 - cur)


def _combine(slot_flat, x1, gate_cols, ln_g, ln_b, expert_out, *, tm, alpha, main_rows, side_rows):
    n = x1.shape[0]
    main_tiles = main_rows // tm
    return pl.pallas_call(
        functools.partial(_combine_kernel, n=n, tm=tm, alpha=alpha, main_tiles=main_tiles),
        grid_spec=pltpu.PrefetchScalarGridSpec(
            num_scalar_prefetch=1,
            grid=(n // tm,),
            in_specs=[pl.BlockSpec((tm, D_MODEL), lambda i, s: (i, 0)),
                      pl.BlockSpec((tm, TOP_K), lambda i, s: (i, 0)),
                      pl.BlockSpec((1, D_MODEL), lambda i, s: (0, 0)),
                      pl.BlockSpec((1, D_MODEL), lambda i, s: (0, 0)),
                      pl.BlockSpec(memory_space=pl.ANY)],
            out_specs=[pl.BlockSpec((tm, D_MODEL), lambda i, s: (jnp.minimum(i, main_tiles - 1), 0)),
                       pl.BlockSpec((side_rows, D_MODEL), lambda i, s: (0, 0))],
            scratch_shapes=[pltpu.VMEM((2, TOP_K, tm, D_MODEL), F32), pltpu.SemaphoreType.DMA((2,))],
        ),
        out_shape=[jax.ShapeDtypeStruct((main_rows, D_MODEL), F32),
                   jax.ShapeDtypeStruct((side_rows, D_MODEL), F32)],
        compiler_params=_params(),
        name="moe_combine",
    )(slot_flat, x1, gate_cols, ln_g, ln_b, expert_out)


def _hier_moe_ln(x1, ei, gt, w_g, w_u, w_d, ln_g, ln_b, *, tl, tm, bm, alpha, main_rows, side_rows):
    n = x1.shape[0]
    m = n * TOP_K
    slot, cnt = _slots(ei, tl=tl, bm=bm)
    counts = cnt[:, 0]
    pend = jnp.cumsum((counts + bm - 1) // bm * bm)
    nblk = (m + N_EXPERTS * (bm - 1) + bm - 1) // bm
    blk_start = jnp.arange(nblk, dtype=I32) * bm
    n_used = (pend[-1] // bm).astype(I32)
    be = jnp.minimum(jnp.sum(pend[None, :] <= blk_start[:, None], axis=1), N_EXPERTS - 1).astype(I32)
    be = jnp.where(jnp.arange(nblk) < n_used, be, jnp.take(be, n_used - 1))
    slot_flat = slot[0:TOP_K].reshape(m)
    buf = _dispatch(slot_flat, pend.astype(I32), counts, x1, tm=tm, bm=bm, nblk=nblk)
    ids = jnp.arange(N_EXPERTS, dtype=I32)
    first_at_or_after = lax.cummin(jnp.where(counts > 0, ids, N_EXPERTS), reverse=True)
    next_expert = jnp.concatenate([first_at_or_after[1:], jnp.full((1,), N_EXPERTS, I32)])
    next_expert = jnp.where(next_expert == N_EXPERTS, -1, next_expert).astype(I32)
    eo = _experts(be, n_used.reshape(1), next_expert, buf, w_g, w_u, w_d, bm=bm)
    gate_cols = gt[0:TOP_K].T
    return _combine(slot_flat, x1, gate_cols, ln_g, ln_b, eo, tm=tm, alpha=alpha,
                    main_rows=main_rows, side_rows=side_rows)


def _t5_bucket(n):
    nf = jnp.maximum(n, 1).astype(F32)
    large = MAX_EXACT + (jnp.log(nf / MAX_EXACT) / math.log(MAX_DISTANCE / MAX_EXACT)
                         * (N_BUCKETS - MAX_EXACT)).astype(I32)
    large = jnp.minimum(large, N_BUCKETS - 1)
    return jnp.where(n < MAX_EXACT, n, large)


def _bias_per_group(rel_bias):
    offs = jnp.arange(N_KEYS, dtype=I32)[None, :] * jnp.array(DILATIONS, I32)[:, None]
    bucket = _t5_bucket(offs)
    table = rel_bias.reshape(N_BUCKETS, N_GROUPS, HEADS)
    b = table[bucket, jnp.arange(N_GROUPS)[:, None]]
    return jnp.transpose(b, (0, 2, 1)).astype(F32)


def _prompt_bias_tables(bias):
    width = 3 * Q_BLOCK
    neg = jnp.full((N_GROUPS, HEADS, Q_BLOCK - 1), NEG_INF, F32)
    r = jnp.concatenate([neg, bias[:, :, ::-1], neg, jnp.full((N_GROUPS, HEADS, 1), NEG_INF, F32)], axis=-1)
    flat = jnp.tile(r, (1, 1, Q_BLOCK))[:, :, :Q_BLOCK * (width - 1)]
    skew = flat.reshape(N_GROUPS, HEADS, Q_BLOCK, width - 1)
    later = skew[:, :, :, Q_BLOCK - 1:3 * Q_BLOCK - 1]
    has_prev = (np.arange(2 * Q_BLOCK) >= Q_BLOCK)[None, None, None, :]
    first = jnp.where(has_prev, later, NEG_INF)
    tb = jnp.stack([first, later], axis=1)
    return tb.reshape(N_GROUPS, 2, HEADS // 2, 2 * Q_BLOCK, 2 * Q_BLOCK)


def _sample_bias_tables(bias):
    t = np.arange(T_NEW)
    bcs = []
    for g, d in enumerate(DILATIONS):
        rev = bias[g][:, ::-1][:, :WINDOW_KEYS]
        if d == 1:
            dist = np.arange(WINDOW_KEYS)[None, :] - t[:, None]
            vals = jnp.take(rev, np.clip(dist, 0, WINDOW_KEYS - 1), axis=1)
            bcs.append(jnp.where(dist[None] >= 0, vals, NEG_INF))
        else:
            cls = np.arange(d)[None, :] == t[:, None]
            table = jnp.where(cls[None, :, None, :], rev[:, None, :, None], NEG_INF)
            bcs.append(table.reshape(HEADS, T_NEW, WINDOW_KEYS * d))
    back = t[:, None] - t[None, :]
    vals = jnp.take(bias, np.clip(back, 0, T_NEW - 1), axis=2)
    ok = np.stack([(back >= 0) if d == 1 else (back == 0) for d in DILATIONS])
    return bcs, jnp.where(ok[:, None], vals, NEG_INF)


def _split_bf16(w):
    hi = w.astype(BF16)
    lo = (w - hi.astype(F32)).astype(BF16)
    return hi, lo


def kernel(x_prompt, x_sample, cache_attn_w128, cache_attn_w512, cache_attn_w2048, state_conv, rel_bias, w_in, w_conv, w_pa, w_pb, w_o, ln1_g, ln1_b, w_router_group, w_router_expert, w_expert_gate, w_expert_up, w_expert_down, ln2_g, ln2_b):
    depth = w_in.shape[0]
    assert depth == 1 and x_prompt.shape[0] == 1
    alpha = (2.0 * depth) ** 0.25
    s = x_prompt.shape[1]
    bd, t_len = x_sample.shape[0], x_sample.shape[1]
    assert t_len == T_NEW and s % (DILATIONS[-1] * Q_BLOCK * Q_BLOCKS_PER_STEP) == 0
    assert s % PROJ_TILE == 0 and s % MIX_TILE == 0 and s % MOE_TILE == 0
    assert (s + MIX_TILE) % SLOT_TILE == 0 and (s + MIX_TILE) % MOE_TILE == 0

    bias = _bias_per_group(rel_bias)
    tb = _prompt_bias_tables(bias)
    bcs, bn = _sample_bias_tables(bias)

    w_in_bf = w_in[0].astype(BF16)
    w_pa_bf = w_pa[0].astype(BF16)
    w_pb_bf = w_pb[0].astype(BF16)
    w_o_bf = w_o[0].astype(BF16)
    wr = jnp.zeros((ROUTER_ROWS, D_MODEL), F32)
    wr = wr.at[0:N_EXPERT_GROUPS].set(w_router_group[0].T).at[8:8 + N_EXPERTS].set(w_router_expert[0].T)
    wr_hi, wr_lo = _split_bf16(wr)
    g1, b1 = ln1_g[0][None], ln1_b[0][None]
    g2, b2 = ln2_g[0][None], ln2_b[0][None]
    wg, wu, wd = w_expert_gate[0], w_expert_up[0], w_expert_down[0]

    xp = x_prompt[0]
    kv_tail = min(MAX_DISTANCE, s)
    q, kb, vb, (k32, v32, yb, sga, sgb, ut) = _proj(
        xp, w_in_bf, w_conv[0], None, tm=PROJ_TILE, u_tail=8, q_dtype=BF16, dils=DILATIONS, kv_f32=True)
    k32, v32 = k32[s - kv_tail:], v32[s - kv_tail:]
    o_l = [_attn_prompt_group(q[g], kb[g], vb[g], tb[g], g) for g in range(N_GROUPS)]
    routed = _mix(xp, [a[0] for a in o_l], [a[1] for a in o_l], yb, sga, sgb,
                  w_pa_bf, w_pb_bf, w_o_bf, g1, b1, wr_hi, wr_lo, tm=MIX_TILE, alpha=alpha, dils=DILATIONS,
                  extra_tiles=1)

    kv_prompt = []
    for g, d in enumerate(DILATIONS):
        length = min(WINDOW_KEYS * d, s)
        cols = slice(g * GROUP_WIDTH, (g + 1) * GROUP_WIDTH)
        kg = k32[kv_tail - length:, cols].reshape(length, HEADS, HEAD_DIM)
        vg = v32[kv_tail - length:, cols].reshape(length, HEADS, HEAD_DIM)
        kv_prompt.append(jnp.stack([kg, vg], axis=1)[None, None])
    conv_prompt = ut[6:8][None, None]

    ns = bd * t_len
    xs = x_sample.reshape(ns, D_MODEL)
    st = state_conv[0]
    s0 = jnp.repeat(st[:, 0], t_len, axis=0)
    s1 = jnp.repeat(st[:, 1], t_len, axis=0)
    qs, _, _, (k32s, v32s, ybs, sgas, sgbs, us) = _proj(
        xs, w_in_bf, w_conv[0], (s0, s1), tm=ns, u_tail=ns, q_dtype=F32, dils=NO_DILATION, kv_f32=True)
    qs = jnp.concatenate([a[0] for a in qs], axis=1)
    packed = jnp.stack([qs, k32s, v32s]).reshape(3, bd, t_len, N_GROUPS, GROUP_WIDTH)
    qkv_t = jnp.transpose(packed, (1, 4, 0, 3, 2)).reshape(bd, GROUP_WIDTH, 3 * N_GROUPS * t_len)
    qkv_t = jnp.pad(qkv_t, ((0, 0), (0, 0), (0, 128 - 3 * N_GROUPS * t_len)))
    caches = (cache_attn_w128[0], cache_attn_w512[0], cache_attn_w2048[0])
    caches_t = [jnp.transpose(c, (0, 2, 3, 4, 1)).reshape(bd, 2, GROUP_WIDTH, c.shape[1]) for c in caches]
    pair = (bd, t_len, N_GROUPS, HEADS // 2, 2, HEAD_DIM)
    q6 = jnp.transpose(qs.reshape(pair), (0, 3, 2, 4, 1, 5))
    zeros = jnp.zeros_like(q6[:, :, :, 0])
    qbd = jnp.stack([jnp.concatenate([q6[:, :, :, 0], zeros], axis=-1),
                     jnp.concatenate([zeros, q6[:, :, :, 1]], axis=-1)], axis=3)
    qbd = qbd.reshape(bd, HEADS // 2, N_GROUPS, 2 * t_len, 128)
    new_rows = jnp.stack([k32s, v32s]).reshape(2, bd, t_len, N_GROUPS, HEADS // 2, 128)
    new_rows = jnp.transpose(new_rows, (1, 4, 3, 0, 2, 5))
    bcs = [t.reshape(HEADS // 2, 2 * t_len, t.shape[-1]) for t in bcs]
    bn = bn.reshape(N_GROUPS, HEADS // 2, 2 * t_len, t_len)
    n0, n1, n2, o_s, lse_s = _sample_cache(qbd, new_rows, qkv_t, caches_t, bcs, bn)

    def unpack(a):
        a = a[:, :N_GROUPS * t_len].reshape(bd, N_GROUPS, t_len, GROUP_WIDTH)
        return jnp.transpose(a, (1, 0, 2, 3)).reshape(N_GROUPS, 1, ns, GROUP_WIDTH)

    o_s, lse_s = unpack(o_s), unpack(lse_s)
    assert ns <= min(MIX_TILE, MOE_TILE) and s % ns == 0
    x1, ei, gt = _mix(xs, o_s, lse_s, ybs, sgas, sgbs, w_pa_bf, w_pb_bf, w_o_bf, g1, b1, wr_hi, wr_lo,
                      tm=ns, alpha=alpha, dils=NO_DILATION, into=routed, into_tile=s // ns)
    y_prompt, y_sample = _hier_moe_ln(x1, ei, gt, wg, wu, wd, g2, b2, tl=SLOT_TILE, tm=MOE_TILE, bm=EXPERT_BLOCK,
                                      alpha=alpha, main_rows=s, side_rows=ns)
    y_prompt = y_prompt[None]
    y_sample = y_sample.reshape(bd, t_len, D_MODEL)

    kv_sample = [jnp.transpose(c.reshape(bd, 2, HEADS, HEAD_DIM, c.shape[-1]), (0, 4, 1, 2, 3))[None]
                 for c in (n0, n1, n2)]
    conv_sample = us.reshape(bd, t_len, CONV_CHANNELS)[:, t_len - 2:][None]

    return (y_prompt, y_sample, kv_prompt[0], kv_prompt[1], kv_prompt[2], conv_prompt,
            kv_sample[0], kv_sample[1], kv_sample[2], conv_sample)
```
